```python
import math
import jax, jax.numpy as jnp
from jax import lax
import numpy as np

D_MODEL = 1024
BATCH = 8
SEQ = 2048
DEPTH = 2

HEAD_DIM = 64
SWA_Q_HEADS = 8
SWA_KV_HEADS = 2
SWA_WINDOW = 128
SWA_BLOCK = 128
SWA_Q = SWA_Q_HEADS * HEAD_DIM
SWA_KV = SWA_KV_HEADS * HEAD_DIM
REL_BUCKETS = 32
REL_MAX_DIST = 128
SSM_CH = 256
SSM_GROUP = 16
SSM_GROUPS = SSM_CH // SSM_GROUP
SSM_STATE = 64
DT_MIN = 1e-3
DT_MAX = 1e-1
MLA_HEADS = 4
MLA_Q_RANK = 256
MLA_KV_RANK = 128
MLA_NOPE = 64
MLA_ROPE = 32
MLA_V = 64
MLA_BLOCK = 128
ROPE_THETA = 10000.0
D_FF = ((8 * D_MODEL // 3 + 255) // 256) * 256
EPS = 1e-6
NEG = -1e30

MIX_WIDTH = SWA_Q + SSM_CH + MLA_HEADS * MLA_V
IN_SIZES = [SWA_Q, SWA_KV, SWA_KV, SSM_CH, MLA_Q_RANK, MLA_KV_RANK, MLA_ROPE]
IN_COLS = sum(IN_SIZES)
IN_SPLITS = [int(v) for v in np.cumsum(IN_SIZES)[:-1]]

kernel_name = "hymba_swa_s5_mla_hybrid"


def rms_norm(x, g):
    xf = x.astype(jnp.float32)
    y = xf * lax.rsqrt(jnp.mean(xf * xf, axis=-1, keepdims=True) + EPS)
    return (y * g.astype(jnp.float32)).astype(x.dtype)


def t5_bucket(dist):
    n = jnp.maximum(dist, 0)
    max_exact = REL_BUCKETS // 2
    large = max_exact + (jnp.log(jnp.maximum(n, 1).astype(jnp.float32) / max_exact)
                         / math.log(REL_MAX_DIST / max_exact)
                         * (REL_BUCKETS - max_exact)).astype(jnp.int32)
    large = jnp.minimum(large, REL_BUCKETS - 1)
    return jnp.where(n < max_exact, n, large)


def band_distance():
    qi = jnp.arange(SWA_BLOCK)[:, None]
    kj = jnp.arange(2 * SWA_BLOCK)[None, :]
    return qi + SWA_BLOCK - kj, kj


def band_bias(rel_bias):
    dist, _ = band_distance()
    b = rel_bias[t5_bucket(dist)]
    return jnp.transpose(b, (2, 0, 1)).astype(jnp.float32)


def apply_rope(x, positions):
    r = x.shape[-1]
    half = r // 2
    inv_freq = jnp.power(ROPE_THETA, -jnp.arange(half, dtype=jnp.float32) * 2.0 / r)
    ang = positions.astype(jnp.float32)[:, :, None, None] * inv_freq
    cos, sin = jnp.cos(ang), jnp.sin(ang)
    xf = x.astype(jnp.float32)
    x1, x2 = xf[..., :half], xf[..., half:]
    return jnp.concatenate([x1 * cos - x2 * sin, x1 * sin + x2 * cos], -1).astype(x.dtype)


def swa_attention(q, k, v, sinks, bias):
    b_, s_ = q.shape[:2]
    nb = s_ // SWA_BLOCK
    g = SWA_Q_HEADS // SWA_KV_HEADS
    qb = q.reshape(b_, nb, SWA_BLOCK, SWA_KV_HEADS, g, HEAD_DIM)

    def band(t):
        tb = t.reshape(b_, nb, SWA_BLOCK, SWA_KV_HEADS, HEAD_DIM)
        prev = jnp.pad(tb, ((0, 0), (1, 0), (0, 0), (0, 0), (0, 0)))[:, :-1]
        return jnp.concatenate([prev, tb], axis=2)

    kb, vb = band(k), band(v)
    s = jnp.einsum('bnqhgd,bnkhd->bnhgqk', qb, kb).astype(jnp.float32) * (HEAD_DIM ** -0.5)
    s = s + bias.reshape(SWA_KV_HEADS, g, SWA_BLOCK, 2 * SWA_BLOCK)
    dist, kj = band_distance()
    blk_start = jnp.arange(nb)[:, None, None] * SWA_BLOCK
    valid = (dist >= 0) & (dist < SWA_WINDOW) & (blk_start + kj - SWA_BLOCK >= 0)
    s = jnp.where(valid[None, :, None, None], s, NEG)
    sink_col = jnp.broadcast_to(sinks.astype(jnp.float32).reshape(SWA_KV_HEADS, g, 1, 1),
                                s.shape[:-1] + (1,))
    p = jax.nn.softmax(jnp.concatenate([s, sink_col], axis=-1), axis=-1)[..., :-1]
    o = jnp.einsum('bnhgqk,bnkhd->bnqhgd', p.astype(v.dtype), vb)
    return o.reshape(b_, s_, SWA_Q)


def s5_mixer(u, a_re, a_im, log_dt, b_re, b_im, c_re, c_im, d, w_glu):
    b_, s_ = u.shape[:2]
    f32 = jnp.float32
    lam = lax.complex(a_re.astype(f32), a_im.astype(f32))
    dt = jnp.exp(log_dt.astype(f32))
    a_bar = jnp.exp(lam * dt[:, None])
    bmat = lax.complex(b_re.astype(f32), b_im.astype(f32))
    b_bar = ((a_bar - 1.0) / lam)[..., None] * bmat
    ug = u.astype(f32).reshape(b_, s_, SSM_GROUPS, SSM_GROUP)
    bu = jnp.einsum('bsgc,gpc->bsgp', ug.astype(jnp.complex64), b_bar)
    a_seq = jnp.broadcast_to(a_bar, bu.shape)

    def combine(e1, e2):
        a1, x1 = e1
        a2, x2 = e2
        return a1 * a2, a2 * x1 + x2

    _, states = lax.associative_scan(combine, (a_seq, bu), axis=1)
    cmat = lax.complex(c_re.astype(f32), c_im.astype(f32))
    y = jnp.real(jnp.einsum('bsgp,gcp->bsgc', states, cmat)) \
        + d.astype(f32).reshape(SSM_GROUPS, SSM_GROUP) * ug
    y = jax.nn.gelu(y.reshape(b_, s_, SSM_CH))
    y = y * jax.nn.sigmoid(y @ w_glu.astype(f32))
    return y.astype(u.dtype)


def mla_attention(c_q, c_kv, k_rope, positions, q_norm_g, w_q_up, kv_norm_g, w_kv_up):
    b_, s_ = c_q.shape[:2]
    q = (rms_norm(c_q, q_norm_g) @ w_q_up).reshape(b_, s_, MLA_HEADS, MLA_NOPE + MLA_ROPE)
    q_nope, q_pe = q[..., :MLA_NOPE], apply_rope(q[..., MLA_NOPE:], positions)
    kv = (rms_norm(c_kv, kv_norm_g) @ w_kv_up).reshape(b_, s_, MLA_HEADS, MLA_NOPE + MLA_V)
    k_nope, v = kv[..., :MLA_NOPE], kv[..., MLA_NOPE:]
    k_pe = apply_rope(k_rope[:, :, None, :], positions)[:, :, 0, :]
    scale = (MLA_NOPE + MLA_ROPE) ** -0.5
    nb = s_ // MLA_BLOCK
    key_pos = jnp.arange(s_)

    def attend(args):
        qn, qp, i = args
        s = (jnp.einsum('bqhd,bkhd->bhqk', qn, k_nope)
             + jnp.einsum('bqhr,bkr->bhqk', qp, k_pe)).astype(jnp.float32) * scale
        q_pos = i * MLA_BLOCK + jnp.arange(MLA_BLOCK)
        s = jnp.where(key_pos[None, :] <= q_pos[:, None], s, NEG)
        p = jax.nn.softmax(s, axis=-1).astype(v.dtype)
        return jnp.einsum('bhqk,bkhd->bqhd', p, v)

    qn_b = q_nope.reshape(b_, nb, MLA_BLOCK, MLA_HEADS, MLA_NOPE).swapaxes(0, 1)
    qp_b = q_pe.reshape(b_, nb, MLA_BLOCK, MLA_HEADS, MLA_ROPE).swapaxes(0, 1)
    o = lax.map(attend, (qn_b, qp_b, jnp.arange(nb)))
    return o.swapaxes(0, 1).reshape(b_, s_, MLA_HEADS * MLA_V)


def setup_inputs(seed: int = 0) -> dict:
    key = jax.random.key(seed)
    ks = list(jax.random.split(key, 32))
    nrm = lambda k, shape, s: jax.random.normal(k, shape, jnp.float32) * s
    gain = lambda k, shape: 1.0 + 0.02 * jax.random.normal(k, shape, jnp.float32)
    n_idx = jnp.arange(SSM_STATE, dtype=jnp.float32)
    return {
        "x": nrm(ks[0], (BATCH, SEQ, D_MODEL), 1.0),
        "positions": jnp.broadcast_to(jnp.arange(SEQ, dtype=jnp.int32)[None], (BATCH, SEQ)),
        "rel_bias": nrm(ks[1], (REL_BUCKETS, SWA_Q_HEADS), 0.5),
        "ln1_g": gain(ks[2], (DEPTH, D_MODEL)),
        "w_in": nrm(ks[3], (DEPTH, D_MODEL, IN_COLS), D_MODEL ** -0.5),
        "sinks": nrm(ks[4], (DEPTH, SWA_Q_HEADS), 0.5),
        "ssm_a_re": -0.5 * jnp.exp(nrm(ks[5], (DEPTH, SSM_GROUPS, SSM_STATE), 0.05)),
        "ssm_a_im": math.pi * n_idx + nrm(ks[6], (DEPTH, SSM_GROUPS, SSM_STATE), 0.01),
        "ssm_log_dt": jax.random.uniform(ks[7], (DEPTH, SSM_GROUPS), jnp.float32,
                                         math.log(DT_MIN), math.log(DT_MAX)),
        "ssm_b_re": nrm(ks[8], (DEPTH, SSM_GROUPS, SSM_STATE, SSM_GROUP), (2 * SSM_GROUP) ** -0.5),
        "ssm_b_im": nrm(ks[9], (DEPTH, SSM_GROUPS, SSM_STATE, SSM_GROUP), (2 * SSM_GROUP) ** -0.5),
        "ssm_c_re": nrm(ks[10], (DEPTH, SSM_GROUPS, SSM_GROUP, SSM_STATE), (2 * SSM_STATE) ** -0.5),
        "ssm_c_im": nrm(ks[11], (DEPTH, SSM_GROUPS, SSM_GROUP, SSM_STATE), (2 * SSM_STATE) ** -0.5),
        "ssm_d": nrm(ks[12], (DEPTH, SSM_CH), 1.0),
        "ssm_w_glu": nrm(ks[13], (DEPTH, SSM_CH, SSM_CH), SSM_CH ** -0.5),
        "mla_q_norm_g": gain(ks[14], (DEPTH, MLA_Q_RANK)),
        "mla_w_q_up": nrm(ks[15], (DEPTH, MLA_Q_RANK, MLA_HEADS * (MLA_NOPE + MLA_ROPE)), MLA_Q_RANK ** -0.5),
        "mla_kv_norm_g": gain(ks[16], (DEPTH, MLA_KV_RANK)),
        "mla_w_kv_up": nrm(ks[17], (DEPTH, MLA_KV_RANK, MLA_HEADS * (MLA_NOPE + MLA_V)), MLA_KV_RANK ** -0.5),
        "w_out": nrm(ks[18], (DEPTH, MIX_WIDTH, D_MODEL), MIX_WIDTH ** -0.5),
        "ln2_g": gain(ks[19], (DEPTH, D_MODEL)),
        "w_gate": nrm(ks[20], (DEPTH, D_MODEL, D_FF), D_MODEL ** -0.5),
        "w_up": nrm(ks[21], (DEPTH, D_MODEL, D_FF), D_MODEL ** -0.5),
        "w_down": nrm(ks[22], (DEPTH, D_FF, D_MODEL), D_FF ** -0.5),
        "final_g": gain(ks[23], (D_MODEL,)),
    }


def reference(x, positions, rel_bias, ln1_g, w_in, sinks, ssm_a_re, ssm_a_im, ssm_log_dt,
              ssm_b_re, ssm_b_im, ssm_c_re, ssm_c_im, ssm_d, ssm_w_glu,
              mla_q_norm_g, mla_w_q_up, mla_kv_norm_g, mla_w_kv_up, w_out,
              ln2_g, w_gate, w_up, w_down, final_g):
    bias = band_bias(rel_bias)
    for l in range(DEPTH):
        h = rms_norm(x, ln1_g[l])
        proj = h @ w_in[l]
        q_a, k_a, v_a, u_b, c_q, c_kv, k_r = jnp.split(proj, IN_SPLITS, axis=-1)
        o_a = swa_attention(q_a, k_a, v_a, sinks[l], bias)
        o_b = s5_mixer(u_b, ssm_a_re[l], ssm_a_im[l], ssm_log_dt[l], ssm_b_re[l], ssm_b_im[l],
                       ssm_c_re[l], ssm_c_im[l], ssm_d[l], ssm_w_glu[l])
        o_c = mla_attention(c_q, c_kv, k_r, positions, mla_q_norm_g[l], mla_w_q_up[l],
                            mla_kv_norm_g[l], mla_w_kv_up[l])
        x = x + jnp.concatenate([o_a, o_b, o_c], axis=-1) @ w_out[l]
        h = rms_norm(x, ln2_g[l])
        x = x + (jax.nn.silu(h @ w_gate[l]) * (h @ w_up[l])) @ w_down[l]
    return rms_norm(x, final_g)
```

```python
import functools
import math

import jax
import jax.numpy as jnp
from jax import lax
from jax.experimental import pallas as pl
from jax.experimental.pallas import tpu as pltpu

D_MODEL = 1024
DEPTH = 2
HEAD_DIM = 64
SWA_Q_HEADS = 8
SWA_KV_HEADS = 2
SWA_BLOCK = 128
SWA_Q = SWA_Q_HEADS * HEAD_DIM
SWA_KV = SWA_KV_HEADS * HEAD_DIM
REL_BUCKETS = 32
REL_MAX_DIST = 128
SSM_CH = 256
SSM_GROUP = 16
SSM_GROUPS = SSM_CH // SSM_GROUP
SSM_STATE = 64
SSM_STATES = SSM_GROUPS * SSM_STATE
MLA_HEADS = 4
MLA_Q_RANK = 256
MLA_KV_RANK = 128
MLA_NOPE = 64
MLA_ROPE = 32
MLA_V = 64
ROPE_THETA = 10000.0
D_FF = 2816
EPS = 1e-6
NEG = -1e30

LANES = 128
SUBLANES = 8
MLA_HEAD_PAD = LANES
VMEM_LIMIT = 56 * 1024 * 1024

IN_PROJ_TM = 512
SWA_TQ = 512
MLA_TQ = 256
MLA_TK = 256
SSM_TS = 128
FFN_TM = 512
FFN_CHUNK = 256

BF16 = jnp.bfloat16
F32 = jnp.float32


def _rms(x, g):
    return x * lax.rsqrt(jnp.mean(x * x, axis=-1, keepdims=True) + EPS) * g


def _dot(a, b):
    return jnp.dot(a, b, preferred_element_type=F32)


def _dot_nt(a, b):
    return lax.dot_general(a, b, (((1,), (1,)), ((), ())), preferred_element_type=F32)


def _trig_kernel(pos_ref, freq_ref, cos_ref, sin_ref):
    ang = pos_ref[...].astype(F32) * freq_ref[...]
    cos_ref[...] = jnp.cos(ang)
    sin_ref[...] = jnp.sin(ang)


def _trig_tables(positions):
    rows = positions.size
    half = MLA_ROPE // 2
    inv_freq = jnp.power(ROPE_THETA, -jnp.arange(half, dtype=F32) * 2.0 / MLA_ROPE)
    freq = jnp.zeros((1, MLA_HEAD_PAD), F32).at[0, MLA_NOPE:MLA_NOPE + MLA_ROPE].set(jnp.tile(inv_freq, 2))
    tm = 2048
    return pl.pallas_call(
        _trig_kernel,
        grid=(rows // tm,),
        in_specs=[pl.BlockSpec((tm, 1), lambda i: (i, 0)),
                  pl.BlockSpec((1, MLA_HEAD_PAD), lambda i: (0, 0))],
        out_specs=[pl.BlockSpec((tm, MLA_HEAD_PAD), lambda i: (i, 0))] * 2,
        out_shape=[jax.ShapeDtypeStruct((rows, MLA_HEAD_PAD), F32)] * 2,
        name="rope_tables",
    )(positions.reshape(rows, 1), freq)


def _in_proj_kernel(x_ref, g_ref, w_ref, cos_ref, sin_ref, qg_ref, wq_ref, kvg_ref, wkv_ref,
                    qa_ref, ka_ref, va_ref, u_ref, qm_ref, km_ref, vm_ref):
    h = _rms(x_ref[...], g_ref[...]).astype(BF16)
    proj = _dot(h, w_ref[...])
    qa_ref[...] = (proj[:, :SWA_Q] * (HEAD_DIM ** -0.5)).astype(BF16)
    c = SWA_Q
    ka_ref[...] = proj[:, c:c + SWA_KV].astype(BF16)
    c += SWA_KV
    va_ref[...] = proj[:, c:c + SWA_KV].astype(BF16)
    c += SWA_KV
    u_ref[...] = proj[:, c:c + SSM_CH]
    c += SSM_CH
    c_q = proj[:, c:c + MLA_Q_RANK]
    c += MLA_Q_RANK
    c_kv = proj[:, c:c + MLA_KV_RANK]
    c += MLA_KV_RANK
    k_r = proj[:, c:c + LANES]
    k_r_rot = proj[:, c + LANES:c + 2 * LANES]
    cos = cos_ref[...]
    sin = sin_ref[...]
    width = MLA_HEADS * MLA_HEAD_PAD
    q2 = _dot(_rms(c_q, qg_ref[...]).astype(BF16), wq_ref[...])
    kv2 = _dot(_rms(c_kv, kvg_ref[...]).astype(BF16), wkv_ref[...])
    k_pe = k_r * cos + k_r_rot * sin
    for hd in range(MLA_HEADS):
        sl = slice(hd * MLA_HEAD_PAD, (hd + 1) * MLA_HEAD_PAD)
        sl_rot = slice(width + hd * MLA_HEAD_PAD, width + (hd + 1) * MLA_HEAD_PAD)
        qm_ref[:, sl] = (q2[:, sl] * cos + q2[:, sl_rot] * sin).astype(BF16)
        km_ref[:, sl] = (kv2[:, sl] + k_pe).astype(BF16)
    vm_ref[...] = kv2[:, width:].astype(BF16)


def _in_proj(x, g, w_ext, cos_t, sin_t, qg, wq_ext, kvg, wkv_ext):
    b_, s_, _ = x.shape
    tm = IN_PROJ_TM
    nt = s_ // tm
    const = lambda b, i: (0, 0)
    row3 = lambda b, i: (b, i, 0)
    trig = pl.BlockSpec((tm, MLA_HEAD_PAD), lambda b, i: (b * nt + i, 0))
    mla_w = MLA_HEADS * MLA_HEAD_PAD
    return pl.pallas_call(
        _in_proj_kernel,
        grid=(b_, nt),
        in_specs=[pl.BlockSpec((None, tm, D_MODEL), row3),
                  pl.BlockSpec((1, D_MODEL), const),
                  pl.BlockSpec(w_ext.shape, const),
                  trig, trig,
                  pl.BlockSpec((1, MLA_Q_RANK), const),
                  pl.BlockSpec(wq_ext.shape, const),
                  pl.BlockSpec((1, MLA_KV_RANK), const),
                  pl.BlockSpec(wkv_ext.shape, const)],
        out_specs=[pl.BlockSpec((None, tm, SWA_Q), row3),
                   pl.BlockSpec((None, tm, SWA_KV), row3),
                   pl.BlockSpec((None, tm, SWA_KV), row3),
                   pl.BlockSpec((tm, SSM_CH), lambda b, i: (i, b)),
                   pl.BlockSpec((None, tm, mla_w), row3),
                   pl.BlockSpec((None, tm, mla_w), row3),
                   pl.BlockSpec((None, tm, MLA_HEADS * MLA_V), row3)],
        out_shape=[jax.ShapeDtypeStruct((b_, s_, SWA_Q), BF16),
                   jax.ShapeDtypeStruct((b_, s_, SWA_KV), BF16),
                   jax.ShapeDtypeStruct((b_, s_, SWA_KV), BF16),
                   jax.ShapeDtypeStruct((s_, b_ * SSM_CH), F32),
                   jax.ShapeDtypeStruct((b_, s_, mla_w), BF16),
                   jax.ShapeDtypeStruct((b_, s_, mla_w), BF16),
                   jax.ShapeDtypeStruct((b_, s_, MLA_HEADS * MLA_V), BF16)],
        compiler_params=pltpu.CompilerParams(
            dimension_semantics=("arbitrary", "arbitrary"), vmem_limit_bytes=VMEM_LIMIT),
        name="in_proj",
    )(x, g, w_ext, cos_t, sin_t, qg, wq_ext, kvg, wkv_ext)


def _swa_kernel(sink_ref, q_ref, k_ref, v_ref, bias_ref, o_ref):
    qi = pl.program_id(1)
    blocks = SWA_TQ // SWA_BLOCK
    lane = lax.broadcasted_iota(jnp.int32, (2 * SWA_BLOCK, LANES), 1)
    low = lane < HEAD_DIM
    out_low = lax.broadcasted_iota(jnp.int32, (SWA_BLOCK, LANES), 1) < HEAD_DIM
    for j in range(blocks):
        blk = qi * blocks + j
        start = pl.multiple_of(jnp.maximum(blk - 1, 0) * SWA_BLOCK, SWA_BLOCK)
        table = jnp.minimum(blk, 1)
        kb = k_ref[pl.ds(start, 2 * SWA_BLOCK), :]
        vb = v_ref[pl.ds(start, 2 * SWA_BLOCK), :]
        kb_sw = pltpu.roll(kb, HEAD_DIM, 1)
        vb_sw = pltpu.roll(vb, HEAD_DIM, 1)
        zero = jnp.zeros_like(kb)
        k_var = ((jnp.where(low, kb, zero), jnp.where(low, zero, kb_sw)),
                 (jnp.where(low, kb_sw, zero), jnp.where(low, zero, kb)))
        v_var = (jnp.where(low, vb, vb_sw), jnp.where(low, vb_sw, vb))
        rows = slice(j * SWA_BLOCK, (j + 1) * SWA_BLOCK)
        for i in range(SWA_Q // LANES):
            kvh = (2 * i) // (SWA_Q_HEADS // SWA_KV_HEADS)
            q = q_ref[rows, i * LANES:(i + 1) * LANES]
            halves = []
            for half in range(2):
                hq = 2 * i + half
                sink = sink_ref[hq]
                s = _dot_nt(q, k_var[kvh][half]) + bias_ref[table, hq]
                m = jnp.maximum(jnp.max(s, axis=-1, keepdims=True), sink)
                p = jnp.exp(s - m)
                l = jnp.sum(p, axis=-1, keepdims=True) + jnp.exp(sink - m)
                halves.append(_dot(p.astype(BF16), v_var[kvh]) / l)
            o_ref[rows, i * LANES:(i + 1) * LANES] = jnp.where(out_low, halves[0], halves[1]).astype(BF16)


def _swa(q, k, v, sinks, bias2):
    b_, s_, _ = q.shape
    tq = SWA_TQ
    return pl.pallas_call(
        _swa_kernel,
        grid=(b_, s_ // tq),
        in_specs=[pl.BlockSpec(memory_space=pltpu.SMEM),
                  pl.BlockSpec((None, tq, SWA_Q), lambda b, i: (b, i, 0)),
                  pl.BlockSpec((None, s_, SWA_KV), lambda b, i: (b, 0, 0)),
                  pl.BlockSpec((None, s_, SWA_KV), lambda b, i: (b, 0, 0)),
                  pl.BlockSpec(bias2.shape, lambda b, i: (0, 0, 0, 0))],
        out_specs=pl.BlockSpec((None, tq, SWA_Q), lambda b, i: (b, i, 0)),
        out_shape=jax.ShapeDtypeStruct((b_, s_, SWA_Q), BF16),
        compiler_params=pltpu.CompilerParams(
            dimension_semantics=("arbitrary", "arbitrary"), vmem_limit_bytes=VMEM_LIMIT),
        name="swa_attention",
    )(sinks, q, k, v, bias2)


def _t5_bucket(dist):
    n = jnp.maximum(dist, 0)
    max_exact = REL_BUCKETS // 2
    large = max_exact + (jnp.log(jnp.maximum(n, 1).astype(F32) / max_exact)
                         / math.log(REL_MAX_DIST / max_exact)
                         * (REL_BUCKETS - max_exact)).astype(jnp.int32)
    large = jnp.minimum(large, REL_BUCKETS - 1)
    return jnp.where(n < max_exact, n, large)


def _band_bias_tables(rel_bias):
    qi = jnp.arange(SWA_BLOCK)[:, None]
    kj = jnp.arange(2 * SWA_BLOCK)[None, :]
    dist = qi + SWA_BLOCK - kj
    b = jnp.transpose(rel_bias[_t5_bucket(dist)], (2, 0, 1)).astype(F32)
    valid = (dist >= 0) & (dist < SWA_BLOCK)
    normal = jnp.where(valid[None], b, NEG)
    first = jnp.concatenate([normal[:, :, SWA_BLOCK:], jnp.full_like(normal[:, :, SWA_BLOCK:], NEG)], axis=-1)
    return jnp.stack([first, normal])


def _mla_kernel(q_ref, k_ref, v_ref, o_ref):
    qi = pl.program_id(1)
    tq, tk = MLA_TQ, MLA_TK
    scale = (MLA_NOPE + MLA_ROPE) ** -0.5
    row = lax.broadcasted_iota(jnp.int32, (tq, tk), 0)
    col = lax.broadcasted_iota(jnp.int32, (tq, tk), 1)
    causal = col <= row
    out_low = lax.broadcasted_iota(jnp.int32, (tq, LANES), 1) < MLA_V
    outs = []
    for hd in range(MLA_HEADS):
        head = slice(hd * MLA_HEAD_PAD, (hd + 1) * MLA_HEAD_PAD)
        vt = slice((hd // 2) * LANES, (hd // 2 + 1) * LANES)
        q = q_ref[:, head]

        def step(j, carry, masked):
            m, l, acc = carry
            k0 = pl.multiple_of(j * tk, tk)
            s = _dot_nt(q, k_ref[pl.ds(k0, tk), head]) * scale
            if masked:
                s = jnp.where(causal, s, NEG)
            m_new = jnp.maximum(m, jnp.max(s, axis=-1, keepdims=True))
            alpha = jnp.exp(m - m_new)
            p = jnp.exp(s - m_new)
            l = alpha * l + jnp.sum(p, axis=-1, keepdims=True)
            acc = alpha * acc + _dot(p.astype(BF16), v_ref[pl.ds(k0, tk), vt])
            return m_new, l, acc

        init = (jnp.full((tq, 1), NEG, F32), jnp.zeros((tq, 1), F32), jnp.zeros((tq, LANES), F32))
        carry = lax.fori_loop(0, qi, functools.partial(step, masked=False), init)
        _, l, acc = step(qi, carry, masked=True)
        outs.append(acc / l)
    for t in range(MLA_HEADS // 2):
        o_ref[:, t * LANES:(t + 1) * LANES] = jnp.where(out_low, outs[2 * t], outs[2 * t + 1]).astype(BF16)


def _mla(q, k, v):
    b_, s_, w = q.shape
    tq = MLA_TQ
    return pl.pallas_call(
        _mla_kernel,
        grid=(b_, s_ // tq),
        in_specs=[pl.BlockSpec((None, tq, w), lambda b, i: (b, i, 0)),
                  pl.BlockSpec((None, s_, w), lambda b, i: (b, 0, 0)),
                  pl.BlockSpec((None, s_, MLA_HEADS * MLA_V), lambda b, i: (b, 0, 0))],
        out_specs=pl.BlockSpec((None, tq, MLA_HEADS * MLA_V), lambda b, i: (b, i, 0)),
        out_shape=jax.ShapeDtypeStruct((b_, s_, MLA_HEADS * MLA_V), BF16),
        compiler_params=pltpu.CompilerParams(
            dimension_semantics=("arbitrary", "arbitrary"), vmem_limit_bytes=VMEM_LIMIT),
        name="mla_attention",
    )(q, k, v)


def _gelu_tanh(x):
    return 0.5 * x * (1.0 + jnp.tanh(math.sqrt(2.0 / math.pi) * (x + 0.044715 * (x * x * x))))


def _ssm_kernel(u_ref, bd_ref, ar_ref, ai_ref, cd_ref, d_ref, wglu_ref, y_ref, x_scr, sr_scr, si_scr):
    n = SSM_STATES

    @pl.when(pl.program_id(0) == 0)
    def _():
        sr_scr[...] = jnp.zeros_like(sr_scr)
        si_scr[...] = jnp.zeros_like(si_scr)

    u = u_ref[...]
    x_scr[...] = _dot(u.astype(BF16), bd_ref[...])
    ar = ar_ref[...]
    ai = ai_ref[...]
    batch = sr_scr.shape[0]

    def step(t, carry):
        xr, xi = carry
        r0 = pl.multiple_of(t * batch, batch)
        nr = ar * xr - ai * xi + x_scr[pl.ds(r0, batch), :n]
        ni = ar * xi + ai * xr + x_scr[pl.ds(r0, batch), n:]
        x_scr[pl.ds(r0, batch), :n] = nr
        x_scr[pl.ds(r0, batch), n:] = ni
        return nr, ni

    xr, xi = lax.fori_loop(0, SSM_TS, step, (sr_scr[...], si_scr[...]), unroll=4)
    sr_scr[...] = xr
    si_scr[...] = xi
    y = _dot(x_scr[...].astype(BF16), cd_ref[...]) + d_ref[...] * u
    y = _gelu_tanh(y)
    y = y * jax.nn.sigmoid(_dot(y.astype(BF16), wglu_ref[...]))
    y_ref[...] = y.astype(BF16)


def _ssm(u_tb, batch, bd, ar, ai, cd, d, wglu):
    rows = u_tb.shape[0]
    tr = SSM_TS * batch
    const = lambda i: (0, 0)
    return pl.pallas_call(
        _ssm_kernel,
        grid=(rows // tr,),
        in_specs=[pl.BlockSpec((tr, SSM_CH), lambda i: (i, 0)),
                  pl.BlockSpec(bd.shape, const),
                  pl.BlockSpec(ar.shape, const),
                  pl.BlockSpec(ai.shape, const),
                  pl.BlockSpec(cd.shape, const),
                  pl.BlockSpec(d.shape, const),
                  pl.BlockSpec(wglu.shape, const)],
        out_specs=pl.BlockSpec((tr, SSM_CH), lambda i: (i, 0)),
        out_shape=jax.ShapeDtypeStruct((rows, SSM_CH), BF16),
        scratch_shapes=[pltpu.VMEM((tr, 2 * SSM_STATES), F32),
                        pltpu.VMEM((batch, SSM_STATES), F32),
                        pltpu.VMEM((batch, SSM_STATES), F32)],
        compiler_params=pltpu.CompilerParams(
            dimension_semantics=("arbitrary",), vmem_limit_bytes=VMEM_LIMIT),
        name="s5_mixer",
    )(u_tb, bd, ar, ai, cd, d, wglu)


def _ssm_params(a_re, a_im, log_dt, b_re, b_im, c_re, c_im, batch):
    lam = lax.complex(a_re.astype(F32), a_im.astype(F32))
    dt = jnp.exp(log_dt.astype(F32))
    a_bar = jnp.exp(lam * dt[:, None])
    b_bar = ((a_bar - 1.0) / lam)[..., None] * lax.complex(b_re.astype(F32), b_im.astype(F32))
    eye = jnp.eye(SSM_GROUPS, dtype=F32)

    def in_op(m):
        return jnp.einsum('gpc,gh->gchp', m, eye).reshape(SSM_CH, SSM_STATES)

    def out_op(m):
        return jnp.einsum('gcp,gh->gphc', m, eye).reshape(SSM_STATES, SSM_CH)

    bd = jnp.concatenate([in_op(jnp.real(b_bar)), in_op(jnp.imag(b_bar))], axis=1).astype(BF16)
    cd = jnp.concatenate([out_op(c_re.astype(F32)), -out_op(c_im.astype(F32))], axis=0).astype(BF16)
    ar = jnp.broadcast_to(jnp.real(a_bar).reshape(1, SSM_STATES), (batch, SSM_STATES))
    ai = jnp.broadcast_to(jnp.imag(a_bar).reshape(1, SSM_STATES), (batch, SSM_STATES))
    return bd, ar, ai, cd


def _ffn_kernel(x_ref, oa_ref, ob_ref, oc_ref, wo_ref, g2_ref, wg_ref, wu_ref, wd_ref, gf_ref, o_ref, h_scr,
                *, final_norm):
    wo_a = wo_ref[:SWA_Q, :]
    wo_b = wo_ref[SWA_Q:SWA_Q + SSM_CH, :]
    wo_c = wo_ref[SWA_Q + SSM_CH:, :]
    x = x_ref[...] + _dot(oa_ref[...], wo_a) + _dot(ob_ref[...], wo_b) + _dot(oc_ref[...], wo_c)
    o_ref[...] = x
    h_scr[...] = _rms(x, g2_ref[...]).astype(BF16)

    def chunk(c, carry):
        h = h_scr[...]
        gate = _dot(h, wg_ref[c])
        up = _dot(h, wu_ref[c])
        act = (gate * jax.nn.sigmoid(gate) * up).astype(BF16)
        o_ref[...] += _dot(act, wd_ref[c])
        return carry

    lax.fori_loop(0, D_FF // FFN_CHUNK, chunk, 0)
    if final_norm:
        o_ref[...] = _rms(o_ref[...], gf_ref[...])


def _ffn_chunked_cols(w):
    return w.reshape(D_MODEL, D_FF // FFN_CHUNK, FFN_CHUNK).transpose(1, 0, 2).astype(BF16)


def _ffn(x, oa, ob_tb, oc, wo, g2, wg, wu, wd, gf, final_norm):
    b_, s_, _ = x.shape
    tm = FFN_TM
    const = lambda b, i: (0, 0)
    const3 = lambda b, i: (0, 0, 0)
    row3 = lambda b, i: (b, i, 0)
    return pl.pallas_call(
        functools.partial(_ffn_kernel, final_norm=final_norm),
        grid=(b_, s_ // tm),
        in_specs=[pl.BlockSpec((None, tm, D_MODEL), row3),
                  pl.BlockSpec((None, tm, SWA_Q), row3),
                  pl.BlockSpec((tm, SSM_CH), lambda b, i: (i, b)),
                  pl.BlockSpec((None, tm, MLA_HEADS * MLA_V), row3),
                  pl.BlockSpec(wo.shape, const),
                  pl.BlockSpec((1, D_MODEL), const),
                  pl.BlockSpec(wg.shape, const3),
                  pl.BlockSpec(wu.shape, const3),
                  pl.BlockSpec(wd.shape, const3),
                  pl.BlockSpec((1, D_MODEL), const)],
        out_specs=pl.BlockSpec((None, tm, D_MODEL), row3),
        out_shape=jax.ShapeDtypeStruct(x.shape, F32),
        scratch_shapes=[pltpu.VMEM((tm, D_MODEL), BF16)],
        compiler_params=pltpu.CompilerParams(
            dimension_semantics=("arbitrary", "arbitrary"), vmem_limit_bytes=VMEM_LIMIT),
        name="out_proj_ffn",
    )(x, oa, ob_tb, oc, wo, g2, wg, wu, wd, gf)


def _rot_half_cols(w):
    half = w.shape[-1] // 2
    return jnp.concatenate([-w[..., half:], w[..., :half]], axis=-1)


def _rope_tile(w):
    z = jnp.zeros(w.shape[:-1] + (MLA_NOPE,), w.dtype)
    z2 = jnp.zeros(w.shape[:-1] + (MLA_HEAD_PAD - MLA_NOPE - MLA_ROPE,), w.dtype)
    return jnp.concatenate([z, w, z2], axis=-1)


def _layer_weights(w_in, w_q_up, w_kv_up):
    kr = w_in[:, -MLA_ROPE:]
    w_ext = jnp.concatenate([w_in[:, :-MLA_ROPE], _rope_tile(kr), _rope_tile(_rot_half_cols(kr))], axis=1)
    qh = w_q_up.reshape(MLA_Q_RANK, MLA_HEADS, MLA_NOPE + MLA_ROPE)
    pad = jnp.zeros((MLA_Q_RANK, MLA_HEADS, MLA_HEAD_PAD - MLA_NOPE - MLA_ROPE), w_q_up.dtype)
    q_tiles = jnp.concatenate([qh, pad], axis=-1).reshape(MLA_Q_RANK, -1)
    q_rot = _rope_tile(_rot_half_cols(qh[..., MLA_NOPE:])).reshape(MLA_Q_RANK, -1)
    wq_ext = jnp.concatenate([q_tiles, q_rot], axis=1)
    kvh = w_kv_up.reshape(MLA_KV_RANK, MLA_HEADS, MLA_NOPE + MLA_V)
    kpad = jnp.zeros((MLA_KV_RANK, MLA_HEADS, MLA_HEAD_PAD - MLA_NOPE), w_kv_up.dtype)
    k_tiles = jnp.concatenate([kvh[..., :MLA_NOPE], kpad], axis=-1).reshape(MLA_KV_RANK, -1)
    v_cols = kvh[..., MLA_NOPE:].reshape(MLA_KV_RANK, -1)
    wkv_ext = jnp.concatenate([k_tiles, v_cols], axis=1)
    return w_ext.astype(BF16), wq_ext.astype(BF16), wkv_ext.astype(BF16)


def kernel(x, positions, rel_bias, ln1_g, w_in, sinks, ssm_a_re, ssm_a_im, ssm_log_dt, ssm_b_re, ssm_b_im,
           ssm_c_re, ssm_c_im, ssm_d, ssm_w_glu, mla_q_norm_g, mla_w_q_up, mla_kv_norm_g, mla_w_kv_up, w_out,
           ln2_g, w_gate, w_up, w_down, final_g):
    b_, s_, _ = x.shape
    cos_t, sin_t = _trig_tables(positions)
    bias2 = _band_bias_tables(rel_bias)
    gf = final_g.reshape(1, D_MODEL)
    for l in range(DEPTH):
        w_ext, wq_ext, wkv_ext = _layer_weights(w_in[l], mla_w_q_up[l], mla_w_kv_up[l])
        qa, ka, va, u_tb, qm, km, vm = _in_proj(
            x, ln1_g[l].reshape(1, D_MODEL), w_ext, cos_t, sin_t,
            mla_q_norm_g[l].reshape(1, MLA_Q_RANK), wq_ext,
            mla_kv_norm_g[l].reshape(1, MLA_KV_RANK), wkv_ext)
        o_a = _swa(qa, ka, va, sinks[l], bias2)
        bd, ar, ai, cd = _ssm_params(ssm_a_re[l], ssm_a_im[l], ssm_log_dt[l], ssm_b_re[l], ssm_b_im[l],
                                     ssm_c_re[l], ssm_c_im[l], b_)
        o_b = _ssm(u_tb.reshape(s_ * b_, SSM_CH), b_, bd, ar, ai, cd,
                   ssm_d[l].reshape(1, SSM_CH), ssm_w_glu[l].astype(BF16))
        o_c = _mla(qm, km, vm)
        x = _ffn(x, o_a, o_b.reshape(s_, b_ * SSM_CH), o_c, w_out[l].astype(BF16),
                 ln2_g[l].reshape(1, D_MODEL), _ffn_chunked_cols(w_gate[l]), _ffn_chunked_cols(w_up[l]),
                 w_down[l].astype(BF16).reshape(D_FF // FFN_CHUNK, FFN_CHUNK, D_MODEL), gf,
                 final_norm=(l == DEPTH - 1))
    return x
```

```python
import functools
import math

import jax
import jax.numpy as jnp
from jax import lax
from jax.experimental import pallas as pl
from jax.experimental.pallas import tpu as pltpu

D_MODEL = 1024
DEPTH = 2
HEAD_DIM = 64
SWA_Q_HEADS = 8
SWA_KV_HEADS = 2
SWA_BLOCK = 128
SWA_Q = SWA_Q_HEADS * HEAD_DIM
SWA_KV = SWA_KV_HEADS * HEAD_DIM
REL_BUCKETS = 32
REL_MAX_DIST = 128
SSM_CH = 256
SSM_GROUP = 16
SSM_GROUPS = SSM_CH // SSM_GROUP
SSM_STATE = 64
SSM_STATES = SSM_GROUPS * SSM_STATE
MLA_HEADS = 4
MLA_Q_RANK = 256
MLA_KV_RANK = 128
MLA_NOPE = 64
MLA_ROPE = 32
MLA_V = 64
ROPE_THETA = 10000.0
D_FF = 2816
EPS = 1e-6
NEG = -1e30

LANES = 128
SUBLANES = 8
MLA_HEAD_PAD = LANES
VMEM_LIMIT = 56 * 1024 * 1024

IN_PROJ_TM = 512
SWA_TQ = 512
MLA_TQ = 512
MLA_TK = 512
SSM_TS = 128
FFN_TM = 512
FFN_CHUNK = 256

BF16 = jnp.bfloat16
F32 = jnp.float32


def _rms(x, g):
    return x * lax.rsqrt(jnp.mean(x * x, axis=-1, keepdims=True) + EPS) * g


def _dot(a, b):
    return jnp.dot(a, b, preferred_element_type=F32)


def _dot_nt(a, b):
    return lax.dot_general(a, b, (((1,), (1,)), ((), ())), preferred_element_type=F32)


def _trig_kernel(pos_ref, freq_ref, cos_ref, sin_ref):
    ang = pos_ref[...].astype(F32) * freq_ref[...]
    cos_ref[...] = jnp.cos(ang)
    sin_ref[...] = jnp.sin(ang)


def _trig_tables(positions):
    rows = positions.size
    half = MLA_ROPE // 2
    inv_freq = jnp.power(ROPE_THETA, -jnp.arange(half, dtype=F32) * 2.0 / MLA_ROPE)
    freq = jnp.zeros((1, MLA_HEAD_PAD), F32).at[0, MLA_NOPE:MLA_NOPE + MLA_ROPE].set(jnp.tile(inv_freq, 2))
    tm = 2048
    return pl.pallas_call(
        _trig_kernel,
        grid=(rows // tm,),
        in_specs=[pl.BlockSpec((tm, 1), lambda i: (i, 0)),
                  pl.BlockSpec((1, MLA_HEAD_PAD), lambda i: (0, 0))],
        out_specs=[pl.BlockSpec((tm, MLA_HEAD_PAD), lambda i: (i, 0))] * 2,
        out_shape=[jax.ShapeDtypeStruct((rows, MLA_HEAD_PAD), F32)] * 2,
        name="rope_tables",
    )(positions.reshape(rows, 1), freq)


def _in_proj_kernel(x_ref, g_ref, w_ref, cos_ref, sin_ref, qg_ref, wq_ref, kvg_ref, wkv_ref,
                    qa_ref, ka_ref, va_ref, u_ref, qm_ref, km_ref, vm_ref):
    h = _rms(x_ref[...], g_ref[...]).astype(BF16)
    proj = _dot(h, w_ref[...])
    qa_ref[...] = (proj[:, :SWA_Q] * (HEAD_DIM ** -0.5)).astype(BF16)
    c = SWA_Q
    ka_ref[...] = proj[:, c:c + SWA_KV].astype(BF16)
    c += SWA_KV
    va_ref[...] = proj[:, c:c + SWA_KV].astype(BF16)
    c += SWA_KV
    u_ref[...] = proj[:, c:c + SSM_CH]
    c += SSM_CH
    c_q = proj[:, c:c + MLA_Q_RANK]
    c += MLA_Q_RANK
    c_kv = proj[:, c:c + MLA_KV_RANK]
    c += MLA_KV_RANK
    k_r = proj[:, c:c + LANES]
    k_r_rot = proj[:, c + LANES:c + 2 * LANES]
    cos = cos_ref[...]
    sin = sin_ref[...]
    width = MLA_HEADS * MLA_HEAD_PAD
    q2 = _dot(_rms(c_q, qg_ref[...]).astype(BF16), wq_ref[...])
    kv2 = _dot(_rms(c_kv, kvg_ref[...]).astype(BF16), wkv_ref[...])
    k_pe = k_r * cos + k_r_rot * sin
    low = lax.broadcasted_iota(jnp.int32, (cos.shape[0], LANES), 1) < MLA_V
    for hd in range(MLA_HEADS):
        sl = slice(hd * MLA_HEAD_PAD, (hd + 1) * MLA_HEAD_PAD)
        sl_rot = slice(width + hd * MLA_HEAD_PAD, width + (hd + 1) * MLA_HEAD_PAD)
        qm_ref[:, sl] = (q2[:, sl] * cos + q2[:, sl_rot] * sin).astype(BF16)
        km_ref[:, sl] = (kv2[:, sl] + k_pe).astype(BF16)
        pair = kv2[:, width + (hd // 2) * LANES:width + (hd // 2 + 1) * LANES]
        keep = low if hd % 2 == 0 else jnp.logical_not(low)
        vm_ref[:, sl] = jnp.where(keep, pair, 1.0).astype(BF16)


def _in_proj(x, g, w_ext, cos_t, sin_t, qg, wq_ext, kvg, wkv_ext):
    b_, s_, _ = x.shape
    tm = IN_PROJ_TM
    nt = s_ // tm
    const = lambda b, i: (0, 0)
    row3 = lambda b, i: (b, i, 0)
    trig = pl.BlockSpec((tm, MLA_HEAD_PAD), lambda b, i: (b * nt + i, 0))
    mla_w = MLA_HEADS * MLA_HEAD_PAD
    return pl.pallas_call(
        _in_proj_kernel,
        grid=(b_, nt),
        in_specs=[pl.BlockSpec((None, tm, D_MODEL), row3),
                  pl.BlockSpec((1, D_MODEL), const),
                  pl.BlockSpec(w_ext.shape, const),
                  trig, trig,
                  pl.BlockSpec((1, MLA_Q_RANK), const),
                  pl.BlockSpec(wq_ext.shape, const),
                  pl.BlockSpec((1, MLA_KV_RANK), const),
                  pl.BlockSpec(wkv_ext.shape, const)],
        out_specs=[pl.BlockSpec((None, tm, SWA_Q), row3),
                   pl.BlockSpec((None, tm, SWA_KV), row3),
                   pl.BlockSpec((None, tm, SWA_KV), row3),
                   pl.BlockSpec((tm, SSM_CH), lambda b, i: (i, b)),
                   pl.BlockSpec((None, tm, mla_w), row3),
                   pl.BlockSpec((None, tm, mla_w), row3),
                   pl.BlockSpec((None, tm, mla_w), row3)],
        out_shape=[jax.ShapeDtypeStruct((b_, s_, SWA_Q), BF16),
                   jax.ShapeDtypeStruct((b_, s_, SWA_KV), BF16),
                   jax.ShapeDtypeStruct((b_, s_, SWA_KV), BF16),
                   jax.ShapeDtypeStruct((s_, b_ * SSM_CH), F32),
                   jax.ShapeDtypeStruct((b_, s_, mla_w), BF16),
                   jax.ShapeDtypeStruct((b_, s_, mla_w), BF16),
                   jax.ShapeDtypeStruct((b_, s_, mla_w), BF16)],
        compiler_params=pltpu.CompilerParams(
            dimension_semantics=("arbitrary", "arbitrary"), vmem_limit_bytes=VMEM_LIMIT),
        name="in_proj",
    )(x, g, w_ext, cos_t, sin_t, qg, wq_ext, kvg, wkv_ext)


def _swa_kernel(sink_ref, q_ref, k_ref, v_ref, bias_ref, o_ref):
    qi = pl.program_id(1)
    blocks = SWA_TQ // SWA_BLOCK
    lane = lax.broadcasted_iota(jnp.int32, (2 * SWA_BLOCK, LANES), 1)
    low = lane < HEAD_DIM
    out_low = lax.broadcasted_iota(jnp.int32, (SWA_BLOCK, LANES), 1) < HEAD_DIM
    for j in range(blocks):
        blk = qi * blocks + j
        start = pl.multiple_of(jnp.maximum(blk - 1, 0) * SWA_BLOCK, SWA_BLOCK)
        table = jnp.minimum(blk, 1)
        kb = k_ref[pl.ds(start, 2 * SWA_BLOCK), :]
        vb = v_ref[pl.ds(start, 2 * SWA_BLOCK), :]
        kb_sw = pltpu.roll(kb, HEAD_DIM, 1)
        vb_sw = pltpu.roll(vb, HEAD_DIM, 1)
        zero = jnp.zeros_like(kb)
        k_var = ((jnp.where(low, kb, zero), jnp.where(low, zero, kb_sw)),
                 (jnp.where(low, kb_sw, zero), jnp.where(low, zero, kb)))
        v_var = (jnp.where(low, vb, vb_sw), jnp.where(low, vb_sw, vb))
        rows = slice(j * SWA_BLOCK, (j + 1) * SWA_BLOCK)
        for i in range(SWA_Q // LANES):
            kvh = (2 * i) // (SWA_Q_HEADS // SWA_KV_HEADS)
            q = q_ref[rows, i * LANES:(i + 1) * LANES]
            halves = []
            for half in range(2):
                hq = 2 * i + half
                sink = sink_ref[hq]
                s = _dot_nt(q, k_var[kvh][half]) + bias_ref[table, hq]
                m = jnp.maximum(jnp.max(s, axis=-1, keepdims=True), sink)
                p = jnp.exp(s - m)
                l = jnp.sum(p, axis=-1, keepdims=True) + jnp.exp(sink - m)
                halves.append(_dot(p.astype(BF16), v_var[kvh]) / l)
            o_ref[rows, i * LANES:(i + 1) * LANES] = jnp.where(out_low, halves[0], halves[1]).astype(BF16)


def _swa(q, k, v, sinks, bias2):
    b_, s_, _ = q.shape
    tq = SWA_TQ
    return pl.pallas_call(
        _swa_kernel,
        grid=(b_, s_ // tq),
        in_specs=[pl.BlockSpec(memory_space=pltpu.SMEM),
                  pl.BlockSpec((None, tq, SWA_Q), lambda b, i: (b, i, 0)),
                  pl.BlockSpec((None, s_, SWA_KV), lambda b, i: (b, 0, 0)),
                  pl.BlockSpec((None, s_, SWA_KV), lambda b, i: (b, 0, 0)),
                  pl.BlockSpec(bias2.shape, lambda b, i: (0, 0, 0, 0))],
        out_specs=pl.BlockSpec((None, tq, SWA_Q), lambda b, i: (b, i, 0)),
        out_shape=jax.ShapeDtypeStruct((b_, s_, SWA_Q), BF16),
        compiler_params=pltpu.CompilerParams(
            dimension_semantics=("arbitrary", "arbitrary"), vmem_limit_bytes=VMEM_LIMIT),
        name="swa_attention",
    )(sinks, q, k, v, bias2)


def _t5_bucket(dist):
    n = jnp.maximum(dist, 0)
    max_exact = REL_BUCKETS // 2
    large = max_exact + (jnp.log(jnp.maximum(n, 1).astype(F32) / max_exact)
                         / math.log(REL_MAX_DIST / max_exact)
                         * (REL_BUCKETS - max_exact)).astype(jnp.int32)
    large = jnp.minimum(large, REL_BUCKETS - 1)
    return jnp.where(n < max_exact, n, large)


def _band_bias_tables(rel_bias):
    qi = jnp.arange(SWA_BLOCK)[:, None]
    kj = jnp.arange(2 * SWA_BLOCK)[None, :]
    dist = qi + SWA_BLOCK - kj
    onehot = (_t5_bucket(dist)[None] == jnp.arange(REL_BUCKETS)[:, None, None]).astype(F32)
    b = jnp.einsum('nh,nqk->hqk', rel_bias.astype(F32), onehot, precision=lax.Precision.HIGHEST)
    valid = (dist >= 0) & (dist < SWA_BLOCK)
    normal = jnp.where(valid[None], b, NEG)
    first = jnp.concatenate([normal[:, :, SWA_BLOCK:], jnp.full_like(normal[:, :, SWA_BLOCK:], NEG)], axis=-1)
    return jnp.stack([first, normal])


def _mla_kernel(q_ref, k_ref, v_ref, o_ref, s_scr, m_scr, acc_scr):
    qi = pl.program_id(1)
    tq, tk = MLA_TQ, MLA_TK
    exp2_scale = (MLA_NOPE + MLA_ROPE) ** -0.5 * math.log2(math.e)
    row = lax.broadcasted_iota(jnp.int32, (tq, tk), 0)
    col = lax.broadcasted_iota(jnp.int32, (tq, tk), 1)
    causal = col <= row
    out_low = lax.broadcasted_iota(jnp.int32, (tq, LANES), 1) < MLA_V
    heads = [slice(hd * MLA_HEAD_PAD, (hd + 1) * MLA_HEAD_PAD) for hd in range(MLA_HEADS)]

    lane_tiles = [slice(t * LANES, (t + 1) * LANES) for t in range(tk // LANES)]
    m_scr[...] = jnp.full(m_scr.shape, NEG, F32)
    acc_scr[...] = jnp.zeros(acc_scr.shape, F32)

    def scores(j, carry, masked):
        k0 = pl.multiple_of(j * tk, tk)
        for hd in range(MLA_HEADS):
            s = _dot_nt(q_ref[:, heads[hd]], k_ref[pl.ds(k0, tk), heads[hd]])
            if masked:
                s = jnp.where(causal, s, NEG)
            s_scr[hd, j] = s
            m_scr[hd] = functools.reduce(jnp.maximum, [s[:, lt] for lt in lane_tiles], m_scr[hd])
        return carry

    lax.fori_loop(0, qi, functools.partial(scores, masked=False), 0)
    scores(qi, 0, masked=True)
    for hd in range(MLA_HEADS):
        m_scr[hd] = jnp.broadcast_to(jnp.max(m_scr[hd], axis=-1, keepdims=True), (tq, LANES))

    def weighted(j, carry):
        k0 = pl.multiple_of(j * tk, tk)
        for hd in range(MLA_HEADS):
            m = m_scr[hd]
            p = jnp.concatenate([jnp.exp2((s_scr[hd, j, :, lt] - m) * exp2_scale).astype(BF16)
                                 for lt in lane_tiles], axis=-1)
            acc_scr[hd] += _dot(p, v_ref[pl.ds(k0, tk), heads[hd]])
        return carry

    lax.fori_loop(0, qi + 1, weighted, 0)
    for t in range(MLA_HEADS // 2):
        lo, hi = acc_scr[2 * t], acc_scr[2 * t + 1]
        lo = lo / lo[:, MLA_V:MLA_V + 1]
        hi = hi / hi[:, 0:1]
        o_ref[:, t * LANES:(t + 1) * LANES] = jnp.where(out_low, lo, hi).astype(BF16)


def _mla(q, k, v):
    b_, s_, w = q.shape
    tq = MLA_TQ
    return pl.pallas_call(
        _mla_kernel,
        grid=(b_, s_ // tq),
        in_specs=[pl.BlockSpec((None, tq, w), lambda b, i: (b, i, 0)),
                  pl.BlockSpec((None, s_, w), lambda b, i: (b, 0, 0)),
                  pl.BlockSpec((None, s_, w), lambda b, i: (b, 0, 0))],
        out_specs=pl.BlockSpec((None, tq, MLA_HEADS * MLA_V), lambda b, i: (b, i, 0)),
        out_shape=jax.ShapeDtypeStruct((b_, s_, MLA_HEADS * MLA_V), BF16),
        scratch_shapes=[pltpu.VMEM((MLA_HEADS, s_ // MLA_TK, tq, MLA_TK), F32),
                        pltpu.VMEM((MLA_HEADS, tq, LANES), F32),
                        pltpu.VMEM((MLA_HEADS, tq, LANES), F32)],
        compiler_params=pltpu.CompilerParams(
            dimension_semantics=("arbitrary", "arbitrary"), vmem_limit_bytes=VMEM_LIMIT),
        name="mla_attention",
    )(q, k, v)


def _gelu_tanh(x):
    return 0.5 * x * (1.0 + jnp.tanh(math.sqrt(2.0 / math.pi) * (x + 0.044715 * (x * x * x))))


def _ssm_kernel(u_ref, bd_ref, ar_ref, ai_ref, cd_ref, d_ref, wglu_ref, y_ref, x_scr, sr_scr, si_scr):
    n = SSM_STATES

    @pl.when(pl.program_id(0) == 0)
    def _():
        sr_scr[...] = jnp.zeros_like(sr_scr)
        si_scr[...] = jnp.zeros_like(si_scr)

    u = u_ref[...]
    x_scr[...] = _dot(u.astype(BF16), bd_ref[...])
    ar = ar_ref[...]
    ai = ai_ref[...]
    batch = sr_scr.shape[0]

    def step(t, carry):
        xr, xi = carry
        r0 = pl.multiple_of(t * batch, batch)
        nr = ar * xr - ai * xi + x_scr[pl.ds(r0, batch), :n]
        ni = ar * xi + ai * xr + x_scr[pl.ds(r0, batch), n:]
        x_scr[pl.ds(r0, batch), :n] = nr
        x_scr[pl.ds(r0, batch), n:] = ni
        return nr, ni

    xr, xi = lax.fori_loop(0, SSM_TS, step, (sr_scr[...], si_scr[...]), unroll=4)
    sr_scr[...] = xr
    si_scr[...] = xi
    y = _dot(x_scr[...].astype(BF16), cd_ref[...]) + d_ref[...] * u
    y = _gelu_tanh(y)
    y = y * jax.nn.sigmoid(_dot(y.astype(BF16), wglu_ref[...]))
    y_ref[...] = y.astype(BF16)


def _ssm(u_tb, batch, bd, ar, ai, cd, d, wglu):
    rows = u_tb.shape[0]
    tr = SSM_TS * batch
    const = lambda i: (0, 0)
    return pl.pallas_call(
        _ssm_kernel,
        grid=(rows // tr,),
        in_specs=[pl.BlockSpec((tr, SSM_CH), lambda i: (i, 0)),
                  pl.BlockSpec(bd.shape, const),
                  pl.BlockSpec(ar.shape, const),
                  pl.BlockSpec(ai.shape, const),
                  pl.BlockSpec(cd.shape, const),
                  pl.BlockSpec(d.shape, const),
                  pl.BlockSpec(wglu.shape, const)],
        out_specs=pl.BlockSpec((tr, SSM_CH), lambda i: (i, 0)),
        out_shape=jax.ShapeDtypeStruct((rows, SSM_CH), BF16),
        scratch_shapes=[pltpu.VMEM((tr, 2 * SSM_STATES), F32),
                        pltpu.VMEM((batch, SSM_STATES), F32),
                        pltpu.VMEM((batch, SSM_STATES), F32)],
        compiler_params=pltpu.CompilerParams(
            dimension_semantics=("arbitrary",), vmem_limit_bytes=VMEM_LIMIT),
        name="s5_mixer",
    )(u_tb, bd, ar, ai, cd, d, wglu)


def _ssm_params(a_re, a_im, log_dt, b_re, b_im, c_re, c_im, batch):
    lam = lax.complex(a_re.astype(F32), a_im.astype(F32))
    dt = jnp.exp(log_dt.astype(F32))
    a_bar = jnp.exp(lam * dt[:, None])
    b_bar = ((a_bar - 1.0) / lam)[..., None] * lax.complex(b_re.astype(F32), b_im.astype(F32))
    eye = jnp.eye(SSM_GROUPS, dtype=F32)

    def in_op(m):
        return jnp.einsum('gpc,gh->gchp', m, eye).reshape(SSM_CH, SSM_STATES)

    def out_op(m):
        return jnp.einsum('gcp,gh->gphc', m, eye).reshape(SSM_STATES, SSM_CH)

    bd = jnp.concatenate([in_op(jnp.real(b_bar)), in_op(jnp.imag(b_bar))], axis=1).astype(BF16)
    cd = jnp.concatenate([out_op(c_re.astype(F32)), -out_op(c_im.astype(F32))], axis=0).astype(BF16)
    ar = jnp.broadcast_to(jnp.real(a_bar).reshape(1, SSM_STATES), (batch, SSM_STATES))
    ai = jnp.broadcast_to(jnp.imag(a_bar).reshape(1, SSM_STATES), (batch, SSM_STATES))
    return bd, ar, ai, cd


def _ffn_kernel(x_ref, oa_ref, ob_ref, oc_ref, wo_ref, g2_ref, wg_ref, wu_ref, wd_ref, gf_ref, o_ref, h_scr,
                *, final_norm):
    wo_a = wo_ref[:SWA_Q, :]
    wo_b = wo_ref[SWA_Q:SWA_Q + SSM_CH, :]
    wo_c = wo_ref[SWA_Q + SSM_CH:, :]
    x = x_ref[...] + _dot(oa_ref[...], wo_a) + _dot(ob_ref[...], wo_b) + _dot(oc_ref[...], wo_c)
    o_ref[...] = x
    h_scr[...] = _rms(x, g2_ref[...]).astype(BF16)

    def chunk(c, carry):
        h = h_scr[...]
        gate = _dot(h, wg_ref[c])
        up = _dot(h, wu_ref[c])
        act = (gate * jax.nn.sigmoid(gate) * up).astype(BF16)
        o_ref[...] += _dot(act, wd_ref[c])
        return carry

    lax.fori_loop(0, D_FF // FFN_CHUNK, chunk, 0)
    if final_norm:
        o_ref[...] = _rms(o_ref[...], gf_ref[...])


def _ffn_chunked_cols(w):
    return w.reshape(D_MODEL, D_FF // FFN_CHUNK, FFN_CHUNK).transpose(1, 0, 2).astype(BF16)


def _ffn(x, oa, ob_tb, oc, wo, g2, wg, wu, wd, gf, final_norm):
    b_, s_, _ = x.shape
    tm = FFN_TM
    const = lambda b, i: (0, 0)
    const3 = lambda b, i: (0, 0, 0)
    row3 = lambda b, i: (b, i, 0)
    return pl.pallas_call(
        functools.partial(_ffn_kernel, final_norm=final_norm),
        grid=(b_, s_ // tm),
        in_specs=[pl.BlockSpec((None, tm, D_MODEL), row3),
                  pl.BlockSpec((None, tm, SWA_Q), row3),
                  pl.BlockSpec((tm, SSM_CH), lambda b, i: (i, b)),
                  pl.BlockSpec((None, tm, MLA_HEADS * MLA_V), row3),
                  pl.BlockSpec(wo.shape, const),
                  pl.BlockSpec((1, D_MODEL), const),
                  pl.BlockSpec(wg.shape, const3),
                  pl.BlockSpec(wu.shape, const3),
                  pl.BlockSpec(wd.shape, const3),
                  pl.BlockSpec((1, D_MODEL), const)],
        out_specs=pl.BlockSpec((None, tm, D_MODEL), row3),
        out_shape=jax.ShapeDtypeStruct(x.shape, F32),
        scratch_shapes=[pltpu.VMEM((tm, D_MODEL), BF16)],
        compiler_params=pltpu.CompilerParams(
            dimension_semantics=("arbitrary", "arbitrary"), vmem_limit_bytes=VMEM_LIMIT),
        name="out_proj_ffn",
    )(x, oa, ob_tb, oc, wo, g2, wg, wu, wd, gf)


def _rot_half_cols(w):
    half = w.shape[-1] // 2
    return jnp.concatenate([-w[..., half:], w[..., :half]], axis=-1)


def _rope_tile(w):
    z = jnp.zeros(w.shape[:-1] + (MLA_NOPE,), w.dtype)
    z2 = jnp.zeros(w.shape[:-1] + (MLA_HEAD_PAD - MLA_NOPE - MLA_ROPE,), w.dtype)
    return jnp.concatenate([z, w, z2], axis=-1)


def _layer_weights(w_in, w_q_up, w_kv_up):
    kr = w_in[:, -MLA_ROPE:]
    w_ext = jnp.concatenate([w_in[:, :-MLA_ROPE], _rope_tile(kr), _rope_tile(_rot_half_cols(kr))], axis=1)
    qh = w_q_up.reshape(MLA_Q_RANK, MLA_HEADS, MLA_NOPE + MLA_ROPE)
    pad = jnp.zeros((MLA_Q_RANK, MLA_HEADS, MLA_HEAD_PAD - MLA_NOPE - MLA_ROPE), w_q_up.dtype)
    q_tiles = jnp.concatenate([qh, pad], axis=-1).reshape(MLA_Q_RANK, -1)
    q_rot = _rope_tile(_rot_half_cols(qh[..., MLA_NOPE:])).reshape(MLA_Q_RANK, -1)
    wq_ext = jnp.concatenate([q_tiles, q_rot], axis=1)
    kvh = w_kv_up.reshape(MLA_KV_RANK, MLA_HEADS, MLA_NOPE + MLA_V)
    kpad = jnp.zeros((MLA_KV_RANK, MLA_HEADS, MLA_HEAD_PAD - MLA_NOPE), w_kv_up.dtype)
    k_tiles = jnp.concatenate([kvh[..., :MLA_NOPE], kpad], axis=-1).reshape(MLA_KV_RANK, -1)
    v_cols = kvh[..., MLA_NOPE:].reshape(MLA_KV_RANK, -1)
    wkv_ext = jnp.concatenate([k_tiles, v_cols], axis=1)
    return w_ext.astype(BF16), wq_ext.astype(BF16), wkv_ext.astype(BF16)


def kernel(x, positions, rel_bias, ln1_g, w_in, sinks, ssm_a_re, ssm_a_im, ssm_log_dt, ssm_b_re, ssm_b_im,
           ssm_c_re, ssm_c_im, ssm_d, ssm_w_glu, mla_q_norm_g, mla_w_q_up, mla_kv_norm_g, mla_w_kv_up, w_out,
           ln2_g, w_gate, w_up, w_down, final_g):
    b_, s_, _ = x.shape
    cos_t, sin_t = _trig_tables(positions)
    bias2 = _band_bias_tables(rel_bias)
    gf = final_g.reshape(1, D_MODEL)
    for l in range(DEPTH):
        w_ext, wq_ext, wkv_ext = _layer_weights(w_in[l], mla_w_q_up[l], mla_w_kv_up[l])
        qa, ka, va, u_tb, qm, km, vm = _in_proj(
            x, ln1_g[l].reshape(1, D_MODEL), w_ext, cos_t, sin_t,
            mla_q_norm_g[l].reshape(1, MLA_Q_RANK), wq_ext,
            mla_kv_norm_g[l].reshape(1, MLA_KV_RANK), wkv_ext)
        o_a = _swa(qa, ka, va, sinks[l], bias2)
        bd, ar, ai, cd = _ssm_params(ssm_a_re[l], ssm_a_im[l], ssm_log_dt[l], ssm_b_re[l], ssm_b_im[l],
                                     ssm_c_re[l], ssm_c_im[l], b_)
        o_b = _ssm(u_tb.reshape(s_ * b_, SSM_CH), b_, bd, ar, ai, cd,
                   ssm_d[l].reshape(1, SSM_CH), ssm_w_glu[l].astype(BF16))
        o_c = _mla(qm, km, vm)
        x = _ffn(x, o_a, o_b.reshape(s_, b_ * SSM_CH), o_c, w_out[l].astype(BF16),
                 ln2_g[l].reshape(1, D_MODEL), _ffn_chunked_cols(w_gate[l]), _ffn_chunked_cols(w_up[l]),
                 w_down[l].astype(BF16).reshape(D_FF // FFN_CHUNK, FFN_CHUNK, D_MODEL), gf,
                 final_norm=(l == DEPTH - 1))
    return x
```

```python
import functools
import math

import jax
import jax.numpy as jnp
from jax import lax
from jax.experimental import pallas as pl
from jax.experimental.pallas import tpu as pltpu

D_MODEL = 1024
DEPTH = 2
HEAD_DIM = 64
SWA_Q_HEADS = 8
SWA_KV_HEADS = 2
SWA_BLOCK = 128
SWA_Q = SWA_Q_HEADS * HEAD_DIM
SWA_KV = SWA_KV_HEADS * HEAD_DIM
REL_BUCKETS = 32
REL_MAX_DIST = 128
SSM_CH = 256
SSM_GROUP = 16
SSM_GROUPS = SSM_CH // SSM_GROUP
SSM_STATE = 64
SSM_STATES = SSM_GROUPS * SSM_STATE
MLA_HEADS = 4
MLA_Q_RANK = 256
MLA_KV_RANK = 128
MLA_NOPE = 64
MLA_ROPE = 32
MLA_V = 64
ROPE_THETA = 10000.0
D_FF = 2816
EPS = 1e-6
NEG = -1e30

LANES = 128
SUBLANES = 8
MLA_HEAD_PAD = LANES
VMEM_LIMIT = 56 * 1024 * 1024

IN_PROJ_TM = 512
SWA_TQ = 512
MLA_TQ = 512
MLA_TK = 512
SSM_TS = 128
FFN_TM = 512
FFN_CHUNK = 1408

BF16 = jnp.bfloat16
F32 = jnp.float32


def _rms(x, g):
    return x * lax.rsqrt(jnp.mean(x * x, axis=-1, keepdims=True) + EPS) * g


def _dot(a, b):
    return jnp.dot(a, b, preferred_element_type=F32)


def _dot_nt(a, b):
    return lax.dot_general(a, b, (((1,), (1,)), ((), ())), preferred_element_type=F32)


def _layer_spec(arr, l):
    return pl.BlockSpec((None,) + arr.shape[1:], lambda *_: (l,) + (0,) * (arr.ndim - 1))


def _trig_kernel(pos_ref, freq_ref, cos_ref, sin_ref):
    ang = pos_ref[...].astype(F32) * freq_ref[...]
    cos_ref[...] = jnp.cos(ang)
    sin_ref[...] = jnp.sin(ang)


def _trig_tables(positions):
    rows = positions.size
    half = MLA_ROPE // 2
    inv_freq = jnp.power(ROPE_THETA, -jnp.arange(half, dtype=F32) * 2.0 / MLA_ROPE)
    freq = jnp.zeros((1, MLA_HEAD_PAD), F32).at[0, MLA_NOPE:MLA_NOPE + MLA_ROPE].set(jnp.tile(inv_freq, 2))
    tm = 2048
    return pl.pallas_call(
        _trig_kernel,
        grid=(rows // tm,),
        in_specs=[pl.BlockSpec((tm, 1), lambda i: (i, 0)),
                  pl.BlockSpec((1, MLA_HEAD_PAD), lambda i: (0, 0))],
        out_specs=[pl.BlockSpec((tm, MLA_HEAD_PAD), lambda i: (i, 0))] * 2,
        out_shape=[jax.ShapeDtypeStruct((rows, MLA_HEAD_PAD), F32)] * 2,
        name="rope_tables",
    )(positions.reshape(rows, 1), freq)


def _in_proj_kernel(x_ref, g_ref, w_ref, cos_ref, sin_ref, qg_ref, wq_ref, kvg_ref, wkv_ref,
                    qa_ref, ka_ref, va_ref, u_ref, qm_ref, km_ref, vm_ref):
    h = _rms(x_ref[...], g_ref[...]).astype(BF16)
    proj = _dot(h, w_ref[...])
    qa_ref[...] = (proj[:, :SWA_Q] * (HEAD_DIM ** -0.5)).astype(BF16)
    c = SWA_Q
    ka_ref[...] = proj[:, c:c + SWA_KV].astype(BF16)
    c += SWA_KV
    va_ref[...] = proj[:, c:c + SWA_KV].astype(BF16)
    c += SWA_KV
    u_ref[...] = proj[:, c:c + SSM_CH]
    c += SSM_CH
    c_q = proj[:, c:c + MLA_Q_RANK]
    c += MLA_Q_RANK
    c_kv = proj[:, c:c + MLA_KV_RANK]
    c += MLA_KV_RANK
    k_r = proj[:, c:c + LANES]
    k_r_rot = proj[:, c + LANES:c + 2 * LANES]
    cos = cos_ref[...]
    sin = sin_ref[...]
    width = MLA_HEADS * MLA_HEAD_PAD
    q2 = _dot(_rms(c_q, qg_ref[...]).astype(BF16), wq_ref[...])
    kv2 = _dot(_rms(c_kv, kvg_ref[...]).astype(BF16), wkv_ref[...])
    k_pe = k_r * cos + k_r_rot * sin
    low = lax.broadcasted_iota(jnp.int32, (cos.shape[0], LANES), 1) < MLA_V
    for hd in range(MLA_HEADS):
        sl = slice(hd * MLA_HEAD_PAD, (hd + 1) * MLA_HEAD_PAD)
        sl_rot = slice(width + hd * MLA_HEAD_PAD, width + (hd + 1) * MLA_HEAD_PAD)
        qm_ref[:, sl] = (q2[:, sl] * cos + q2[:, sl_rot] * sin).astype(BF16)
        km_ref[:, sl] = (kv2[:, sl] + k_pe).astype(BF16)
        pair = kv2[:, width + (hd // 2) * LANES:width + (hd // 2 + 1) * LANES]
        keep = low if hd % 2 == 0 else jnp.logical_not(low)
        vm_ref[:, sl] = jnp.where(keep, pair, 1.0).astype(BF16)


def _in_proj(x, l, g, w_ext, cos_t, sin_t, qg, wq_ext, kvg, wkv_ext):
    b_, s_, _ = x.shape
    tm = IN_PROJ_TM
    nt = s_ // tm
    row3 = lambda b, i: (b, i, 0)
    trig = pl.BlockSpec((tm, MLA_HEAD_PAD), lambda b, i: (b * nt + i, 0))
    mla_w = MLA_HEADS * MLA_HEAD_PAD
    return pl.pallas_call(
        _in_proj_kernel,
        grid=(b_, nt),
        in_specs=[pl.BlockSpec((None, tm, D_MODEL), row3),
                  _layer_spec(g, l), _layer_spec(w_ext, l),
                  trig, trig,
                  _layer_spec(qg, l), _layer_spec(wq_ext, l), _layer_spec(kvg, l), _layer_spec(wkv_ext, l)],
        out_specs=[pl.BlockSpec((None, tm, SWA_Q), row3),
                   pl.BlockSpec((None, tm, SWA_KV), row3),
                   pl.BlockSpec((None, tm, SWA_KV), row3),
                   pl.BlockSpec((tm, SSM_CH), lambda b, i: (i, b)),
                   pl.BlockSpec((None, tm, mla_w), row3),
                   pl.BlockSpec((None, tm, mla_w), row3),
                   pl.BlockSpec((None, tm, mla_w), row3)],
        out_shape=[jax.ShapeDtypeStruct((b_, s_, SWA_Q), BF16),
                   jax.ShapeDtypeStruct((b_, s_, SWA_KV), BF16),
                   jax.ShapeDtypeStruct((b_, s_, SWA_KV), BF16),
                   jax.ShapeDtypeStruct((s_, b_ * SSM_CH), F32),
                   jax.ShapeDtypeStruct((b_, s_, mla_w), BF16),
                   jax.ShapeDtypeStruct((b_, s_, mla_w), BF16),
                   jax.ShapeDtypeStruct((b_, s_, mla_w), BF16)],
        compiler_params=pltpu.CompilerParams(
            dimension_semantics=("arbitrary", "arbitrary"), vmem_limit_bytes=VMEM_LIMIT),
        name="in_proj",
    )(x, g, w_ext, cos_t, sin_t, qg, wq_ext, kvg, wkv_ext)


def _swa_kernel(sink_ref, q_ref, k_ref, v_ref, bias_ref, o_ref, *, layer):
    qi = pl.program_id(1)
    blocks = SWA_TQ // SWA_BLOCK
    lane = lax.broadcasted_iota(jnp.int32, (2 * SWA_BLOCK, LANES), 1)
    low = lane < HEAD_DIM
    out_low = lax.broadcasted_iota(jnp.int32, (SWA_BLOCK, LANES), 1) < HEAD_DIM
    for j in range(blocks):
        blk = qi * blocks + j
        start = pl.multiple_of(jnp.maximum(blk - 1, 0) * SWA_BLOCK, SWA_BLOCK)
        table = jnp.minimum(blk, 1)
        kb = k_ref[pl.ds(start, 2 * SWA_BLOCK), :]
        vb = v_ref[pl.ds(start, 2 * SWA_BLOCK), :]
        kb_sw = pltpu.roll(kb, HEAD_DIM, 1)
        vb_sw = pltpu.roll(vb, HEAD_DIM, 1)
        zero = jnp.zeros_like(kb)
        k_var = ((jnp.where(low, kb, zero), jnp.where(low, zero, kb_sw)),
                 (jnp.where(low, kb_sw, zero), jnp.where(low, zero, kb)))
        v_var = (jnp.where(low, vb, vb_sw), jnp.where(low, vb_sw, vb))
        rows = slice(j * SWA_BLOCK, (j + 1) * SWA_BLOCK)
        for i in range(SWA_Q // LANES):
            kvh = (2 * i) // (SWA_Q_HEADS // SWA_KV_HEADS)
            q = q_ref[rows, i * LANES:(i + 1) * LANES]
            halves = []
            for half in range(2):
                hq = 2 * i + half
                sink = sink_ref[layer, hq]
                s = _dot_nt(q, k_var[kvh][half]) + bias_ref[table, hq]
                m = jnp.maximum(jnp.max(s, axis=-1, keepdims=True), sink)
                p = jnp.exp(s - m)
                l = jnp.sum(p, axis=-1, keepdims=True) + jnp.exp(sink - m)
                halves.append(_dot(p.astype(BF16), v_var[kvh]) / l)
            o_ref[rows, i * LANES:(i + 1) * LANES] = jnp.where(out_low, halves[0], halves[1]).astype(BF16)


def _swa(q, k, v, l, sinks, bias2):
    b_, s_, _ = q.shape
    tq = SWA_TQ
    return pl.pallas_call(
        functools.partial(_swa_kernel, layer=l),
        grid=(b_, s_ // tq),
        in_specs=[pl.BlockSpec(memory_space=pltpu.SMEM),
                  pl.BlockSpec((None, tq, SWA_Q), lambda b, i: (b, i, 0)),
                  pl.BlockSpec((None, s_, SWA_KV), lambda b, i: (b, 0, 0)),
                  pl.BlockSpec((None, s_, SWA_KV), lambda b, i: (b, 0, 0)),
                  pl.BlockSpec(bias2.shape, lambda b, i: (0, 0, 0, 0))],
        out_specs=pl.BlockSpec((None, tq, SWA_Q), lambda b, i: (b, i, 0)),
        out_shape=jax.ShapeDtypeStruct((b_, s_, SWA_Q), BF16),
        compiler_params=pltpu.CompilerParams(
            dimension_semantics=("arbitrary", "arbitrary"), vmem_limit_bytes=VMEM_LIMIT),
        name="swa_attention",
    )(sinks, q, k, v, bias2)


def _t5_bucket(dist):
    n = jnp.maximum(dist, 0)
    max_exact = REL_BUCKETS // 2
    large = max_exact + (jnp.log(jnp.maximum(n, 1).astype(F32) / max_exact)
                         / math.log(REL_MAX_DIST / max_exact)
                         * (REL_BUCKETS - max_exact)).astype(jnp.int32)
    large = jnp.minimum(large, REL_BUCKETS - 1)
    return jnp.where(n < max_exact, n, large)


def _band_bias_tables(rel_bias):
    qi = jnp.arange(SWA_BLOCK)[:, None]
    kj = jnp.arange(2 * SWA_BLOCK)[None, :]
    dist = qi + SWA_BLOCK - kj
    onehot = (_t5_bucket(dist)[None] == jnp.arange(REL_BUCKETS)[:, None, None]).astype(F32)
    b = jnp.einsum('nh,nqk->hqk', rel_bias.astype(F32), onehot, precision=lax.Precision.HIGHEST)
    valid = (dist >= 0) & (dist < SWA_BLOCK)
    normal = jnp.where(valid[None], b, NEG)
    first = jnp.concatenate([normal[:, :, SWA_BLOCK:], jnp.full_like(normal[:, :, SWA_BLOCK:], NEG)], axis=-1)
    return jnp.stack([first, normal])


def _mla_kernel(q_ref, k_ref, v_ref, o_ref, s_scr, m_scr, acc_scr):
    qi = pl.program_id(1)
    tq, tk = MLA_TQ, MLA_TK
    exp2_scale = (MLA_NOPE + MLA_ROPE) ** -0.5 * math.log2(math.e)
    row = lax.broadcasted_iota(jnp.int32, (tq, tk), 0)
    col = lax.broadcasted_iota(jnp.int32, (tq, tk), 1)
    causal = col <= row
    out_low = lax.broadcasted_iota(jnp.int32, (tq, LANES), 1) < MLA_V
    heads = [slice(hd * MLA_HEAD_PAD, (hd + 1) * MLA_HEAD_PAD) for hd in range(MLA_HEADS)]

    lane_tiles = [slice(t * LANES, (t + 1) * LANES) for t in range(tk // LANES)]
    m_scr[...] = jnp.full(m_scr.shape, NEG, F32)
    acc_scr[...] = jnp.zeros(acc_scr.shape, F32)

    def scores(j, carry, masked):
        k0 = pl.multiple_of(j * tk, tk)
        for hd in range(MLA_HEADS):
            s = _dot_nt(q_ref[:, heads[hd]], k_ref[pl.ds(k0, tk), heads[hd]])
            if masked:
                s = jnp.where(causal, s, NEG)
            s_scr[hd, j] = s
            m_scr[hd] = functools.reduce(jnp.maximum, [s[:, lt] for lt in lane_tiles], m_scr[hd])
        return carry

    lax.fori_loop(0, qi, functools.partial(scores, masked=False), 0)
    scores(qi, 0, masked=True)
    for hd in range(MLA_HEADS):
        m_scr[hd] = jnp.broadcast_to(jnp.max(m_scr[hd], axis=-1, keepdims=True), (tq, LANES))

    def weighted(j, carry):
        k0 = pl.multiple_of(j * tk, tk)
        for hd in range(MLA_HEADS):
            m = m_scr[hd]
            p = jnp.concatenate([jnp.exp2((s_scr[hd, j, :, lt] - m) * exp2_scale).astype(BF16)
                                 for lt in lane_tiles], axis=-1)
            acc_scr[hd] += _dot(p, v_ref[pl.ds(k0, tk), heads[hd]])
        return carry

    lax.fori_loop(0, qi + 1, weighted, 0)
    for t in range(MLA_HEADS // 2):
        lo, hi = acc_scr[2 * t], acc_scr[2 * t + 1]
        lo = lo / lo[:, MLA_V:MLA_V + 1]
        hi = hi / hi[:, 0:1]
        o_ref[:, t * LANES:(t + 1) * LANES] = jnp.where(out_low, lo, hi).astype(BF16)


def _mla(q, k, v):
    b_, s_, w = q.shape
    tq = MLA_TQ
    return pl.pallas_call(
        _mla_kernel,
        grid=(b_, s_ // tq),
        in_specs=[pl.BlockSpec((None, tq, w), lambda b, i: (b, i, 0)),
                  pl.BlockSpec((None, s_, w), lambda b, i: (b, 0, 0)),
                  pl.BlockSpec((None, s_, w), lambda b, i: (b, 0, 0))],
        out_specs=pl.BlockSpec((None, tq, MLA_HEADS * MLA_V), lambda b, i: (b, i, 0)),
        out_shape=jax.ShapeDtypeStruct((b_, s_, MLA_HEADS * MLA_V), BF16),
        scratch_shapes=[pltpu.VMEM((MLA_HEADS, s_ // MLA_TK, tq, MLA_TK), F32),
                        pltpu.VMEM((MLA_HEADS, tq, LANES), F32),
                        pltpu.VMEM((MLA_HEADS, tq, LANES), F32)],
        compiler_params=pltpu.CompilerParams(
            dimension_semantics=("arbitrary", "arbitrary"), vmem_limit_bytes=VMEM_LIMIT),
        name="mla_attention",
    )(q, k, v)


def _gelu_tanh(x):
    return 0.5 * x * (1.0 + jnp.tanh(math.sqrt(2.0 / math.pi) * (x + 0.044715 * (x * x * x))))


def _ssm_kernel(u_ref, bd_ref, ar_ref, ai_ref, cd_ref, d_ref, wglu_ref, y_ref,
                u_scr, x_scr, y_scr, sr_scr, si_scr):
    n = SSM_STATES
    batch = sr_scr.shape[0]
    lane_tiles = SSM_CH // LANES

    @pl.when(pl.program_id(0) == 0)
    def _():
        sr_scr[...] = jnp.zeros_like(sr_scr)
        si_scr[...] = jnp.zeros_like(si_scr)

    for b in range(batch):
        for lt in range(lane_tiles):
            c0 = b * SSM_CH + lt * LANES
            u_scr[lt, pl.ds(b, SSM_TS, stride=batch), :] = u_ref[:, c0:c0 + LANES]
    u = jnp.concatenate([u_scr[lt] for lt in range(lane_tiles)], axis=-1)
    x_scr[...] = _dot(u.astype(BF16), bd_ref[...])
    ar = ar_ref[...]
    ai = ai_ref[...]

    def step(t, carry):
        xr, xi = carry
        r0 = pl.multiple_of(t * batch, batch)
        nr = ar * xr - ai * xi + x_scr[pl.ds(r0, batch), :n]
        ni = ar * xi + ai * xr + x_scr[pl.ds(r0, batch), n:]
        x_scr[pl.ds(r0, batch), :n] = nr
        x_scr[pl.ds(r0, batch), n:] = ni
        return nr, ni

    xr, xi = lax.fori_loop(0, SSM_TS, step, (sr_scr[...], si_scr[...]), unroll=4)
    sr_scr[...] = xr
    si_scr[...] = xi
    y = _dot(x_scr[...].astype(BF16), cd_ref[...]) + d_ref[...] * u
    y = _gelu_tanh(y)
    y = y * jax.nn.sigmoid(_dot(y.astype(BF16), wglu_ref[...]))
    for lt in range(lane_tiles):
        y_scr[lt] = y[:, lt * LANES:(lt + 1) * LANES]
    for b in range(batch):
        for lt in range(lane_tiles):
            c0 = b * SSM_CH + lt * LANES
            y_ref[:, c0:c0 + LANES] = y_scr[lt, pl.ds(b, SSM_TS, stride=batch), :].astype(BF16)


def _ssm(u, batch, l, bd, ar, ai, cd, d, wglu):
    s_, width = u.shape
    tr = SSM_TS * batch
    return pl.pallas_call(
        _ssm_kernel,
        grid=(s_ // SSM_TS,),
        in_specs=[pl.BlockSpec((SSM_TS, width), lambda i: (i, 0))]
        + [_layer_spec(a, l) for a in (bd, ar, ai, cd, d, wglu)],
        out_specs=pl.BlockSpec((SSM_TS, width), lambda i: (i, 0)),
        out_shape=jax.ShapeDtypeStruct((s_, width), BF16),
        scratch_shapes=[pltpu.VMEM((SSM_CH // LANES, tr, LANES), F32),
                        pltpu.VMEM((tr, 2 * SSM_STATES), F32),
                        pltpu.VMEM((SSM_CH // LANES, tr, LANES), F32),
                        pltpu.VMEM((batch, SSM_STATES), F32),
                        pltpu.VMEM((batch, SSM_STATES), F32)],
        compiler_params=pltpu.CompilerParams(
            dimension_semantics=("arbitrary",), vmem_limit_bytes=VMEM_LIMIT),
        name="s5_mixer",
    )(u, bd, ar, ai, cd, d, wglu)


def _ssm_params(a_re, a_im, log_dt, b_re, b_im, c_re, c_im, batch):
    lam = lax.complex(a_re.astype(F32), a_im.astype(F32))
    dt = jnp.exp(log_dt.astype(F32))
    a_bar = jnp.exp(lam * dt[:, None])
    b_bar = ((a_bar - 1.0) / lam)[..., None] * lax.complex(b_re.astype(F32), b_im.astype(F32))
    eye = jnp.eye(SSM_GROUPS, dtype=F32)

    def in_op(m):
        return jnp.einsum('gpc,gh->gchp', m, eye).reshape(SSM_CH, SSM_STATES)

    def out_op(m):
        return jnp.einsum('gcp,gh->gphc', m, eye).reshape(SSM_STATES, SSM_CH)

    bd = jnp.concatenate([in_op(jnp.real(b_bar)), in_op(jnp.imag(b_bar))], axis=1).astype(BF16)
    cd = jnp.concatenate([out_op(c_re.astype(F32)), -out_op(c_im.astype(F32))], axis=0).astype(BF16)
    ar = jnp.broadcast_to(jnp.real(a_bar).reshape(1, SSM_STATES), (batch, SSM_STATES))
    ai = jnp.broadcast_to(jnp.imag(a_bar).reshape(1, SSM_STATES), (batch, SSM_STATES))
    return bd, ar, ai, cd


def _ffn_kernel(x_ref, oa_ref, ob_ref, oc_ref, wo_ref, g2_ref, wg_ref, wu_ref, wd_ref, gf_ref, o_ref, h_scr,
                *, final_norm):
    wo_a = wo_ref[:SWA_Q, :]
    wo_b = wo_ref[SWA_Q:SWA_Q + SSM_CH, :]
    wo_c = wo_ref[SWA_Q + SSM_CH:, :]
    x = x_ref[...] + _dot(oa_ref[...], wo_a) + _dot(ob_ref[...], wo_b) + _dot(oc_ref[...], wo_c)
    o_ref[...] = x
    h_scr[...] = _rms(x, g2_ref[...]).astype(BF16)

    def chunk(c, carry):
        c0 = pl.multiple_of(c * FFN_CHUNK, LANES)
        h = h_scr[...]
        gate = _dot(h, wg_ref[:, pl.ds(c0, FFN_CHUNK)])
        up = _dot(h, wu_ref[:, pl.ds(c0, FFN_CHUNK)])
        act = (gate * jax.nn.sigmoid(gate) * up).astype(BF16)
        o_ref[...] += _dot(act, wd_ref[pl.ds(c0, FFN_CHUNK), :])
        return carry

    lax.fori_loop(0, D_FF // FFN_CHUNK, chunk, 0)
    if final_norm:
        o_ref[...] = _rms(o_ref[...], gf_ref[...])


def _ffn(x, oa, ob, oc, l, wo, g2, wg, wu, wd, gf, final_norm):
    b_, s_, _ = x.shape
    tm = FFN_TM
    row3 = lambda b, i: (b, i, 0)
    return pl.pallas_call(
        functools.partial(_ffn_kernel, final_norm=final_norm),
        grid=(b_, s_ // tm),
        in_specs=[pl.BlockSpec((None, tm, D_MODEL), row3),
                  pl.BlockSpec((None, tm, SWA_Q), row3),
                  pl.BlockSpec((tm, SSM_CH), lambda b, i: (i, b)),
                  pl.BlockSpec((None, tm, MLA_HEADS * MLA_V), row3),
                  _layer_spec(wo, l), _layer_spec(g2, l), _layer_spec(wg, l), _layer_spec(wu, l),
                  _layer_spec(wd, l),
                  pl.BlockSpec((1, D_MODEL), lambda b, i: (0, 0))],
        out_specs=pl.BlockSpec((None, tm, D_MODEL), row3),
        out_shape=jax.ShapeDtypeStruct(x.shape, F32),
        scratch_shapes=[pltpu.VMEM((tm, D_MODEL), BF16)],
        compiler_params=pltpu.CompilerParams(
            dimension_semantics=("arbitrary", "arbitrary"), vmem_limit_bytes=VMEM_LIMIT),
        name="out_proj_ffn",
    )(x, oa, ob, oc, wo, g2, wg, wu, wd, gf)


def _rot_half_cols(w):
    half = w.shape[-1] // 2
    return jnp.concatenate([-w[..., half:], w[..., :half]], axis=-1)


def _rope_tile(w):
    z = jnp.zeros(w.shape[:-1] + (MLA_NOPE,), w.dtype)
    z2 = jnp.zeros(w.shape[:-1] + (MLA_HEAD_PAD - MLA_NOPE - MLA_ROPE,), w.dtype)
    return jnp.concatenate([z, w, z2], axis=-1)


def _stacked_weights(w_in, w_q_up, w_kv_up):
    lead = w_in.shape[:-2]
    kr = w_in[..., -MLA_ROPE:]
    w_ext = jnp.concatenate([w_in[..., :-MLA_ROPE], _rope_tile(kr), _rope_tile(_rot_half_cols(kr))], axis=-1)
    qh = w_q_up.reshape(lead + (MLA_Q_RANK, MLA_HEADS, MLA_NOPE + MLA_ROPE))
    pad = jnp.zeros(lead + (MLA_Q_RANK, MLA_HEADS, MLA_HEAD_PAD - MLA_NOPE - MLA_ROPE), w_q_up.dtype)
    q_tiles = jnp.concatenate([qh, pad], axis=-1).reshape(lead + (MLA_Q_RANK, -1))
    q_rot = _rope_tile(_rot_half_cols(qh[..., MLA_NOPE:])).reshape(lead + (MLA_Q_RANK, -1))
    wq_ext = jnp.concatenate([q_tiles, q_rot], axis=-1)
    kvh = w_kv_up.reshape(lead + (MLA_KV_RANK, MLA_HEADS, MLA_NOPE + MLA_V))
    kpad = jnp.zeros(lead + (MLA_KV_RANK, MLA_HEADS, MLA_HEAD_PAD - MLA_NOPE), w_kv_up.dtype)
    k_tiles = jnp.concatenate([kvh[..., :MLA_NOPE], kpad], axis=-1).reshape(lead + (MLA_KV_RANK, -1))
    v_cols = kvh[..., MLA_NOPE:].reshape(lead + (MLA_KV_RANK, -1))
    wkv_ext = jnp.concatenate([k_tiles, v_cols], axis=-1)
    return w_ext.astype(BF16), wq_ext.astype(BF16), wkv_ext.astype(BF16)


def kernel(x, positions, rel_bias, ln1_g, w_in, sinks, ssm_a_re, ssm_a_im, ssm_log_dt, ssm_b_re, ssm_b_im,
           ssm_c_re, ssm_c_im, ssm_d, ssm_w_glu, mla_q_norm_g, mla_w_q_up, mla_kv_norm_g, mla_w_kv_up, w_out,
           ln2_g, w_gate, w_up, w_down, final_g):
    b_ = x.shape[0]
    cos_t, sin_t = _trig_tables(positions)
    bias2 = _band_bias_tables(rel_bias)
    row = lambda g: g[:, None, :]
    w_ext, wq_ext, wkv_ext = _stacked_weights(w_in, mla_w_q_up, mla_w_kv_up)
    bd, ar, ai, cd = jax.vmap(functools.partial(_ssm_params, batch=b_))(
        ssm_a_re, ssm_a_im, ssm_log_dt, ssm_b_re, ssm_b_im, ssm_c_re, ssm_c_im)
    wglu = ssm_w_glu.astype(BF16)
    wo, wg, wu, wd = (w.astype(BF16) for w in (w_out, w_gate, w_up, w_down))
    gf = final_g.reshape(1, D_MODEL)
    for l in range(DEPTH):
        qa, ka, va, u, qm, km, vm = _in_proj(x, l, row(ln1_g), w_ext, cos_t, sin_t,
                                             row(mla_q_norm_g), wq_ext, row(mla_kv_norm_g), wkv_ext)
        o_a = _swa(qa, ka, va, l, sinks, bias2)
        o_b = _ssm(u, b_, l, bd, ar, ai, cd, row(ssm_d), wglu)
        o_c = _mla(qm, km, vm)
        x = _ffn(x, o_a, o_b, o_c, l, wo, row(ln2_g), wg, wu, wd, gf, final_norm=(l == DEPTH - 1))
    return x
```

```python
import functools
import math

import jax
import jax.numpy as jnp
from jax import lax
from jax.experimental import pallas as pl
from jax.experimental.pallas import tpu as pltpu

D_MODEL = 1024
DEPTH = 2
HEAD_DIM = 64
SWA_Q_HEADS = 8
SWA_KV_HEADS = 2
SWA_BLOCK = 128
SWA_Q = SWA_Q_HEADS * HEAD_DIM
SWA_KV = SWA_KV_HEADS * HEAD_DIM
REL_BUCKETS = 32
REL_MAX_DIST = 128
SSM_CH = 256
SSM_GROUP = 16
SSM_GROUPS = SSM_CH // SSM_GROUP
SSM_STATE = 64
SSM_STATES = SSM_GROUPS * SSM_STATE
MLA_HEADS = 4
MLA_Q_RANK = 256
MLA_KV_RANK = 128
MLA_NOPE = 64
MLA_ROPE = 32
MLA_V = 64
ROPE_THETA = 10000.0
D_FF = 2816
EPS = 1e-6
NEG = -1e30

LANES = 128
SUBLANES = 8
MLA_HEAD_PAD = LANES
MLA_EXP2_SCALE = (MLA_NOPE + MLA_ROPE) ** -0.5 * math.log2(math.e)
VMEM_LIMIT = 56 * 1024 * 1024

IN_PROJ_TM = 512
SWA_TQ = 512
MLA_TQ = 512
MLA_TK = 512
SSM_TS = 128
FFN_TM = 512
FFN_CHUNK = 1408

BF16 = jnp.bfloat16
F32 = jnp.float32


def _rms(x, g):
    return x * lax.rsqrt(jnp.mean(x * x, axis=-1, keepdims=True) + EPS) * g


def _dot(a, b):
    return jnp.dot(a, b, preferred_element_type=F32)


def _dot_nt(a, b):
    return lax.dot_general(a, b, (((1,), (1,)), ((), ())), preferred_element_type=F32)


def _layer_spec(arr, l):
    return pl.BlockSpec((None,) + arr.shape[1:], lambda *_: (l,) + (0,) * (arr.ndim - 1))


def _trig_kernel(pos_ref, freq_ref, cos_ref, sin_ref):
    ang = pos_ref[...].astype(F32) * freq_ref[...]
    cos_ref[...] = jnp.cos(ang)
    sin_ref[...] = jnp.sin(ang)


def _trig_tables(positions):
    rows = positions.size
    half = MLA_ROPE // 2
    inv_freq = jnp.power(ROPE_THETA, -jnp.arange(half, dtype=F32) * 2.0 / MLA_ROPE)
    freq = jnp.zeros((1, MLA_HEAD_PAD), F32).at[0, MLA_NOPE:MLA_NOPE + MLA_ROPE].set(jnp.tile(inv_freq, 2))
    tm = 2048
    return pl.pallas_call(
        _trig_kernel,
        grid=(rows // tm,),
        in_specs=[pl.BlockSpec((tm, 1), lambda i: (i, 0)),
                  pl.BlockSpec((1, MLA_HEAD_PAD), lambda i: (0, 0))],
        out_specs=[pl.BlockSpec((tm, MLA_HEAD_PAD), lambda i: (i, 0))] * 2,
        out_shape=[jax.ShapeDtypeStruct((rows, MLA_HEAD_PAD), F32)] * 2,
        name="rope_tables",
    )(positions.reshape(rows, 1), freq)


def _in_proj_kernel(x_ref, g_ref, w_ref, cos_ref, sin_ref, qg_ref, wq_ref, kvg_ref, wkv_ref,
                    qa_ref, ka_ref, va_ref, u_ref, qm_ref, km_ref, vm_ref):
    h = _rms(x_ref[...], g_ref[...]).astype(BF16)
    proj = _dot(h, w_ref[...])
    qa_ref[...] = (proj[:, :SWA_Q] * (HEAD_DIM ** -0.5)).astype(BF16)
    c = SWA_Q
    ka_ref[...] = proj[:, c:c + SWA_KV].astype(BF16)
    c += SWA_KV
    va_ref[...] = proj[:, c:c + SWA_KV].astype(BF16)
    c += SWA_KV
    u_ref[...] = proj[:, c:c + SSM_CH]
    c += SSM_CH
    c_q = proj[:, c:c + MLA_Q_RANK]
    c += MLA_Q_RANK
    c_kv = proj[:, c:c + MLA_KV_RANK]
    c += MLA_KV_RANK
    k_r = proj[:, c:c + LANES]
    k_r_rot = proj[:, c + LANES:c + 2 * LANES]
    cos = cos_ref[...]
    sin = sin_ref[...]
    width = MLA_HEADS * MLA_HEAD_PAD
    q2 = _dot(_rms(c_q, qg_ref[...]).astype(BF16), wq_ref[...])
    kv2 = _dot(_rms(c_kv, kvg_ref[...]).astype(BF16), wkv_ref[...])
    k_pe = k_r * cos + k_r_rot * sin
    low = lax.broadcasted_iota(jnp.int32, (cos.shape[0], LANES), 1) < MLA_V
    for hd in range(MLA_HEADS):
        sl = slice(hd * MLA_HEAD_PAD, (hd + 1) * MLA_HEAD_PAD)
        sl_rot = slice(width + hd * MLA_HEAD_PAD, width + (hd + 1) * MLA_HEAD_PAD)
        qm_ref[:, sl] = (q2[:, sl] * cos + q2[:, sl_rot] * sin).astype(BF16)
        km_ref[:, sl] = (kv2[:, sl] + k_pe).astype(BF16)
        pair = kv2[:, width + (hd // 2) * LANES:width + (hd // 2 + 1) * LANES]
        keep = low if hd % 2 == 0 else jnp.logical_not(low)
        vm_ref[:, sl] = jnp.where(keep, pair, 1.0).astype(BF16)


def _in_proj(x, l, g, w_ext, cos_t, sin_t, qg, wq_ext, kvg, wkv_ext):
    b_, s_, _ = x.shape
    tm = IN_PROJ_TM
    nt = s_ // tm
    row3 = lambda b, i: (b, i, 0)
    trig = pl.BlockSpec((tm, MLA_HEAD_PAD), lambda b, i: (b * nt + i, 0))
    mla_w = MLA_HEADS * MLA_HEAD_PAD
    return pl.pallas_call(
        _in_proj_kernel,
        grid=(b_, nt),
        in_specs=[pl.BlockSpec((None, tm, D_MODEL), row3),
                  _layer_spec(g, l), _layer_spec(w_ext, l),
                  trig, trig,
                  _layer_spec(qg, l), _layer_spec(wq_ext, l), _layer_spec(kvg, l), _layer_spec(wkv_ext, l)],
        out_specs=[pl.BlockSpec((None, tm, SWA_Q), row3),
                   pl.BlockSpec((None, tm, SWA_KV), row3),
                   pl.BlockSpec((None, tm, SWA_KV), row3),
                   pl.BlockSpec((tm, SSM_CH), lambda b, i: (i, b)),
                   pl.BlockSpec((None, tm, mla_w), row3),
                   pl.BlockSpec((None, tm, mla_w), row3),
                   pl.BlockSpec((None, tm, mla_w), row3)],
        out_shape=[jax.ShapeDtypeStruct((b_, s_, SWA_Q), BF16),
                   jax.ShapeDtypeStruct((b_, s_, SWA_KV), BF16),
                   jax.ShapeDtypeStruct((b_, s_, SWA_KV), BF16),
                   jax.ShapeDtypeStruct((s_, b_ * SSM_CH), F32),
                   jax.ShapeDtypeStruct((b_, s_, mla_w), BF16),
                   jax.ShapeDtypeStruct((b_, s_, mla_w), BF16),
                   jax.ShapeDtypeStruct((b_, s_, mla_w), BF16)],
        compiler_params=pltpu.CompilerParams(
            dimension_semantics=("arbitrary", "arbitrary"), vmem_limit_bytes=VMEM_LIMIT),
        name="in_proj",
    )(x, g, w_ext, cos_t, sin_t, qg, wq_ext, kvg, wkv_ext)


def _swa_kernel(sink_ref, q_ref, k_ref, v_ref, bias_ref, o_ref, *, layer):
    qi = pl.program_id(1)
    blocks = SWA_TQ // SWA_BLOCK
    lane = lax.broadcasted_iota(jnp.int32, (2 * SWA_BLOCK, LANES), 1)
    low = lane < HEAD_DIM
    out_low = lax.broadcasted_iota(jnp.int32, (SWA_BLOCK, LANES), 1) < HEAD_DIM
    for j in range(blocks):
        blk = qi * blocks + j
        start = pl.multiple_of(jnp.maximum(blk - 1, 0) * SWA_BLOCK, SWA_BLOCK)
        table = jnp.minimum(blk, 1)
        kb = k_ref[pl.ds(start, 2 * SWA_BLOCK), :]
        vb = v_ref[pl.ds(start, 2 * SWA_BLOCK), :]
        kb_sw = pltpu.roll(kb, HEAD_DIM, 1)
        vb_sw = pltpu.roll(vb, HEAD_DIM, 1)
        zero = jnp.zeros_like(kb)
        k_var = ((jnp.where(low, kb, zero), jnp.where(low, zero, kb_sw)),
                 (jnp.where(low, kb_sw, zero), jnp.where(low, zero, kb)))
        v_var = (jnp.where(low, vb, vb_sw), jnp.where(low, vb_sw, vb))
        rows = slice(j * SWA_BLOCK, (j + 1) * SWA_BLOCK)
        for i in range(SWA_Q // LANES):
            kvh = (2 * i) // (SWA_Q_HEADS // SWA_KV_HEADS)
            q = q_ref[rows, i * LANES:(i + 1) * LANES]
            halves = []
            for half in range(2):
                hq = 2 * i + half
                sink = sink_ref[layer, hq]
                s = _dot_nt(q, k_var[kvh][half]) + bias_ref[table, hq]
                m = jnp.maximum(jnp.max(s, axis=-1, keepdims=True), sink)
                p = jnp.exp(s - m)
                l = jnp.sum(p, axis=-1, keepdims=True) + jnp.exp(sink - m)
                halves.append(_dot(p.astype(BF16), v_var[kvh]) / l)
            o_ref[rows, i * LANES:(i + 1) * LANES] = jnp.where(out_low, halves[0], halves[1]).astype(BF16)


def _swa(q, k, v, l, sinks, bias2):
    b_, s_, _ = q.shape
    tq = SWA_TQ
    return pl.pallas_call(
        functools.partial(_swa_kernel, layer=l),
        grid=(b_, s_ // tq),
        in_specs=[pl.BlockSpec(memory_space=pltpu.SMEM),
                  pl.BlockSpec((None, tq, SWA_Q), lambda b, i: (b, i, 0)),
                  pl.BlockSpec((None, s_, SWA_KV), lambda b, i: (b, 0, 0)),
                  pl.BlockSpec((None, s_, SWA_KV), lambda b, i: (b, 0, 0)),
                  pl.BlockSpec(bias2.shape, lambda b, i: (0, 0, 0, 0))],
        out_specs=pl.BlockSpec((None, tq, SWA_Q), lambda b, i: (b, i, 0)),
        out_shape=jax.ShapeDtypeStruct((b_, s_, SWA_Q), BF16),
        compiler_params=pltpu.CompilerParams(
            dimension_semantics=("arbitrary", "arbitrary"), vmem_limit_bytes=VMEM_LIMIT),
        name="swa_attention",
    )(sinks, q, k, v, bias2)


def _t5_bucket(dist):
    n = jnp.maximum(dist, 0)
    max_exact = REL_BUCKETS // 2
    large = max_exact + (jnp.log(jnp.maximum(n, 1).astype(F32) / max_exact)
                         / math.log(REL_MAX_DIST / max_exact)
                         * (REL_BUCKETS - max_exact)).astype(jnp.int32)
    large = jnp.minimum(large, REL_BUCKETS - 1)
    return jnp.where(n < max_exact, n, large)


def _band_bias_tables(rel_bias):
    qi = jnp.arange(SWA_BLOCK)[:, None]
    kj = jnp.arange(2 * SWA_BLOCK)[None, :]
    dist = qi + SWA_BLOCK - kj
    onehot = (_t5_bucket(dist)[None] == jnp.arange(REL_BUCKETS)[:, None, None]).astype(F32)
    b = jnp.einsum('nh,nqk->hqk', rel_bias.astype(F32), onehot, precision=lax.Precision.HIGHEST)
    valid = (dist >= 0) & (dist < SWA_BLOCK)
    normal = jnp.where(valid[None], b, NEG)
    first = jnp.concatenate([normal[:, :, SWA_BLOCK:], jnp.full_like(normal[:, :, SWA_BLOCK:], NEG)], axis=-1)
    return jnp.stack([first, normal])


def _mla_kernel(q_ref, k_ref, v_ref, o_ref, s_scr, m_scr, acc_scr):
    qi = pl.program_id(1)
    tq, tk = MLA_TQ, MLA_TK
    row = lax.broadcasted_iota(jnp.int32, (tq, tk), 0)
    col = lax.broadcasted_iota(jnp.int32, (tq, tk), 1)
    causal = col <= row
    out_low = lax.broadcasted_iota(jnp.int32, (tq, LANES), 1) < MLA_V
    heads = [slice(hd * MLA_HEAD_PAD, (hd + 1) * MLA_HEAD_PAD) for hd in range(MLA_HEADS)]

    lane_tiles = [slice(t * LANES, (t + 1) * LANES) for t in range(tk // LANES)]
    m_scr[...] = jnp.full(m_scr.shape, NEG, F32)
    acc_scr[...] = jnp.zeros(acc_scr.shape, F32)

    def scores(j, carry, masked):
        k0 = pl.multiple_of(j * tk, tk)
        for hd in range(MLA_HEADS):
            s = _dot_nt(q_ref[:, heads[hd]], k_ref[pl.ds(k0, tk), heads[hd]])
            if masked:
                s = jnp.where(causal, s, NEG)
            s_scr[hd, j] = s
            m_scr[hd] = functools.reduce(jnp.maximum, [s[:, lt] for lt in lane_tiles], m_scr[hd])
        return carry

    lax.fori_loop(0, qi, functools.partial(scores, masked=False), 0)
    scores(qi, 0, masked=True)
    for hd in range(MLA_HEADS):
        m_scr[hd] = jnp.broadcast_to(jnp.max(m_scr[hd], axis=-1, keepdims=True), (tq, LANES))

    def weighted(j, carry):
        k0 = pl.multiple_of(j * tk, tk)
        for hd in range(MLA_HEADS):
            m = m_scr[hd]
            p = jnp.concatenate([jnp.exp2((s_scr[hd, j, :, lt] - m) * MLA_EXP2_SCALE).astype(BF16)
                                 for lt in lane_tiles], axis=-1)
            acc_scr[hd] += _dot(p, v_ref[pl.ds(k0, tk), heads[hd]])
        return carry

    lax.fori_loop(0, qi + 1, weighted, 0)
    for t in range(MLA_HEADS // 2):
        lo, hi = acc_scr[2 * t], acc_scr[2 * t + 1]
        lo = lo / lo[:, MLA_V:MLA_V + 1]
        hi = hi / hi[:, 0:1]
        o_ref[:, t * LANES:(t + 1) * LANES] = jnp.where(out_low, lo, hi).astype(BF16)


def _mla(q, k, v):
    b_, s_, w = q.shape
    tq = MLA_TQ
    return pl.pallas_call(
        _mla_kernel,
        grid=(b_, s_ // tq),
        in_specs=[pl.BlockSpec((None, tq, w), lambda b, i: (b, i, 0)),
                  pl.BlockSpec((None, s_, w), lambda b, i: (b, 0, 0)),
                  pl.BlockSpec((None, s_, w), lambda b, i: (b, 0, 0))],
        out_specs=pl.BlockSpec((None, tq, MLA_HEADS * MLA_V), lambda b, i: (b, i, 0)),
        out_shape=jax.ShapeDtypeStruct((b_, s_, MLA_HEADS * MLA_V), BF16),
        scratch_shapes=[pltpu.VMEM((MLA_HEADS, s_ // MLA_TK, tq, MLA_TK), F32),
                        pltpu.VMEM((MLA_HEADS, tq, LANES), F32),
                        pltpu.VMEM((MLA_HEADS, tq, LANES), F32)],
        compiler_params=pltpu.CompilerParams(
            dimension_semantics=("arbitrary", "arbitrary"), vmem_limit_bytes=VMEM_LIMIT),
        name="mla_attention",
    )(q, k, v)


def _gelu_tanh(x):
    return 0.5 * x * (1.0 + jnp.tanh(math.sqrt(2.0 / math.pi) * (x + 0.044715 * (x * x * x))))


def _ssm_kernel(u_ref, bd_ref, ar_ref, ai_ref, cd_ref, d_ref, wglu_ref, y_ref,
                u_scr, x_scr, y_scr, sr_scr, si_scr):
    n = SSM_STATES
    batch = sr_scr.shape[0]
    lane_tiles = SSM_CH // LANES

    @pl.when(pl.program_id(0) == 0)
    def _():
        sr_scr[...] = jnp.zeros_like(sr_scr)
        si_scr[...] = jnp.zeros_like(si_scr)

    for b in range(batch):
        for lt in range(lane_tiles):
            c0 = b * SSM_CH + lt * LANES
            u_scr[lt, pl.ds(b, SSM_TS, stride=batch), :] = u_ref[:, c0:c0 + LANES]
    u = jnp.concatenate([u_scr[lt] for lt in range(lane_tiles)], axis=-1)
    u_bf = u.astype(BF16)
    x_scr[:, :n] = _dot(u_bf, bd_ref[:, :n])
    x_scr[:, n:] = _dot(u_bf, bd_ref[:, n:])
    ar = ar_ref[...]
    ai = ai_ref[...]

    def step(t, carry):
        xr, xi = carry
        r0 = pl.multiple_of(t * batch, batch)
        nr = ar * xr - ai * xi + x_scr[pl.ds(r0, batch), :n]
        ni = ar * xi + ai * xr + x_scr[pl.ds(r0, batch), n:]
        x_scr[pl.ds(r0, batch), :n] = nr
        x_scr[pl.ds(r0, batch), n:] = ni
        return nr, ni

    xr, xi = lax.fori_loop(0, SSM_TS, step, (sr_scr[...], si_scr[...]), unroll=4)
    sr_scr[...] = xr
    si_scr[...] = xi
    half = x_scr.shape[0] // 2
    y = jnp.concatenate([_dot(x_scr[:half, :].astype(BF16), cd_ref[...]),
                         _dot(x_scr[half:, :].astype(BF16), cd_ref[...])], axis=0) + d_ref[...] * u
    y = _gelu_tanh(y)
    y = y * jax.nn.sigmoid(_dot(y.astype(BF16), wglu_ref[...]))
    for lt in range(lane_tiles):
        y_scr[lt] = y[:, lt * LANES:(lt + 1) * LANES]
    for b in range(batch):
        for lt in range(lane_tiles):
            c0 = b * SSM_CH + lt * LANES
            y_ref[:, c0:c0 + LANES] = y_scr[lt, pl.ds(b, SSM_TS, stride=batch), :].astype(BF16)


def _ssm(u, batch, l, bd, ar, ai, cd, d, wglu):
    s_, width = u.shape
    tr = SSM_TS * batch
    return pl.pallas_call(
        _ssm_kernel,
        grid=(s_ // SSM_TS,),
        in_specs=[pl.BlockSpec((SSM_TS, width), lambda i: (i, 0))]
        + [_layer_spec(a, l) for a in (bd, ar, ai, cd, d, wglu)],
        out_specs=pl.BlockSpec((SSM_TS, width), lambda i: (i, 0)),
        out_shape=jax.ShapeDtypeStruct((s_, width), BF16),
        scratch_shapes=[pltpu.VMEM((SSM_CH // LANES, tr, LANES), F32),
                        pltpu.VMEM((tr, 2 * SSM_STATES), F32),
                        pltpu.VMEM((SSM_CH // LANES, tr, LANES), F32),
                        pltpu.VMEM((batch, SSM_STATES), F32),
                        pltpu.VMEM((batch, SSM_STATES), F32)],
        compiler_params=pltpu.CompilerParams(
            dimension_semantics=("arbitrary",), vmem_limit_bytes=VMEM_LIMIT),
        name="s5_mixer",
    )(u, bd, ar, ai, cd, d, wglu)


def _ssm_params(a_re, a_im, log_dt, b_re, b_im, c_re, c_im, batch):
    lam = lax.complex(a_re.astype(F32), a_im.astype(F32))
    dt = jnp.exp(log_dt.astype(F32))
    a_bar = jnp.exp(lam * dt[:, None])
    b_bar = ((a_bar - 1.0) / lam)[..., None] * lax.complex(b_re.astype(F32), b_im.astype(F32))
    eye = jnp.eye(SSM_GROUPS, dtype=F32)

    def in_op(m):
        return jnp.einsum('gpc,gh->gchp', m, eye).reshape(SSM_CH, SSM_STATES)

    def out_op(m):
        return jnp.einsum('gcp,gh->gphc', m, eye).reshape(SSM_STATES, SSM_CH)

    bd = jnp.concatenate([in_op(jnp.real(b_bar)), in_op(jnp.imag(b_bar))], axis=1).astype(BF16)
    cd = jnp.concatenate([out_op(c_re.astype(F32)), -out_op(c_im.astype(F32))], axis=0).astype(BF16)
    ar = jnp.broadcast_to(jnp.real(a_bar).reshape(1, SSM_STATES), (batch, SSM_STATES))
    ai = jnp.broadcast_to(jnp.imag(a_bar).reshape(1, SSM_STATES), (batch, SSM_STATES))
    return bd, ar, ai, cd


def _ffn_kernel(x_ref, oa_ref, ob_ref, oc_ref, wo_ref, g2_ref, wg_ref, wu_ref, wd_ref, gf_ref, o_ref, h_scr,
                *, final_norm):
    wo_a = wo_ref[:SWA_Q, :]
    wo_b = wo_ref[SWA_Q:SWA_Q + SSM_CH, :]
    wo_c = wo_ref[SWA_Q + SSM_CH:, :]
    x = x_ref[...] + _dot(oa_ref[...], wo_a) + _dot(ob_ref[...], wo_b) + _dot(oc_ref[...], wo_c)
    o_ref[...] = x
    h_scr[...] = _rms(x, g2_ref[...]).astype(BF16)

    def chunk(c, carry):
        c0 = pl.multiple_of(c * FFN_CHUNK, LANES)
        h = h_scr[...]
        gate = _dot(h, wg_ref[:, pl.ds(c0, FFN_CHUNK)])
        up = _dot(h, wu_ref[:, pl.ds(c0, FFN_CHUNK)])
        act = (gate * jax.nn.sigmoid(gate) * up).astype(BF16)
        o_ref[...] += _dot(act, wd_ref[pl.ds(c0, FFN_CHUNK), :])
        return carry

    lax.fori_loop(0, D_FF // FFN_CHUNK, chunk, 0)
    if final_norm:
        o_ref[...] = _rms(o_ref[...], gf_ref[...])


def _ffn(x, oa, ob, oc, l, wo, g2, wg, wu, wd, gf, final_norm):
    b_, s_, _ = x.shape
    tm = FFN_TM
    row3 = lambda b, i: (b, i, 0)
    return pl.pallas_call(
        functools.partial(_ffn_kernel, final_norm=final_norm),
        grid=(b_, s_ // tm),
        in_specs=[pl.BlockSpec((None, tm, D_MODEL), row3),
                  pl.BlockSpec((None, tm, SWA_Q), row3),
                  pl.BlockSpec((tm, SSM_CH), lambda b, i: (i, b)),
                  pl.BlockSpec((None, tm, MLA_HEADS * MLA_V), row3),
                  _layer_spec(wo, l), _layer_spec(g2, l), _layer_spec(wg, l), _layer_spec(wu, l),
                  _layer_spec(wd, l),
                  pl.BlockSpec((1, D_MODEL), lambda b, i: (0, 0))],
        out_specs=pl.BlockSpec((None, tm, D_MODEL), row3),
        out_shape=jax.ShapeDtypeStruct(x.shape, F32),
        scratch_shapes=[pltpu.VMEM((tm, D_MODEL), BF16)],
        compiler_params=pltpu.CompilerParams(
            dimension_semantics=("arbitrary", "arbitrary"), vmem_limit_bytes=VMEM_LIMIT),
        name="out_proj_ffn",
    )(x, oa, ob, oc, wo, g2, wg, wu, wd, gf)


def _rot_half_cols(w):
    half = w.shape[-1] // 2
    return jnp.concatenate([-w[..., half:], w[..., :half]], axis=-1)


def _rope_tile(w):
    z = jnp.zeros(w.shape[:-1] + (MLA_NOPE,), w.dtype)
    z2 = jnp.zeros(w.shape[:-1] + (MLA_HEAD_PAD - MLA_NOPE - MLA_ROPE,), w.dtype)
    return jnp.concatenate([z, w, z2], axis=-1)


def _stacked_weights(w_in, w_q_up, w_kv_up):
    lead = w_in.shape[:-2]
    kr = w_in[..., -MLA_ROPE:]
    w_ext = jnp.concatenate([w_in[..., :-MLA_ROPE], _rope_tile(kr), _rope_tile(_rot_half_cols(kr))], axis=-1)
    qh = w_q_up.reshape(lead + (MLA_Q_RANK, MLA_HEADS, MLA_NOPE + MLA_ROPE))
    pad = jnp.zeros(lead + (MLA_Q_RANK, MLA_HEADS, MLA_HEAD_PAD - MLA_NOPE - MLA_ROPE), w_q_up.dtype)
    q_tiles = jnp.concatenate([qh, pad], axis=-1).reshape(lead + (MLA_Q_RANK, -1))
    q_rot = _rope_tile(_rot_half_cols(qh[..., MLA_NOPE:])).reshape(lead + (MLA_Q_RANK, -1))
    wq_ext = jnp.concatenate([q_tiles, q_rot], axis=-1)
    kvh = w_kv_up.reshape(lead + (MLA_KV_RANK, MLA_HEADS, MLA_NOPE + MLA_V))
    kpad = jnp.zeros(lead + (MLA_KV_RANK, MLA_HEADS, MLA_HEAD_PAD - MLA_NOPE), w_kv_up.dtype)
    k_tiles = jnp.concatenate([kvh[..., :MLA_NOPE], kpad], axis=-1).reshape(lead + (MLA_KV_RANK, -1))
    v_cols = kvh[..., MLA_NOPE:].reshape(lead + (MLA_KV_RANK, -1))
    wkv_ext = jnp.concatenate([k_tiles, v_cols], axis=-1)
    return w_ext.astype(BF16), wq_ext.astype(BF16), wkv_ext.astype(BF16)


def kernel(x, positions, rel_bias, ln1_g, w_in, sinks, ssm_a_re, ssm_a_im, ssm_log_dt, ssm_b_re, ssm_b_im,
           ssm_c_re, ssm_c_im, ssm_d, ssm_w_glu, mla_q_norm_g, mla_w_q_up, mla_kv_norm_g, mla_w_kv_up, w_out,
           ln2_g, w_gate, w_up, w_down, final_g):
    b_ = x.shape[0]
    cos_t, sin_t = _trig_tables(positions)
    bias2 = _band_bias_tables(rel_bias)
    row = lambda g: g[:, None, :]
    w_ext, wq_ext, wkv_ext = _stacked_weights(w_in, mla_w_q_up, mla_w_kv_up)
    bd, ar, ai, cd = jax.vmap(functools.partial(_ssm_params, batch=b_))(
        ssm_a_re, ssm_a_im, ssm_log_dt, ssm_b_re, ssm_b_im, ssm_c_re, ssm_c_im)
    wglu = ssm_w_glu.astype(BF16)
    wo, wg, wu, wd = (w.astype(BF16) for w in (w_out, w_gate, w_up, w_down))
    gf = final_g.reshape(1, D_MODEL)
    for l in range(DEPTH):
        qa, ka, va, u, qm, km, vm = _in_proj(x, l, row(ln1_g), w_ext, cos_t, sin_t,
                                             row(mla_q_norm_g), wq_ext, row(mla_kv_norm_g), wkv_ext)
        o_a = _swa(qa, ka, va, l, sinks, bias2)
        o_b = _ssm(u, b_, l, bd, ar, ai, cd, row(ssm_d), wglu)
        o_c = _mla(qm, km, vm)
        x = _ffn(x, o_a, o_b, o_c, l, wo, row(ln2_g), wg, wu, wd, gf, final_norm=(l == DEPTH - 1))
    return x
```

```python
import functools
import math

import jax
import jax.numpy as jnp
from jax import lax
from jax.experimental import pallas as pl
from jax.experimental.pallas import tpu as pltpu

D_MODEL = 1024
DEPTH = 2
HEAD_DIM = 64
SWA_Q_HEADS = 8
SWA_KV_HEADS = 2
SWA_BLOCK = 128
SWA_Q = SWA_Q_HEADS * HEAD_DIM
SWA_KV = SWA_KV_HEADS * HEAD_DIM
REL_BUCKETS = 32
REL_MAX_DIST = 128
SSM_CH = 256
SSM_GROUP = 16
SSM_GROUPS = SSM_CH // SSM_GROUP
SSM_STATE = 64
SSM_STATES = SSM_GROUPS * SSM_STATE
MLA_HEADS = 4
MLA_Q_RANK = 256
MLA_KV_RANK = 128
MLA_NOPE = 64
MLA_ROPE = 32
MLA_V = 64
ROPE_THETA = 10000.0
D_FF = 2816
EPS = 1e-6
NEG = -1e30

LANES = 128
SUBLANES = 8
MLA_HEAD_PAD = LANES
MLA_EXP2_SCALE = (MLA_NOPE + MLA_ROPE) ** -0.5 * math.log2(math.e)
VMEM_LIMIT = 56 * 1024 * 1024

IN_PROJ_TM = 512
SWA_TQ = 512
MLA_TQ = 512
MLA_TK = 512
SSM_TS = 128
SSM_SUB = 2
FFN_TM = 512
FFN_CHUNK = 1408

BF16 = jnp.bfloat16
F32 = jnp.float32


def _rms(x, g):
    return x * lax.rsqrt(jnp.mean(x * x, axis=-1, keepdims=True) + EPS) * g


def _dot(a, b):
    return jnp.dot(a, b, preferred_element_type=F32)


def _dot_nt(a, b):
    return lax.dot_general(a, b, (((1,), (1,)), ((), ())), preferred_element_type=F32)


def _layer_spec(arr, l):
    return pl.BlockSpec((None,) + arr.shape[1:], lambda *_: (l,) + (0,) * (arr.ndim - 1))


def _trig_kernel(pos_ref, freq_ref, cos_ref, sin_ref):
    ang = pos_ref[...].astype(F32) * freq_ref[...]
    cos_ref[...] = jnp.cos(ang)
    sin_ref[...] = jnp.sin(ang)


def _trig_tables(positions):
    rows = positions.size
    half = MLA_ROPE // 2
    inv_freq = jnp.power(ROPE_THETA, -jnp.arange(half, dtype=F32) * 2.0 / MLA_ROPE)
    freq = jnp.zeros((1, MLA_HEAD_PAD), F32).at[0, MLA_NOPE:MLA_NOPE + MLA_ROPE].set(jnp.tile(inv_freq, 2))
    tm = 2048
    return pl.pallas_call(
        _trig_kernel,
        grid=(rows // tm,),
        in_specs=[pl.BlockSpec((tm, 1), lambda i: (i, 0)),
                  pl.BlockSpec((1, MLA_HEAD_PAD), lambda i: (0, 0))],
        out_specs=[pl.BlockSpec((tm, MLA_HEAD_PAD), lambda i: (i, 0))] * 2,
        out_shape=[jax.ShapeDtypeStruct((rows, MLA_HEAD_PAD), F32)] * 2,
        name="rope_tables",
    )(positions.reshape(rows, 1), freq)


def _in_proj_kernel(x_ref, g_ref, w_ref, cos_ref, sin_ref, qg_ref, wq_ref, kvg_ref, wkv_ref,
                    qa_ref, ka_ref, va_ref, u_ref, qm_ref, km_ref, vm_ref):
    h = _rms(x_ref[...], g_ref[...]).astype(BF16)
    proj = _dot(h, w_ref[...])
    qa_ref[...] = (proj[:, :SWA_Q] * (HEAD_DIM ** -0.5)).astype(BF16)
    c = SWA_Q
    ka_ref[...] = proj[:, c:c + SWA_KV].astype(BF16)
    c += SWA_KV
    va_ref[...] = proj[:, c:c + SWA_KV].astype(BF16)
    c += SWA_KV
    u_ref[...] = proj[:, c:c + SSM_CH]
    c += SSM_CH
    c_q = proj[:, c:c + MLA_Q_RANK]
    c += MLA_Q_RANK
    c_kv = proj[:, c:c + MLA_KV_RANK]
    c += MLA_KV_RANK
    k_r = proj[:, c:c + LANES]
    k_r_rot = proj[:, c + LANES:c + 2 * LANES]
    cos = cos_ref[...]
    sin = sin_ref[...]
    width = MLA_HEADS * MLA_HEAD_PAD
    q2 = _dot(_rms(c_q, qg_ref[...]).astype(BF16), wq_ref[...])
    kv2 = _dot(_rms(c_kv, kvg_ref[...]).astype(BF16), wkv_ref[...])
    k_pe = k_r * cos + k_r_rot * sin
    low = lax.broadcasted_iota(jnp.int32, (cos.shape[0], LANES), 1) < MLA_V
    for hd in range(MLA_HEADS):
        sl = slice(hd * MLA_HEAD_PAD, (hd + 1) * MLA_HEAD_PAD)
        sl_rot = slice(width + hd * MLA_HEAD_PAD, width + (hd + 1) * MLA_HEAD_PAD)
        qm_ref[:, sl] = (q2[:, sl] * cos + q2[:, sl_rot] * sin).astype(BF16)
        km_ref[:, sl] = (kv2[:, sl] + k_pe).astype(BF16)
        pair = kv2[:, width + (hd // 2) * LANES:width + (hd // 2 + 1) * LANES]
        keep = low if hd % 2 == 0 else jnp.logical_not(low)
        vm_ref[:, sl] = jnp.where(keep, pair, 1.0).astype(BF16)


def _in_proj(x, l, g, w_ext, cos_t, sin_t, qg, wq_ext, kvg, wkv_ext):
    b_, s_, _ = x.shape
    tm = IN_PROJ_TM
    nt = s_ // tm
    row3 = lambda b, i: (b, i, 0)
    trig = pl.BlockSpec((tm, MLA_HEAD_PAD), lambda b, i: (b * nt + i, 0))
    mla_w = MLA_HEADS * MLA_HEAD_PAD
    return pl.pallas_call(
        _in_proj_kernel,
        grid=(b_, nt),
        in_specs=[pl.BlockSpec((None, tm, D_MODEL), row3),
                  _layer_spec(g, l), _layer_spec(w_ext, l),
                  trig, trig,
                  _layer_spec(qg, l), _layer_spec(wq_ext, l), _layer_spec(kvg, l), _layer_spec(wkv_ext, l)],
        out_specs=[pl.BlockSpec((None, tm, SWA_Q), row3),
                   pl.BlockSpec((None, tm, SWA_KV), row3),
                   pl.BlockSpec((None, tm, SWA_KV), row3),
                   pl.BlockSpec((tm, SSM_CH), lambda b, i: (i, b)),
                   pl.BlockSpec((None, tm, mla_w), row3),
                   pl.BlockSpec((None, tm, mla_w), row3),
                   pl.BlockSpec((None, tm, mla_w), row3)],
        out_shape=[jax.ShapeDtypeStruct((b_, s_, SWA_Q), BF16),
                   jax.ShapeDtypeStruct((b_, s_, SWA_KV), BF16),
                   jax.ShapeDtypeStruct((b_, s_, SWA_KV), BF16),
                   jax.ShapeDtypeStruct((s_, b_ * SSM_CH), F32),
                   jax.ShapeDtypeStruct((b_, s_, mla_w), BF16),
                   jax.ShapeDtypeStruct((b_, s_, mla_w), BF16),
                   jax.ShapeDtypeStruct((b_, s_, mla_w), BF16)],
        compiler_params=pltpu.CompilerParams(
            dimension_semantics=("arbitrary", "arbitrary"), vmem_limit_bytes=VMEM_LIMIT),
        name="in_proj",
    )(x, g, w_ext, cos_t, sin_t, qg, wq_ext, kvg, wkv_ext)


def _swa_kernel(sink_ref, q_ref, k_ref, v_ref, bias_ref, o_ref, *, layer):
    qi = pl.program_id(1)
    blocks = SWA_TQ // SWA_BLOCK
    lane = lax.broadcasted_iota(jnp.int32, (2 * SWA_BLOCK, LANES), 1)
    low = lane < HEAD_DIM
    out_low = lax.broadcasted_iota(jnp.int32, (SWA_BLOCK, LANES), 1) < HEAD_DIM
    for j in range(blocks):
        blk = qi * blocks + j
        start = pl.multiple_of(jnp.maximum(blk - 1, 0) * SWA_BLOCK, SWA_BLOCK)
        table = jnp.minimum(blk, 1)
        kb = k_ref[pl.ds(start, 2 * SWA_BLOCK), :]
        vb = v_ref[pl.ds(start, 2 * SWA_BLOCK), :]
        kb_sw = pltpu.roll(kb, HEAD_DIM, 1)
        vb_sw = pltpu.roll(vb, HEAD_DIM, 1)
        zero = jnp.zeros_like(kb)
        k_var = ((jnp.where(low, kb, zero), jnp.where(low, zero, kb_sw)),
                 (jnp.where(low, kb_sw, zero), jnp.where(low, zero, kb)))
        v_var = (jnp.where(low, vb, vb_sw), jnp.where(low, vb_sw, vb))
        rows = slice(j * SWA_BLOCK, (j + 1) * SWA_BLOCK)
        for i in range(SWA_Q // LANES):
            kvh = (2 * i) // (SWA_Q_HEADS // SWA_KV_HEADS)
            q = q_ref[rows, i * LANES:(i + 1) * LANES]
            halves = []
            for half in range(2):
                hq = 2 * i + half
                sink = sink_ref[layer, hq]
                s = _dot_nt(q, k_var[kvh][half]) + bias_ref[table, hq]
                m = jnp.maximum(jnp.max(s, axis=-1, keepdims=True), sink)
                p = jnp.exp(s - m)
                l = jnp.sum(p, axis=-1, keepdims=True) + jnp.exp(sink - m)
                halves.append(_dot(p.astype(BF16), v_var[kvh]) / l)
            o_ref[rows, i * LANES:(i + 1) * LANES] = jnp.where(out_low, halves[0], halves[1]).astype(BF16)


def _swa(q, k, v, l, sinks, bias2):
    b_, s_, _ = q.shape
    tq = SWA_TQ
    return pl.pallas_call(
        functools.partial(_swa_kernel, layer=l),
        grid=(b_, s_ // tq),
        in_specs=[pl.BlockSpec(memory_space=pltpu.SMEM),
                  pl.BlockSpec((None, tq, SWA_Q), lambda b, i: (b, i, 0)),
                  pl.BlockSpec((None, s_, SWA_KV), lambda b, i: (b, 0, 0)),
                  pl.BlockSpec((None, s_, SWA_KV), lambda b, i: (b, 0, 0)),
                  pl.BlockSpec(bias2.shape, lambda b, i: (0, 0, 0, 0))],
        out_specs=pl.BlockSpec((None, tq, SWA_Q), lambda b, i: (b, i, 0)),
        out_shape=jax.ShapeDtypeStruct((b_, s_, SWA_Q), BF16),
        compiler_params=pltpu.CompilerParams(
            dimension_semantics=("arbitrary", "arbitrary"), vmem_limit_bytes=VMEM_LIMIT),
        name="swa_attention",
    )(sinks, q, k, v, bias2)


def _t5_bucket(dist):
    n = jnp.maximum(dist, 0)
    max_exact = REL_BUCKETS // 2
    large = max_exact + (jnp.log(jnp.maximum(n, 1).astype(F32) / max_exact)
                         / math.log(REL_MAX_DIST / max_exact)
                         * (REL_BUCKETS - max_exact)).astype(jnp.int32)
    large = jnp.minimum(large, REL_BUCKETS - 1)
    return jnp.where(n < max_exact, n, large)


def _band_bias_tables(rel_bias):
    qi = jnp.arange(SWA_BLOCK)[:, None]
    kj = jnp.arange(2 * SWA_BLOCK)[None, :]
    dist = qi + SWA_BLOCK - kj
    onehot = (_t5_bucket(dist)[None] == jnp.arange(REL_BUCKETS)[:, None, None]).astype(F32)
    b = jnp.einsum('nh,nqk->hqk', rel_bias.astype(F32), onehot, precision=lax.Precision.HIGHEST)
    valid = (dist >= 0) & (dist < SWA_BLOCK)
    normal = jnp.where(valid[None], b, NEG)
    first = jnp.concatenate([normal[:, :, SWA_BLOCK:], jnp.full_like(normal[:, :, SWA_BLOCK:], NEG)], axis=-1)
    return jnp.stack([first, normal])


def _mla_kernel(q_ref, k_ref, v_ref, o_ref, s_scr, m_scr, acc_scr):
    qi = pl.program_id(1)
    tq, tk = MLA_TQ, MLA_TK
    row = lax.broadcasted_iota(jnp.int32, (tq, tk), 0)
    col = lax.broadcasted_iota(jnp.int32, (tq, tk), 1)
    causal = col <= row
    out_low = lax.broadcasted_iota(jnp.int32, (tq, LANES), 1) < MLA_V
    heads = [slice(hd * MLA_HEAD_PAD, (hd + 1) * MLA_HEAD_PAD) for hd in range(MLA_HEADS)]

    lane_tiles = [slice(t * LANES, (t + 1) * LANES) for t in range(tk // LANES)]
    m_scr[...] = jnp.full(m_scr.shape, NEG, F32)
    acc_scr[...] = jnp.zeros(acc_scr.shape, F32)

    def scores(j, carry, masked):
        k0 = pl.multiple_of(j * tk, tk)
        for hd in range(MLA_HEADS):
            s = _dot_nt(q_ref[:, heads[hd]], k_ref[pl.ds(k0, tk), heads[hd]])
            if masked:
                s = jnp.where(causal, s, NEG)
            s_scr[hd, j] = s
            m_scr[hd] = functools.reduce(jnp.maximum, [s[:, lt] for lt in lane_tiles], m_scr[hd])
        return carry

    lax.fori_loop(0, qi, functools.partial(scores, masked=False), 0)
    scores(qi, 0, masked=True)
    for hd in range(MLA_HEADS):
        m_scr[hd] = jnp.broadcast_to(jnp.max(m_scr[hd], axis=-1, keepdims=True), (tq, LANES))

    def weighted(j, carry):
        k0 = pl.multiple_of(j * tk, tk)
        for hd in range(MLA_HEADS):
            m = m_scr[hd]
            p = jnp.concatenate([jnp.exp2((s_scr[hd, j, :, lt] - m) * MLA_EXP2_SCALE).astype(BF16)
                                 for lt in lane_tiles], axis=-1)
            acc_scr[hd] += _dot(p, v_ref[pl.ds(k0, tk), heads[hd]])
        return carry

    lax.fori_loop(0, qi + 1, weighted, 0)
    for t in range(MLA_HEADS // 2):
        lo, hi = acc_scr[2 * t], acc_scr[2 * t + 1]
        lo = lo / lo[:, MLA_V:MLA_V + 1]
        hi = hi / hi[:, 0:1]
        o_ref[:, t * LANES:(t + 1) * LANES] = jnp.where(out_low, lo, hi).astype(BF16)


def _mla(q, k, v):
    b_, s_, w = q.shape
    tq = MLA_TQ
    return pl.pallas_call(
        _mla_kernel,
        grid=(b_, s_ // tq),
        in_specs=[pl.BlockSpec((None, tq, w), lambda b, i: (b, i, 0)),
                  pl.BlockSpec((None, s_, w), lambda b, i: (b, 0, 0)),
                  pl.BlockSpec((None, s_, w), lambda b, i: (b, 0, 0))],
        out_specs=pl.BlockSpec((None, tq, MLA_HEADS * MLA_V), lambda b, i: (b, i, 0)),
        out_shape=jax.ShapeDtypeStruct((b_, s_, MLA_HEADS * MLA_V), BF16),
        scratch_shapes=[pltpu.VMEM((MLA_HEADS, s_ // MLA_TK, tq, MLA_TK), F32),
                        pltpu.VMEM((MLA_HEADS, tq, LANES), F32),
                        pltpu.VMEM((MLA_HEADS, tq, LANES), F32)],
        compiler_params=pltpu.CompilerParams(
            dimension_semantics=("arbitrary", "arbitrary"), vmem_limit_bytes=VMEM_LIMIT),
        name="mla_attention",
    )(q, k, v)


def _gelu_tanh(x):
    return 0.5 * x * (1.0 + jnp.tanh(math.sqrt(2.0 / math.pi) * (x + 0.044715 * (x * x * x))))


def _ssm_kernel(u_ref, bd_ref, ar_ref, ai_ref, cd_ref, d_ref, wglu_ref, y_ref,
                u_scr, x_scr, y_scr, sr_scr, si_scr):
    n = SSM_STATES
    batch = sr_scr.shape[0]
    lane_tiles = SSM_CH // LANES

    @pl.when(pl.program_id(0) == 0)
    def _():
        sr_scr[...] = jnp.zeros_like(sr_scr)
        si_scr[...] = jnp.zeros_like(si_scr)

    for b in range(batch):
        for lt in range(lane_tiles):
            c0 = b * SSM_CH + lt * LANES
            u_scr[lt, pl.ds(b, SSM_TS, stride=batch), :] = u_ref[:, c0:c0 + LANES]
    sub_t = SSM_TS // SSM_SUB
    sub_r = sub_t * batch
    subs = [slice(q * sub_r, (q + 1) * sub_r) for q in range(SSM_SUB)]
    us = [jnp.concatenate([u_scr[lt, rows, :] for lt in range(lane_tiles)], axis=-1) for rows in subs]
    for rows, u in zip(subs, us):
        u_bf = u.astype(BF16)
        x_scr[rows, :n] = _dot(u_bf, bd_ref[:, :n])
        x_scr[rows, n:] = _dot(u_bf, bd_ref[:, n:])
    ar = ar_ref[...]
    ai = ai_ref[...]
    xr, xi = sr_scr[...], si_scr[...]
    for rows, u in zip(subs, us):
        for t in range(sub_t):
            r = slice(rows.start + t * batch, rows.start + (t + 1) * batch)
            xr, xi = (ar * xr - ai * xi + x_scr[r, :n], ar * xi + ai * xr + x_scr[r, n:])
            x_scr[r, :n] = xr
            x_scr[r, n:] = xi
        mid = rows.start + sub_r // 2
        cx = jnp.concatenate([_dot(x_scr[rows.start:mid, :].astype(BF16), cd_ref[...]),
                              _dot(x_scr[mid:rows.stop, :].astype(BF16), cd_ref[...])], axis=0)
        y = cx + d_ref[...] * u
        y = _gelu_tanh(y)
        y = y * jax.nn.sigmoid(_dot(y.astype(BF16), wglu_ref[...]))
        for lt in range(lane_tiles):
            y_scr[lt, rows, :] = y[:, lt * LANES:(lt + 1) * LANES]
    sr_scr[...] = xr
    si_scr[...] = xi
    for b in range(batch):
        for lt in range(lane_tiles):
            c0 = b * SSM_CH + lt * LANES
            y_ref[:, c0:c0 + LANES] = y_scr[lt, pl.ds(b, SSM_TS, stride=batch), :].astype(BF16)


def _ssm(u, batch, l, bd, ar, ai, cd, d, wglu):
    s_, width = u.shape
    tr = SSM_TS * batch
    return pl.pallas_call(
        _ssm_kernel,
        grid=(s_ // SSM_TS,),
        in_specs=[pl.BlockSpec((SSM_TS, width), lambda i: (i, 0))]
        + [_layer_spec(a, l) for a in (bd, ar, ai, cd, d, wglu)],
        out_specs=pl.BlockSpec((SSM_TS, width), lambda i: (i, 0)),
        out_shape=jax.ShapeDtypeStruct((s_, width), BF16),
        scratch_shapes=[pltpu.VMEM((SSM_CH // LANES, tr, LANES), F32),
                        pltpu.VMEM((tr, 2 * SSM_STATES), F32),
                        pltpu.VMEM((SSM_CH // LANES, tr, LANES), F32),
                        pltpu.VMEM((batch, SSM_STATES), F32),
                        pltpu.VMEM((batch, SSM_STATES), F32)],
        compiler_params=pltpu.CompilerParams(
            dimension_semantics=("arbitrary",), vmem_limit_bytes=VMEM_LIMIT),
        name="s5_mixer",
    )(u, bd, ar, ai, cd, d, wglu)


def _ssm_params(a_re, a_im, log_dt, b_re, b_im, c_re, c_im, batch):
    lam = lax.complex(a_re.astype(F32), a_im.astype(F32))
    dt = jnp.exp(log_dt.astype(F32))
    a_bar = jnp.exp(lam * dt[:, None])
    b_bar = ((a_bar - 1.0) / lam)[..., None] * lax.complex(b_re.astype(F32), b_im.astype(F32))
    eye = jnp.eye(SSM_GROUPS, dtype=F32)

    def in_op(m):
        return jnp.einsum('gpc,gh->gchp', m, eye).reshape(SSM_CH, SSM_STATES)

    def out_op(m):
        return jnp.einsum('gcp,gh->gphc', m, eye).reshape(SSM_STATES, SSM_CH)

    bd = jnp.concatenate([in_op(jnp.real(b_bar)), in_op(jnp.imag(b_bar))], axis=1).astype(BF16)
    cd = jnp.concatenate([out_op(c_re.astype(F32)), -out_op(c_im.astype(F32))], axis=0).astype(BF16)
    ar = jnp.broadcast_to(jnp.real(a_bar).reshape(1, SSM_STATES), (batch, SSM_STATES))
    ai = jnp.broadcast_to(jnp.imag(a_bar).reshape(1, SSM_STATES), (batch, SSM_STATES))
    return bd, ar, ai, cd


def _ffn_kernel(x_ref, oa_ref, ob_ref, oc_ref, wo_ref, g2_ref, wg_ref, wu_ref, wd_ref, gf_ref, o_ref, h_scr,
                *, final_norm):
    wo_a = wo_ref[:SWA_Q, :]
    wo_b = wo_ref[SWA_Q:SWA_Q + SSM_CH, :]
    wo_c = wo_ref[SWA_Q + SSM_CH:, :]
    x = x_ref[...] + _dot(oa_ref[...], wo_a) + _dot(ob_ref[...], wo_b) + _dot(oc_ref[...], wo_c)
    o_ref[...] = x
    h_scr[...] = _rms(x, g2_ref[...]).astype(BF16)

    def chunk(c, carry):
        c0 = pl.multiple_of(c * FFN_CHUNK, LANES)
        h = h_scr[...]
        gate = _dot(h, wg_ref[:, pl.ds(c0, FFN_CHUNK)])
        up = _dot(h, wu_ref[:, pl.ds(c0, FFN_CHUNK)])
        act = (gate * jax.nn.sigmoid(gate) * up).astype(BF16)
        o_ref[...] += _dot(act, wd_ref[pl.ds(c0, FFN_CHUNK), :])
        return carry

    lax.fori_loop(0, D_FF // FFN_CHUNK, chunk, 0)
    if final_norm:
        o_ref[...] = _rms(o_ref[...], gf_ref[...])


def _ffn(x, oa, ob, oc, l, wo, g2, wg, wu, wd, gf, final_norm):
    b_, s_, _ = x.shape
    tm = FFN_TM
    row3 = lambda b, i: (b, i, 0)
    return pl.pallas_call(
        functools.partial(_ffn_kernel, final_norm=final_norm),
        grid=(b_, s_ // tm),
        in_specs=[pl.BlockSpec((None, tm, D_MODEL), row3),
                  pl.BlockSpec((None, tm, SWA_Q), row3),
                  pl.BlockSpec((tm, SSM_CH), lambda b, i: (i, b)),
                  pl.BlockSpec((None, tm, MLA_HEADS * MLA_V), row3),
                  _layer_spec(wo, l), _layer_spec(g2, l), _layer_spec(wg, l), _layer_spec(wu, l),
                  _layer_spec(wd, l),
                  pl.BlockSpec((1, D_MODEL), lambda b, i: (0, 0))],
        out_specs=pl.BlockSpec((None, tm, D_MODEL), row3),
        out_shape=jax.ShapeDtypeStruct(x.shape, F32),
        scratch_shapes=[pltpu.VMEM((tm, D_MODEL), BF16)],
        compiler_params=pltpu.CompilerParams(
            dimension_semantics=("arbitrary", "arbitrary"), vmem_limit_bytes=VMEM_LIMIT),
        name="out_proj_ffn",
    )(x, oa, ob, oc, wo, g2, wg, wu, wd, gf)


def _rot_half_cols(w):
    half = w.shape[-1] // 2
    return jnp.concatenate([-w[..., half:], w[..., :half]], axis=-1)


def _rope_tile(w):
    z = jnp.zeros(w.shape[:-1] + (MLA_NOPE,), w.dtype)
    z2 = jnp.zeros(w.shape[:-1] + (MLA_HEAD_PAD - MLA_NOPE - MLA_ROPE,), w.dtype)
    return jnp.concatenate([z, w, z2], axis=-1)


def _stacked_weights(w_in, w_q_up, w_kv_up):
    lead = w_in.shape[:-2]
    kr = w_in[..., -MLA_ROPE:]
    w_ext = jnp.concatenate([w_in[..., :-MLA_ROPE], _rope_tile(kr), _rope_tile(_rot_half_cols(kr))], axis=-1)
    qh = w_q_up.reshape(lead + (MLA_Q_RANK, MLA_HEADS, MLA_NOPE + MLA_ROPE))
    pad = jnp.zeros(lead + (MLA_Q_RANK, MLA_HEADS, MLA_HEAD_PAD - MLA_NOPE - MLA_ROPE), w_q_up.dtype)
    q_tiles = jnp.concatenate([qh, pad], axis=-1).reshape(lead + (MLA_Q_RANK, -1))
    q_rot = _rope_tile(_rot_half_cols(qh[..., MLA_NOPE:])).reshape(lead + (MLA_Q_RANK, -1))
    wq_ext = jnp.concatenate([q_tiles, q_rot], axis=-1)
    kvh = w_kv_up.reshape(lead + (MLA_KV_RANK, MLA_HEADS, MLA_NOPE + MLA_V))
    kpad = jnp.zeros(lead + (MLA_KV_RANK, MLA_HEADS, MLA_HEAD_PAD - MLA_NOPE), w_kv_up.dtype)
    k_tiles = jnp.concatenate([kvh[..., :MLA_NOPE], kpad], axis=-1).reshape(lead + (MLA_KV_RANK, -1))
    v_cols = kvh[..., MLA_NOPE:].reshape(lead + (MLA_KV_RANK, -1))
    wkv_ext = jnp.concatenate([k_tiles, v_cols], axis=-1)
    return w_ext.astype(BF16), wq_ext.astype(BF16), wkv_ext.astype(BF16)


def kernel(x, positions, rel_bias, ln1_g, w_in, sinks, ssm_a_re, ssm_a_im, ssm_log_dt, ssm_b_re, ssm_b_im,
           ssm_c_re, ssm_c_im, ssm_d, ssm_w_glu, mla_q_norm_g, mla_w_q_up, mla_kv_norm_g, mla_w_kv_up, w_out,
           ln2_g, w_gate, w_up, w_down, final_g):
    b_ = x.shape[0]
    cos_t, sin_t = _trig_tables(positions)
    bias2 = _band_bias_tables(rel_bias)
    row = lambda g: g[:, None, :]
    w_ext, wq_ext, wkv_ext = _stacked_weights(w_in, mla_w_q_up, mla_w_kv_up)
    bd, ar, ai, cd = jax.vmap(functools.partial(_ssm_params, batch=b_))(
        ssm_a_re, ssm_a_im, ssm_log_dt, ssm_b_re, ssm_b_im, ssm_c_re, ssm_c_im)
    wglu = ssm_w_glu.astype(BF16)
    wo, wg, wu, wd = (w.astype(BF16) for w in (w_out, w_gate, w_up, w_down))
    gf = final_g.reshape(1, D_MODEL)
    for l in range(DEPTH):
        qa, ka, va, u, qm, km, vm = _in_proj(x, l, row(ln1_g), w_ext, cos_t, sin_t,
                                             row(mla_q_norm_g), wq_ext, row(mla_kv_norm_g), wkv_ext)
        o_a = _swa(qa, ka, va, l, sinks, bias2)
        o_b = _ssm(u, b_, l, bd, ar, ai, cd, row(ssm_d), wglu)
        o_c = _mla(qm, km, vm)
        x = _ffn(x, o_a, o_b, o_c, l, wo, row(ln2_g), wg, wu, wd, gf, final_norm=(l == DEPTH - 1))
    return x
```

```python
import functools
import math

import jax
import jax.numpy as jnp
from jax import lax
from jax.experimental import pallas as pl
from jax.experimental.pallas import tpu as pltpu

D_MODEL = 1024
DEPTH = 2
HEAD_DIM = 64
SWA_Q_HEADS = 8
SWA_KV_HEADS = 2
SWA_BLOCK = 128
SWA_Q = SWA_Q_HEADS * HEAD_DIM
SWA_KV = SWA_KV_HEADS * HEAD_DIM
REL_BUCKETS = 32
REL_MAX_DIST = 128
SSM_CH = 256
SSM_GROUP = 16
SSM_GROUPS = SSM_CH // SSM_GROUP
SSM_STATE = 64
SSM_STATES = SSM_GROUPS * SSM_STATE
MLA_HEADS = 4
MLA_Q_RANK = 256
MLA_KV_RANK = 128
MLA_NOPE = 64
MLA_ROPE = 32
MLA_V = 64
ROPE_THETA = 10000.0
D_FF = 2816
EPS = 1e-6
NEG = -1e30

LANES = 128
SUBLANES = 8
MLA_HEAD_PAD = LANES
MLA_EXP2_SCALE = (MLA_NOPE + MLA_ROPE) ** -0.5 * math.log2(math.e)
VMEM_LIMIT = 56 * 1024 * 1024

IN_PROJ_TM = 512
SWA_TQ = 512
MLA_TQ = 512
MLA_TK = 512
SSM_TS = 128
SSM_SUB = 2
FFN_TM = 512
FFN_CHUNK = 1408

BF16 = jnp.bfloat16
F32 = jnp.float32


def _rms(x, g):
    return x * lax.rsqrt(jnp.mean(x * x, axis=-1, keepdims=True) + EPS) * g


def _dot(a, b):
    return jnp.dot(a, b, preferred_element_type=F32)


def _dot_nt(a, b):
    return lax.dot_general(a, b, (((1,), (1,)), ((), ())), preferred_element_type=F32)


def _layer_spec(arr, l):
    return pl.BlockSpec((None,) + arr.shape[1:], lambda *_: (l,) + (0,) * (arr.ndim - 1))


def _trig_kernel(pos_ref, freq_ref, cos_ref, sin_ref):
    ang = pos_ref[...].astype(F32) * freq_ref[...]
    cos_ref[...] = jnp.cos(ang)
    sin_ref[...] = jnp.sin(ang)


def _trig_tables(positions):
    rows = positions.size
    half = MLA_ROPE // 2
    inv_freq = jnp.power(ROPE_THETA, -jnp.arange(half, dtype=F32) * 2.0 / MLA_ROPE)
    freq = jnp.zeros((1, MLA_HEAD_PAD), F32).at[0, MLA_NOPE:MLA_NOPE + MLA_ROPE].set(jnp.tile(inv_freq, 2))
    tm = 2048
    return pl.pallas_call(
        _trig_kernel,
        grid=(rows // tm,),
        in_specs=[pl.BlockSpec((tm, 1), lambda i: (i, 0)),
                  pl.BlockSpec((1, MLA_HEAD_PAD), lambda i: (0, 0))],
        out_specs=[pl.BlockSpec((tm, MLA_HEAD_PAD), lambda i: (i, 0))] * 2,
        out_shape=[jax.ShapeDtypeStruct((rows, MLA_HEAD_PAD), F32)] * 2,
        name="rope_tables",
    )(positions.reshape(rows, 1), freq)


def _in_proj_kernel(x_ref, g_ref, w_ref, cos_ref, sin_ref, qg_ref, wq_ref, kvg_ref, wkv_ref,
                    qa_ref, ka_ref, va_ref, u_ref, qm_ref, km_ref, vm_ref):
    h = _rms(x_ref[...], g_ref[...]).astype(BF16)
    proj = _dot(h, w_ref[...])
    qa_ref[...] = (proj[:, :SWA_Q] * (HEAD_DIM ** -0.5)).astype(BF16)
    c = SWA_Q
    ka_ref[...] = proj[:, c:c + SWA_KV].astype(BF16)
    c += SWA_KV
    va_ref[...] = proj[:, c:c + SWA_KV].astype(BF16)
    c += SWA_KV
    u_ref[...] = proj[:, c:c + SSM_CH]
    c += SSM_CH
    c_q = proj[:, c:c + MLA_Q_RANK]
    c += MLA_Q_RANK
    c_kv = proj[:, c:c + MLA_KV_RANK]
    c += MLA_KV_RANK
    k_r = proj[:, c:c + LANES]
    k_r_rot = proj[:, c + LANES:c + 2 * LANES]
    cos = cos_ref[...]
    sin = sin_ref[...]
    width = MLA_HEADS * MLA_HEAD_PAD
    q2 = _dot(_rms(c_q, qg_ref[...]).astype(BF16), wq_ref[...])
    kv2 = _dot(_rms(c_kv, kvg_ref[...]).astype(BF16), wkv_ref[...])
    k_pe = k_r * cos + k_r_rot * sin
    low = lax.broadcasted_iota(jnp.int32, (cos.shape[0], LANES), 1) < MLA_V
    for hd in range(MLA_HEADS):
        sl = slice(hd * MLA_HEAD_PAD, (hd + 1) * MLA_HEAD_PAD)
        sl_rot = slice(width + hd * MLA_HEAD_PAD, width + (hd + 1) * MLA_HEAD_PAD)
        qm_ref[:, sl] = (q2[:, sl] * cos + q2[:, sl_rot] * sin).astype(BF16)
        km_ref[:, sl] = (kv2[:, sl] + k_pe).astype(BF16)
        pair = kv2[:, width + (hd // 2) * LANES:width + (hd // 2 + 1) * LANES]
        keep = low if hd % 2 == 0 else jnp.logical_not(low)
        vm_ref[:, sl] = jnp.where(keep, pair, 1.0).astype(BF16)


def _in_proj(x, l, g, w_ext, cos_t, sin_t, qg, wq_ext, kvg, wkv_ext):
    b_, s_, _ = x.shape
    tm = IN_PROJ_TM
    nt = s_ // tm
    row3 = lambda b, i: (b, i, 0)
    trig = pl.BlockSpec((tm, MLA_HEAD_PAD), lambda b, i: (b * nt + i, 0))
    mla_w = MLA_HEADS * MLA_HEAD_PAD
    return pl.pallas_call(
        _in_proj_kernel,
        grid=(b_, nt),
        in_specs=[pl.BlockSpec((None, tm, D_MODEL), row3),
                  _layer_spec(g, l), _layer_spec(w_ext, l),
                  trig, trig,
                  _layer_spec(qg, l), _layer_spec(wq_ext, l), _layer_spec(kvg, l), _layer_spec(wkv_ext, l)],
        out_specs=[pl.BlockSpec((None, tm, SWA_Q), row3),
                   pl.BlockSpec((None, tm, SWA_KV), row3),
                   pl.BlockSpec((None, tm, SWA_KV), row3),
                   pl.BlockSpec((tm, SSM_CH), lambda b, i: (i, b)),
                   pl.BlockSpec((None, tm, mla_w), row3),
                   pl.BlockSpec((None, tm, mla_w), row3),
                   pl.BlockSpec((None, tm, mla_w), row3)],
        out_shape=[jax.ShapeDtypeStruct((b_, s_, SWA_Q), BF16),
                   jax.ShapeDtypeStruct((b_, s_, SWA_KV), BF16),
                   jax.ShapeDtypeStruct((b_, s_, SWA_KV), BF16),
                   jax.ShapeDtypeStruct((s_, b_ * SSM_CH), F32),
                   jax.ShapeDtypeStruct((b_, s_, mla_w), BF16),
                   jax.ShapeDtypeStruct((b_, s_, mla_w), BF16),
                   jax.ShapeDtypeStruct((b_, s_, mla_w), BF16)],
        compiler_params=pltpu.CompilerParams(
            dimension_semantics=("arbitrary", "arbitrary"), vmem_limit_bytes=VMEM_LIMIT),
        name="in_proj",
    )(x, g, w_ext, cos_t, sin_t, qg, wq_ext, kvg, wkv_ext)


def _swa_kernel(sink_ref, q_ref, k_ref, v_ref, bias_ref, o_ref, *, layer):
    qi = pl.program_id(1)
    blocks = SWA_TQ // SWA_BLOCK
    lane = lax.broadcasted_iota(jnp.int32, (2 * SWA_BLOCK, LANES), 1)
    low = lane < HEAD_DIM
    out_low = lax.broadcasted_iota(jnp.int32, (SWA_BLOCK, LANES), 1) < HEAD_DIM
    for j in range(blocks):
        blk = qi * blocks + j
        start = pl.multiple_of(jnp.maximum(blk - 1, 0) * SWA_BLOCK, SWA_BLOCK)
        table = jnp.minimum(blk, 1)
        kb = k_ref[pl.ds(start, 2 * SWA_BLOCK), :]
        vb = v_ref[pl.ds(start, 2 * SWA_BLOCK), :]
        kb_sw = pltpu.roll(kb, HEAD_DIM, 1)
        vb_sw = pltpu.roll(vb, HEAD_DIM, 1)
        zero = jnp.zeros_like(kb)
        k_var = ((jnp.where(low, kb, zero), jnp.where(low, zero, kb_sw)),
                 (jnp.where(low, kb_sw, zero), jnp.where(low, zero, kb)))
        v_var = (jnp.where(low, vb, vb_sw), jnp.where(low, vb_sw, vb))
        rows = slice(j * SWA_BLOCK, (j + 1) * SWA_BLOCK)
        for i in range(SWA_Q // LANES):
            kvh = (2 * i) // (SWA_Q_HEADS // SWA_KV_HEADS)
            q = q_ref[rows, i * LANES:(i + 1) * LANES]
            halves = []
            for half in range(2):
                hq = 2 * i + half
                sink = sink_ref[layer, hq]
                s = _dot_nt(q, k_var[kvh][half]) + bias_ref[table, hq]
                m = jnp.maximum(jnp.max(s, axis=-1, keepdims=True), sink)
                p = jnp.exp(s - m)
                l = jnp.sum(p, axis=-1, keepdims=True) + jnp.exp(sink - m)
                halves.append(_dot(p.astype(BF16), v_var[kvh]) / l)
            o_ref[rows, i * LANES:(i + 1) * LANES] = jnp.where(out_low, halves[0], halves[1]).astype(BF16)


def _swa(q, k, v, l, sinks, bias2):
    b_, s_, _ = q.shape
    tq = SWA_TQ
    return pl.pallas_call(
        functools.partial(_swa_kernel, layer=l),
        grid=(b_, s_ // tq),
        in_specs=[pl.BlockSpec(memory_space=pltpu.SMEM),
                  pl.BlockSpec((None, tq, SWA_Q), lambda b, i: (b, i, 0)),
                  pl.BlockSpec((None, s_, SWA_KV), lambda b, i: (b, 0, 0)),
                  pl.BlockSpec((None, s_, SWA_KV), lambda b, i: (b, 0, 0)),
                  pl.BlockSpec(bias2.shape, lambda b, i: (0, 0, 0, 0))],
        out_specs=pl.BlockSpec((None, tq, SWA_Q), lambda b, i: (b, i, 0)),
        out_shape=jax.ShapeDtypeStruct((b_, s_, SWA_Q), BF16),
        compiler_params=pltpu.CompilerParams(
            dimension_semantics=("arbitrary", "arbitrary"), vmem_limit_bytes=VMEM_LIMIT),
        name="swa_attention",
    )(sinks, q, k, v, bias2)


def _t5_bucket(dist):
    n = jnp.maximum(dist, 0)
    max_exact = REL_BUCKETS // 2
    large = max_exact + (jnp.log(jnp.maximum(n, 1).astype(F32) / max_exact)
                         / math.log(REL_MAX_DIST / max_exact)
                         * (REL_BUCKETS - max_exact)).astype(jnp.int32)
    large = jnp.minimum(large, REL_BUCKETS - 1)
    return jnp.where(n < max_exact, n, large)


def _band_bias_tables(rel_bias):
    qi = jnp.arange(SWA_BLOCK)[:, None]
    kj = jnp.arange(2 * SWA_BLOCK)[None, :]
    dist = qi + SWA_BLOCK - kj
    onehot = (_t5_bucket(dist)[None] == jnp.arange(REL_BUCKETS)[:, None, None]).astype(F32)
    b = jnp.einsum('nh,nqk->hqk', rel_bias.astype(F32), onehot, precision=lax.Precision.HIGHEST)
    valid = (dist >= 0) & (dist < SWA_BLOCK)
    normal = jnp.where(valid[None], b, NEG)
    first = jnp.concatenate([normal[:, :, SWA_BLOCK:], jnp.full_like(normal[:, :, SWA_BLOCK:], NEG)], axis=-1)
    return jnp.stack([first, normal])


def _mla_kernel(q_ref, k_ref, v_ref, o_ref, s_scr, m_scr, acc_scr):
    p = pl.program_id(1)
    tq, tk = MLA_TQ, MLA_TK
    n = q_ref.shape[0] // tq
    row = lax.broadcasted_iota(jnp.int32, (tq, tk), 0)
    col = lax.broadcasted_iota(jnp.int32, (tq, tk), 1)
    causal = col <= row
    out_low = lax.broadcasted_iota(jnp.int32, (tq, LANES), 1) < MLA_V
    heads = [slice(hd * MLA_HEAD_PAD, (hd + 1) * MLA_HEAD_PAD) for hd in range(MLA_HEADS)]
    lane_tiles = [slice(t * LANES, (t + 1) * LANES) for t in range(tk // LANES)]
    q_tiles = (p, n - 1 - p)

    def slot(t):
        if t >= n // 2:
            return 1, t - p - 1, t == n
        first = t <= p
        return jnp.where(first, 0, 1), jnp.where(first, t, t - p - 1), jnp.logical_and(first, t == p)

    def rows(idx, size):
        return pl.ds(pl.multiple_of(idx * size, size), size)

    m_scr[...] = jnp.full(m_scr.shape, NEG, F32)
    acc_scr[...] = jnp.zeros(acc_scr.shape, F32)
    for t in range(n + 1):
        which, kt, diag = slot(t)
        qt = q_tiles[which] if isinstance(which, int) else jnp.where(which == 0, q_tiles[0], q_tiles[1])
        for hd in range(MLA_HEADS):
            s = _dot_nt(q_ref[rows(qt, tq), heads[hd]], k_ref[rows(kt, tk), heads[hd]])
            if diag is True:
                s = jnp.where(causal, s, NEG)
            elif diag is not False:
                s = jnp.where(jnp.logical_or(causal, jnp.logical_not(diag)), s, NEG)
            s_scr[hd, t] = s
            m_scr[which, hd] = functools.reduce(jnp.maximum, [s[:, lt] for lt in lane_tiles], m_scr[which, hd])
    for which in range(2):
        for hd in range(MLA_HEADS):
            m_scr[which, hd] = jnp.broadcast_to(jnp.max(m_scr[which, hd], axis=-1, keepdims=True), (tq, LANES))
    for t in range(n + 1):
        which, kt, _ = slot(t)
        for hd in range(MLA_HEADS):
            m = m_scr[which, hd]
            e = jnp.concatenate([jnp.exp2((s_scr[hd, t, :, lt] - m) * MLA_EXP2_SCALE).astype(BF16)
                                 for lt in lane_tiles], axis=-1)
            acc_scr[which, hd] += _dot(e, v_ref[rows(kt, tk), heads[hd]])
    for which in range(2):
        for t in range(MLA_HEADS // 2):
            lo, hi = acc_scr[which, 2 * t], acc_scr[which, 2 * t + 1]
            lo = lo / lo[:, MLA_V:MLA_V + 1]
            hi = hi / hi[:, 0:1]
            o_ref[rows(q_tiles[which], tq), t * LANES:(t + 1) * LANES] = jnp.where(out_low, lo, hi).astype(BF16)


def _mla(q, k, v):
    b_, s_, w = q.shape
    n = s_ // MLA_TQ
    whole = lambda width: pl.BlockSpec((None, s_, width), lambda b, p: (b, 0, 0))
    return pl.pallas_call(
        _mla_kernel,
        grid=(b_, n // 2),
        in_specs=[whole(w), whole(w), whole(w)],
        out_specs=whole(MLA_HEADS * MLA_V),
        out_shape=jax.ShapeDtypeStruct((b_, s_, MLA_HEADS * MLA_V), BF16),
        scratch_shapes=[pltpu.VMEM((MLA_HEADS, n + 1, MLA_TQ, MLA_TK), F32),
                        pltpu.VMEM((2, MLA_HEADS, MLA_TQ, LANES), F32),
                        pltpu.VMEM((2, MLA_HEADS, MLA_TQ, LANES), F32)],
        compiler_params=pltpu.CompilerParams(
            dimension_semantics=("arbitrary", "arbitrary"), vmem_limit_bytes=VMEM_LIMIT),
        name="mla_attention",
    )(q, k, v)


def _gelu_tanh(x):
    return 0.5 * x * (1.0 + jnp.tanh(math.sqrt(2.0 / math.pi) * (x + 0.044715 * (x * x * x))))


def _ssm_kernel(u_ref, bd_ref, ar_ref, ai_ref, cd_ref, d_ref, wglu_ref, y_ref,
                u_scr, x_scr, y_scr, sr_scr, si_scr):
    n = SSM_STATES
    batch = sr_scr.shape[0]
    lane_tiles = SSM_CH // LANES

    @pl.when(pl.program_id(0) == 0)
    def _():
        sr_scr[...] = jnp.zeros_like(sr_scr)
        si_scr[...] = jnp.zeros_like(si_scr)

    for b in range(batch):
        for lt in range(lane_tiles):
            c0 = b * SSM_CH + lt * LANES
            u_scr[lt, pl.ds(b, SSM_TS, stride=batch), :] = u_ref[:, c0:c0 + LANES]
    sub_t = SSM_TS // SSM_SUB
    sub_r = sub_t * batch
    subs = [slice(q * sub_r, (q + 1) * sub_r) for q in range(SSM_SUB)]
    us = [jnp.concatenate([u_scr[lt, rows, :] for lt in range(lane_tiles)], axis=-1) for rows in subs]
    for rows, u in zip(subs, us):
        u_bf = u.astype(BF16)
        x_scr[rows, :n] = _dot(u_bf, bd_ref[:, :n])
        x_scr[rows, n:] = _dot(u_bf, bd_ref[:, n:])
    ar = ar_ref[...]
    ai = ai_ref[...]
    xr, xi = sr_scr[...], si_scr[...]
    for rows, u in zip(subs, us):
        for t in range(sub_t):
            r = slice(rows.start + t * batch, rows.start + (t + 1) * batch)
            xr, xi = (ar * xr - ai * xi + x_scr[r, :n], ar * xi + ai * xr + x_scr[r, n:])
            x_scr[r, :n] = xr
            x_scr[r, n:] = xi
        mid = rows.start + sub_r // 2
        cx = jnp.concatenate([_dot(x_scr[rows.start:mid, :].astype(BF16), cd_ref[...]),
                              _dot(x_scr[mid:rows.stop, :].astype(BF16), cd_ref[...])], axis=0)
        y = cx + d_ref[...] * u
        y = _gelu_tanh(y)
        y = y * jax.nn.sigmoid(_dot(y.astype(BF16), wglu_ref[...]))
        for lt in range(lane_tiles):
            y_scr[lt, rows, :] = y[:, lt * LANES:(lt + 1) * LANES]
    sr_scr[...] = xr
    si_scr[...] = xi
    for b in range(batch):
        for lt in range(lane_tiles):
            c0 = b * SSM_CH + lt * LANES
            y_ref[:, c0:c0 + LANES] = y_scr[lt, pl.ds(b, SSM_TS, stride=batch), :].astype(BF16)


def _ssm(u, batch, l, bd, ar, ai, cd, d, wglu):
    s_, width = u.shape
    tr = SSM_TS * batch
    return pl.pallas_call(
        _ssm_kernel,
        grid=(s_ // SSM_TS,),
        in_specs=[pl.BlockSpec((SSM_TS, width), lambda i: (i, 0))]
        + [_layer_spec(a, l) for a in (bd, ar, ai, cd, d, wglu)],
        out_specs=pl.BlockSpec((SSM_TS, width), lambda i: (i, 0)),
        out_shape=jax.ShapeDtypeStruct((s_, width), BF16),
        scratch_shapes=[pltpu.VMEM((SSM_CH // LANES, tr, LANES), F32),
                        pltpu.VMEM((tr, 2 * SSM_STATES), F32),
                        pltpu.VMEM((SSM_CH // LANES, tr, LANES), F32),
                        pltpu.VMEM((batch, SSM_STATES), F32),
                        pltpu.VMEM((batch, SSM_STATES), F32)],
        compiler_params=pltpu.CompilerParams(
            dimension_semantics=("arbitrary",), vmem_limit_bytes=VMEM_LIMIT),
        name="s5_mixer",
    )(u, bd, ar, ai, cd, d, wglu)


def _ssm_params(a_re, a_im, log_dt, b_re, b_im, c_re, c_im, batch):
    lam = lax.complex(a_re.astype(F32), a_im.astype(F32))
    dt = jnp.exp(log_dt.astype(F32))
    a_bar = jnp.exp(lam * dt[:, None])
    b_bar = ((a_bar - 1.0) / lam)[..., None] * lax.complex(b_re.astype(F32), b_im.astype(F32))
    eye = jnp.eye(SSM_GROUPS, dtype=F32)

    def in_op(m):
        return jnp.einsum('gpc,gh->gchp', m, eye).reshape(SSM_CH, SSM_STATES)

    def out_op(m):
        return jnp.einsum('gcp,gh->gphc', m, eye).reshape(SSM_STATES, SSM_CH)

    bd = jnp.concatenate([in_op(jnp.real(b_bar)), in_op(jnp.imag(b_bar))], axis=1).astype(BF16)
    cd = jnp.concatenate([out_op(c_re.astype(F32)), -out_op(c_im.astype(F32))], axis=0).astype(BF16)
    ar = jnp.broadcast_to(jnp.real(a_bar).reshape(1, SSM_STATES), (batch, SSM_STATES))
    ai = jnp.broadcast_to(jnp.imag(a_bar).reshape(1, SSM_STATES), (batch, SSM_STATES))
    return bd, ar, ai, cd


def _ffn_kernel(x_ref, oa_ref, ob_ref, oc_ref, wo_ref, g2_ref, wg_ref, wu_ref, wd_ref, gf_ref, o_ref, h_scr,
                *, final_norm):
    wo_a = wo_ref[:SWA_Q, :]
    wo_b = wo_ref[SWA_Q:SWA_Q + SSM_CH, :]
    wo_c = wo_ref[SWA_Q + SSM_CH:, :]
    x = x_ref[...] + _dot(oa_ref[...], wo_a) + _dot(ob_ref[...], wo_b) + _dot(oc_ref[...], wo_c)
    o_ref[...] = x
    h_scr[...] = _rms(x, g2_ref[...]).astype(BF16)

    def chunk(c, carry):
        c0 = pl.multiple_of(c * FFN_CHUNK, LANES)
        h = h_scr[...]
        gate = _dot(h, wg_ref[:, pl.ds(c0, FFN_CHUNK)])
        up = _dot(h, wu_ref[:, pl.ds(c0, FFN_CHUNK)])
        act = (gate * jax.nn.sigmoid(gate) * up).astype(BF16)
        o_ref[...] += _dot(act, wd_ref[pl.ds(c0, FFN_CHUNK), :])
        return carry

    lax.fori_loop(0, D_FF // FFN_CHUNK, chunk, 0)
    if final_norm:
        o_ref[...] = _rms(o_ref[...], gf_ref[...])


def _ffn(x, oa, ob, oc, l, wo, g2, wg, wu, wd, gf, final_norm):
    b_, s_, _ = x.shape
    tm = FFN_TM
    row3 = lambda b, i: (b, i, 0)
    return pl.pallas_call(
        functools.partial(_ffn_kernel, final_norm=final_norm),
        grid=(b_, s_ // tm),
        in_specs=[pl.BlockSpec((None, tm, D_MODEL), row3),
                  pl.BlockSpec((None, tm, SWA_Q), row3),
                  pl.BlockSpec((tm, SSM_CH), lambda b, i: (i, b)),
                  pl.BlockSpec((None, tm, MLA_HEADS * MLA_V), row3),
                  _layer_spec(wo, l), _layer_spec(g2, l), _layer_spec(wg, l), _layer_spec(wu, l),
                  _layer_spec(wd, l),
                  pl.BlockSpec((1, D_MODEL), lambda b, i: (0, 0))],
        out_specs=pl.BlockSpec((None, tm, D_MODEL), row3),
        out_shape=jax.ShapeDtypeStruct(x.shape, F32),
        scratch_shapes=[pltpu.VMEM((tm, D_MODEL), BF16)],
        compiler_params=pltpu.CompilerParams(
            dimension_semantics=("arbitrary", "arbitrary"), vmem_limit_bytes=VMEM_LIMIT),
        name="out_proj_ffn",
    )(x, oa, ob, oc, wo, g2, wg, wu, wd, gf)


def _rot_half_cols(w):
    half = w.shape[-1] // 2
    return jnp.concatenate([-w[..., half:], w[..., :half]], axis=-1)


def _rope_tile(w):
    z = jnp.zeros(w.shape[:-1] + (MLA_NOPE,), w.dtype)
    z2 = jnp.zeros(w.shape[:-1] + (MLA_HEAD_PAD - MLA_NOPE - MLA_ROPE,), w.dtype)
    return jnp.concatenate([z, w, z2], axis=-1)


def _stacked_weights(w_in, w_q_up, w_kv_up):
    lead = w_in.shape[:-2]
    kr = w_in[..., -MLA_ROPE:]
    w_ext = jnp.concatenate([w_in[..., :-MLA_ROPE], _rope_tile(kr), _rope_tile(_rot_half_cols(kr))], axis=-1)
    qh = w_q_up.reshape(lead + (MLA_Q_RANK, MLA_HEADS, MLA_NOPE + MLA_ROPE))
    pad = jnp.zeros(lead + (MLA_Q_RANK, MLA_HEADS, MLA_HEAD_PAD - MLA_NOPE - MLA_ROPE), w_q_up.dtype)
    q_tiles = jnp.concatenate([qh, pad], axis=-1).reshape(lead + (MLA_Q_RANK, -1))
    q_rot = _rope_tile(_rot_half_cols(qh[..., MLA_NOPE:])).reshape(lead + (MLA_Q_RANK, -1))
    wq_ext = jnp.concatenate([q_tiles, q_rot], axis=-1)
    kvh = w_kv_up.reshape(lead + (MLA_KV_RANK, MLA_HEADS, MLA_NOPE + MLA_V))
    kpad = jnp.zeros(lead + (MLA_KV_RANK, MLA_HEADS, MLA_HEAD_PAD - MLA_NOPE), w_kv_up.dtype)
    k_tiles = jnp.concatenate([kvh[..., :MLA_NOPE], kpad], axis=-1).reshape(lead + (MLA_KV_RANK, -1))
    v_cols = kvh[..., MLA_NOPE:].reshape(lead + (MLA_KV_RANK, -1))
    wkv_ext = jnp.concatenate([k_tiles, v_cols], axis=-1)
    return w_ext.astype(BF16), wq_ext.astype(BF16), wkv_ext.astype(BF16)


def kernel(x, positions, rel_bias, ln1_g, w_in, sinks, ssm_a_re, ssm_a_im, ssm_log_dt, ssm_b_re, ssm_b_im,
           ssm_c_re, ssm_c_im, ssm_d, ssm_w_glu, mla_q_norm_g, mla_w_q_up, mla_kv_norm_g, mla_w_kv_up, w_out,
           ln2_g, w_gate, w_up, w_down, final_g):
    b_ = x.shape[0]
    cos_t, sin_t = _trig_tables(positions)
    bias2 = _band_bias_tables(rel_bias)
    row = lambda g: g[:, None, :]
    w_ext, wq_ext, wkv_ext = _stacked_weights(w_in, mla_w_q_up, mla_w_kv_up)
    bd, ar, ai, cd = jax.vmap(functools.partial(_ssm_params, batch=b_))(
        ssm_a_re, ssm_a_im, ssm_log_dt, ssm_b_re, ssm_b_im, ssm_c_re, ssm_c_im)
    wglu = ssm_w_glu.astype(BF16)
    wo, wg, wu, wd = (w.astype(BF16) for w in (w_out, w_gate, w_up, w_down))
    gf = final_g.reshape(1, D_MODEL)
    for l in range(DEPTH):
        qa, ka, va, u, qm, km, vm = _in_proj(x, l, row(ln1_g), w_ext, cos_t, sin_t,
                                             row(mla_q_norm_g), wq_ext, row(mla_kv_norm_g), wkv_ext)
        o_a = _swa(qa, ka, va, l, sinks, bias2)
        o_b = _ssm(u, b_, l, bd, ar, ai, cd, row(ssm_d), wglu)
        o_c = _mla(qm, km, vm)
        x = _ffn(x, o_a, o_b, o_c, l, wo, row(ln2_g), wg, wu, wd, gf, final_norm=(l == DEPTH - 1))
    return x
```

```python
import functools
import math

import jax
import jax.numpy as jnp
from jax import lax
from jax.experimental import pallas as pl
from jax.experimental.pallas import tpu as pltpu

D_MODEL = 1024
DEPTH = 2
HEAD_DIM = 64
SWA_Q_HEADS = 8
SWA_KV_HEADS = 2
SWA_BLOCK = 128
SWA_Q = SWA_Q_HEADS * HEAD_DIM
SWA_KV = SWA_KV_HEADS * HEAD_DIM
REL_BUCKETS = 32
REL_MAX_DIST = 128
SSM_CH = 256
SSM_GROUP = 16
SSM_GROUPS = SSM_CH // SSM_GROUP
SSM_STATE = 64
SSM_STATES = SSM_GROUPS * SSM_STATE
MLA_HEADS = 4
MLA_Q_RANK = 256
MLA_KV_RANK = 128
MLA_NOPE = 64
MLA_ROPE = 32
MLA_V = 64
ROPE_THETA = 10000.0
D_FF = 2816
EPS = 1e-6
NEG = -1e30

LANES = 128
SUBLANES = 8
MLA_HEAD_PAD = LANES
MLA_EXP2_SCALE = (MLA_NOPE + MLA_ROPE) ** -0.5 * math.log2(math.e)
VMEM_LIMIT = 56 * 1024 * 1024

IN_PROJ_TM = 512
SWA_TQ = 512
MLA_TQ = 512
MLA_TK = 512
SSM_TS = 128
SSM_SUB = 2
FFN_TM = 512
FFN_CHUNK = D_FF

BF16 = jnp.bfloat16
F32 = jnp.float32


def _rms(x, g):
    return x * lax.rsqrt(jnp.mean(x * x, axis=-1, keepdims=True) + EPS) * g


def _dot(a, b):
    return jnp.dot(a, b, preferred_element_type=F32)


def _dot_nt(a, b):
    return lax.dot_general(a, b, (((1,), (1,)), ((), ())), preferred_element_type=F32)


def _layer_spec(arr, l):
    return pl.BlockSpec((None,) + arr.shape[1:], lambda *_: (l,) + (0,) * (arr.ndim - 1))


def _trig_kernel(pos_ref, freq_ref, cos_ref, sin_ref):
    ang = pos_ref[...].astype(F32) * freq_ref[...]
    cos_ref[...] = jnp.cos(ang)
    sin_ref[...] = jnp.sin(ang)


def _trig_tables(positions):
    rows = positions.size
    half = MLA_ROPE // 2
    inv_freq = jnp.power(ROPE_THETA, -jnp.arange(half, dtype=F32) * 2.0 / MLA_ROPE)
    freq = jnp.zeros((1, MLA_HEAD_PAD), F32).at[0, MLA_NOPE:MLA_NOPE + MLA_ROPE].set(jnp.tile(inv_freq, 2))
    tm = 2048
    return pl.pallas_call(
        _trig_kernel,
        grid=(rows // tm,),
        in_specs=[pl.BlockSpec((tm, 1), lambda i: (i, 0)),
                  pl.BlockSpec((1, MLA_HEAD_PAD), lambda i: (0, 0))],
        out_specs=[pl.BlockSpec((tm, MLA_HEAD_PAD), lambda i: (i, 0))] * 2,
        out_shape=[jax.ShapeDtypeStruct((rows, MLA_HEAD_PAD), F32)] * 2,
        name="rope_tables",
    )(positions.reshape(rows, 1), freq)


def _in_proj_kernel(x_ref, g_ref, w_ref, cos_ref, sin_ref, qg_ref, wq_ref, kvg_ref, wkv_ref,
                    qa_ref, ka_ref, va_ref, u_ref, qm_ref, km_ref, vm_ref):
    h = _rms(x_ref[...], g_ref[...]).astype(BF16)
    proj = _dot(h, w_ref[...])
    qa_ref[...] = (proj[:, :SWA_Q] * (HEAD_DIM ** -0.5)).astype(BF16)
    c = SWA_Q
    ka_ref[...] = proj[:, c:c + SWA_KV].astype(BF16)
    c += SWA_KV
    va_ref[...] = proj[:, c:c + SWA_KV].astype(BF16)
    c += SWA_KV
    u_ref[...] = proj[:, c:c + SSM_CH]
    c += SSM_CH
    c_q = proj[:, c:c + MLA_Q_RANK]
    c += MLA_Q_RANK
    c_kv = proj[:, c:c + MLA_KV_RANK]
    c += MLA_KV_RANK
    k_r = proj[:, c:c + LANES]
    k_r_rot = proj[:, c + LANES:c + 2 * LANES]
    cos = cos_ref[...]
    sin = sin_ref[...]
    width = MLA_HEADS * MLA_HEAD_PAD
    q2 = _dot(_rms(c_q, qg_ref[...]).astype(BF16), wq_ref[...])
    kv2 = _dot(_rms(c_kv, kvg_ref[...]).astype(BF16), wkv_ref[...])
    k_pe = k_r * cos + k_r_rot * sin
    low = lax.broadcasted_iota(jnp.int32, (cos.shape[0], LANES), 1) < MLA_V
    for hd in range(MLA_HEADS):
        sl = slice(hd * MLA_HEAD_PAD, (hd + 1) * MLA_HEAD_PAD)
        sl_rot = slice(width + hd * MLA_HEAD_PAD, width + (hd + 1) * MLA_HEAD_PAD)
        qm_ref[:, sl] = (q2[:, sl] * cos + q2[:, sl_rot] * sin).astype(BF16)
        km_ref[:, sl] = (kv2[:, sl] + k_pe).astype(BF16)
        pair = kv2[:, width + (hd // 2) * LANES:width + (hd // 2 + 1) * LANES]
        keep = low if hd % 2 == 0 else jnp.logical_not(low)
        vm_ref[:, sl] = jnp.where(keep, pair, 1.0).astype(BF16)


def _in_proj(x, l, g, w_ext, cos_t, sin_t, qg, wq_ext, kvg, wkv_ext):
    b_, s_, _ = x.shape
    tm = IN_PROJ_TM
    nt = s_ // tm
    row3 = lambda b, i: (b, i, 0)
    trig = pl.BlockSpec((tm, MLA_HEAD_PAD), lambda b, i: (b * nt + i, 0))
    mla_w = MLA_HEADS * MLA_HEAD_PAD
    return pl.pallas_call(
        _in_proj_kernel,
        grid=(b_, nt),
        in_specs=[pl.BlockSpec((None, tm, D_MODEL), row3),
                  _layer_spec(g, l), _layer_spec(w_ext, l),
                  trig, trig,
                  _layer_spec(qg, l), _layer_spec(wq_ext, l), _layer_spec(kvg, l), _layer_spec(wkv_ext, l)],
        out_specs=[pl.BlockSpec((None, tm, SWA_Q), row3),
                   pl.BlockSpec((None, tm, SWA_KV), row3),
                   pl.BlockSpec((None, tm, SWA_KV), row3),
                   pl.BlockSpec((tm, SSM_CH), lambda b, i: (i, b)),
                   pl.BlockSpec((None, tm, mla_w), row3),
                   pl.BlockSpec((None, tm, mla_w), row3),
                   pl.BlockSpec((None, tm, mla_w), row3)],
        out_shape=[jax.ShapeDtypeStruct((b_, s_, SWA_Q), BF16),
                   jax.ShapeDtypeStruct((b_, s_, SWA_KV), BF16),
                   jax.ShapeDtypeStruct((b_, s_, SWA_KV), BF16),
                   jax.ShapeDtypeStruct((s_, b_ * SSM_CH), F32),
                   jax.ShapeDtypeStruct((b_, s_, mla_w), BF16),
                   jax.ShapeDtypeStruct((b_, s_, mla_w), BF16),
                   jax.ShapeDtypeStruct((b_, s_, mla_w), BF16)],
        compiler_params=pltpu.CompilerParams(
            dimension_semantics=("arbitrary", "arbitrary"), vmem_limit_bytes=VMEM_LIMIT),
        name="in_proj",
    )(x, g, w_ext, cos_t, sin_t, qg, wq_ext, kvg, wkv_ext)


def _swa_kernel(sink_ref, q_ref, k_ref, v_ref, bias_ref, o_ref, *, layer):
    qi = pl.program_id(1)
    blocks = SWA_TQ // SWA_BLOCK
    lane = lax.broadcasted_iota(jnp.int32, (2 * SWA_BLOCK, LANES), 1)
    low = lane < HEAD_DIM
    out_low = lax.broadcasted_iota(jnp.int32, (SWA_BLOCK, LANES), 1) < HEAD_DIM
    for j in range(blocks):
        blk = qi * blocks + j
        start = pl.multiple_of(jnp.maximum(blk - 1, 0) * SWA_BLOCK, SWA_BLOCK)
        table = jnp.minimum(blk, 1)
        kb = k_ref[pl.ds(start, 2 * SWA_BLOCK), :]
        vb = v_ref[pl.ds(start, 2 * SWA_BLOCK), :]
        kb_sw = pltpu.roll(kb, HEAD_DIM, 1)
        vb_sw = pltpu.roll(vb, HEAD_DIM, 1)
        zero = jnp.zeros_like(kb)
        k_var = ((jnp.where(low, kb, zero), jnp.where(low, zero, kb_sw)),
                 (jnp.where(low, kb_sw, zero), jnp.where(low, zero, kb)))
        v_var = (jnp.where(low, vb, vb_sw), jnp.where(low, vb_sw, vb))
        rows = slice(j * SWA_BLOCK, (j + 1) * SWA_BLOCK)
        for i in range(SWA_Q // LANES):
            kvh = (2 * i) // (SWA_Q_HEADS // SWA_KV_HEADS)
            q = q_ref[rows, i * LANES:(i + 1) * LANES]
            halves = []
            for half in range(2):
                hq = 2 * i + half
                sink = sink_ref[layer, hq]
                s = _dot_nt(q, k_var[kvh][half]) + bias_ref[table, hq]
                m = jnp.maximum(jnp.max(s, axis=-1, keepdims=True), sink)
                p = jnp.exp(s - m)
                l = jnp.sum(p, axis=-1, keepdims=True) + jnp.exp(sink - m)
                halves.append(_dot(p.astype(BF16), v_var[kvh]) / l)
            o_ref[rows, i * LANES:(i + 1) * LANES] = jnp.where(out_low, halves[0], halves[1]).astype(BF16)


def _swa(q, k, v, l, sinks, bias2):
    b_, s_, _ = q.shape
    tq = SWA_TQ
    return pl.pallas_call(
        functools.partial(_swa_kernel, layer=l),
        grid=(b_, s_ // tq),
        in_specs=[pl.BlockSpec(memory_space=pltpu.SMEM),
                  pl.BlockSpec((None, tq, SWA_Q), lambda b, i: (b, i, 0)),
                  pl.BlockSpec((None, s_, SWA_KV), lambda b, i: (b, 0, 0)),
                  pl.BlockSpec((None, s_, SWA_KV), lambda b, i: (b, 0, 0)),
                  pl.BlockSpec(bias2.shape, lambda b, i: (0, 0, 0, 0))],
        out_specs=pl.BlockSpec((None, tq, SWA_Q), lambda b, i: (b, i, 0)),
        out_shape=jax.ShapeDtypeStruct((b_, s_, SWA_Q), BF16),
        compiler_params=pltpu.CompilerParams(
            dimension_semantics=("arbitrary", "arbitrary"), vmem_limit_bytes=VMEM_LIMIT),
        name="swa_attention",
    )(sinks, q, k, v, bias2)


def _t5_bucket(dist):
    n = jnp.maximum(dist, 0)
    max_exact = REL_BUCKETS // 2
    large = max_exact + (jnp.log(jnp.maximum(n, 1).astype(F32) / max_exact)
                         / math.log(REL_MAX_DIST / max_exact)
                         * (REL_BUCKETS - max_exact)).astype(jnp.int32)
    large = jnp.minimum(large, REL_BUCKETS - 1)
    return jnp.where(n < max_exact, n, large)


def _band_bias_tables(rel_bias):
    qi = jnp.arange(SWA_BLOCK)[:, None]
    kj = jnp.arange(2 * SWA_BLOCK)[None, :]
    dist = qi + SWA_BLOCK - kj
    onehot = (_t5_bucket(dist)[None] == jnp.arange(REL_BUCKETS)[:, None, None]).astype(F32)
    b = jnp.einsum('nh,nqk->hqk', rel_bias.astype(F32), onehot, precision=lax.Precision.HIGHEST)
    valid = (dist >= 0) & (dist < SWA_BLOCK)
    normal = jnp.where(valid[None], b, NEG)
    first = jnp.concatenate([normal[:, :, SWA_BLOCK:], jnp.full_like(normal[:, :, SWA_BLOCK:], NEG)], axis=-1)
    return jnp.stack([first, normal])


def _mla_kernel(q_ref, k_ref, v_ref, o_ref, s_scr, m_scr, acc_scr):
    p = pl.program_id(1)
    tq, tk = MLA_TQ, MLA_TK
    n = q_ref.shape[0] // tq
    row = lax.broadcasted_iota(jnp.int32, (tq, tk), 0)
    col = lax.broadcasted_iota(jnp.int32, (tq, tk), 1)
    causal = col <= row
    out_low = lax.broadcasted_iota(jnp.int32, (tq, LANES), 1) < MLA_V
    heads = [slice(hd * MLA_HEAD_PAD, (hd + 1) * MLA_HEAD_PAD) for hd in range(MLA_HEADS)]
    lane_tiles = [slice(t * LANES, (t + 1) * LANES) for t in range(tk // LANES)]
    q_tiles = (p, n - 1 - p)

    def slot(t):
        if t >= n // 2:
            return 1, t - p - 1, t == n
        first = t <= p
        return jnp.where(first, 0, 1), jnp.where(first, t, t - p - 1), jnp.logical_and(first, t == p)

    def rows(idx, size):
        return pl.ds(pl.multiple_of(idx * size, size), size)

    m_scr[...] = jnp.full(m_scr.shape, NEG, F32)
    acc_scr[...] = jnp.zeros(acc_scr.shape, F32)
    for t in range(n + 1):
        which, kt, diag = slot(t)
        qt = q_tiles[which] if isinstance(which, int) else jnp.where(which == 0, q_tiles[0], q_tiles[1])
        for hd in range(MLA_HEADS):
            s = _dot_nt(q_ref[rows(qt, tq), heads[hd]], k_ref[rows(kt, tk), heads[hd]])
            if diag is True:
                s = jnp.where(causal, s, NEG)
            elif diag is not False:
                s = jnp.where(jnp.logical_or(causal, jnp.logical_not(diag)), s, NEG)
            s_scr[hd, t] = s
            m_scr[which, hd] = functools.reduce(jnp.maximum, [s[:, lt] for lt in lane_tiles], m_scr[which, hd])
    for which in range(2):
        for hd in range(MLA_HEADS):
            m_scr[which, hd] = jnp.broadcast_to(jnp.max(m_scr[which, hd], axis=-1, keepdims=True), (tq, LANES))
    for t in range(n + 1):
        which, kt, _ = slot(t)
        for hd in range(MLA_HEADS):
            m = m_scr[which, hd]
            e = jnp.concatenate([jnp.exp2((s_scr[hd, t, :, lt] - m) * MLA_EXP2_SCALE).astype(BF16)
                                 for lt in lane_tiles], axis=-1)
            acc_scr[which, hd] += _dot(e, v_ref[rows(kt, tk), heads[hd]])
    for which in range(2):
        for t in range(MLA_HEADS // 2):
            lo, hi = acc_scr[which, 2 * t], acc_scr[which, 2 * t + 1]
            lo = lo / lo[:, MLA_V:MLA_V + 1]
            hi = hi / hi[:, 0:1]
            o_ref[rows(q_tiles[which], tq), t * LANES:(t + 1) * LANES] = jnp.where(out_low, lo, hi).astype(BF16)


def _mla(q, k, v):
    b_, s_, w = q.shape
    n = s_ // MLA_TQ
    whole = lambda width: pl.BlockSpec((None, s_, width), lambda b, p: (b, 0, 0))
    return pl.pallas_call(
        _mla_kernel,
        grid=(b_, n // 2),
        in_specs=[whole(w), whole(w), whole(w)],
        out_specs=whole(MLA_HEADS * MLA_V),
        out_shape=jax.ShapeDtypeStruct((b_, s_, MLA_HEADS * MLA_V), BF16),
        scratch_shapes=[pltpu.VMEM((MLA_HEADS, n + 1, MLA_TQ, MLA_TK), F32),
                        pltpu.VMEM((2, MLA_HEADS, MLA_TQ, LANES), F32),
                        pltpu.VMEM((2, MLA_HEADS, MLA_TQ, LANES), F32)],
        compiler_params=pltpu.CompilerParams(
            dimension_semantics=("arbitrary", "arbitrary"), vmem_limit_bytes=VMEM_LIMIT),
        name="mla_attention",
    )(q, k, v)


def _gelu_tanh(x):
    return 0.5 * x * (1.0 + jnp.tanh(math.sqrt(2.0 / math.pi) * (x + 0.044715 * (x * x * x))))


def _ssm_kernel(u_ref, bd_ref, ar_ref, ai_ref, cd_ref, d_ref, wglu_ref, y_ref,
                u_scr, x_scr, y_scr, sr_scr, si_scr):
    n = SSM_STATES
    batch = sr_scr.shape[0]
    lane_tiles = SSM_CH // LANES

    @pl.when(pl.program_id(0) == 0)
    def _():
        sr_scr[...] = jnp.zeros_like(sr_scr)
        si_scr[...] = jnp.zeros_like(si_scr)

    for b in range(batch):
        for lt in range(lane_tiles):
            c0 = b * SSM_CH + lt * LANES
            u_scr[lt, pl.ds(b, SSM_TS, stride=batch), :] = u_ref[:, c0:c0 + LANES]
    sub_t = SSM_TS // SSM_SUB
    sub_r = sub_t * batch
    subs = [slice(q * sub_r, (q + 1) * sub_r) for q in range(SSM_SUB)]
    us = [jnp.concatenate([u_scr[lt, rows, :] for lt in range(lane_tiles)], axis=-1) for rows in subs]
    for rows, u in zip(subs, us):
        u_bf = u.astype(BF16)
        x_scr[rows, :n] = _dot(u_bf, bd_ref[:, :n])
        x_scr[rows, n:] = _dot(u_bf, bd_ref[:, n:])
    ar = ar_ref[...]
    ai = ai_ref[...]
    xr, xi = sr_scr[...], si_scr[...]
    for rows, u in zip(subs, us):
        for t in range(sub_t):
            r = slice(rows.start + t * batch, rows.start + (t + 1) * batch)
            xr, xi = (ar * xr - ai * xi + x_scr[r, :n], ar * xi + ai * xr + x_scr[r, n:])
            x_scr[r, :n] = xr
            x_scr[r, n:] = xi
        mid = rows.start + sub_r // 2
        cx = jnp.concatenate([_dot(x_scr[rows.start:mid, :].astype(BF16), cd_ref[...]),
                              _dot(x_scr[mid:rows.stop, :].astype(BF16), cd_ref[...])], axis=0)
        y = cx + d_ref[...] * u
        y = _gelu_tanh(y)
        y = y * jax.nn.sigmoid(_dot(y.astype(BF16), wglu_ref[...]))
        for lt in range(lane_tiles):
            y_scr[lt, rows, :] = y[:, lt * LANES:(lt + 1) * LANES]
    sr_scr[...] = xr
    si_scr[...] = xi
    for b in range(batch):
        for lt in range(lane_tiles):
            c0 = b * SSM_CH + lt * LANES
            y_ref[:, c0:c0 + LANES] = y_scr[lt, pl.ds(b, SSM_TS, stride=batch), :].astype(BF16)


def _ssm(u, batch, l, bd, ar, ai, cd, d, wglu):
    s_, width = u.shape
    tr = SSM_TS * batch
    return pl.pallas_call(
        _ssm_kernel,
        grid=(s_ // SSM_TS,),
        in_specs=[pl.BlockSpec((SSM_TS, width), lambda i: (i, 0))]
        + [_layer_spec(a, l) for a in (bd, ar, ai, cd, d, wglu)],
        out_specs=pl.BlockSpec((SSM_TS, width), lambda i: (i, 0)),
        out_shape=jax.ShapeDtypeStruct((s_, width), BF16),
        scratch_shapes=[pltpu.VMEM((SSM_CH // LANES, tr, LANES), F32),
                        pltpu.VMEM((tr, 2 * SSM_STATES), F32),
                        pltpu.VMEM((SSM_CH // LANES, tr, LANES), F32),
                        pltpu.VMEM((batch, SSM_STATES), F32),
                        pltpu.VMEM((batch, SSM_STATES), F32)],
        compiler_params=pltpu.CompilerParams(
            dimension_semantics=("arbitrary",), vmem_limit_bytes=VMEM_LIMIT),
        name="s5_mixer",
    )(u, bd, ar, ai, cd, d, wglu)


def _ssm_params(a_re, a_im, log_dt, b_re, b_im, c_re, c_im, batch):
    lam = lax.complex(a_re.astype(F32), a_im.astype(F32))
    dt = jnp.exp(log_dt.astype(F32))
    a_bar = jnp.exp(lam * dt[:, None])
    b_bar = ((a_bar - 1.0) / lam)[..., None] * lax.complex(b_re.astype(F32), b_im.astype(F32))
    eye = jnp.eye(SSM_GROUPS, dtype=F32)

    def in_op(m):
        return jnp.einsum('gpc,gh->gchp', m, eye).reshape(SSM_CH, SSM_STATES)

    def out_op(m):
        return jnp.einsum('gcp,gh->gphc', m, eye).reshape(SSM_STATES, SSM_CH)

    bd = jnp.concatenate([in_op(jnp.real(b_bar)), in_op(jnp.imag(b_bar))], axis=1).astype(BF16)
    cd = jnp.concatenate([out_op(c_re.astype(F32)), -out_op(c_im.astype(F32))], axis=0).astype(BF16)
    ar = jnp.broadcast_to(jnp.real(a_bar).reshape(1, SSM_STATES), (batch, SSM_STATES))
    ai = jnp.broadcast_to(jnp.imag(a_bar).reshape(1, SSM_STATES), (batch, SSM_STATES))
    return bd, ar, ai, cd


def _ffn_kernel(x_ref, oa_ref, ob_ref, oc_ref, wo_ref, g2_ref, wg_ref, wu_ref, wd_ref, gf_ref, o_ref, h_scr,
                *, final_norm):
    wo_a = wo_ref[:SWA_Q, :]
    wo_b = wo_ref[SWA_Q:SWA_Q + SSM_CH, :]
    wo_c = wo_ref[SWA_Q + SSM_CH:, :]
    x = x_ref[...] + _dot(oa_ref[...], wo_a) + _dot(ob_ref[...], wo_b) + _dot(oc_ref[...], wo_c)
    o_ref[...] = x
    h_scr[...] = _rms(x, g2_ref[...]).astype(BF16)

    for c0 in range(0, D_FF, FFN_CHUNK):
        h = h_scr[...]
        gate = _dot(h, wg_ref[:, c0:c0 + FFN_CHUNK])
        up = _dot(h, wu_ref[:, c0:c0 + FFN_CHUNK])
        act = (gate * jax.nn.sigmoid(gate) * up).astype(BF16)
        o_ref[...] += _dot(act, wd_ref[c0:c0 + FFN_CHUNK, :])
    if final_norm:
        o_ref[...] = _rms(o_ref[...], gf_ref[...])


def _ffn(x, oa, ob, oc, l, wo, g2, wg, wu, wd, gf, final_norm):
    b_, s_, _ = x.shape
    tm = FFN_TM
    row3 = lambda b, i: (b, i, 0)
    return pl.pallas_call(
        functools.partial(_ffn_kernel, final_norm=final_norm),
        grid=(b_, s_ // tm),
        in_specs=[pl.BlockSpec((None, tm, D_MODEL), row3),
                  pl.BlockSpec((None, tm, SWA_Q), row3),
                  pl.BlockSpec((tm, SSM_CH), lambda b, i: (i, b)),
                  pl.BlockSpec((None, tm, MLA_HEADS * MLA_V), row3),
                  _layer_spec(wo, l), _layer_spec(g2, l), _layer_spec(wg, l), _layer_spec(wu, l),
                  _layer_spec(wd, l),
                  pl.BlockSpec((1, D_MODEL), lambda b, i: (0, 0))],
        out_specs=pl.BlockSpec((None, tm, D_MODEL), row3),
        out_shape=jax.ShapeDtypeStruct(x.shape, F32),
        scratch_shapes=[pltpu.VMEM((tm, D_MODEL), BF16)],
        compiler_params=pltpu.CompilerParams(
            dimension_semantics=("arbitrary", "arbitrary"), vmem_limit_bytes=VMEM_LIMIT),
        name="out_proj_ffn",
    )(x, oa, ob, oc, wo, g2, wg, wu, wd, gf)


def _rot_half_cols(w):
    half = w.shape[-1] // 2
    return jnp.concatenate([-w[..., half:], w[..., :half]], axis=-1)


def _rope_tile(w):
    z = jnp.zeros(w.shape[:-1] + (MLA_NOPE,), w.dtype)
    z2 = jnp.zeros(w.shape[:-1] + (MLA_HEAD_PAD - MLA_NOPE - MLA_ROPE,), w.dtype)
    return jnp.concatenate([z, w, z2], axis=-1)


def _stacked_weights(w_in, w_q_up, w_kv_up):
    lead = w_in.shape[:-2]
    kr = w_in[..., -MLA_ROPE:]
    w_ext = jnp.concatenate([w_in[..., :-MLA_ROPE], _rope_tile(kr), _rope_tile(_rot_half_cols(kr))], axis=-1)
    qh = w_q_up.reshape(lead + (MLA_Q_RANK, MLA_HEADS, MLA_NOPE + MLA_ROPE))
    pad = jnp.zeros(lead + (MLA_Q_RANK, MLA_HEADS, MLA_HEAD_PAD - MLA_NOPE - MLA_ROPE), w_q_up.dtype)
    q_tiles = jnp.concatenate([qh, pad], axis=-1).reshape(lead + (MLA_Q_RANK, -1))
    q_rot = _rope_tile(_rot_half_cols(qh[..., MLA_NOPE:])).reshape(lead + (MLA_Q_RANK, -1))
    wq_ext = jnp.concatenate([q_tiles, q_rot], axis=-1)
    kvh = w_kv_up.reshape(lead + (MLA_KV_RANK, MLA_HEADS, MLA_NOPE + MLA_V))
    kpad = jnp.zeros(lead + (MLA_KV_RANK, MLA_HEADS, MLA_HEAD_PAD - MLA_NOPE), w_kv_up.dtype)
    k_tiles = jnp.concatenate([kvh[..., :MLA_NOPE], kpad], axis=-1).reshape(lead + (MLA_KV_RANK, -1))
    v_cols = kvh[..., MLA_NOPE:].reshape(lead + (MLA_KV_RANK, -1))
    wkv_ext = jnp.concatenate([k_tiles, v_cols], axis=-1)
    return w_ext.astype(BF16), wq_ext.astype(BF16), wkv_ext.astype(BF16)


def kernel(x, positions, rel_bias, ln1_g, w_in, sinks, ssm_a_re, ssm_a_im, ssm_log_dt, ssm_b_re, ssm_b_im,
           ssm_c_re, ssm_c_im, ssm_d, ssm_w_glu, mla_q_norm_g, mla_w_q_up, mla_kv_norm_g, mla_w_kv_up, w_out,
           ln2_g, w_gate, w_up, w_down, final_g):
    b_ = x.shape[0]
    cos_t, sin_t = _trig_tables(positions)
    bias2 = _band_bias_tables(rel_bias)
    row = lambda g: g[:, None, :]
    w_ext, wq_ext, wkv_ext = _stacked_weights(w_in, mla_w_q_up, mla_w_kv_up)
    bd, ar, ai, cd = jax.vmap(functools.partial(_ssm_params, batch=b_))(
        ssm_a_re, ssm_a_im, ssm_log_dt, ssm_b_re, ssm_b_im, ssm_c_re, ssm_c_im)
    wglu = ssm_w_glu.astype(BF16)
    wo, wg, wu, wd = (w.astype(BF16) for w in (w_out, w_gate, w_up, w_down))
    gf = final_g.reshape(1, D_MODEL)
    for l in range(DEPTH):
        qa, ka, va, u, qm, km, vm = _in_proj(x, l, row(ln1_g), w_ext, cos_t, sin_t,
                                             row(mla_q_norm_g), wq_ext, row(mla_kv_norm_g), wkv_ext)
        o_a = _swa(qa, ka, va, l, sinks, bias2)
        o_b = _ssm(u, b_, l, bd, ar, ai, cd, row(ssm_d), wglu)
        o_c = _mla(qm, km, vm)
        x = _ffn(x, o_a, o_b, o_c, l, wo, row(ln2_g), wg, wu, wd, gf, final_norm=(l == DEPTH - 1))
    return x
```

```python
import functools
import math

import jax
import jax.numpy as jnp
from jax import lax
from jax.experimental import pallas as pl
from jax.experimental.pallas import tpu as pltpu

D_MODEL = 1024
DEPTH = 2
HEAD_DIM = 64
SWA_Q_HEADS = 8
SWA_KV_HEADS = 2
SWA_BLOCK = 128
SWA_Q = SWA_Q_HEADS * HEAD_DIM
SWA_KV = SWA_KV_HEADS * HEAD_DIM
REL_BUCKETS = 32
REL_MAX_DIST = 128
SSM_CH = 256
SSM_GROUP = 16
SSM_GROUPS = SSM_CH // SSM_GROUP
SSM_STATE = 64
SSM_STATES = SSM_GROUPS * SSM_STATE
MLA_HEADS = 4
MLA_Q_RANK = 256
MLA_KV_RANK = 128
MLA_NOPE = 64
MLA_ROPE = 32
MLA_V = 64
ROPE_THETA = 10000.0
D_FF = 2816
EPS = 1e-6
NEG = -1e30

LANES = 128
SUBLANES = 8
MLA_HEAD_PAD = LANES
MLA_EXP2_SCALE = (MLA_NOPE + MLA_ROPE) ** -0.5 * math.log2(math.e)
VMEM_LIMIT = 56 * 1024 * 1024

IN_PROJ_TM = 512
SWA_TQ = 512
MLA_TQ = 512
MLA_TK = 512
SSM_TS = 128
SSM_SUB = 2
FFN_TM = 512
FFN_CHUNK = D_FF

BF16 = jnp.bfloat16
F32 = jnp.float32


def _rms(x, g):
    return x * lax.rsqrt(jnp.mean(x * x, axis=-1, keepdims=True) + EPS) * g


def _dot(a, b):
    return jnp.dot(a, b, preferred_element_type=F32)


def _dot_nt(a, b):
    return lax.dot_general(a, b, (((1,), (1,)), ((), ())), preferred_element_type=F32)


def _layer_spec(arr, l):
    return pl.BlockSpec((None,) + arr.shape[1:], lambda *_: (l,) + (0,) * (arr.ndim - 1))


def _trig_kernel(pos_ref, freq_ref, cos_ref, sin_ref):
    ang = pos_ref[...].astype(F32) * freq_ref[...]
    cos_ref[...] = jnp.cos(ang)
    sin_ref[...] = jnp.sin(ang)


def _trig_tables(positions):
    rows = positions.size
    half = MLA_ROPE // 2
    inv_freq = jnp.power(ROPE_THETA, -jnp.arange(half, dtype=F32) * 2.0 / MLA_ROPE)
    freq = jnp.zeros((1, MLA_HEAD_PAD), F32).at[0, MLA_NOPE:MLA_NOPE + MLA_ROPE].set(jnp.tile(inv_freq, 2))
    tm = 2048
    return pl.pallas_call(
        _trig_kernel,
        grid=(rows // tm,),
        in_specs=[pl.BlockSpec((tm, 1), lambda i: (i, 0)),
                  pl.BlockSpec((1, MLA_HEAD_PAD), lambda i: (0, 0))],
        out_specs=[pl.BlockSpec((tm, MLA_HEAD_PAD), lambda i: (i, 0))] * 2,
        out_shape=[jax.ShapeDtypeStruct((rows, MLA_HEAD_PAD), F32)] * 2,
        name="rope_tables",
    )(positions.reshape(rows, 1), freq)


def _in_proj_kernel(x_ref, g_ref, w_ref, cos_ref, sin_ref, qg_ref, wq_ref, kvg_ref, wkv_ref,
                    qa_ref, ka_ref, va_ref, u_ref, qm_ref, km_ref, vm_ref):
    h = _rms(x_ref[...], g_ref[...]).astype(BF16)
    proj = _dot(h, w_ref[...])
    qa_ref[...] = (proj[:, :SWA_Q] * (HEAD_DIM ** -0.5)).astype(BF16)
    c = SWA_Q
    ka_ref[...] = proj[:, c:c + SWA_KV].astype(BF16)
    c += SWA_KV
    va_ref[...] = proj[:, c:c + SWA_KV].astype(BF16)
    c += SWA_KV
    u_ref[...] = proj[:, c:c + SSM_CH]
    c += SSM_CH
    c_q = proj[:, c:c + MLA_Q_RANK]
    c += MLA_Q_RANK
    c_kv = proj[:, c:c + MLA_KV_RANK]
    c += MLA_KV_RANK
    k_r = proj[:, c:c + LANES]
    k_r_rot = proj[:, c + LANES:c + 2 * LANES]
    cos = cos_ref[...]
    sin = sin_ref[...]
    width = MLA_HEADS * MLA_HEAD_PAD
    q2 = _dot(_rms(c_q, qg_ref[...]).astype(BF16), wq_ref[...])
    kv2 = _dot(_rms(c_kv, kvg_ref[...]).astype(BF16), wkv_ref[...])
    k_pe = k_r * cos + k_r_rot * sin
    low = lax.broadcasted_iota(jnp.int32, (cos.shape[0], LANES), 1) < MLA_V
    for hd in range(MLA_HEADS):
        sl = slice(hd * MLA_HEAD_PAD, (hd + 1) * MLA_HEAD_PAD)
        sl_rot = slice(width + hd * MLA_HEAD_PAD, width + (hd + 1) * MLA_HEAD_PAD)
        qm_ref[:, sl] = (q2[:, sl] * cos + q2[:, sl_rot] * sin).astype(BF16)
        km_ref[:, sl] = (kv2[:, sl] + k_pe).astype(BF16)
        pair = kv2[:, width + (hd // 2) * LANES:width + (hd // 2 + 1) * LANES]
        keep = low if hd % 2 == 0 else jnp.logical_not(low)
        vm_ref[:, sl] = jnp.where(keep, pair, 1.0).astype(BF16)


def _in_proj(x, l, g, w_ext, cos_t, sin_t, qg, wq_ext, kvg, wkv_ext):
    b_, s_, _ = x.shape
    tm = IN_PROJ_TM
    nt = s_ // tm
    row3 = lambda b, i: (b, i, 0)
    trig = pl.BlockSpec((tm, MLA_HEAD_PAD), lambda b, i: (b * nt + i, 0))
    mla_w = MLA_HEADS * MLA_HEAD_PAD
    return pl.pallas_call(
        _in_proj_kernel,
        grid=(b_, nt),
        in_specs=[pl.BlockSpec((None, tm, D_MODEL), row3),
                  _layer_spec(g, l), _layer_spec(w_ext, l),
                  trig, trig,
                  _layer_spec(qg, l), _layer_spec(wq_ext, l), _layer_spec(kvg, l), _layer_spec(wkv_ext, l)],
        out_specs=[pl.BlockSpec((None, tm, SWA_Q), row3),
                   pl.BlockSpec((None, tm, SWA_KV), row3),
                   pl.BlockSpec((None, tm, SWA_KV), row3),
                   pl.BlockSpec((tm, SSM_CH), lambda b, i: (i, b)),
                   pl.BlockSpec((None, tm, mla_w), row3),
                   pl.BlockSpec((None, tm, mla_w), row3),
                   pl.BlockSpec((None, tm, mla_w), row3)],
        out_shape=[jax.ShapeDtypeStruct((b_, s_, SWA_Q), BF16),
                   jax.ShapeDtypeStruct((b_, s_, SWA_KV), BF16),
                   jax.ShapeDtypeStruct((b_, s_, SWA_KV), BF16),
                   jax.ShapeDtypeStruct((s_, b_ * SSM_CH), F32),
                   jax.ShapeDtypeStruct((b_, s_, mla_w), BF16),
                   jax.ShapeDtypeStruct((b_, s_, mla_w), BF16),
                   jax.ShapeDtypeStruct((b_, s_, mla_w), BF16)],
        compiler_params=pltpu.CompilerParams(
            dimension_semantics=("arbitrary", "arbitrary"), vmem_limit_bytes=VMEM_LIMIT),
        name="in_proj",
    )(x, g, w_ext, cos_t, sin_t, qg, wq_ext, kvg, wkv_ext)


def _swa_kernel(q_ref, k_ref, v_ref, bias_ref, o_ref, s_scr, m_scr):
    qi = pl.program_id(1)
    blocks = SWA_TQ // SWA_BLOCK
    tiles = SWA_Q // LANES
    band = 2 * SWA_BLOCK
    lane = lax.broadcasted_iota(jnp.int32, (band, LANES), 1)
    key = lax.broadcasted_iota(jnp.int32, (band, LANES), 0)
    low = lane < HEAD_DIM
    out_low = lax.broadcasted_iota(jnp.int32, (SWA_BLOCK, LANES), 1) < HEAD_DIM

    def band_of(ref, j):
        blk = qi * blocks + j
        start = pl.multiple_of(jnp.maximum(blk - 1, 0) * SWA_BLOCK, SWA_BLOCK)
        keep = key != jnp.where(blk == 0, band - 1, 0)
        t = ref[pl.ds(start, band), :]
        t = jnp.where(keep, t, jnp.zeros_like(t))
        return t, pltpu.roll(t, HEAD_DIM, 1)

    for j in range(blocks):
        table = jnp.minimum(qi * blocks + j, 1)
        kb, kb_sw = band_of(k_ref, j)
        zero = jnp.zeros_like(kb)
        k_var = ((jnp.where(low, kb, zero), jnp.where(low, zero, kb_sw)),
                 (jnp.where(low, kb_sw, zero), jnp.where(low, zero, kb)))
        rows = slice(j * SWA_BLOCK, (j + 1) * SWA_BLOCK)
        for i in range(tiles):
            kvh = (2 * i) // (SWA_Q_HEADS // SWA_KV_HEADS)
            q = q_ref[rows, i * LANES:(i + 1) * LANES]
            for half in range(2):
                u = (j * tiles + i) * 2 + half
                s = _dot_nt(q, k_var[kvh][half]) + bias_ref[table, 2 * i + half]
                s_scr[u] = s
                m_scr[u] = jnp.broadcast_to(jnp.max(s, axis=-1, keepdims=True), (SWA_BLOCK, LANES))
    for j in range(blocks):
        vb, vb_sw = band_of(v_ref, j)
        one = jnp.ones_like(vb)
        v_var = ((jnp.where(low, vb, one), jnp.where(low, one, vb_sw)),
                 (jnp.where(low, vb_sw, one), jnp.where(low, one, vb)))
        rows = slice(j * SWA_BLOCK, (j + 1) * SWA_BLOCK)
        for i in range(tiles):
            kvh = (2 * i) // (SWA_Q_HEADS // SWA_KV_HEADS)
            r = []
            for half in range(2):
                u = (j * tiles + i) * 2 + half
                m = m_scr[u]
                p = jnp.concatenate([jnp.exp(s_scr[u, :, t * LANES:(t + 1) * LANES] - m).astype(BF16)
                                     for t in range(band // LANES)], axis=-1)
                r.append(_dot(p, v_var[kvh][half]))
            denom = pltpu.roll(jnp.where(out_low, r[1], r[0]), HEAD_DIM, 1)
            o_ref[rows, i * LANES:(i + 1) * LANES] = (jnp.where(out_low, r[0], r[1]) / denom).astype(BF16)


def _swa(q, k, v, l, bias):
    b_, s_, _ = q.shape
    tq = SWA_TQ
    units = (tq // SWA_BLOCK) * SWA_Q_HEADS
    return pl.pallas_call(
        _swa_kernel,
        grid=(b_, s_ // tq),
        in_specs=[pl.BlockSpec((None, tq, SWA_Q), lambda b, i: (b, i, 0)),
                  pl.BlockSpec((None, s_, SWA_KV), lambda b, i: (b, 0, 0)),
                  pl.BlockSpec((None, s_, SWA_KV), lambda b, i: (b, 0, 0)),
                  _layer_spec(bias, l)],
        out_specs=pl.BlockSpec((None, tq, SWA_Q), lambda b, i: (b, i, 0)),
        out_shape=jax.ShapeDtypeStruct((b_, s_, SWA_Q), BF16),
        scratch_shapes=[pltpu.VMEM((units, SWA_BLOCK, 2 * SWA_BLOCK), F32),
                        pltpu.VMEM((units, SWA_BLOCK, LANES), F32)],
        compiler_params=pltpu.CompilerParams(
            dimension_semantics=("arbitrary", "arbitrary"), vmem_limit_bytes=VMEM_LIMIT),
        name="swa_attention",
    )(q, k, v, bias)


def _t5_bucket(dist):
    n = jnp.maximum(dist, 0)
    max_exact = REL_BUCKETS // 2
    large = max_exact + (jnp.log(jnp.maximum(n, 1).astype(F32) / max_exact)
                         / math.log(REL_MAX_DIST / max_exact)
                         * (REL_BUCKETS - max_exact)).astype(jnp.int32)
    large = jnp.minimum(large, REL_BUCKETS - 1)
    return jnp.where(n < max_exact, n, large)


def _band_bias_tables(rel_bias, sinks):
    qi = jnp.arange(SWA_BLOCK)[:, None]
    kj = jnp.arange(2 * SWA_BLOCK)[None, :]
    dist = qi + SWA_BLOCK - kj
    onehot = (_t5_bucket(dist)[None] == jnp.arange(REL_BUCKETS)[:, None, None]).astype(F32)
    b = jnp.einsum('nh,nqk->hqk', rel_bias.astype(F32), onehot, precision=lax.Precision.HIGHEST)
    valid = (dist >= 0) & (dist < SWA_BLOCK)
    normal = jnp.where(valid[None], b, NEG)
    first = jnp.concatenate([normal[:, :, SWA_BLOCK:], jnp.full_like(normal[:, :, SWA_BLOCK:], NEG)], axis=-1)
    sink = jnp.broadcast_to(sinks.astype(F32)[:, :, None], sinks.shape + (SWA_BLOCK,))
    shape = sinks.shape[:1] + normal.shape
    first = jnp.broadcast_to(first, shape).at[..., 2 * SWA_BLOCK - 1].set(sink)
    normal = jnp.broadcast_to(normal, shape).at[..., 0].set(sink)
    return jnp.stack([first, normal], axis=1)


def _mla_kernel(q_ref, k_ref, v_ref, o_ref, s_scr, m_scr, acc_scr):
    p = pl.program_id(1)
    tq, tk = MLA_TQ, MLA_TK
    n = q_ref.shape[0] // tq
    row = lax.broadcasted_iota(jnp.int32, (tq, tk), 0)
    col = lax.broadcasted_iota(jnp.int32, (tq, tk), 1)
    causal = col <= row
    out_low = lax.broadcasted_iota(jnp.int32, (tq, LANES), 1) < MLA_V
    heads = [slice(hd * MLA_HEAD_PAD, (hd + 1) * MLA_HEAD_PAD) for hd in range(MLA_HEADS)]
    lane_tiles = [slice(t * LANES, (t + 1) * LANES) for t in range(tk // LANES)]
    q_tiles = (p, n - 1 - p)

    def slot(t):
        if t >= n // 2:
            return 1, t - p - 1, t == n
        first = t <= p
        return jnp.where(first, 0, 1), jnp.where(first, t, t - p - 1), jnp.logical_and(first, t == p)

    def rows(idx, size):
        return pl.ds(pl.multiple_of(idx * size, size), size)

    m_scr[...] = jnp.full(m_scr.shape, NEG, F32)
    acc_scr[...] = jnp.zeros(acc_scr.shape, F32)
    for t in range(n + 1):
        which, kt, diag = slot(t)
        qt = q_tiles[which] if isinstance(which, int) else jnp.where(which == 0, q_tiles[0], q_tiles[1])
        for hd in range(MLA_HEADS):
            s = _dot_nt(q_ref[rows(qt, tq), heads[hd]], k_ref[rows(kt, tk), heads[hd]])
            if diag is True:
                s = jnp.where(causal, s, NEG)
            elif diag is not False:
                s = jnp.where(jnp.logical_or(causal, jnp.logical_not(diag)), s, NEG)
            s_scr[hd, t] = s
            m_scr[which, hd] = functools.reduce(jnp.maximum, [s[:, lt] for lt in lane_tiles], m_scr[which, hd])
    for which in range(2):
        for hd in range(MLA_HEADS):
            m_scr[which, hd] = jnp.broadcast_to(jnp.max(m_scr[which, hd], axis=-1, keepdims=True), (tq, LANES))
    for t in range(n + 1):
        which, kt, _ = slot(t)
        for hd in range(MLA_HEADS):
            m = m_scr[which, hd]
            e = jnp.concatenate([jnp.exp2((s_scr[hd, t, :, lt] - m) * MLA_EXP2_SCALE).astype(BF16)
                                 for lt in lane_tiles], axis=-1)
            acc_scr[which, hd] += _dot(e, v_ref[rows(kt, tk), heads[hd]])
    for which in range(2):
        for t in range(MLA_HEADS // 2):
            lo, hi = acc_scr[which, 2 * t], acc_scr[which, 2 * t + 1]
            lo = lo / lo[:, MLA_V:MLA_V + 1]
            hi = hi / hi[:, 0:1]
            o_ref[rows(q_tiles[which], tq), t * LANES:(t + 1) * LANES] = jnp.where(out_low, lo, hi).astype(BF16)


def _mla(q, k, v):
    b_, s_, w = q.shape
    n = s_ // MLA_TQ
    whole = lambda width: pl.BlockSpec((None, s_, width), lambda b, p: (b, 0, 0))
    return pl.pallas_call(
        _mla_kernel,
        grid=(b_, n // 2),
        in_specs=[whole(w), whole(w), whole(w)],
        out_specs=whole(MLA_HEADS * MLA_V),
        out_shape=jax.ShapeDtypeStruct((b_, s_, MLA_HEADS * MLA_V), BF16),
        scratch_shapes=[pltpu.VMEM((MLA_HEADS, n + 1, MLA_TQ, MLA_TK), F32),
                        pltpu.VMEM((2, MLA_HEADS, MLA_TQ, LANES), F32),
                        pltpu.VMEM((2, MLA_HEADS, MLA_TQ, LANES), F32)],
        compiler_params=pltpu.CompilerParams(
            dimension_semantics=("arbitrary", "arbitrary"), vmem_limit_bytes=VMEM_LIMIT),
        name="mla_attention",
    )(q, k, v)


def _gelu_tanh(x):
    return 0.5 * x * (1.0 + jnp.tanh(math.sqrt(2.0 / math.pi) * (x + 0.044715 * (x * x * x))))


def _ssm_kernel(u_ref, bd_ref, ar_ref, ai_ref, cd_ref, d_ref, wglu_ref, y_ref,
                u_scr, x_scr, y_scr, sr_scr, si_scr):
    n = SSM_STATES
    batch = sr_scr.shape[0]
    lane_tiles = SSM_CH // LANES

    @pl.when(pl.program_id(0) == 0)
    def _():
        sr_scr[...] = jnp.zeros_like(sr_scr)
        si_scr[...] = jnp.zeros_like(si_scr)

    for b in range(batch):
        for lt in range(lane_tiles):
            c0 = b * SSM_CH + lt * LANES
            u_scr[lt, pl.ds(b, SSM_TS, stride=batch), :] = u_ref[:, c0:c0 + LANES]
    sub_t = SSM_TS // SSM_SUB
    sub_r = sub_t * batch
    subs = [slice(q * sub_r, (q + 1) * sub_r) for q in range(SSM_SUB)]
    us = [jnp.concatenate([u_scr[lt, rows, :] for lt in range(lane_tiles)], axis=-1) for rows in subs]
    for rows, u in zip(subs, us):
        u_bf = u.astype(BF16)
        x_scr[rows, :n] = _dot(u_bf, bd_ref[:, :n])
        x_scr[rows, n:] = _dot(u_bf, bd_ref[:, n:])
    ar = ar_ref[...]
    ai = ai_ref[...]
    xr, xi = sr_scr[...], si_scr[...]
    for rows, u in zip(subs, us):
        for t in range(sub_t):
            r = slice(rows.start + t * batch, rows.start + (t + 1) * batch)
            xr, xi = (ar * xr - ai * xi + x_scr[r, :n], ar * xi + ai * xr + x_scr[r, n:])
            x_scr[r, :n] = xr
            x_scr[r, n:] = xi
        mid = rows.start + sub_r // 2
        cx = jnp.concatenate([_dot(x_scr[rows.start:mid, :].astype(BF16), cd_ref[...]),
                              _dot(x_scr[mid:rows.stop, :].astype(BF16), cd_ref[...])], axis=0)
        y = cx + d_ref[...] * u
        y = _gelu_tanh(y)
        y = y * jax.nn.sigmoid(_dot(y.astype(BF16), wglu_ref[...]))
        for lt in range(lane_tiles):
            y_scr[lt, rows, :] = y[:, lt * LANES:(lt + 1) * LANES]
    sr_scr[...] = xr
    si_scr[...] = xi
    for b in range(batch):
        for lt in range(lane_tiles):
            c0 = b * SSM_CH + lt * LANES
            y_ref[:, c0:c0 + LANES] = y_scr[lt, pl.ds(b, SSM_TS, stride=batch), :].astype(BF16)


def _ssm(u, batch, l, bd, ar, ai, cd, d, wglu):
    s_, width = u.shape
    tr = SSM_TS * batch
    return pl.pallas_call(
        _ssm_kernel,
        grid=(s_ // SSM_TS,),
        in_specs=[pl.BlockSpec((SSM_TS, width), lambda i: (i, 0))]
        + [_layer_spec(a, l) for a in (bd, ar, ai, cd, d, wglu)],
        out_specs=pl.BlockSpec((SSM_TS, width), lambda i: (i, 0)),
        out_shape=jax.ShapeDtypeStruct((s_, width), BF16),
        scratch_shapes=[pltpu.VMEM((SSM_CH // LANES, tr, LANES), F32),
                        pltpu.VMEM((tr, 2 * SSM_STATES), F32),
                        pltpu.VMEM((SSM_CH // LANES, tr, LANES), F32),
                        pltpu.VMEM((batch, SSM_STATES), F32),
                        pltpu.VMEM((batch, SSM_STATES), F32)],
        compiler_params=pltpu.CompilerParams(
            dimension_semantics=("arbitrary",), vmem_limit_bytes=VMEM_LIMIT),
        name="s5_mixer",
    )(u, bd, ar, ai, cd, d, wglu)


def _ssm_params(a_re, a_im, log_dt, b_re, b_im, c_re, c_im, batch):
    lam = lax.complex(a_re.astype(F32), a_im.astype(F32))
    dt = jnp.exp(log_dt.astype(F32))
    a_bar = jnp.exp(lam * dt[:, None])
    b_bar = ((a_bar - 1.0) / lam)[..., None] * lax.complex(b_re.astype(F32), b_im.astype(F32))
    eye = jnp.eye(SSM_GROUPS, dtype=F32)

    def in_op(m):
        return jnp.einsum('gpc,gh->gchp', m, eye).reshape(SSM_CH, SSM_STATES)

    def out_op(m):
        return jnp.einsum('gcp,gh->gphc', m, eye).reshape(SSM_STATES, SSM_CH)

    bd = jnp.concatenate([in_op(jnp.real(b_bar)), in_op(jnp.imag(b_bar))], axis=1).astype(BF16)
    cd = jnp.concatenate([out_op(c_re.astype(F32)), -out_op(c_im.astype(F32))], axis=0).astype(BF16)
    ar = jnp.broadcast_to(jnp.real(a_bar).reshape(1, SSM_STATES), (batch, SSM_STATES))
    ai = jnp.broadcast_to(jnp.imag(a_bar).reshape(1, SSM_STATES), (batch, SSM_STATES))
    return bd, ar, ai, cd


def _ffn_kernel(x_ref, oa_ref, ob_ref, oc_ref, wo_ref, g2_ref, wg_ref, wu_ref, wd_ref, gf_ref, o_ref, h_scr,
                *, final_norm):
    wo_a = wo_ref[:SWA_Q, :]
    wo_b = wo_ref[SWA_Q:SWA_Q + SSM_CH, :]
    wo_c = wo_ref[SWA_Q + SSM_CH:, :]
    x = x_ref[...] + _dot(oa_ref[...], wo_a) + _dot(ob_ref[...], wo_b) + _dot(oc_ref[...], wo_c)
    o_ref[...] = x
    h_scr[...] = _rms(x, g2_ref[...]).astype(BF16)

    for c0 in range(0, D_FF, FFN_CHUNK):
        h = h_scr[...]
        gate = _dot(h, wg_ref[:, c0:c0 + FFN_CHUNK])
        up = _dot(h, wu_ref[:, c0:c0 + FFN_CHUNK])
        act = (gate * jax.nn.sigmoid(gate) * up).astype(BF16)
        o_ref[...] += _dot(act, wd_ref[c0:c0 + FFN_CHUNK, :])
    if final_norm:
        o_ref[...] = _rms(o_ref[...], gf_ref[...])


def _ffn(x, oa, ob, oc, l, wo, g2, wg, wu, wd, gf, final_norm):
    b_, s_, _ = x.shape
    tm = FFN_TM
    row3 = lambda b, i: (b, i, 0)
    return pl.pallas_call(
        functools.partial(_ffn_kernel, final_norm=final_norm),
        grid=(b_, s_ // tm),
        in_specs=[pl.BlockSpec((None, tm, D_MODEL), row3),
                  pl.BlockSpec((None, tm, SWA_Q), row3),
                  pl.BlockSpec((tm, SSM_CH), lambda b, i: (i, b)),
                  pl.BlockSpec((None, tm, MLA_HEADS * MLA_V), row3),
                  _layer_spec(wo, l), _layer_spec(g2, l), _layer_spec(wg, l), _layer_spec(wu, l),
                  _layer_spec(wd, l),
                  pl.BlockSpec((1, D_MODEL), lambda b, i: (0, 0))],
        out_specs=pl.BlockSpec((None, tm, D_MODEL), row3),
        out_shape=jax.ShapeDtypeStruct(x.shape, F32),
        scratch_shapes=[pltpu.VMEM((tm, D_MODEL), BF16)],
        compiler_params=pltpu.CompilerParams(
            dimension_semantics=("arbitrary", "arbitrary"), vmem_limit_bytes=VMEM_LIMIT),
        name="out_proj_ffn",
    )(x, oa, ob, oc, wo, g2, wg, wu, wd, gf)


def _rot_half_cols(w):
    half = w.shape[-1] // 2
    return jnp.concatenate([-w[..., half:], w[..., :half]], axis=-1)


def _rope_tile(w):
    z = jnp.zeros(w.shape[:-1] + (MLA_NOPE,), w.dtype)
    z2 = jnp.zeros(w.shape[:-1] + (MLA_HEAD_PAD - MLA_NOPE - MLA_ROPE,), w.dtype)
    return jnp.concatenate([z, w, z2], axis=-1)


def _stacked_weights(w_in, w_q_up, w_kv_up):
    lead = w_in.shape[:-2]
    kr = w_in[..., -MLA_ROPE:]
    w_ext = jnp.concatenate([w_in[..., :-MLA_ROPE], _rope_tile(kr), _rope_tile(_rot_half_cols(kr))], axis=-1)
    qh = w_q_up.reshape(lead + (MLA_Q_RANK, MLA_HEADS, MLA_NOPE + MLA_ROPE))
    pad = jnp.zeros(lead + (MLA_Q_RANK, MLA_HEADS, MLA_HEAD_PAD - MLA_NOPE - MLA_ROPE), w_q_up.dtype)
    q_tiles = jnp.concatenate([qh, pad], axis=-1).reshape(lead + (MLA_Q_RANK, -1))
    q_rot = _rope_tile(_rot_half_cols(qh[..., MLA_NOPE:])).reshape(lead + (MLA_Q_RANK, -1))
    wq_ext = jnp.concatenate([q_tiles, q_rot], axis=-1)
    kvh = w_kv_up.reshape(lead + (MLA_KV_RANK, MLA_HEADS, MLA_NOPE + MLA_V))
    kpad = jnp.zeros(lead + (MLA_KV_RANK, MLA_HEADS, MLA_HEAD_PAD - MLA_NOPE), w_kv_up.dtype)
    k_tiles = jnp.concatenate([kvh[..., :MLA_NOPE], kpad], axis=-1).reshape(lead + (MLA_KV_RANK, -1))
    v_cols = kvh[..., MLA_NOPE:].reshape(lead + (MLA_KV_RANK, -1))
    wkv_ext = jnp.concatenate([k_tiles, v_cols], axis=-1)
    return w_ext.astype(BF16), wq_ext.astype(BF16), wkv_ext.astype(BF16)


def kernel(x, positions, rel_bias, ln1_g, w_in, sinks, ssm_a_re, ssm_a_im, ssm_log_dt, ssm_b_re, ssm_b_im,
           ssm_c_re, ssm_c_im, ssm_d, ssm_w_glu, mla_q_norm_g, mla_w_q_up, mla_kv_norm_g, mla_w_kv_up, w_out,
           ln2_g, w_gate, w_up, w_down, final_g):
    b_ = x.shape[0]
    cos_t, sin_t = _trig_tables(positions)
    bias = _band_bias_tables(rel_bias, sinks)
    row = lambda g: g[:, None, :]
    w_ext, wq_ext, wkv_ext = _stacked_weights(w_in, mla_w_q_up, mla_w_kv_up)
    bd, ar, ai, cd = jax.vmap(functools.partial(_ssm_params, batch=b_))(
        ssm_a_re, ssm_a_im, ssm_log_dt, ssm_b_re, ssm_b_im, ssm_c_re, ssm_c_im)
    wglu = ssm_w_glu.astype(BF16)
    wo, wg, wu, wd = (w.astype(BF16) for w in (w_out, w_gate, w_up, w_down))
    gf = final_g.reshape(1, D_MODEL)
    for l in range(DEPTH):
        qa, ka, va, u, qm, km, vm = _in_proj(x, l, row(ln1_g), w_ext, cos_t, sin_t,
                                             row(mla_q_norm_g), wq_ext, row(mla_kv_norm_g), wkv_ext)
        o_a = _swa(qa, ka, va, l, bias)
        o_b = _ssm(u, b_, l, bd, ar, ai, cd, row(ssm_d), wglu)
        o_c = _mla(qm, km, vm)
        x = _ffn(x, o_a, o_b, o_c, l, wo, row(ln2_g), wg, wu, wd, gf, final_norm=(l == DEPTH - 1))
    return x
```

```python
import functools
import math

import jax
import jax.numpy as jnp
from jax import lax
from jax.experimental import pallas as pl
from jax.experimental.pallas import tpu as pltpu

D_MODEL = 1024
DEPTH = 2
HEAD_DIM = 64
SWA_Q_HEADS = 8
SWA_KV_HEADS = 2
SWA_BLOCK = 128
SWA_Q = SWA_Q_HEADS * HEAD_DIM
SWA_KV = SWA_KV_HEADS * HEAD_DIM
REL_BUCKETS = 32
REL_MAX_DIST = 128
SSM_CH = 256
SSM_GROUP = 16
SSM_GROUPS = SSM_CH // SSM_GROUP
SSM_STATE = 64
SSM_STATES = SSM_GROUPS * SSM_STATE
MLA_HEADS = 4
MLA_Q_RANK = 256
MLA_KV_RANK = 128
MLA_NOPE = 64
MLA_ROPE = 32
MLA_V = 64
ROPE_THETA = 10000.0
D_FF = 2816
EPS = 1e-6
NEG = -1e30

LANES = 128
SUBLANES = 8
MLA_HEAD_PAD = LANES
MLA_EXP2_SCALE = (MLA_NOPE + MLA_ROPE) ** -0.5 * math.log2(math.e)
VMEM_LIMIT = 56 * 1024 * 1024

IN_PROJ_TM = 512
SWA_TQ = 512
MLA_TQ = 512
MLA_TK = 512
SSM_TS = 128
SSM_SUB = 2
FFN_TM = 512
FFN_CHUNK = D_FF

BF16 = jnp.bfloat16
F32 = jnp.float32


def _rms(x, g):
    return x * lax.rsqrt(jnp.mean(x * x, axis=-1, keepdims=True) + EPS) * g


def _dot(a, b):
    return jnp.dot(a, b, preferred_element_type=F32)


def _dot_nt(a, b):
    return lax.dot_general(a, b, (((1,), (1,)), ((), ())), preferred_element_type=F32)


def _layer_spec(arr, l):
    return pl.BlockSpec((None,) + arr.shape[1:], lambda *_: (l,) + (0,) * (arr.ndim - 1))


def _trig_kernel(pos_ref, freq_ref, place_ref, one_ref, cos_ref, sin_ref):
    pos = pos_ref[pl.ds(pl.program_id(0), 1), :].astype(F32)
    ang = freq_ref[...] * pos

    def place(t):
        hi = t.astype(BF16)
        rest = t - hi.astype(F32)
        mid = rest.astype(BF16)
        lo = (rest - mid.astype(F32)).astype(BF16)
        return sum(lax.dot_general(piece, place_ref[...], (((0,), (0,)), ((), ())), preferred_element_type=F32)
                   for piece in (hi, mid, lo))

    cos_ref[...] = place(jnp.cos(ang)) + one_ref[...]
    sin_ref[...] = place(jnp.sin(ang))


def _trig_tables(positions):
    b_, s_ = positions.shape
    half = MLA_ROPE // 2
    inv_freq = jnp.power(ROPE_THETA, -jnp.arange(half, dtype=F32) * 2.0 / MLA_ROPE).reshape(half, 1)
    lane = jnp.arange(MLA_HEAD_PAD)[None, :]
    rope_lane = (lane >= MLA_NOPE) & (lane < MLA_NOPE + MLA_ROPE)
    placement = (rope_lane & ((lane - MLA_NOPE) % half == jnp.arange(half)[:, None])).astype(BF16)
    off_rope = 1.0 - rope_lane.astype(F32)
    const = lambda b: (0, 0)
    cos_t, sin_t = pl.pallas_call(
        _trig_kernel,
        grid=(b_,),
        in_specs=[pl.BlockSpec((b_, s_), const),
                  pl.BlockSpec((half, 1), const),
                  pl.BlockSpec((half, MLA_HEAD_PAD), const),
                  pl.BlockSpec((1, MLA_HEAD_PAD), const)],
        out_specs=[pl.BlockSpec((None, s_, MLA_HEAD_PAD), lambda b: (b, 0, 0))] * 2,
        out_shape=[jax.ShapeDtypeStruct((b_, s_, MLA_HEAD_PAD), F32)] * 2,
        name="rope_tables",
    )(positions, inv_freq, placement, off_rope)
    return cos_t.reshape(b_ * s_, MLA_HEAD_PAD), sin_t.reshape(b_ * s_, MLA_HEAD_PAD)


def _in_proj_kernel(x_ref, g_ref, w_ref, cos_ref, sin_ref, qg_ref, wq_ref, kvg_ref, wkv_ref,
                    qa_ref, ka_ref, va_ref, u_ref, qm_ref, km_ref, vm_ref):
    h = _rms(x_ref[...], g_ref[...]).astype(BF16)
    proj = _dot(h, w_ref[...])
    qa_ref[...] = (proj[:, :SWA_Q] * (HEAD_DIM ** -0.5)).astype(BF16)
    c = SWA_Q
    ka_ref[...] = proj[:, c:c + SWA_KV].astype(BF16)
    c += SWA_KV
    va_ref[...] = proj[:, c:c + SWA_KV].astype(BF16)
    c += SWA_KV
    u_ref[...] = proj[:, c:c + SSM_CH]
    c += SSM_CH
    c_q = proj[:, c:c + MLA_Q_RANK]
    c += MLA_Q_RANK
    c_kv = proj[:, c:c + MLA_KV_RANK]
    c += MLA_KV_RANK
    k_r = proj[:, c:c + LANES]
    k_r_rot = proj[:, c + LANES:c + 2 * LANES]
    cos = cos_ref[...]
    sin = sin_ref[...]
    width = MLA_HEADS * MLA_HEAD_PAD
    q2 = _dot(_rms(c_q, qg_ref[...]).astype(BF16), wq_ref[...])
    kv2 = _dot(_rms(c_kv, kvg_ref[...]).astype(BF16), wkv_ref[...])
    k_pe = k_r * cos + k_r_rot * sin
    low = lax.broadcasted_iota(jnp.int32, (cos.shape[0], LANES), 1) < MLA_V
    for hd in range(MLA_HEADS):
        sl = slice(hd * MLA_HEAD_PAD, (hd + 1) * MLA_HEAD_PAD)
        sl_rot = slice(width + hd * MLA_HEAD_PAD, width + (hd + 1) * MLA_HEAD_PAD)
        qm_ref[:, sl] = (q2[:, sl] * cos + q2[:, sl_rot] * sin).astype(BF16)
        km_ref[:, sl] = (kv2[:, sl] + k_pe).astype(BF16)
        pair = kv2[:, width + (hd // 2) * LANES:width + (hd // 2 + 1) * LANES]
        keep = low if hd % 2 == 0 else jnp.logical_not(low)
        vm_ref[:, sl] = jnp.where(keep, pair, 1.0).astype(BF16)


def _in_proj(x, l, g, w_ext, cos_t, sin_t, qg, wq_ext, kvg, wkv_ext):
    b_, s_, _ = x.shape
    tm = IN_PROJ_TM
    nt = s_ // tm
    row3 = lambda b, i: (b, i, 0)
    trig = pl.BlockSpec((tm, MLA_HEAD_PAD), lambda b, i: (b * nt + i, 0))
    mla_w = MLA_HEADS * MLA_HEAD_PAD
    return pl.pallas_call(
        _in_proj_kernel,
        grid=(b_, nt),
        in_specs=[pl.BlockSpec((None, tm, D_MODEL), row3),
                  _layer_spec(g, l), _layer_spec(w_ext, l),
                  trig, trig,
                  _layer_spec(qg, l), _layer_spec(wq_ext, l), _layer_spec(kvg, l), _layer_spec(wkv_ext, l)],
        out_specs=[pl.BlockSpec((None, tm, SWA_Q), row3),
                   pl.BlockSpec((None, tm, SWA_KV), row3),
                   pl.BlockSpec((None, tm, SWA_KV), row3),
                   pl.BlockSpec((tm, SSM_CH), lambda b, i: (i, b)),
                   pl.BlockSpec((None, tm, mla_w), row3),
                   pl.BlockSpec((None, tm, mla_w), row3),
                   pl.BlockSpec((None, tm, mla_w), row3)],
        out_shape=[jax.ShapeDtypeStruct((b_, s_, SWA_Q), BF16),
                   jax.ShapeDtypeStruct((b_, s_, SWA_KV), BF16),
                   jax.ShapeDtypeStruct((b_, s_, SWA_KV), BF16),
                   jax.ShapeDtypeStruct((s_, b_ * SSM_CH), F32),
                   jax.ShapeDtypeStruct((b_, s_, mla_w), BF16),
                   jax.ShapeDtypeStruct((b_, s_, mla_w), BF16),
                   jax.ShapeDtypeStruct((b_, s_, mla_w), BF16)],
        compiler_params=pltpu.CompilerParams(
            dimension_semantics=("arbitrary", "arbitrary"), vmem_limit_bytes=VMEM_LIMIT),
        name="in_proj",
    )(x, g, w_ext, cos_t, sin_t, qg, wq_ext, kvg, wkv_ext)


def _swa_kernel(q_ref, k_ref, v_ref, bias_ref, o_ref, s_scr, m_scr):
    qi = pl.program_id(1)
    blocks = SWA_TQ // SWA_BLOCK
    tiles = SWA_Q // LANES
    band = 2 * SWA_BLOCK
    lane = lax.broadcasted_iota(jnp.int32, (band, LANES), 1)
    key = lax.broadcasted_iota(jnp.int32, (band, LANES), 0)
    low = lane < HEAD_DIM
    out_low = lax.broadcasted_iota(jnp.int32, (SWA_BLOCK, LANES), 1) < HEAD_DIM

    def band_of(ref, j):
        blk = qi * blocks + j
        start = pl.multiple_of(jnp.maximum(blk - 1, 0) * SWA_BLOCK, SWA_BLOCK)
        keep = key != jnp.where(blk == 0, band - 1, 0)
        t = ref[pl.ds(start, band), :]
        t = jnp.where(keep, t, jnp.zeros_like(t))
        return t, pltpu.roll(t, HEAD_DIM, 1)

    for j in range(blocks):
        table = jnp.minimum(qi * blocks + j, 1)
        kb, kb_sw = band_of(k_ref, j)
        zero = jnp.zeros_like(kb)
        k_var = ((jnp.where(low, kb, zero), jnp.where(low, zero, kb_sw)),
                 (jnp.where(low, kb_sw, zero), jnp.where(low, zero, kb)))
        rows = slice(j * SWA_BLOCK, (j + 1) * SWA_BLOCK)
        for i in range(tiles):
            kvh = (2 * i) // (SWA_Q_HEADS // SWA_KV_HEADS)
            q = q_ref[rows, i * LANES:(i + 1) * LANES]
            for half in range(2):
                u = (j * tiles + i) * 2 + half
                s = _dot_nt(q, k_var[kvh][half]) + bias_ref[table, 2 * i + half]
                s_scr[u] = s
                m_scr[u] = jnp.broadcast_to(jnp.max(s, axis=-1, keepdims=True), (SWA_BLOCK, LANES))
    for j in range(blocks):
        vb, vb_sw = band_of(v_ref, j)
        one = jnp.ones_like(vb)
        v_var = ((jnp.where(low, vb, one), jnp.where(low, one, vb_sw)),
                 (jnp.where(low, vb_sw, one), jnp.where(low, one, vb)))
        rows = slice(j * SWA_BLOCK, (j + 1) * SWA_BLOCK)
        for i in range(tiles):
            kvh = (2 * i) // (SWA_Q_HEADS // SWA_KV_HEADS)
            r = []
            for half in range(2):
                u = (j * tiles + i) * 2 + half
                m = m_scr[u]
                p = jnp.concatenate([jnp.exp(s_scr[u, :, t * LANES:(t + 1) * LANES] - m).astype(BF16)
                                     for t in range(band // LANES)], axis=-1)
                r.append(_dot(p, v_var[kvh][half]))
            denom = pltpu.roll(jnp.where(out_low, r[1], r[0]), HEAD_DIM, 1)
            o_ref[rows, i * LANES:(i + 1) * LANES] = (jnp.where(out_low, r[0], r[1]) / denom).astype(BF16)


def _swa(q, k, v, l, bias):
    b_, s_, _ = q.shape
    tq = SWA_TQ
    units = (tq // SWA_BLOCK) * SWA_Q_HEADS
    return pl.pallas_call(
        _swa_kernel,
        grid=(b_, s_ // tq),
        in_specs=[pl.BlockSpec((None, tq, SWA_Q), lambda b, i: (b, i, 0)),
                  pl.BlockSpec((None, s_, SWA_KV), lambda b, i: (b, 0, 0)),
                  pl.BlockSpec((None, s_, SWA_KV), lambda b, i: (b, 0, 0)),
                  _layer_spec(bias, l)],
        out_specs=pl.BlockSpec((None, tq, SWA_Q), lambda b, i: (b, i, 0)),
        out_shape=jax.ShapeDtypeStruct((b_, s_, SWA_Q), BF16),
        scratch_shapes=[pltpu.VMEM((units, SWA_BLOCK, 2 * SWA_BLOCK), F32),
                        pltpu.VMEM((units, SWA_BLOCK, LANES), F32)],
        compiler_params=pltpu.CompilerParams(
            dimension_semantics=("arbitrary", "arbitrary"), vmem_limit_bytes=VMEM_LIMIT),
        name="swa_attention",
    )(q, k, v, bias)


def _t5_bucket(dist):
    n = jnp.maximum(dist, 0)
    max_exact = REL_BUCKETS // 2
    large = max_exact + (jnp.log(jnp.maximum(n, 1).astype(F32) / max_exact)
                         / math.log(REL_MAX_DIST / max_exact)
                         * (REL_BUCKETS - max_exact)).astype(jnp.int32)
    large = jnp.minimum(large, REL_BUCKETS - 1)
    return jnp.where(n < max_exact, n, large)


def _band_bias_tables(rel_bias, sinks):
    qi = jnp.arange(SWA_BLOCK)[:, None]
    kj = jnp.arange(2 * SWA_BLOCK)[None, :]
    dist = qi + SWA_BLOCK - kj
    onehot = (_t5_bucket(dist)[None] == jnp.arange(REL_BUCKETS)[:, None, None]).astype(F32)
    b = jnp.einsum('nh,nqk->hqk', rel_bias.astype(F32), onehot, precision=lax.Precision.HIGHEST)
    valid = (dist >= 0) & (dist < SWA_BLOCK)
    normal = jnp.where(valid[None], b, NEG)
    first = jnp.concatenate([normal[:, :, SWA_BLOCK:], jnp.full_like(normal[:, :, SWA_BLOCK:], NEG)], axis=-1)
    sink = jnp.broadcast_to(sinks.astype(F32)[:, :, None], sinks.shape + (SWA_BLOCK,))
    shape = sinks.shape[:1] + normal.shape
    first = jnp.broadcast_to(first, shape).at[..., 2 * SWA_BLOCK - 1].set(sink)
    normal = jnp.broadcast_to(normal, shape).at[..., 0].set(sink)
    return jnp.stack([first, normal], axis=1)


def _mla_kernel(q_ref, k_ref, v_ref, o_ref, s_scr, m_scr, acc_scr):
    p = pl.program_id(1)
    tq, tk = MLA_TQ, MLA_TK
    n = q_ref.shape[0] // tq
    row = lax.broadcasted_iota(jnp.int32, (tq, tk), 0)
    col = lax.broadcasted_iota(jnp.int32, (tq, tk), 1)
    causal = col <= row
    out_low = lax.broadcasted_iota(jnp.int32, (tq, LANES), 1) < MLA_V
    heads = [slice(hd * MLA_HEAD_PAD, (hd + 1) * MLA_HEAD_PAD) for hd in range(MLA_HEADS)]
    lane_tiles = [slice(t * LANES, (t + 1) * LANES) for t in range(tk // LANES)]
    q_tiles = (p, n - 1 - p)

    def slot(t):
        if t >= n // 2:
            return 1, t - p - 1, t == n
        first = t <= p
        return jnp.where(first, 0, 1), jnp.where(first, t, t - p - 1), jnp.logical_and(first, t == p)

    def rows(idx, size):
        return pl.ds(pl.multiple_of(idx * size, size), size)

    m_scr[...] = jnp.full(m_scr.shape, NEG, F32)
    acc_scr[...] = jnp.zeros(acc_scr.shape, F32)
    for t in range(n + 1):
        which, kt, diag = slot(t)
        qt = q_tiles[which] if isinstance(which, int) else jnp.where(which == 0, q_tiles[0], q_tiles[1])
        for hd in range(MLA_HEADS):
            s = _dot_nt(q_ref[rows(qt, tq), heads[hd]], k_ref[rows(kt, tk), heads[hd]])
            if diag is True:
                s = jnp.where(causal, s, NEG)
            elif diag is not False:
                s = jnp.where(jnp.logical_or(causal, jnp.logical_not(diag)), s, NEG)
            s_scr[hd, t] = s
            m_scr[which, hd] = functools.reduce(jnp.maximum, [s[:, lt] for lt in lane_tiles], m_scr[which, hd])
    for which in range(2):
        for hd in range(MLA_HEADS):
            m_scr[which, hd] = jnp.broadcast_to(jnp.max(m_scr[which, hd], axis=-1, keepdims=True), (tq, LANES))
    for t in range(n + 1):
        which, kt, _ = slot(t)
        for hd in range(MLA_HEADS):
            m = m_scr[which, hd]
            e = jnp.concatenate([jnp.exp2((s_scr[hd, t, :, lt] - m) * MLA_EXP2_SCALE).astype(BF16)
                                 for lt in lane_tiles], axis=-1)
            acc_scr[which, hd] += _dot(e, v_ref[rows(kt, tk), heads[hd]])
    for which in range(2):
        for t in range(MLA_HEADS // 2):
            lo, hi = acc_scr[which, 2 * t], acc_scr[which, 2 * t + 1]
            lo = lo / lo[:, MLA_V:MLA_V + 1]
            hi = hi / hi[:, 0:1]
            o_ref[rows(q_tiles[which], tq), t * LANES:(t + 1) * LANES] = jnp.where(out_low, lo, hi).astype(BF16)


def _mla(q, k, v):
    b_, s_, w = q.shape
    n = s_ // MLA_TQ
    whole = lambda width: pl.BlockSpec((None, s_, width), lambda b, p: (b, 0, 0))
    return pl.pallas_call(
        _mla_kernel,
        grid=(b_, n // 2),
        in_specs=[whole(w), whole(w), whole(w)],
        out_specs=whole(MLA_HEADS * MLA_V),
        out_shape=jax.ShapeDtypeStruct((b_, s_, MLA_HEADS * MLA_V), BF16),
        scratch_shapes=[pltpu.VMEM((MLA_HEADS, n + 1, MLA_TQ, MLA_TK), F32),
                        pltpu.VMEM((2, MLA_HEADS, MLA_TQ, LANES), F32),
                        pltpu.VMEM((2, MLA_HEADS, MLA_TQ, LANES), F32)],
        compiler_params=pltpu.CompilerParams(
            dimension_semantics=("arbitrary", "arbitrary"), vmem_limit_bytes=VMEM_LIMIT),
        name="mla_attention",
    )(q, k, v)


def _gelu_tanh(x):
    return 0.5 * x * (1.0 + jnp.tanh(math.sqrt(2.0 / math.pi) * (x + 0.044715 * (x * x * x))))


def _ssm_kernel(u_ref, bd_ref, ar_ref, ai_ref, cd_ref, d_ref, wglu_ref, y_ref,
                u_scr, x_scr, y_scr, sr_scr, si_scr):
    n = SSM_STATES
    batch = sr_scr.shape[0]
    lane_tiles = SSM_CH // LANES

    @pl.when(pl.program_id(0) == 0)
    def _():
        sr_scr[...] = jnp.zeros_like(sr_scr)
        si_scr[...] = jnp.zeros_like(si_scr)

    for b in range(batch):
        for lt in range(lane_tiles):
            c0 = b * SSM_CH + lt * LANES
            u_scr[lt, pl.ds(b, SSM_TS, stride=batch), :] = u_ref[:, c0:c0 + LANES]
    sub_t = SSM_TS // SSM_SUB
    sub_r = sub_t * batch
    subs = [slice(q * sub_r, (q + 1) * sub_r) for q in range(SSM_SUB)]
    us = [jnp.concatenate([u_scr[lt, rows, :] for lt in range(lane_tiles)], axis=-1) for rows in subs]
    for rows, u in zip(subs, us):
        u_bf = u.astype(BF16)
        x_scr[rows, :n] = _dot(u_bf, bd_ref[:, :n])
        x_scr[rows, n:] = _dot(u_bf, bd_ref[:, n:])
    ar = ar_ref[...]
    ai = ai_ref[...]
    xr, xi = sr_scr[...], si_scr[...]
    for rows, u in zip(subs, us):
        for t in range(sub_t):
            r = slice(rows.start + t * batch, rows.start + (t + 1) * batch)
            xr, xi = (ar * xr - ai * xi + x_scr[r, :n], ar * xi + ai * xr + x_scr[r, n:])
            x_scr[r, :n] = xr
            x_scr[r, n:] = xi
        mid = rows.start + sub_r // 2
        cx = jnp.concatenate([_dot(x_scr[rows.start:mid, :].astype(BF16), cd_ref[...]),
                              _dot(x_scr[mid:rows.stop, :].astype(BF16), cd_ref[...])], axis=0)
        y = cx + d_ref[...] * u
        y = _gelu_tanh(y)
        y = y * jax.nn.sigmoid(_dot(y.astype(BF16), wglu_ref[...]))
        for lt in range(lane_tiles):
            y_scr[lt, rows, :] = y[:, lt * LANES:(lt + 1) * LANES]
    sr_scr[...] = xr
    si_scr[...] = xi
    for b in range(batch):
        for lt in range(lane_tiles):
            c0 = b * SSM_CH + lt * LANES
            y_ref[:, c0:c0 + LANES] = y_scr[lt, pl.ds(b, SSM_TS, stride=batch), :].astype(BF16)


def _ssm(u, batch, l, bd, ar, ai, cd, d, wglu):
    s_, width = u.shape
    tr = SSM_TS * batch
    return pl.pallas_call(
        _ssm_kernel,
        grid=(s_ // SSM_TS,),
        in_specs=[pl.BlockSpec((SSM_TS, width), lambda i: (i, 0))]
        + [_layer_spec(a, l) for a in (bd, ar, ai, cd, d, wglu)],
        out_specs=pl.BlockSpec((SSM_TS, width), lambda i: (i, 0)),
        out_shape=jax.ShapeDtypeStruct((s_, width), BF16),
        scratch_shapes=[pltpu.VMEM((SSM_CH // LANES, tr, LANES), F32),
                        pltpu.VMEM((tr, 2 * SSM_STATES), F32),
                        pltpu.VMEM((SSM_CH // LANES, tr, LANES), F32),
                        pltpu.VMEM((batch, SSM_STATES), F32),
                        pltpu.VMEM((batch, SSM_STATES), F32)],
        compiler_params=pltpu.CompilerParams(
            dimension_semantics=("arbitrary",), vmem_limit_bytes=VMEM_LIMIT),
        name="s5_mixer",
    )(u, bd, ar, ai, cd, d, wglu)


def _ssm_params(a_re, a_im, log_dt, b_re, b_im, c_re, c_im, batch):
    lam = lax.complex(a_re.astype(F32), a_im.astype(F32))
    dt = jnp.exp(log_dt.astype(F32))
    a_bar = jnp.exp(lam * dt[:, None])
    b_bar = ((a_bar - 1.0) / lam)[..., None] * lax.complex(b_re.astype(F32), b_im.astype(F32))
    eye = jnp.eye(SSM_GROUPS, dtype=F32)

    def in_op(m):
        return jnp.einsum('gpc,gh->gchp', m, eye).reshape(SSM_CH, SSM_STATES)

    def out_op(m):
        return jnp.einsum('gcp,gh->gphc', m, eye).reshape(SSM_STATES, SSM_CH)

    bd = jnp.concatenate([in_op(jnp.real(b_bar)), in_op(jnp.imag(b_bar))], axis=1).astype(BF16)
    cd = jnp.concatenate([out_op(c_re.astype(F32)), -out_op(c_im.astype(F32))], axis=0).astype(BF16)
    ar = jnp.broadcast_to(jnp.real(a_bar).reshape(1, SSM_STATES), (batch, SSM_STATES))
    ai = jnp.broadcast_to(jnp.imag(a_bar).reshape(1, SSM_STATES), (batch, SSM_STATES))
    return bd, ar, ai, cd


def _ffn_kernel(x_ref, oa_ref, ob_ref, oc_ref, wo_ref, g2_ref, wg_ref, wu_ref, wd_ref, gf_ref, o_ref, h_scr,
                *, final_norm):
    wo_a = wo_ref[:SWA_Q, :]
    wo_b = wo_ref[SWA_Q:SWA_Q + SSM_CH, :]
    wo_c = wo_ref[SWA_Q + SSM_CH:, :]
    x = x_ref[...] + _dot(oa_ref[...], wo_a) + _dot(ob_ref[...], wo_b) + _dot(oc_ref[...], wo_c)
    o_ref[...] = x
    h_scr[...] = _rms(x, g2_ref[...]).astype(BF16)

    for c0 in range(0, D_FF, FFN_CHUNK):
        h = h_scr[...]
        gate = _dot(h, wg_ref[:, c0:c0 + FFN_CHUNK])
        up = _dot(h, wu_ref[:, c0:c0 + FFN_CHUNK])
        act = (gate * jax.nn.sigmoid(gate) * up).astype(BF16)
        o_ref[...] += _dot(act, wd_ref[c0:c0 + FFN_CHUNK, :])
    if final_norm:
        o_ref[...] = _rms(o_ref[...], gf_ref[...])


def _ffn(x, oa, ob, oc, l, wo, g2, wg, wu, wd, gf, final_norm):
    b_, s_, _ = x.shape
    tm = FFN_TM
    row3 = lambda b, i: (b, i, 0)
    return pl.pallas_call(
        functools.partial(_ffn_kernel, final_norm=final_norm),
        grid=(b_, s_ // tm),
        in_specs=[pl.BlockSpec((None, tm, D_MODEL), row3),
                  pl.BlockSpec((None, tm, SWA_Q), row3),
                  pl.BlockSpec((tm, SSM_CH), lambda b, i: (i, b)),
                  pl.BlockSpec((None, tm, MLA_HEADS * MLA_V), row3),
                  _layer_spec(wo, l), _layer_spec(g2, l), _layer_spec(wg, l), _layer_spec(wu, l),
                  _layer_spec(wd, l),
                  pl.BlockSpec((1, D_MODEL), lambda b, i: (0, 0))],
        out_specs=pl.BlockSpec((None, tm, D_MODEL), row3),
        out_shape=jax.ShapeDtypeStruct(x.shape, F32),
        scratch_shapes=[pltpu.VMEM((tm, D_MODEL), BF16)],
        compiler_params=pltpu.CompilerParams(
            dimension_semantics=("arbitrary", "arbitrary"), vmem_limit_bytes=VMEM_LIMIT),
        name="out_proj_ffn",
    )(x, oa, ob, oc, wo, g2, wg, wu, wd, gf)


def _rot_half_cols(w):
    half = w.shape[-1] // 2
    return jnp.concatenate([-w[..., half:], w[..., :half]], axis=-1)


def _rope_tile(w):
    z = jnp.zeros(w.shape[:-1] + (MLA_NOPE,), w.dtype)
    z2 = jnp.zeros(w.shape[:-1] + (MLA_HEAD_PAD - MLA_NOPE - MLA_ROPE,), w.dtype)
    return jnp.concatenate([z, w, z2], axis=-1)


def _stacked_weights(w_in, w_q_up, w_kv_up):
    lead = w_in.shape[:-2]
    kr = w_in[..., -MLA_ROPE:]
    w_ext = jnp.concatenate([w_in[..., :-MLA_ROPE], _rope_tile(kr), _rope_tile(_rot_half_cols(kr))], axis=-1)
    qh = w_q_up.reshape(lead + (MLA_Q_RANK, MLA_HEADS, MLA_NOPE + MLA_ROPE))
    pad = jnp.zeros(lead + (MLA_Q_RANK, MLA_HEADS, MLA_HEAD_PAD - MLA_NOPE - MLA_ROPE), w_q_up.dtype)
    q_tiles = jnp.concatenate([qh, pad], axis=-1).reshape(lead + (MLA_Q_RANK, -1))
    q_rot = _rope_tile(_rot_half_cols(qh[..., MLA_NOPE:])).reshape(lead + (MLA_Q_RANK, -1))
    wq_ext = jnp.concatenate([q_tiles, q_rot], axis=-1)
    kvh = w_kv_up.reshape(lead + (MLA_KV_RANK, MLA_HEADS, MLA_NOPE + MLA_V))
    kpad = jnp.zeros(lead + (MLA_KV_RANK, MLA_HEADS, MLA_HEAD_PAD - MLA_NOPE), w_kv_up.dtype)
    k_tiles = jnp.concatenate([kvh[..., :MLA_NOPE], kpad], axis=-1).reshape(lead + (MLA_KV_RANK, -1))
    v_cols = kvh[..., MLA_NOPE:].reshape(lead + (MLA_KV_RANK, -1))
    wkv_ext = jnp.concatenate([k_tiles, v_cols], axis=-1)
    return w_ext.astype(BF16), wq_ext.astype(BF16), wkv_ext.astype(BF16)


def kernel(x, positions, rel_bias, ln1_g, w_in, sinks, ssm_a_re, ssm_a_im, ssm_log_dt, ssm_b_re, ssm_b_im,
           ssm_c_re, ssm_c_im, ssm_d, ssm_w_glu, mla_q_norm_g, mla_w_q_up, mla_kv_norm_g, mla_w_kv_up, w_out,
           ln2_g, w_gate, w_up, w_down, final_g):
    b_ = x.shape[0]
    cos_t, sin_t = _trig_tables(positions)
    bias = _band_bias_tables(rel_bias, sinks)
    row = lambda g: g[:, None, :]
    w_ext, wq_ext, wkv_ext = _stacked_weights(w_in, mla_w_q_up, mla_w_kv_up)
    bd, ar, ai, cd = jax.vmap(functools.partial(_ssm_params, batch=b_))(
        ssm_a_re, ssm_a_im, ssm_log_dt, ssm_b_re, ssm_b_im, ssm_c_re, ssm_c_im)
    wglu = ssm_w_glu.astype(BF16)
    wo, wg, wu, wd = (w.astype(BF16) for w in (w_out, w_gate, w_up, w_down))
    gf = final_g.reshape(1, D_MODEL)
    for l in range(DEPTH):
        qa, ka, va, u, qm, km, vm = _in_proj(x, l, row(ln1_g), w_ext, cos_t, sin_t,
                                             row(mla_q_norm_g), wq_ext, row(mla_kv_norm_g), wkv_ext)
        o_a = _swa(qa, ka, va, l, bias)
        o_b = _ssm(u, b_, l, bd, ar, ai, cd, row(ssm_d), wglu)
        o_c = _mla(qm, km, vm)
        x = _ffn(x, o_a, o_b, o_c, l, wo, row(ln2_g), wg, wu, wd, gf, final_norm=(l == DEPTH - 1))
    return x
```

```python
import functools
import math

import jax
import jax.numpy as jnp
from jax import lax
from jax.experimental import pallas as pl
from jax.experimental.pallas import tpu as pltpu

D_MODEL = 1024
DEPTH = 2
HEAD_DIM = 64
SWA_Q_HEADS = 8
SWA_KV_HEADS = 2
SWA_BLOCK = 128
SWA_Q = SWA_Q_HEADS * HEAD_DIM
SWA_KV = SWA_KV_HEADS * HEAD_DIM
REL_BUCKETS = 32
REL_MAX_DIST = 128
SSM_CH = 256
SSM_GROUP = 16
SSM_GROUPS = SSM_CH // SSM_GROUP
SSM_STATE = 64
SSM_STATES = SSM_GROUPS * SSM_STATE
MLA_HEADS = 4
MLA_Q_RANK = 256
MLA_KV_RANK = 128
MLA_NOPE = 64
MLA_ROPE = 32
MLA_V = 64
ROPE_THETA = 10000.0
D_FF = 2816
EPS = 1e-6
NEG = -1e30

LANES = 128
SUBLANES = 8
MLA_HEAD_PAD = LANES
MLA_EXP2_SCALE = (MLA_NOPE + MLA_ROPE) ** -0.5 * math.log2(math.e)
VMEM_LIMIT = 56 * 1024 * 1024

IN_PROJ_TM = 512
SWA_TQ = 512
MLA_TQ = 512
MLA_TK = 512
SSM_TS = 128
SSM_SUB = 2
FFN_TM = 512
FFN_CHUNK = D_FF

BF16 = jnp.bfloat16
F32 = jnp.float32


def _rms(x, g):
    return x * lax.rsqrt(jnp.mean(x * x, axis=-1, keepdims=True) + EPS) * g


def _dot(a, b):
    return jnp.dot(a, b, preferred_element_type=F32)


def _dot_nt(a, b):
    return lax.dot_general(a, b, (((1,), (1,)), ((), ())), preferred_element_type=F32)


def _layer_spec(arr, l):
    return pl.BlockSpec((None,) + arr.shape[1:], lambda *_: (l,) + (0,) * (arr.ndim - 1))


def _trig_kernel(pos_ref, freq_ref, place_ref, one_ref, cos_ref, sin_ref):
    pos = pos_ref[pl.ds(pl.program_id(0), 1), :].astype(F32)
    ang = freq_ref[...] * pos

    def place(t):
        hi = t.astype(BF16)
        rest = t - hi.astype(F32)
        mid = rest.astype(BF16)
        lo = (rest - mid.astype(F32)).astype(BF16)
        return sum(lax.dot_general(piece, place_ref[...], (((0,), (0,)), ((), ())), preferred_element_type=F32)
                   for piece in (hi, mid, lo))

    cos_ref[...] = place(jnp.cos(ang)) + one_ref[...]
    sin_ref[...] = place(jnp.sin(ang))


def _trig_tables(positions):
    b_, s_ = positions.shape
    half = MLA_ROPE // 2
    inv_freq = jnp.power(ROPE_THETA, -jnp.arange(half, dtype=F32) * 2.0 / MLA_ROPE).reshape(half, 1)
    lane = jnp.arange(MLA_HEAD_PAD)[None, :]
    rope_lane = (lane >= MLA_NOPE) & (lane < MLA_NOPE + MLA_ROPE)
    placement = (rope_lane & ((lane - MLA_NOPE) % half == jnp.arange(half)[:, None])).astype(BF16)
    off_rope = 1.0 - rope_lane.astype(F32)
    const = lambda b: (0, 0)
    cos_t, sin_t = pl.pallas_call(
        _trig_kernel,
        grid=(b_,),
        in_specs=[pl.BlockSpec((b_, s_), const),
                  pl.BlockSpec((half, 1), const),
                  pl.BlockSpec((half, MLA_HEAD_PAD), const),
                  pl.BlockSpec((1, MLA_HEAD_PAD), const)],
        out_specs=[pl.BlockSpec((None, s_, MLA_HEAD_PAD), lambda b: (b, 0, 0))] * 2,
        out_shape=[jax.ShapeDtypeStruct((b_, s_, MLA_HEAD_PAD), F32)] * 2,
        name="rope_tables",
    )(positions, inv_freq, placement, off_rope)
    return cos_t.reshape(b_ * s_, MLA_HEAD_PAD), sin_t.reshape(b_ * s_, MLA_HEAD_PAD)


def _in_proj_kernel(x_ref, g_ref, w_ref, cos_ref, sin_ref, qg_ref, wq_ref, kvg_ref, wkv_ref,
                    qa_ref, ka_ref, va_ref, u_ref, qm_ref, km_ref, vm_ref):
    h = _rms(x_ref[...], g_ref[...]).astype(BF16)
    proj = _dot(h, w_ref[...])
    qa_ref[...] = (proj[:, :SWA_Q] * (HEAD_DIM ** -0.5)).astype(BF16)
    c = SWA_Q
    ka_ref[...] = proj[:, c:c + SWA_KV].astype(BF16)
    c += SWA_KV
    va_ref[...] = proj[:, c:c + SWA_KV].astype(BF16)
    c += SWA_KV
    u_ref[...] = proj[:, c:c + SSM_CH]
    c += SSM_CH
    c_q = proj[:, c:c + MLA_Q_RANK]
    c += MLA_Q_RANK
    c_kv = proj[:, c:c + MLA_KV_RANK]
    c += MLA_KV_RANK
    k_r = proj[:, c:c + LANES]
    k_r_rot = proj[:, c + LANES:c + 2 * LANES]
    cos = cos_ref[...]
    sin = sin_ref[...]
    width = MLA_HEADS * MLA_HEAD_PAD
    q2 = _dot(_rms(c_q, qg_ref[...]).astype(BF16), wq_ref[...])
    kv2 = _dot(_rms(c_kv, kvg_ref[...]).astype(BF16), wkv_ref[...])
    k_pe = k_r * cos + k_r_rot * sin
    low = lax.broadcasted_iota(jnp.int32, (cos.shape[0], LANES), 1) < MLA_V
    for hd in range(MLA_HEADS):
        sl = slice(hd * MLA_HEAD_PAD, (hd + 1) * MLA_HEAD_PAD)
        sl_rot = slice(width + hd * MLA_HEAD_PAD, width + (hd + 1) * MLA_HEAD_PAD)
        qm_ref[:, sl] = (q2[:, sl] * cos + q2[:, sl_rot] * sin).astype(BF16)
        km_ref[:, sl] = (kv2[:, sl] + k_pe).astype(BF16)
        pair = kv2[:, width + (hd // 2) * LANES:width + (hd // 2 + 1) * LANES]
        keep = low if hd % 2 == 0 else jnp.logical_not(low)
        vm_ref[:, sl] = jnp.where(keep, pair, 1.0).astype(BF16)


def _in_proj(x, l, g, w_ext, cos_t, sin_t, qg, wq_ext, kvg, wkv_ext):
    b_, s_, _ = x.shape
    tm = IN_PROJ_TM
    nt = s_ // tm
    row3 = lambda b, i: (b, i, 0)
    trig = pl.BlockSpec((tm, MLA_HEAD_PAD), lambda b, i: (b * nt + i, 0))
    mla_w = MLA_HEADS * MLA_HEAD_PAD
    return pl.pallas_call(
        _in_proj_kernel,
        grid=(b_, nt),
        in_specs=[pl.BlockSpec((None, tm, D_MODEL), row3),
                  _layer_spec(g, l), _layer_spec(w_ext, l),
                  trig, trig,
                  _layer_spec(qg, l), _layer_spec(wq_ext, l), _layer_spec(kvg, l), _layer_spec(wkv_ext, l)],
        out_specs=[pl.BlockSpec((None, tm, SWA_Q), row3),
                   pl.BlockSpec((None, tm, SWA_KV), row3),
                   pl.BlockSpec((None, tm, SWA_KV), row3),
                   pl.BlockSpec((tm, SSM_CH), lambda b, i: (i, b)),
                   pl.BlockSpec((None, tm, mla_w), row3),
                   pl.BlockSpec((None, tm, mla_w), row3),
                   pl.BlockSpec((None, tm, mla_w), row3)],
        out_shape=[jax.ShapeDtypeStruct((b_, s_, SWA_Q), BF16),
                   jax.ShapeDtypeStruct((b_, s_, SWA_KV), BF16),
                   jax.ShapeDtypeStruct((b_, s_, SWA_KV), BF16),
                   jax.ShapeDtypeStruct((s_, b_ * SSM_CH), F32),
                   jax.ShapeDtypeStruct((b_, s_, mla_w), BF16),
                   jax.ShapeDtypeStruct((b_, s_, mla_w), BF16),
                   jax.ShapeDtypeStruct((b_, s_, mla_w), BF16)],
        compiler_params=pltpu.CompilerParams(
            dimension_semantics=("arbitrary", "arbitrary"), vmem_limit_bytes=VMEM_LIMIT),
        name="in_proj",
    )(x, g, w_ext, cos_t, sin_t, qg, wq_ext, kvg, wkv_ext)


def _swa_kernel(q_ref, k_ref, v_ref, bias_ref, o_ref, s_scr, m_scr):
    qi = pl.program_id(1)
    blocks = SWA_TQ // SWA_BLOCK
    tiles = SWA_Q // LANES
    band = 2 * SWA_BLOCK
    lane = lax.broadcasted_iota(jnp.int32, (band, LANES), 1)
    key = lax.broadcasted_iota(jnp.int32, (band, LANES), 0)
    low = lane < HEAD_DIM
    out_low = lax.broadcasted_iota(jnp.int32, (SWA_BLOCK, LANES), 1) < HEAD_DIM

    def band_of(ref, j):
        blk = qi * blocks + j
        start = pl.multiple_of(jnp.maximum(blk - 1, 0) * SWA_BLOCK, SWA_BLOCK)
        keep = key != jnp.where(blk == 0, band - 1, 0)
        t = ref[pl.ds(start, band), :]
        t = jnp.where(keep, t, jnp.zeros_like(t))
        return t, pltpu.roll(t, HEAD_DIM, 1)

    for j in range(blocks):
        table = jnp.minimum(qi * blocks + j, 1)
        kb, kb_sw = band_of(k_ref, j)
        zero = jnp.zeros_like(kb)
        k_var = ((jnp.where(low, kb, zero), jnp.where(low, zero, kb_sw)),
                 (jnp.where(low, kb_sw, zero), jnp.where(low, zero, kb)))
        rows = slice(j * SWA_BLOCK, (j + 1) * SWA_BLOCK)
        for i in range(tiles):
            kvh = (2 * i) // (SWA_Q_HEADS // SWA_KV_HEADS)
            q = q_ref[rows, i * LANES:(i + 1) * LANES]
            for half in range(2):
                u = (j * tiles + i) * 2 + half
                s = _dot_nt(q, k_var[kvh][half]) + bias_ref[table, 2 * i + half]
                s_scr[u] = s
                m_scr[u] = jnp.broadcast_to(jnp.max(s, axis=-1, keepdims=True), (SWA_BLOCK, LANES))
    for j in range(blocks):
        vb, vb_sw = band_of(v_ref, j)
        one = jnp.ones_like(vb)
        v_var = ((jnp.where(low, vb, one), jnp.where(low, one, vb_sw)),
                 (jnp.where(low, vb_sw, one), jnp.where(low, one, vb)))
        rows = slice(j * SWA_BLOCK, (j + 1) * SWA_BLOCK)
        for i in range(tiles):
            kvh = (2 * i) // (SWA_Q_HEADS // SWA_KV_HEADS)
            r = []
            for half in range(2):
                u = (j * tiles + i) * 2 + half
                m = m_scr[u]
                p = jnp.concatenate([jnp.exp(s_scr[u, :, t * LANES:(t + 1) * LANES] - m).astype(BF16)
                                     for t in range(band // LANES)], axis=-1)
                r.append(_dot(p, v_var[kvh][half]))
            denom = pltpu.roll(jnp.where(out_low, r[1], r[0]), HEAD_DIM, 1)
            o_ref[rows, i * LANES:(i + 1) * LANES] = (jnp.where(out_low, r[0], r[1]) / denom).astype(BF16)


def _swa(q, k, v, l, bias):
    b_, s_, _ = q.shape
    tq = SWA_TQ
    units = (tq // SWA_BLOCK) * SWA_Q_HEADS
    return pl.pallas_call(
        _swa_kernel,
        grid=(b_, s_ // tq),
        in_specs=[pl.BlockSpec((None, tq, SWA_Q), lambda b, i: (b, i, 0)),
                  pl.BlockSpec((None, s_, SWA_KV), lambda b, i: (b, 0, 0)),
                  pl.BlockSpec((None, s_, SWA_KV), lambda b, i: (b, 0, 0)),
                  _layer_spec(bias, l)],
        out_specs=pl.BlockSpec((None, tq, SWA_Q), lambda b, i: (b, i, 0)),
        out_shape=jax.ShapeDtypeStruct((b_, s_, SWA_Q), BF16),
        scratch_shapes=[pltpu.VMEM((units, SWA_BLOCK, 2 * SWA_BLOCK), F32),
                        pltpu.VMEM((units, SWA_BLOCK, LANES), F32)],
        compiler_params=pltpu.CompilerParams(
            dimension_semantics=("arbitrary", "arbitrary"), vmem_limit_bytes=VMEM_LIMIT),
        name="swa_attention",
    )(q, k, v, bias)


def _t5_bucket(dist):
    n = jnp.maximum(dist, 0)
    max_exact = REL_BUCKETS // 2
    large = max_exact + (jnp.log(jnp.maximum(n, 1).astype(F32) / max_exact)
                         / math.log(REL_MAX_DIST / max_exact)
                         * (REL_BUCKETS - max_exact)).astype(jnp.int32)
    large = jnp.minimum(large, REL_BUCKETS - 1)
    return jnp.where(n < max_exact, n, large)


def _band_bias_tables(rel_bias, sinks):
    qi = jnp.arange(SWA_BLOCK)[:, None]
    kj = jnp.arange(2 * SWA_BLOCK)[None, :]
    dist = qi + SWA_BLOCK - kj
    onehot = (_t5_bucket(dist)[None] == jnp.arange(REL_BUCKETS)[:, None, None]).astype(F32)
    b = jnp.einsum('nh,nqk->hqk', rel_bias.astype(F32), onehot, precision=lax.Precision.HIGHEST)
    valid = (dist >= 0) & (dist < SWA_BLOCK)
    normal = jnp.where(valid[None], b, NEG)
    first = jnp.concatenate([normal[:, :, SWA_BLOCK:], jnp.full_like(normal[:, :, SWA_BLOCK:], NEG)], axis=-1)
    sink = jnp.broadcast_to(sinks.astype(F32)[:, :, None], sinks.shape + (SWA_BLOCK,))
    shape = sinks.shape[:1] + normal.shape
    first = jnp.broadcast_to(first, shape).at[..., 2 * SWA_BLOCK - 1].set(sink)
    normal = jnp.broadcast_to(normal, shape).at[..., 0].set(sink)
    return jnp.stack([first, normal], axis=1)


def _mla_kernel(q_ref, k_ref, v_ref, *rest):
    n_w = (len(rest) - 4) // 2
    w_f32, o_ref, w_bf16 = rest[:n_w], rest[n_w], rest[n_w + 1:2 * n_w + 1]
    s_scr, m_scr, acc_scr = rest[2 * n_w + 1:]
    for src, dst in zip(w_f32, w_bf16):
        dst[...] = src[...].astype(BF16)
    p = pl.program_id(1)
    tq, tk = MLA_TQ, MLA_TK
    n = q_ref.shape[0] // tq
    row = lax.broadcasted_iota(jnp.int32, (tq, tk), 0)
    col = lax.broadcasted_iota(jnp.int32, (tq, tk), 1)
    causal = col <= row
    out_low = lax.broadcasted_iota(jnp.int32, (tq, LANES), 1) < MLA_V
    heads = [slice(hd * MLA_HEAD_PAD, (hd + 1) * MLA_HEAD_PAD) for hd in range(MLA_HEADS)]
    lane_tiles = [slice(t * LANES, (t + 1) * LANES) for t in range(tk // LANES)]
    q_tiles = (p, n - 1 - p)

    def slot(t):
        if t >= n // 2:
            return 1, t - p - 1, t == n
        first = t <= p
        return jnp.where(first, 0, 1), jnp.where(first, t, t - p - 1), jnp.logical_and(first, t == p)

    def rows(idx, size):
        return pl.ds(pl.multiple_of(idx * size, size), size)

    m_scr[...] = jnp.full(m_scr.shape, NEG, F32)
    acc_scr[...] = jnp.zeros(acc_scr.shape, F32)
    for t in range(n + 1):
        which, kt, diag = slot(t)
        qt = q_tiles[which] if isinstance(which, int) else jnp.where(which == 0, q_tiles[0], q_tiles[1])
        for hd in range(MLA_HEADS):
            s = _dot_nt(q_ref[rows(qt, tq), heads[hd]], k_ref[rows(kt, tk), heads[hd]])
            if diag is True:
                s = jnp.where(causal, s, NEG)
            elif diag is not False:
                s = jnp.where(jnp.logical_or(causal, jnp.logical_not(diag)), s, NEG)
            s_scr[hd, t] = s
            m_scr[which, hd] = functools.reduce(jnp.maximum, [s[:, lt] for lt in lane_tiles], m_scr[which, hd])
    for which in range(2):
        for hd in range(MLA_HEADS):
            m_scr[which, hd] = jnp.broadcast_to(jnp.max(m_scr[which, hd], axis=-1, keepdims=True), (tq, LANES))
    for t in range(n + 1):
        which, kt, _ = slot(t)
        for hd in range(MLA_HEADS):
            m = m_scr[which, hd]
            e = jnp.concatenate([jnp.exp2((s_scr[hd, t, :, lt] - m) * MLA_EXP2_SCALE).astype(BF16)
                                 for lt in lane_tiles], axis=-1)
            acc_scr[which, hd] += _dot(e, v_ref[rows(kt, tk), heads[hd]])
    for which in range(2):
        for t in range(MLA_HEADS // 2):
            lo, hi = acc_scr[which, 2 * t], acc_scr[which, 2 * t + 1]
            lo = lo / lo[:, MLA_V:MLA_V + 1]
            hi = hi / hi[:, 0:1]
            o_ref[rows(q_tiles[which], tq), t * LANES:(t + 1) * LANES] = jnp.where(out_low, lo, hi).astype(BF16)


def _mla(q, k, v, l, weights):
    b_, s_, w = q.shape
    n = s_ // MLA_TQ
    steps = b_ * (n // 2)
    whole = lambda width: pl.BlockSpec((None, s_, width), lambda b, p: (b, 0, 0))
    slab_in = [pl.BlockSpec((None, wt.shape[1] // steps, wt.shape[2]), lambda b, p: (l, b * (n // 2) + p, 0))
               for wt in weights]
    slab_out = [pl.BlockSpec((wt.shape[1] // steps, wt.shape[2]), lambda b, p: (b * (n // 2) + p, 0))
                for wt in weights]
    return pl.pallas_call(
        _mla_kernel,
        grid=(b_, n // 2),
        in_specs=[whole(w), whole(w), whole(w)] + slab_in,
        out_specs=[whole(MLA_HEADS * MLA_V)] + slab_out,
        out_shape=[jax.ShapeDtypeStruct((b_, s_, MLA_HEADS * MLA_V), BF16)]
        + [jax.ShapeDtypeStruct(wt.shape[1:], BF16) for wt in weights],
        scratch_shapes=[pltpu.VMEM((MLA_HEADS, n + 1, MLA_TQ, MLA_TK), F32),
                        pltpu.VMEM((2, MLA_HEADS, MLA_TQ, LANES), F32),
                        pltpu.VMEM((2, MLA_HEADS, MLA_TQ, LANES), F32)],
        compiler_params=pltpu.CompilerParams(
            dimension_semantics=("arbitrary", "arbitrary"), vmem_limit_bytes=VMEM_LIMIT),
        name="mla_attention",
    )(q, k, v, *weights)


def _gelu_tanh(x):
    return 0.5 * x * (1.0 + jnp.tanh(math.sqrt(2.0 / math.pi) * (x + 0.044715 * (x * x * x))))


def _ssm_kernel(u_ref, bd_ref, ar_ref, ai_ref, cd_ref, d_ref, wglu_ref, y_ref,
                u_scr, x_scr, y_scr, sr_scr, si_scr):
    n = SSM_STATES
    batch = sr_scr.shape[0]
    lane_tiles = SSM_CH // LANES

    @pl.when(pl.program_id(0) == 0)
    def _():
        sr_scr[...] = jnp.zeros_like(sr_scr)
        si_scr[...] = jnp.zeros_like(si_scr)

    for b in range(batch):
        for lt in range(lane_tiles):
            c0 = b * SSM_CH + lt * LANES
            u_scr[lt, pl.ds(b, SSM_TS, stride=batch), :] = u_ref[:, c0:c0 + LANES]
    sub_t = SSM_TS // SSM_SUB
    sub_r = sub_t * batch
    subs = [slice(q * sub_r, (q + 1) * sub_r) for q in range(SSM_SUB)]
    us = [jnp.concatenate([u_scr[lt, rows, :] for lt in range(lane_tiles)], axis=-1) for rows in subs]
    for rows, u in zip(subs, us):
        u_bf = u.astype(BF16)
        x_scr[rows, :n] = _dot(u_bf, bd_ref[:, :n])
        x_scr[rows, n:] = _dot(u_bf, bd_ref[:, n:])
    ar = ar_ref[...]
    ai = ai_ref[...]
    xr, xi = sr_scr[...], si_scr[...]
    for rows, u in zip(subs, us):
        for t in range(sub_t):
            r = slice(rows.start + t * batch, rows.start + (t + 1) * batch)
            xr, xi = (ar * xr - ai * xi + x_scr[r, :n], ar * xi + ai * xr + x_scr[r, n:])
            x_scr[r, :n] = xr
            x_scr[r, n:] = xi
        mid = rows.start + sub_r // 2
        cx = jnp.concatenate([_dot(x_scr[rows.start:mid, :].astype(BF16), cd_ref[...]),
                              _dot(x_scr[mid:rows.stop, :].astype(BF16), cd_ref[...])], axis=0)
        y = cx + d_ref[...] * u
        y = _gelu_tanh(y)
        y = y * jax.nn.sigmoid(_dot(y.astype(BF16), wglu_ref[...]))
        for lt in range(lane_tiles):
            y_scr[lt, rows, :] = y[:, lt * LANES:(lt + 1) * LANES]
    sr_scr[...] = xr
    si_scr[...] = xi
    for b in range(batch):
        for lt in range(lane_tiles):
            c0 = b * SSM_CH + lt * LANES
            y_ref[:, c0:c0 + LANES] = y_scr[lt, pl.ds(b, SSM_TS, stride=batch), :].astype(BF16)


def _ssm(u, batch, l, bd, ar, ai, cd, d, wglu):
    s_, width = u.shape
    tr = SSM_TS * batch
    return pl.pallas_call(
        _ssm_kernel,
        grid=(s_ // SSM_TS,),
        in_specs=[pl.BlockSpec((SSM_TS, width), lambda i: (i, 0))]
        + [_layer_spec(a, l) for a in (bd, ar, ai, cd, d, wglu)],
        out_specs=pl.BlockSpec((SSM_TS, width), lambda i: (i, 0)),
        out_shape=jax.ShapeDtypeStruct((s_, width), BF16),
        scratch_shapes=[pltpu.VMEM((SSM_CH // LANES, tr, LANES), F32),
                        pltpu.VMEM((tr, 2 * SSM_STATES), F32),
                        pltpu.VMEM((SSM_CH // LANES, tr, LANES), F32),
                        pltpu.VMEM((batch, SSM_STATES), F32),
                        pltpu.VMEM((batch, SSM_STATES), F32)],
        compiler_params=pltpu.CompilerParams(
            dimension_semantics=("arbitrary",), vmem_limit_bytes=VMEM_LIMIT),
        name="s5_mixer",
    )(u, bd, ar, ai, cd, d, wglu)


def _ssm_params(a_re, a_im, log_dt, b_re, b_im, c_re, c_im, batch):
    lam = lax.complex(a_re.astype(F32), a_im.astype(F32))
    dt = jnp.exp(log_dt.astype(F32))
    a_bar = jnp.exp(lam * dt[:, None])
    b_bar = ((a_bar - 1.0) / lam)[..., None] * lax.complex(b_re.astype(F32), b_im.astype(F32))
    eye = jnp.eye(SSM_GROUPS, dtype=F32)

    def in_op(m):
        return jnp.einsum('gpc,gh->gchp', m, eye).reshape(SSM_CH, SSM_STATES)

    def out_op(m):
        return jnp.einsum('gcp,gh->gphc', m, eye).reshape(SSM_STATES, SSM_CH)

    bd = jnp.concatenate([in_op(jnp.real(b_bar)), in_op(jnp.imag(b_bar))], axis=1).astype(BF16)
    cd = jnp.concatenate([out_op(c_re.astype(F32)), -out_op(c_im.astype(F32))], axis=0).astype(BF16)
    ar = jnp.broadcast_to(jnp.real(a_bar).reshape(1, SSM_STATES), (batch, SSM_STATES))
    ai = jnp.broadcast_to(jnp.imag(a_bar).reshape(1, SSM_STATES), (batch, SSM_STATES))
    return bd, ar, ai, cd


def _ffn_kernel(x_ref, oa_ref, ob_ref, oc_ref, wo_ref, g2_ref, wg_ref, wu_ref, wd_ref, gf_ref, o_ref, h_scr,
                *, final_norm):
    wo_a = wo_ref[:SWA_Q, :]
    wo_b = wo_ref[SWA_Q:SWA_Q + SSM_CH, :]
    wo_c = wo_ref[SWA_Q + SSM_CH:, :]
    x = x_ref[...] + _dot(oa_ref[...], wo_a) + _dot(ob_ref[...], wo_b) + _dot(oc_ref[...], wo_c)
    o_ref[...] = x
    h_scr[...] = _rms(x, g2_ref[...]).astype(BF16)

    for c0 in range(0, D_FF, FFN_CHUNK):
        h = h_scr[...]
        gate = _dot(h, wg_ref[:, c0:c0 + FFN_CHUNK])
        up = _dot(h, wu_ref[:, c0:c0 + FFN_CHUNK])
        act = (gate * jax.nn.sigmoid(gate) * up).astype(BF16)
        o_ref[...] += _dot(act, wd_ref[c0:c0 + FFN_CHUNK, :])
    if final_norm:
        o_ref[...] = _rms(o_ref[...], gf_ref[...])


def _ffn(x, oa, ob, oc, l, wo, g2, wg, wu, wd, gf, final_norm):
    b_, s_, _ = x.shape
    tm = FFN_TM
    row3 = lambda b, i: (b, i, 0)
    whole = lambda wt: pl.BlockSpec(wt.shape, lambda b, i: (0, 0))
    return pl.pallas_call(
        functools.partial(_ffn_kernel, final_norm=final_norm),
        grid=(b_, s_ // tm),
        in_specs=[pl.BlockSpec((None, tm, D_MODEL), row3),
                  pl.BlockSpec((None, tm, SWA_Q), row3),
                  pl.BlockSpec((tm, SSM_CH), lambda b, i: (i, b)),
                  pl.BlockSpec((None, tm, MLA_HEADS * MLA_V), row3),
                  whole(wo), _layer_spec(g2, l), whole(wg), whole(wu), whole(wd),
                  pl.BlockSpec((1, D_MODEL), lambda b, i: (0, 0))],
        out_specs=pl.BlockSpec((None, tm, D_MODEL), row3),
        out_shape=jax.ShapeDtypeStruct(x.shape, F32),
        scratch_shapes=[pltpu.VMEM((tm, D_MODEL), BF16)],
        compiler_params=pltpu.CompilerParams(
            dimension_semantics=("arbitrary", "arbitrary"), vmem_limit_bytes=VMEM_LIMIT),
        name="out_proj_ffn",
    )(x, oa, ob, oc, wo, g2, wg, wu, wd, gf)


def _rot_half_cols(w):
    half = w.shape[-1] // 2
    return jnp.concatenate([-w[..., half:], w[..., :half]], axis=-1)


def _rope_tile(w):
    z = jnp.zeros(w.shape[:-1] + (MLA_NOPE,), w.dtype)
    z2 = jnp.zeros(w.shape[:-1] + (MLA_HEAD_PAD - MLA_NOPE - MLA_ROPE,), w.dtype)
    return jnp.concatenate([z, w, z2], axis=-1)


def _stacked_weights(w_in, w_q_up, w_kv_up):
    lead = w_in.shape[:-2]
    kr = w_in[..., -MLA_ROPE:]
    w_ext = jnp.concatenate([w_in[..., :-MLA_ROPE], _rope_tile(kr), _rope_tile(_rot_half_cols(kr))], axis=-1)
    qh = w_q_up.reshape(lead + (MLA_Q_RANK, MLA_HEADS, MLA_NOPE + MLA_ROPE))
    pad = jnp.zeros(lead + (MLA_Q_RANK, MLA_HEADS, MLA_HEAD_PAD - MLA_NOPE - MLA_ROPE), w_q_up.dtype)
    q_tiles = jnp.concatenate([qh, pad], axis=-1).reshape(lead + (MLA_Q_RANK, -1))
    q_rot = _rope_tile(_rot_half_cols(qh[..., MLA_NOPE:])).reshape(lead + (MLA_Q_RANK, -1))
    wq_ext = jnp.concatenate([q_tiles, q_rot], axis=-1)
    kvh = w_kv_up.reshape(lead + (MLA_KV_RANK, MLA_HEADS, MLA_NOPE + MLA_V))
    kpad = jnp.zeros(lead + (MLA_KV_RANK, MLA_HEADS, MLA_HEAD_PAD - MLA_NOPE), w_kv_up.dtype)
    k_tiles = jnp.concatenate([kvh[..., :MLA_NOPE], kpad], axis=-1).reshape(lead + (MLA_KV_RANK, -1))
    v_cols = kvh[..., MLA_NOPE:].reshape(lead + (MLA_KV_RANK, -1))
    wkv_ext = jnp.concatenate([k_tiles, v_cols], axis=-1)
    return w_ext.astype(BF16), wq_ext.astype(BF16), wkv_ext.astype(BF16)


def kernel(x, positions, rel_bias, ln1_g, w_in, sinks, ssm_a_re, ssm_a_im, ssm_log_dt, ssm_b_re, ssm_b_im,
           ssm_c_re, ssm_c_im, ssm_d, ssm_w_glu, mla_q_norm_g, mla_w_q_up, mla_kv_norm_g, mla_w_kv_up, w_out,
           ln2_g, w_gate, w_up, w_down, final_g):
    b_ = x.shape[0]
    cos_t, sin_t = _trig_tables(positions)
    bias = _band_bias_tables(rel_bias, sinks)
    row = lambda g: g[:, None, :]
    w_ext, wq_ext, wkv_ext = _stacked_weights(w_in, mla_w_q_up, mla_w_kv_up)
    bd, ar, ai, cd = jax.vmap(functools.partial(_ssm_params, batch=b_))(
        ssm_a_re, ssm_a_im, ssm_log_dt, ssm_b_re, ssm_b_im, ssm_c_re, ssm_c_im)
    wglu = ssm_w_glu.astype(BF16)
    gf = final_g.reshape(1, D_MODEL)
    for l in range(DEPTH):
        qa, ka, va, u, qm, km, vm = _in_proj(x, l, row(ln1_g), w_ext, cos_t, sin_t,
                                             row(mla_q_norm_g), wq_ext, row(mla_kv_norm_g), wkv_ext)
        o_a = _swa(qa, ka, va, l, bias)
        o_b = _ssm(u, b_, l, bd, ar, ai, cd, row(ssm_d), wglu)
        o_c, wo, wg, wu, wd = _mla(qm, km, vm, l, (w_out, w_gate, w_up, w_down))
        x = _ffn(x, o_a, o_b, o_c, l, wo, row(ln2_g), wg, wu, wd, gf, final_norm=(l == DEPTH - 1))
    return x
```

```python
import functools
import math

import jax
import jax.numpy as jnp
from jax import lax
from jax.experimental import pallas as pl
from jax.experimental.pallas import tpu as pltpu

D_MODEL = 1024
DEPTH = 2
HEAD_DIM = 64
SWA_Q_HEADS = 8
SWA_KV_HEADS = 2
SWA_BLOCK = 128
SWA_Q = SWA_Q_HEADS * HEAD_DIM
SWA_KV = SWA_KV_HEADS * HEAD_DIM
REL_BUCKETS = 32
REL_MAX_DIST = 128
SSM_CH = 256
SSM_GROUP = 16
SSM_GROUPS = SSM_CH // SSM_GROUP
SSM_STATE = 64
SSM_STATES = SSM_GROUPS * SSM_STATE
MLA_HEADS = 4
MLA_Q_RANK = 256
MLA_KV_RANK = 128
MLA_NOPE = 64
MLA_ROPE = 32
MLA_V = 64
ROPE_THETA = 10000.0
D_FF = 2816
EPS = 1e-6
NEG = -1e30

LANES = 128
SUBLANES = 8
MLA_HEAD_PAD = LANES
MLA_EXP2_SCALE = (MLA_NOPE + MLA_ROPE) ** -0.5 * math.log2(math.e)
VMEM_LIMIT = 56 * 1024 * 1024

IN_PROJ_TM = 1024
SWA_TQ = 512
MLA_TQ = 512
MLA_TK = 512
SSM_TS = 128
SSM_SUB = 2
FFN_TM = 512
FFN_CHUNK = D_FF

BF16 = jnp.bfloat16
F32 = jnp.float32


def _rms(x, g):
    return x * lax.rsqrt(jnp.mean(x * x, axis=-1, keepdims=True) + EPS) * g


def _dot(a, b):
    return jnp.dot(a, b, preferred_element_type=F32)


def _dot_nt(a, b):
    return lax.dot_general(a, b, (((1,), (1,)), ((), ())), preferred_element_type=F32)


def _layer_spec(arr, l):
    return pl.BlockSpec((None,) + arr.shape[1:], lambda *_: (l,) + (0,) * (arr.ndim - 1))


def _trig_kernel(pos_ref, freq_ref, place_ref, one_ref, cos_ref, sin_ref):
    pos = pos_ref[pl.ds(pl.program_id(0), 1), :].astype(F32)
    ang = freq_ref[...] * pos

    def place(t):
        hi = t.astype(BF16)
        rest = t - hi.astype(F32)
        mid = rest.astype(BF16)
        lo = (rest - mid.astype(F32)).astype(BF16)
        return sum(lax.dot_general(piece, place_ref[...], (((0,), (0,)), ((), ())), preferred_element_type=F32)
                   for piece in (hi, mid, lo))

    cos_ref[...] = place(jnp.cos(ang)) + one_ref[...]
    sin_ref[...] = place(jnp.sin(ang))


def _trig_tables(positions):
    b_, s_ = positions.shape
    half = MLA_ROPE // 2
    inv_freq = jnp.power(ROPE_THETA, -jnp.arange(half, dtype=F32) * 2.0 / MLA_ROPE).reshape(half, 1)
    lane = jnp.arange(MLA_HEAD_PAD)[None, :]
    rope_lane = (lane >= MLA_NOPE) & (lane < MLA_NOPE + MLA_ROPE)
    placement = (rope_lane & ((lane - MLA_NOPE) % half == jnp.arange(half)[:, None])).astype(BF16)
    off_rope = 1.0 - rope_lane.astype(F32)
    const = lambda b: (0, 0)
    cos_t, sin_t = pl.pallas_call(
        _trig_kernel,
        grid=(b_,),
        in_specs=[pl.BlockSpec((b_, s_), const),
                  pl.BlockSpec((half, 1), const),
                  pl.BlockSpec((half, MLA_HEAD_PAD), const),
                  pl.BlockSpec((1, MLA_HEAD_PAD), const)],
        out_specs=[pl.BlockSpec((None, s_, MLA_HEAD_PAD), lambda b: (b, 0, 0))] * 2,
        out_shape=[jax.ShapeDtypeStruct((b_, s_, MLA_HEAD_PAD), F32)] * 2,
        name="rope_tables",
    )(positions, inv_freq, placement, off_rope)
    return cos_t.reshape(b_ * s_, MLA_HEAD_PAD), sin_t.reshape(b_ * s_, MLA_HEAD_PAD)


def _in_proj_kernel(x_ref, g_ref, w_ref, cos_ref, sin_ref, qg_ref, wq_ref, kvg_ref, wkv_ref,
                    qa_ref, ka_ref, va_ref, u_ref, qm_ref, km_ref, vm_ref):
    h = _rms(x_ref[...], g_ref[...]).astype(BF16)
    proj = _dot(h, w_ref[...])
    qa_ref[...] = (proj[:, :SWA_Q] * (HEAD_DIM ** -0.5)).astype(BF16)
    c = SWA_Q
    ka_ref[...] = proj[:, c:c + SWA_KV].astype(BF16)
    c += SWA_KV
    va_ref[...] = proj[:, c:c + SWA_KV].astype(BF16)
    c += SWA_KV
    u_ref[...] = proj[:, c:c + SSM_CH]
    c += SSM_CH
    c_q = proj[:, c:c + MLA_Q_RANK]
    c += MLA_Q_RANK
    c_kv = proj[:, c:c + MLA_KV_RANK]
    c += MLA_KV_RANK
    k_r = proj[:, c:c + LANES]
    k_r_rot = proj[:, c + LANES:c + 2 * LANES]
    cos = cos_ref[...]
    sin = sin_ref[...]
    width = MLA_HEADS * MLA_HEAD_PAD
    q2 = _dot(_rms(c_q, qg_ref[...]).astype(BF16), wq_ref[...])
    kv2 = _dot(_rms(c_kv, kvg_ref[...]).astype(BF16), wkv_ref[...])
    k_pe = k_r * cos + k_r_rot * sin
    low = lax.broadcasted_iota(jnp.int32, (cos.shape[0], LANES), 1) < MLA_V
    for hd in range(MLA_HEADS):
        sl = slice(hd * MLA_HEAD_PAD, (hd + 1) * MLA_HEAD_PAD)
        sl_rot = slice(width + hd * MLA_HEAD_PAD, width + (hd + 1) * MLA_HEAD_PAD)
        qm_ref[:, sl] = (q2[:, sl] * cos + q2[:, sl_rot] * sin).astype(BF16)
        km_ref[:, sl] = (kv2[:, sl] + k_pe).astype(BF16)
        pair = kv2[:, width + (hd // 2) * LANES:width + (hd // 2 + 1) * LANES]
        keep = low if hd % 2 == 0 else jnp.logical_not(low)
        vm_ref[:, sl] = jnp.where(keep, pair, 1.0).astype(BF16)


def _in_proj(x, l, g, w_ext, cos_t, sin_t, qg, wq_ext, kvg, wkv_ext):
    b_, s_, _ = x.shape
    tm = IN_PROJ_TM
    nt = s_ // tm
    row3 = lambda b, i: (b, i, 0)
    trig = pl.BlockSpec((tm, MLA_HEAD_PAD), lambda b, i: (b * nt + i, 0))
    mla_w = MLA_HEADS * MLA_HEAD_PAD
    return pl.pallas_call(
        _in_proj_kernel,
        grid=(b_, nt),
        in_specs=[pl.BlockSpec((None, tm, D_MODEL), row3),
                  _layer_spec(g, l), _layer_spec(w_ext, l),
                  trig, trig,
                  _layer_spec(qg, l), _layer_spec(wq_ext, l), _layer_spec(kvg, l), _layer_spec(wkv_ext, l)],
        out_specs=[pl.BlockSpec((None, tm, SWA_Q), row3),
                   pl.BlockSpec((None, tm, SWA_KV), row3),
                   pl.BlockSpec((None, tm, SWA_KV), row3),
                   pl.BlockSpec((tm, SSM_CH), lambda b, i: (i, b)),
                   pl.BlockSpec((None, tm, mla_w), row3),
                   pl.BlockSpec((None, tm, mla_w), row3),
                   pl.BlockSpec((None, tm, mla_w), row3)],
        out_shape=[jax.ShapeDtypeStruct((b_, s_, SWA_Q), BF16),
                   jax.ShapeDtypeStruct((b_, s_, SWA_KV), BF16),
                   jax.ShapeDtypeStruct((b_, s_, SWA_KV), BF16),
                   jax.ShapeDtypeStruct((s_, b_ * SSM_CH), F32),
                   jax.ShapeDtypeStruct((b_, s_, mla_w), BF16),
                   jax.ShapeDtypeStruct((b_, s_, mla_w), BF16),
                   jax.ShapeDtypeStruct((b_, s_, mla_w), BF16)],
        compiler_params=pltpu.CompilerParams(
            dimension_semantics=("arbitrary", "arbitrary"), vmem_limit_bytes=VMEM_LIMIT),
        name="in_proj",
    )(x, g, w_ext, cos_t, sin_t, qg, wq_ext, kvg, wkv_ext)


def _swa_kernel(q_ref, k_ref, v_ref, bias_ref, o_ref, s_scr, m_scr):
    qi = pl.program_id(1)
    blocks = SWA_TQ // SWA_BLOCK
    tiles = SWA_Q // LANES
    band = 2 * SWA_BLOCK
    lane = lax.broadcasted_iota(jnp.int32, (band, LANES), 1)
    key = lax.broadcasted_iota(jnp.int32, (band, LANES), 0)
    low = lane < HEAD_DIM
    out_low = lax.broadcasted_iota(jnp.int32, (SWA_BLOCK, LANES), 1) < HEAD_DIM

    def band_of(ref, j):
        blk = qi * blocks + j
        start = pl.multiple_of(jnp.maximum(blk - 1, 0) * SWA_BLOCK, SWA_BLOCK)
        keep = key != jnp.where(blk == 0, band - 1, 0)
        t = ref[pl.ds(start, band), :]
        t = jnp.where(keep, t, jnp.zeros_like(t))
        return t, pltpu.roll(t, HEAD_DIM, 1)

    for j in range(blocks):
        table = jnp.minimum(qi * blocks + j, 1)
        kb, kb_sw = band_of(k_ref, j)
        zero = jnp.zeros_like(kb)
        k_var = ((jnp.where(low, kb, zero), jnp.where(low, zero, kb_sw)),
                 (jnp.where(low, kb_sw, zero), jnp.where(low, zero, kb)))
        rows = slice(j * SWA_BLOCK, (j + 1) * SWA_BLOCK)
        for i in range(tiles):
            kvh = (2 * i) // (SWA_Q_HEADS // SWA_KV_HEADS)
            q = q_ref[rows, i * LANES:(i + 1) * LANES]
            for half in range(2):
                u = (j * tiles + i) * 2 + half
                s = _dot_nt(q, k_var[kvh][half]) + bias_ref[table, 2 * i + half]
                s_scr[u] = s
                m_scr[u] = jnp.broadcast_to(jnp.max(s, axis=-1, keepdims=True), (SWA_BLOCK, LANES))
    for j in range(blocks):
        vb, vb_sw = band_of(v_ref, j)
        one = jnp.ones_like(vb)
        v_var = ((jnp.where(low, vb, one), jnp.where(low, one, vb_sw)),
                 (jnp.where(low, vb_sw, one), jnp.where(low, one, vb)))
        rows = slice(j * SWA_BLOCK, (j + 1) * SWA_BLOCK)
        for i in range(tiles):
            kvh = (2 * i) // (SWA_Q_HEADS // SWA_KV_HEADS)
            r = []
            for half in range(2):
                u = (j * tiles + i) * 2 + half
                m = m_scr[u]
                p = jnp.concatenate([jnp.exp(s_scr[u, :, t * LANES:(t + 1) * LANES] - m).astype(BF16)
                                     for t in range(band // LANES)], axis=-1)
                r.append(_dot(p, v_var[kvh][half]))
            denom = pltpu.roll(jnp.where(out_low, r[1], r[0]), HEAD_DIM, 1)
            o_ref[rows, i * LANES:(i + 1) * LANES] = (jnp.where(out_low, r[0], r[1]) / denom).astype(BF16)


def _swa(q, k, v, l, bias):
    b_, s_, _ = q.shape
    tq = SWA_TQ
    units = (tq // SWA_BLOCK) * SWA_Q_HEADS
    return pl.pallas_call(
        _swa_kernel,
        grid=(b_, s_ // tq),
        in_specs=[pl.BlockSpec((None, tq, SWA_Q), lambda b, i: (b, i, 0)),
                  pl.BlockSpec((None, s_, SWA_KV), lambda b, i: (b, 0, 0)),
                  pl.BlockSpec((None, s_, SWA_KV), lambda b, i: (b, 0, 0)),
                  _layer_spec(bias, l)],
        out_specs=pl.BlockSpec((None, tq, SWA_Q), lambda b, i: (b, i, 0)),
        out_shape=jax.ShapeDtypeStruct((b_, s_, SWA_Q), BF16),
        scratch_shapes=[pltpu.VMEM((units, SWA_BLOCK, 2 * SWA_BLOCK), F32),
                        pltpu.VMEM((units, SWA_BLOCK, LANES), F32)],
        compiler_params=pltpu.CompilerParams(
            dimension_semantics=("arbitrary", "arbitrary"), vmem_limit_bytes=VMEM_LIMIT),
        name="swa_attention",
    )(q, k, v, bias)


def _t5_bucket(dist):
    n = jnp.maximum(dist, 0)
    max_exact = REL_BUCKETS // 2
    large = max_exact + (jnp.log(jnp.maximum(n, 1).astype(F32) / max_exact)
                         / math.log(REL_MAX_DIST / max_exact)
                         * (REL_BUCKETS - max_exact)).astype(jnp.int32)
    large = jnp.minimum(large, REL_BUCKETS - 1)
    return jnp.where(n < max_exact, n, large)


def _band_bias_tables(rel_bias, sinks):
    qi = jnp.arange(SWA_BLOCK)[:, None]
    kj = jnp.arange(2 * SWA_BLOCK)[None, :]
    dist = qi + SWA_BLOCK - kj
    onehot = (_t5_bucket(dist)[None] == jnp.arange(REL_BUCKETS)[:, None, None]).astype(F32)
    b = jnp.einsum('nh,nqk->hqk', rel_bias.astype(F32), onehot, precision=lax.Precision.HIGHEST)
    valid = (dist >= 0) & (dist < SWA_BLOCK)
    normal = jnp.where(valid[None], b, NEG)
    first = jnp.concatenate([normal[:, :, SWA_BLOCK:], jnp.full_like(normal[:, :, SWA_BLOCK:], NEG)], axis=-1)
    sink_col = jnp.array([2 * SWA_BLOCK - 1, 0])[:, None, None, None]
    sink = sinks.astype(F32)[:, None, :, None, None]
    return jnp.where(kj == sink_col, sink, jnp.stack([first, normal])[None])


def _mla_kernel(q_ref, k_ref, v_ref, *rest):
    n_w = (len(rest) - 4) // 2
    w_f32, o_ref, w_bf16 = rest[:n_w], rest[n_w], rest[n_w + 1:2 * n_w + 1]
    s_scr, m_scr, acc_scr = rest[2 * n_w + 1:]
    for src, dst in zip(w_f32, w_bf16):
        dst[...] = src[...].astype(BF16)
    p = pl.program_id(1)
    tq, tk = MLA_TQ, MLA_TK
    n = q_ref.shape[0] // tq
    row = lax.broadcasted_iota(jnp.int32, (tq, tk), 0)
    col = lax.broadcasted_iota(jnp.int32, (tq, tk), 1)
    causal = col <= row
    out_low = lax.broadcasted_iota(jnp.int32, (tq, LANES), 1) < MLA_V
    heads = [slice(hd * MLA_HEAD_PAD, (hd + 1) * MLA_HEAD_PAD) for hd in range(MLA_HEADS)]
    lane_tiles = [slice(t * LANES, (t + 1) * LANES) for t in range(tk // LANES)]
    q_tiles = (p, n - 1 - p)

    def slot(t):
        if t >= n // 2:
            return 1, t - p - 1, t == n
        first = t <= p
        return jnp.where(first, 0, 1), jnp.where(first, t, t - p - 1), jnp.logical_and(first, t == p)

    def rows(idx, size):
        return pl.ds(pl.multiple_of(idx * size, size), size)

    m_scr[...] = jnp.full(m_scr.shape, NEG, F32)
    acc_scr[...] = jnp.zeros(acc_scr.shape, F32)
    for t in range(n + 1):
        which, kt, diag = slot(t)
        qt = q_tiles[which] if isinstance(which, int) else jnp.where(which == 0, q_tiles[0], q_tiles[1])
        for hd in range(MLA_HEADS):
            s = _dot_nt(q_ref[rows(qt, tq), heads[hd]], k_ref[rows(kt, tk), heads[hd]])
            if diag is True:
                s = jnp.where(causal, s, NEG)
            elif diag is not False:
                s = jnp.where(jnp.logical_or(causal, jnp.logical_not(diag)), s, NEG)
            s_scr[hd, t] = s
            m_scr[which, hd] = functools.reduce(jnp.maximum, [s[:, lt] for lt in lane_tiles], m_scr[which, hd])
    for which in range(2):
        for hd in range(MLA_HEADS):
            m_scr[which, hd] = jnp.broadcast_to(jnp.max(m_scr[which, hd], axis=-1, keepdims=True), (tq, LANES))
    for t in range(n + 1):
        which, kt, _ = slot(t)
        for hd in range(MLA_HEADS):
            m = m_scr[which, hd]
            e = jnp.concatenate([jnp.exp2((s_scr[hd, t, :, lt] - m) * MLA_EXP2_SCALE).astype(BF16)
                                 for lt in lane_tiles], axis=-1)
            acc_scr[which, hd] += _dot(e, v_ref[rows(kt, tk), heads[hd]])
    for which in range(2):
        for t in range(MLA_HEADS // 2):
            lo, hi = acc_scr[which, 2 * t], acc_scr[which, 2 * t + 1]
            lo = lo / lo[:, MLA_V:MLA_V + 1]
            hi = hi / hi[:, 0:1]
            o_ref[rows(q_tiles[which], tq), t * LANES:(t + 1) * LANES] = jnp.where(out_low, lo, hi).astype(BF16)


def _mla(q, k, v, l, weights):
    b_, s_, w = q.shape
    n = s_ // MLA_TQ
    steps = b_ * (n // 2)
    whole = lambda width: pl.BlockSpec((None, s_, width), lambda b, p: (b, 0, 0))
    slab_in = [pl.BlockSpec((None, wt.shape[1] // steps, wt.shape[2]), lambda b, p: (l, b * (n // 2) + p, 0))
               for wt in weights]
    slab_out = [pl.BlockSpec((wt.shape[1] // steps, wt.shape[2]), lambda b, p: (b * (n // 2) + p, 0))
                for wt in weights]
    return pl.pallas_call(
        _mla_kernel,
        grid=(b_, n // 2),
        in_specs=[whole(w), whole(w), whole(w)] + slab_in,
        out_specs=[whole(MLA_HEADS * MLA_V)] + slab_out,
        out_shape=[jax.ShapeDtypeStruct((b_, s_, MLA_HEADS * MLA_V), BF16)]
        + [jax.ShapeDtypeStruct(wt.shape[1:], BF16) for wt in weights],
        scratch_shapes=[pltpu.VMEM((MLA_HEADS, n + 1, MLA_TQ, MLA_TK), F32),
                        pltpu.VMEM((2, MLA_HEADS, MLA_TQ, LANES), F32),
                        pltpu.VMEM((2, MLA_HEADS, MLA_TQ, LANES), F32)],
        compiler_params=pltpu.CompilerParams(
            dimension_semantics=("arbitrary", "arbitrary"), vmem_limit_bytes=VMEM_LIMIT),
        name="mla_attention",
    )(q, k, v, *weights)


def _gelu_tanh(x):
    return 0.5 * x * (1.0 + jnp.tanh(math.sqrt(2.0 / math.pi) * (x + 0.044715 * (x * x * x))))


def _ssm_kernel(u_ref, bd_ref, ar_ref, ai_ref, cd_ref, d_ref, wglu_ref, y_ref,
                u_scr, x_scr, y_scr, sr_scr, si_scr):
    n = SSM_STATES
    batch = sr_scr.shape[0]
    lane_tiles = SSM_CH // LANES

    @pl.when(pl.program_id(0) == 0)
    def _():
        sr_scr[...] = jnp.zeros_like(sr_scr)
        si_scr[...] = jnp.zeros_like(si_scr)

    for b in range(batch):
        for lt in range(lane_tiles):
            c0 = b * SSM_CH + lt * LANES
            u_scr[lt, pl.ds(b, SSM_TS, stride=batch), :] = u_ref[:, c0:c0 + LANES]
    sub_t = SSM_TS // SSM_SUB
    sub_r = sub_t * batch
    subs = [slice(q * sub_r, (q + 1) * sub_r) for q in range(SSM_SUB)]
    us = [jnp.concatenate([u_scr[lt, rows, :] for lt in range(lane_tiles)], axis=-1) for rows in subs]
    for rows, u in zip(subs, us):
        u_bf = u.astype(BF16)
        x_scr[rows, :n] = _dot(u_bf, bd_ref[:, :n])
        x_scr[rows, n:] = _dot(u_bf, bd_ref[:, n:])
    ar = ar_ref[...]
    ai = ai_ref[...]
    xr, xi = sr_scr[...], si_scr[...]
    for rows, u in zip(subs, us):
        for t in range(sub_t):
            r = slice(rows.start + t * batch, rows.start + (t + 1) * batch)
            xr, xi = (ar * xr - ai * xi + x_scr[r, :n], ar * xi + ai * xr + x_scr[r, n:])
            x_scr[r, :n] = xr
            x_scr[r, n:] = xi
        mid = rows.start + sub_r // 2
        cx = jnp.concatenate([_dot(x_scr[rows.start:mid, :].astype(BF16), cd_ref[...]),
                              _dot(x_scr[mid:rows.stop, :].astype(BF16), cd_ref[...])], axis=0)
        y = cx + d_ref[...] * u
        y = _gelu_tanh(y)
        y = y * jax.nn.sigmoid(_dot(y.astype(BF16), wglu_ref[...]))
        for lt in range(lane_tiles):
            y_scr[lt, rows, :] = y[:, lt * LANES:(lt + 1) * LANES]
    sr_scr[...] = xr
    si_scr[...] = xi
    for b in range(batch):
        for lt in range(lane_tiles):
            c0 = b * SSM_CH + lt * LANES
            y_ref[:, c0:c0 + LANES] = y_scr[lt, pl.ds(b, SSM_TS, stride=batch), :].astype(BF16)


def _ssm(u, batch, l, bd, ar, ai, cd, d, wglu):
    s_, width = u.shape
    tr = SSM_TS * batch
    return pl.pallas_call(
        _ssm_kernel,
        grid=(s_ // SSM_TS,),
        in_specs=[pl.BlockSpec((SSM_TS, width), lambda i: (i, 0))]
        + [_layer_spec(a, l) for a in (bd, ar, ai, cd, d, wglu)],
        out_specs=pl.BlockSpec((SSM_TS, width), lambda i: (i, 0)),
        out_shape=jax.ShapeDtypeStruct((s_, width), BF16),
        scratch_shapes=[pltpu.VMEM((SSM_CH // LANES, tr, LANES), F32),
                        pltpu.VMEM((tr, 2 * SSM_STATES), F32),
                        pltpu.VMEM((SSM_CH // LANES, tr, LANES), F32),
                        pltpu.VMEM((batch, SSM_STATES), F32),
                        pltpu.VMEM((batch, SSM_STATES), F32)],
        compiler_params=pltpu.CompilerParams(
            dimension_semantics=("arbitrary",), vmem_limit_bytes=VMEM_LIMIT),
        name="s5_mixer",
    )(u, bd, ar, ai, cd, d, wglu)


def _ssm_params(a_re, a_im, log_dt, b_re, b_im, c_re, c_im, batch):
    lam = lax.complex(a_re.astype(F32), a_im.astype(F32))
    dt = jnp.exp(log_dt.astype(F32))
    a_bar = jnp.exp(lam * dt[:, None])
    b_bar = ((a_bar - 1.0) / lam)[..., None] * lax.complex(b_re.astype(F32), b_im.astype(F32))
    eye = jnp.eye(SSM_GROUPS, dtype=F32)

    def in_op(m):
        return jnp.einsum('gpc,gh->gchp', m, eye).reshape(SSM_CH, SSM_STATES)

    def out_op(m):
        return jnp.einsum('gcp,gh->gphc', m, eye).reshape(SSM_STATES, SSM_CH)

    bd = jnp.concatenate([in_op(jnp.real(b_bar)), in_op(jnp.imag(b_bar))], axis=1).astype(BF16)
    cd = jnp.concatenate([out_op(c_re.astype(F32)), -out_op(c_im.astype(F32))], axis=0).astype(BF16)
    ar = jnp.broadcast_to(jnp.real(a_bar).reshape(1, SSM_STATES), (batch, SSM_STATES))
    ai = jnp.broadcast_to(jnp.imag(a_bar).reshape(1, SSM_STATES), (batch, SSM_STATES))
    return bd, ar, ai, cd


def _ffn_kernel(x_ref, oa_ref, ob_ref, oc_ref, wo_ref, g2_ref, wg_ref, wu_ref, wd_ref, gf_ref, o_ref, h_scr,
                *, final_norm):
    wo_a = wo_ref[:SWA_Q, :]
    wo_b = wo_ref[SWA_Q:SWA_Q + SSM_CH, :]
    wo_c = wo_ref[SWA_Q + SSM_CH:, :]
    x = x_ref[...] + _dot(oa_ref[...], wo_a) + _dot(ob_ref[...], wo_b) + _dot(oc_ref[...], wo_c)
    o_ref[...] = x
    h_scr[...] = _rms(x, g2_ref[...]).astype(BF16)

    for c0 in range(0, D_FF, FFN_CHUNK):
        h = h_scr[...]
        gate = _dot(h, wg_ref[:, c0:c0 + FFN_CHUNK])
        up = _dot(h, wu_ref[:, c0:c0 + FFN_CHUNK])
        act = (gate * jax.nn.sigmoid(gate) * up).astype(BF16)
        o_ref[...] += _dot(act, wd_ref[c0:c0 + FFN_CHUNK, :])
    if final_norm:
        o_ref[...] = _rms(o_ref[...], gf_ref[...])


def _ffn(x, oa, ob, oc, l, wo, g2, wg, wu, wd, gf, final_norm):
    b_, s_, _ = x.shape
    tm = FFN_TM
    row3 = lambda b, i: (b, i, 0)
    whole = lambda wt: pl.BlockSpec(wt.shape, lambda b, i: (0, 0))
    return pl.pallas_call(
        functools.partial(_ffn_kernel, final_norm=final_norm),
        grid=(b_, s_ // tm),
        in_specs=[pl.BlockSpec((None, tm, D_MODEL), row3),
                  pl.BlockSpec((None, tm, SWA_Q), row3),
                  pl.BlockSpec((tm, SSM_CH), lambda b, i: (i, b)),
                  pl.BlockSpec((None, tm, MLA_HEADS * MLA_V), row3),
                  whole(wo), _layer_spec(g2, l), whole(wg), whole(wu), whole(wd),
                  pl.BlockSpec((1, D_MODEL), lambda b, i: (0, 0))],
        out_specs=pl.BlockSpec((None, tm, D_MODEL), row3),
        out_shape=jax.ShapeDtypeStruct(x.shape, F32),
        scratch_shapes=[pltpu.VMEM((tm, D_MODEL), BF16)],
        compiler_params=pltpu.CompilerParams(
            dimension_semantics=("arbitrary", "arbitrary"), vmem_limit_bytes=VMEM_LIMIT),
        name="out_proj_ffn",
    )(x, oa, ob, oc, wo, g2, wg, wu, wd, gf)


def _rot_half_cols(w):
    half = w.shape[-1] // 2
    return jnp.concatenate([-w[..., half:], w[..., :half]], axis=-1)


def _rope_tile(w):
    z = jnp.zeros(w.shape[:-1] + (MLA_NOPE,), w.dtype)
    z2 = jnp.zeros(w.shape[:-1] + (MLA_HEAD_PAD - MLA_NOPE - MLA_ROPE,), w.dtype)
    return jnp.concatenate([z, w, z2], axis=-1)


def _stacked_weights(w_in, w_q_up, w_kv_up):
    lead = w_in.shape[:-2]
    kr = w_in[..., -MLA_ROPE:]
    w_ext = jnp.concatenate([w_in[..., :-MLA_ROPE], _rope_tile(kr), _rope_tile(_rot_half_cols(kr))], axis=-1)
    qh = w_q_up.reshape(lead + (MLA_Q_RANK, MLA_HEADS, MLA_NOPE + MLA_ROPE))
    pad = jnp.zeros(lead + (MLA_Q_RANK, MLA_HEADS, MLA_HEAD_PAD - MLA_NOPE - MLA_ROPE), w_q_up.dtype)
    q_tiles = jnp.concatenate([qh, pad], axis=-1).reshape(lead + (MLA_Q_RANK, -1))
    q_rot = _rope_tile(_rot_half_cols(qh[..., MLA_NOPE:])).reshape(lead + (MLA_Q_RANK, -1))
    wq_ext = jnp.concatenate([q_tiles, q_rot], axis=-1)
    kvh = w_kv_up.reshape(lead + (MLA_KV_RANK, MLA_HEADS, MLA_NOPE + MLA_V))
    kpad = jnp.zeros(lead + (MLA_KV_RANK, MLA_HEADS, MLA_HEAD_PAD - MLA_NOPE), w_kv_up.dtype)
    k_tiles = jnp.concatenate([kvh[..., :MLA_NOPE], kpad], axis=-1).reshape(lead + (MLA_KV_RANK, -1))
    v_cols = kvh[..., MLA_NOPE:].reshape(lead + (MLA_KV_RANK, -1))
    wkv_ext = jnp.concatenate([k_tiles, v_cols], axis=-1)
    return w_ext.astype(BF16), wq_ext.astype(BF16), wkv_ext.astype(BF16)


def kernel(x, positions, rel_bias, ln1_g, w_in, sinks, ssm_a_re, ssm_a_im, ssm_log_dt, ssm_b_re, ssm_b_im,
           ssm_c_re, ssm_c_im, ssm_d, ssm_w_glu, mla_q_norm_g, mla_w_q_up, mla_kv_norm_g, mla_w_kv_up, w_out,
           ln2_g, w_gate, w_up, w_down, final_g):
    b_ = x.shape[0]
    cos_t, sin_t = _trig_tables(positions)
    bias = _band_bias_tables(rel_bias, sinks)
    row = lambda g: g[:, None, :]
    w_ext, wq_ext, wkv_ext = _stacked_weights(w_in, mla_w_q_up, mla_w_kv_up)
    bd, ar, ai, cd = jax.vmap(functools.partial(_ssm_params, batch=b_))(
        ssm_a_re, ssm_a_im, ssm_log_dt, ssm_b_re, ssm_b_im, ssm_c_re, ssm_c_im)
    wglu = ssm_w_glu.astype(BF16)
    gf = final_g.reshape(1, D_MODEL)
    for l in range(DEPTH):
        qa, ka, va, u, qm, km, vm = _in_proj(x, l, row(ln1_g), w_ext, cos_t, sin_t,
                                             row(mla_q_norm_g), wq_ext, row(mla_kv_norm_g), wkv_ext)
        o_a = _swa(qa, ka, va, l, bias)
        o_b = _ssm(u, b_, l, bd, ar, ai, cd, row(ssm_d), wglu)
        o_c, wo, wg, wu, wd = _mla(qm, km, vm, l, (w_out, w_gate, w_up, w_down))
        x = _ffn(x, o_a, o_b, o_c, l, wo, row(ln2_g), wg, wu, wd, gf, final_norm=(l == DEPTH - 1))
    return x
```

```python
import functools
import math

import jax
import jax.numpy as jnp
from jax import lax
from jax.experimental import pallas as pl
from jax.experimental.pallas import tpu as pltpu

D_MODEL = 1024
DEPTH = 2
HEAD_DIM = 64
SWA_Q_HEADS = 8
SWA_KV_HEADS = 2
SWA_BLOCK = 128
SWA_Q = SWA_Q_HEADS * HEAD_DIM
SWA_KV = SWA_KV_HEADS * HEAD_DIM
REL_BUCKETS = 32
REL_MAX_DIST = 128
SSM_CH = 256
SSM_GROUP = 16
SSM_GROUPS = SSM_CH // SSM_GROUP
SSM_STATE = 64
SSM_STATES = SSM_GROUPS * SSM_STATE
MLA_HEADS = 4
MLA_Q_RANK = 256
MLA_KV_RANK = 128
MLA_NOPE = 64
MLA_ROPE = 32
MLA_V = 64
ROPE_THETA = 10000.0
D_FF = 2816
EPS = 1e-6
NEG = -1e30

LANES = 128
V7X_VMEM_BYTES = 64 * 1024 * 1024
VMEM_LIMIT = V7X_VMEM_BYTES * 7 // 8
MLA_HEAD_PAD = LANES
MLA_EXP2_SCALE = (MLA_NOPE + MLA_ROPE) ** -0.5 * math.log2(math.e)

IN_PROJ_TM = 1024
SWA_TQ = 512
MLA_TQ = 512
MLA_TK = 512
SSM_TS = 128
SSM_SUB = 2
FFN_TM = 512

BF16 = jnp.bfloat16
F32 = jnp.float32


def _rms(x, g):
    return x * lax.rsqrt(jnp.mean(x * x, axis=-1, keepdims=True) + EPS) * g


def _dot(a, b):
    return jnp.dot(a, b, preferred_element_type=F32)


def _dot_nt(a, b):
    return lax.dot_general(a, b, (((1,), (1,)), ((), ())), preferred_element_type=F32)


def _bf16_pieces(t):
    hi = t.astype(BF16)
    rest = t - hi.astype(F32)
    mid = rest.astype(BF16)
    return hi, mid, (rest - mid.astype(F32)).astype(BF16)


def _layer_spec(arr, l):
    return pl.BlockSpec((None,) + arr.shape[1:], lambda *_: (l,) + (0,) * (arr.ndim - 1))


def _trig_kernel(pos_ref, freq_ref, place_ref, one_ref, cos_ref, sin_ref):
    pos = pos_ref[pl.ds(pl.program_id(0), 1), :].astype(F32)
    ang = freq_ref[...] * pos

    def place(t):
        return sum(lax.dot_general(piece, place_ref[...], (((0,), (0,)), ((), ())), preferred_element_type=F32)
                   for piece in _bf16_pieces(t))

    cos_ref[...] = place(jnp.cos(ang)) + one_ref[...]
    sin_ref[...] = place(jnp.sin(ang))


def _trig_tables(positions):
    b_, s_ = positions.shape
    half = MLA_ROPE // 2
    inv_freq = jnp.power(ROPE_THETA, -jnp.arange(half, dtype=F32) * 2.0 / MLA_ROPE).reshape(half, 1)
    lane = jnp.arange(MLA_HEAD_PAD)[None, :]
    rope_lane = (lane >= MLA_NOPE) & (lane < MLA_NOPE + MLA_ROPE)
    placement = (rope_lane & ((lane - MLA_NOPE) % half == jnp.arange(half)[:, None])).astype(BF16)
    off_rope = 1.0 - rope_lane.astype(F32)
    const = lambda b: (0, 0)
    cos_t, sin_t = pl.pallas_call(
        _trig_kernel,
        grid=(b_,),
        in_specs=[pl.BlockSpec((b_, s_), const),
                  pl.BlockSpec((half, 1), const),
                  pl.BlockSpec((half, MLA_HEAD_PAD), const),
                  pl.BlockSpec((1, MLA_HEAD_PAD), const)],
        out_specs=[pl.BlockSpec((None, s_, MLA_HEAD_PAD), lambda b: (b, 0, 0))] * 2,
        out_shape=[jax.ShapeDtypeStruct((b_, s_, MLA_HEAD_PAD), F32)] * 2,
        name="rope_tables",
    )(positions, inv_freq, placement, off_rope)
    return cos_t.reshape(b_ * s_, MLA_HEAD_PAD), sin_t.reshape(b_ * s_, MLA_HEAD_PAD)


def _in_proj_kernel(x_ref, g_ref, w_ref, kplace_ref, cos_ref, sin_ref, qg_ref, wq_ref, kvg_ref, wkv_ref,
                    qa_ref, ka_ref, va_ref, u_ref, qm_ref, km_ref, vm_ref, w_scr):
    @pl.when(jnp.logical_and(pl.program_id(0) == 0, pl.program_id(1) == 0))
    def _():
        w_scr[...] = w_ref[...].astype(BF16)

    h = _rms(x_ref[...], g_ref[...]).astype(BF16)
    proj = _dot(h, w_scr[...])
    qa_ref[...] = (proj[:, :SWA_Q] * (HEAD_DIM ** -0.5)).astype(BF16)
    c = SWA_Q
    ka_ref[...] = proj[:, c:c + SWA_KV].astype(BF16)
    c += SWA_KV
    va_ref[...] = proj[:, c:c + SWA_KV].astype(BF16)
    c += SWA_KV
    u_ref[...] = proj[:, c:c + SSM_CH]
    c += SSM_CH
    c_q = proj[:, c:c + MLA_Q_RANK]
    c += MLA_Q_RANK
    c_kv = proj[:, c:c + MLA_KV_RANK]
    c += MLA_KV_RANK
    k_r2 = sum(_dot(piece, kplace_ref[...]) for piece in _bf16_pieces(proj[:, c:c + MLA_ROPE]))
    k_r, k_r_rot = k_r2[:, :LANES], k_r2[:, LANES:]
    cos = cos_ref[...]
    sin = sin_ref[...]
    width = MLA_HEADS * MLA_HEAD_PAD
    q2 = _dot(_rms(c_q, qg_ref[...]).astype(BF16), wq_ref[...])
    kv2 = _dot(_rms(c_kv, kvg_ref[...]).astype(BF16), wkv_ref[...])
    k_pe = k_r * cos + k_r_rot * sin
    low = lax.broadcasted_iota(jnp.int32, (cos.shape[0], LANES), 1) < MLA_V
    for hd in range(MLA_HEADS):
        sl = slice(hd * MLA_HEAD_PAD, (hd + 1) * MLA_HEAD_PAD)
        sl_rot = slice(width + hd * MLA_HEAD_PAD, width + (hd + 1) * MLA_HEAD_PAD)
        qm_ref[:, sl] = (q2[:, sl] * cos + q2[:, sl_rot] * sin).astype(BF16)
        km_ref[:, sl] = (kv2[:, sl] + k_pe).astype(BF16)
        pair = kv2[:, width + (hd // 2) * LANES:width + (hd // 2 + 1) * LANES]
        keep = low if hd % 2 == 0 else jnp.logical_not(low)
        vm_ref[:, sl] = jnp.where(keep, pair, 1.0).astype(BF16)


def _in_proj(x, l, g, w_in, cos_t, sin_t, qg, wq_ext, kvg, wkv_ext):
    b_, s_, _ = x.shape
    tm = IN_PROJ_TM
    nt = s_ // tm
    row3 = lambda b, i: (b, i, 0)
    trig = pl.BlockSpec((tm, MLA_HEAD_PAD), lambda b, i: (b * nt + i, 0))
    mla_w = MLA_HEADS * MLA_HEAD_PAD
    kplace = _rope_key_placement()
    return pl.pallas_call(
        _in_proj_kernel,
        grid=(b_, nt),
        in_specs=[pl.BlockSpec((None, tm, D_MODEL), row3),
                  _layer_spec(g, l), _layer_spec(w_in, l),
                  pl.BlockSpec(kplace.shape, lambda b, i: (0, 0)),
                  trig, trig,
                  _layer_spec(qg, l), _layer_spec(wq_ext, l), _layer_spec(kvg, l), _layer_spec(wkv_ext, l)],
        out_specs=[pl.BlockSpec((None, tm, SWA_Q), row3),
                   pl.BlockSpec((None, tm, SWA_KV), row3),
                   pl.BlockSpec((None, tm, SWA_KV), row3),
                   pl.BlockSpec((tm, SSM_CH), lambda b, i: (i, b)),
                   pl.BlockSpec((None, tm, mla_w), row3),
                   pl.BlockSpec((None, tm, mla_w), row3),
                   pl.BlockSpec((None, tm, mla_w), row3)],
        out_shape=[jax.ShapeDtypeStruct((b_, s_, SWA_Q), BF16),
                   jax.ShapeDtypeStruct((b_, s_, SWA_KV), BF16),
                   jax.ShapeDtypeStruct((b_, s_, SWA_KV), BF16),
                   jax.ShapeDtypeStruct((s_, b_ * SSM_CH), F32),
                   jax.ShapeDtypeStruct((b_, s_, mla_w), BF16),
                   jax.ShapeDtypeStruct((b_, s_, mla_w), BF16),
                   jax.ShapeDtypeStruct((b_, s_, mla_w), BF16)],
        scratch_shapes=[pltpu.VMEM(w_in.shape[1:], BF16)],
        compiler_params=pltpu.CompilerParams(
            dimension_semantics=("arbitrary", "arbitrary"), vmem_limit_bytes=VMEM_LIMIT),
        name="in_proj",
    )(x, g, w_in, kplace, cos_t, sin_t, qg, wq_ext, kvg, wkv_ext)


def _swa_kernel(q_ref, k_ref, v_ref, bias_ref, o_ref, s_scr, m_scr):
    qi = pl.program_id(1)
    blocks = SWA_TQ // SWA_BLOCK
    tiles = SWA_Q // LANES
    band = 2 * SWA_BLOCK
    lane = lax.broadcasted_iota(jnp.int32, (band, LANES), 1)
    key = lax.broadcasted_iota(jnp.int32, (band, LANES), 0)
    low = lane < HEAD_DIM
    out_low = lax.broadcasted_iota(jnp.int32, (SWA_BLOCK, LANES), 1) < HEAD_DIM

    def band_of(ref, j):
        blk = qi * blocks + j
        start = pl.multiple_of(jnp.maximum(blk - 1, 0) * SWA_BLOCK, SWA_BLOCK)
        keep = key != jnp.where(blk == 0, band - 1, 0)
        t = ref[pl.ds(start, band), :]
        t = jnp.where(keep, t, jnp.zeros_like(t))
        return t, pltpu.roll(t, HEAD_DIM, 1)

    for j in range(blocks):
        table = jnp.minimum(qi * blocks + j, 1)
        kb, kb_sw = band_of(k_ref, j)
        zero = jnp.zeros_like(kb)
        k_var = ((jnp.where(low, kb, zero), jnp.where(low, zero, kb_sw)),
                 (jnp.where(low, kb_sw, zero), jnp.where(low, zero, kb)))
        rows = slice(j * SWA_BLOCK, (j + 1) * SWA_BLOCK)
        for i in range(tiles):
            kvh = (2 * i) // (SWA_Q_HEADS // SWA_KV_HEADS)
            q = q_ref[rows, i * LANES:(i + 1) * LANES]
            for half in range(2):
                u = (j * tiles + i) * 2 + half
                s = _dot_nt(q, k_var[kvh][half]) + bias_ref[table, 2 * i + half]
                s_scr[u] = s
                m_scr[u] = jnp.broadcast_to(jnp.max(s, axis=-1, keepdims=True), (SWA_BLOCK, LANES))
    for j in range(blocks):
        vb, vb_sw = band_of(v_ref, j)
        one = jnp.ones_like(vb)
        v_var = ((jnp.where(low, vb, one), jnp.where(low, one, vb_sw)),
                 (jnp.where(low, vb_sw, one), jnp.where(low, one, vb)))
        rows = slice(j * SWA_BLOCK, (j + 1) * SWA_BLOCK)
        for i in range(tiles):
            kvh = (2 * i) // (SWA_Q_HEADS // SWA_KV_HEADS)
            r = []
            for half in range(2):
                u = (j * tiles + i) * 2 + half
                m = m_scr[u]
                p = jnp.concatenate([jnp.exp(s_scr[u, :, t * LANES:(t + 1) * LANES] - m).astype(BF16)
                                     for t in range(band // LANES)], axis=-1)
                r.append(_dot(p, v_var[kvh][half]))
            denom = pltpu.roll(jnp.where(out_low, r[1], r[0]), HEAD_DIM, 1)
            o_ref[rows, i * LANES:(i + 1) * LANES] = (jnp.where(out_low, r[0], r[1]) / denom).astype(BF16)


def _swa(q, k, v, l, bias):
    b_, s_, _ = q.shape
    tq = SWA_TQ
    units = (tq // SWA_BLOCK) * SWA_Q_HEADS
    return pl.pallas_call(
        _swa_kernel,
        grid=(b_, s_ // tq),
        in_specs=[pl.BlockSpec((None, tq, SWA_Q), lambda b, i: (b, i, 0)),
                  pl.BlockSpec((None, s_, SWA_KV), lambda b, i: (b, 0, 0)),
                  pl.BlockSpec((None, s_, SWA_KV), lambda b, i: (b, 0, 0)),
                  _layer_spec(bias, l)],
        out_specs=pl.BlockSpec((None, tq, SWA_Q), lambda b, i: (b, i, 0)),
        out_shape=jax.ShapeDtypeStruct((b_, s_, SWA_Q), BF16),
        scratch_shapes=[pltpu.VMEM((units, SWA_BLOCK, 2 * SWA_BLOCK), F32),
                        pltpu.VMEM((units, SWA_BLOCK, LANES), F32)],
        compiler_params=pltpu.CompilerParams(
            dimension_semantics=("arbitrary", "arbitrary"), vmem_limit_bytes=VMEM_LIMIT),
        name="swa_attention",
    )(q, k, v, bias)


def _t5_bucket(dist):
    n = jnp.maximum(dist, 0)
    max_exact = REL_BUCKETS // 2
    large = max_exact + (jnp.log(jnp.maximum(n, 1).astype(F32) / max_exact)
                         / math.log(REL_MAX_DIST / max_exact)
                         * (REL_BUCKETS - max_exact)).astype(jnp.int32)
    large = jnp.minimum(large, REL_BUCKETS - 1)
    return jnp.where(n < max_exact, n, large)


def _band_bias_tables(rel_bias, sinks):
    qi = jnp.arange(SWA_BLOCK)[:, None]
    kj = jnp.arange(2 * SWA_BLOCK)[None, :]
    dist = qi + SWA_BLOCK - kj
    onehot = (_t5_bucket(dist)[None] == jnp.arange(REL_BUCKETS)[:, None, None]).astype(F32)
    b = jnp.einsum('nh,nqk->hqk', rel_bias.astype(F32), onehot, precision=lax.Precision.HIGHEST)
    valid = (dist >= 0) & (dist < SWA_BLOCK)
    normal = jnp.where(valid[None], b, NEG)
    first = jnp.concatenate([normal[:, :, SWA_BLOCK:], jnp.full_like(normal[:, :, SWA_BLOCK:], NEG)], axis=-1)
    sink_col = jnp.array([2 * SWA_BLOCK - 1, 0])[:, None, None, None]
    sink = sinks.astype(F32)[:, None, :, None, None]
    return jnp.where(kj == sink_col, sink, jnp.stack([first, normal])[None])


def _mla_kernel(q_ref, k_ref, v_ref, *rest):
    n_w = (len(rest) - 4) // 2
    w_f32, o_ref, w_bf16 = rest[:n_w], rest[n_w], rest[n_w + 1:2 * n_w + 1]
    s_scr, m_scr, acc_scr = rest[2 * n_w + 1:]
    for src, dst in zip(w_f32, w_bf16):
        dst[...] = src[...].astype(BF16)
    p = pl.program_id(1)
    tq, tk = MLA_TQ, MLA_TK
    n = q_ref.shape[0] // tq
    row = lax.broadcasted_iota(jnp.int32, (tq, tk), 0)
    col = lax.broadcasted_iota(jnp.int32, (tq, tk), 1)
    causal = col <= row
    out_low = lax.broadcasted_iota(jnp.int32, (tq, LANES), 1) < MLA_V
    heads = [slice(hd * MLA_HEAD_PAD, (hd + 1) * MLA_HEAD_PAD) for hd in range(MLA_HEADS)]
    lane_tiles = [slice(t * LANES, (t + 1) * LANES) for t in range(tk // LANES)]
    q_tiles = (p, n - 1 - p)

    def slot(t):
        if t >= n // 2:
            return 1, t - p - 1, t == n
        first = t <= p
        return jnp.where(first, 0, 1), jnp.where(first, t, t - p - 1), jnp.logical_and(first, t == p)

    def rows(idx, size):
        return pl.ds(pl.multiple_of(idx * size, size), size)

    m_scr[...] = jnp.full(m_scr.shape, NEG, F32)
    acc_scr[...] = jnp.zeros(acc_scr.shape, F32)
    for t in range(n + 1):
        which, kt, diag = slot(t)
        qt = q_tiles[which] if isinstance(which, int) else jnp.where(which == 0, q_tiles[0], q_tiles[1])
        for hd in range(MLA_HEADS):
            s = _dot_nt(q_ref[rows(qt, tq), heads[hd]], k_ref[rows(kt, tk), heads[hd]])
            if diag is True:
                s = jnp.where(causal, s, NEG)
            elif diag is not False:
                s = jnp.where(jnp.logical_or(causal, jnp.logical_not(diag)), s, NEG)
            s_scr[hd, t] = s
            m_scr[which, hd] = functools.reduce(jnp.maximum, [s[:, lt] for lt in lane_tiles], m_scr[which, hd])
    for which in range(2):
        for hd in range(MLA_HEADS):
            m_scr[which, hd] = jnp.broadcast_to(jnp.max(m_scr[which, hd], axis=-1, keepdims=True), (tq, LANES))
    for t in range(n + 1):
        which, kt, _ = slot(t)
        for hd in range(MLA_HEADS):
            m = m_scr[which, hd]
            e = jnp.concatenate([jnp.exp2((s_scr[hd, t, :, lt] - m) * MLA_EXP2_SCALE).astype(BF16)
                                 for lt in lane_tiles], axis=-1)
            acc_scr[which, hd] += _dot(e, v_ref[rows(kt, tk), heads[hd]])
    for which in range(2):
        for t in range(MLA_HEADS // 2):
            lo, hi = acc_scr[which, 2 * t], acc_scr[which, 2 * t + 1]
            lo = lo / lo[:, MLA_V:MLA_V + 1]
            hi = hi / hi[:, 0:1]
            o_ref[rows(q_tiles[which], tq), t * LANES:(t + 1) * LANES] = jnp.where(out_low, lo, hi).astype(BF16)


def _mla(q, k, v, l, weights):
    b_, s_, w = q.shape
    n = s_ // MLA_TQ
    steps = b_ * (n // 2)
    whole = lambda width: pl.BlockSpec((None, s_, width), lambda b, p: (b, 0, 0))
    slab_in = [pl.BlockSpec((None, wt.shape[1] // steps, wt.shape[2]), lambda b, p: (l, b * (n // 2) + p, 0))
               for wt in weights]
    slab_out = [pl.BlockSpec((wt.shape[1] // steps, wt.shape[2]), lambda b, p: (b * (n // 2) + p, 0))
                for wt in weights]
    return pl.pallas_call(
        _mla_kernel,
        grid=(b_, n // 2),
        in_specs=[whole(w), whole(w), whole(w)] + slab_in,
        out_specs=[whole(MLA_HEADS * MLA_V)] + slab_out,
        out_shape=[jax.ShapeDtypeStruct((b_, s_, MLA_HEADS * MLA_V), BF16)]
        + [jax.ShapeDtypeStruct(wt.shape[1:], BF16) for wt in weights],
        scratch_shapes=[pltpu.VMEM((MLA_HEADS, n + 1, MLA_TQ, MLA_TK), F32),
                        pltpu.VMEM((2, MLA_HEADS, MLA_TQ, LANES), F32),
                        pltpu.VMEM((2, MLA_HEADS, MLA_TQ, LANES), F32)],
        compiler_params=pltpu.CompilerParams(
            dimension_semantics=("arbitrary", "arbitrary"), vmem_limit_bytes=VMEM_LIMIT),
        name="mla_attention",
    )(q, k, v, *weights)


def _gelu_tanh(x):
    return 0.5 * x * (1.0 + jnp.tanh(math.sqrt(2.0 / math.pi) * (x + 0.044715 * (x * x * x))))


def _ssm_kernel(u_ref, bd_ref, ar_ref, ai_ref, cd_ref, d_ref, wglu_ref, y_ref,
                u_scr, x_scr, y_scr, sr_scr, si_scr):
    n = SSM_STATES
    batch = sr_scr.shape[0]
    lane_tiles = SSM_CH // LANES

    @pl.when(pl.program_id(0) == 0)
    def _():
        sr_scr[...] = jnp.zeros_like(sr_scr)
        si_scr[...] = jnp.zeros_like(si_scr)

    for b in range(batch):
        for lt in range(lane_tiles):
            c0 = b * SSM_CH + lt * LANES
            u_scr[lt, pl.ds(b, SSM_TS, stride=batch), :] = u_ref[:, c0:c0 + LANES]
    sub_t = SSM_TS // SSM_SUB
    sub_r = sub_t * batch
    subs = [slice(q * sub_r, (q + 1) * sub_r) for q in range(SSM_SUB)]
    us = [jnp.concatenate([u_scr[lt, rows, :] for lt in range(lane_tiles)], axis=-1) for rows in subs]
    for rows, u in zip(subs, us):
        u_bf = u.astype(BF16)
        x_scr[rows, :n] = _dot(u_bf, bd_ref[:, :n])
        x_scr[rows, n:] = _dot(u_bf, bd_ref[:, n:])
    ar = ar_ref[...]
    ai = ai_ref[...]
    xr, xi = sr_scr[...], si_scr[...]
    for rows, u in zip(subs, us):
        for t in range(sub_t):
            r = slice(rows.start + t * batch, rows.start + (t + 1) * batch)
            xr, xi = (ar * xr - ai * xi + x_scr[r, :n], ar * xi + ai * xr + x_scr[r, n:])
            x_scr[r, :n] = xr
            x_scr[r, n:] = xi
        mid = rows.start + sub_r // 2
        cx = jnp.concatenate([_dot(x_scr[rows.start:mid, :].astype(BF16), cd_ref[...]),
                              _dot(x_scr[mid:rows.stop, :].astype(BF16), cd_ref[...])], axis=0)
        y = cx + d_ref[...] * u
        y = _gelu_tanh(y)
        y = y * jax.nn.sigmoid(_dot(y.astype(BF16), wglu_ref[...]))
        for lt in range(lane_tiles):
            y_scr[lt, rows, :] = y[:, lt * LANES:(lt + 1) * LANES]
    sr_scr[...] = xr
    si_scr[...] = xi
    for b in range(batch):
        for lt in range(lane_tiles):
            c0 = b * SSM_CH + lt * LANES
            y_ref[:, c0:c0 + LANES] = y_scr[lt, pl.ds(b, SSM_TS, stride=batch), :].astype(BF16)


def _ssm(u, batch, l, bd, ar, ai, cd, d, wglu):
    s_, width = u.shape
    tr = SSM_TS * batch
    return pl.pallas_call(
        _ssm_kernel,
        grid=(s_ // SSM_TS,),
        in_specs=[pl.BlockSpec((SSM_TS, width), lambda i: (i, 0))]
        + [_layer_spec(a, l) for a in (bd, ar, ai, cd, d, wglu)],
        out_specs=pl.BlockSpec((SSM_TS, width), lambda i: (i, 0)),
        out_shape=jax.ShapeDtypeStruct((s_, width), BF16),
        scratch_shapes=[pltpu.VMEM((SSM_CH // LANES, tr, LANES), F32),
                        pltpu.VMEM((tr, 2 * SSM_STATES), F32),
                        pltpu.VMEM((SSM_CH // LANES, tr, LANES), F32),
                        pltpu.VMEM((batch, SSM_STATES), F32),
                        pltpu.VMEM((batch, SSM_STATES), F32)],
        compiler_params=pltpu.CompilerParams(
            dimension_semantics=("arbitrary",), vmem_limit_bytes=VMEM_LIMIT),
        name="s5_mixer",
    )(u, bd, ar, ai, cd, d, wglu)


def _ssm_params(a_re, a_im, log_dt, b_re, b_im, c_re, c_im, batch):
    lam = lax.complex(a_re.astype(F32), a_im.astype(F32))
    dt = jnp.exp(log_dt.astype(F32))
    a_bar = jnp.exp(lam * dt[:, None])
    b_bar = ((a_bar - 1.0) / lam)[..., None] * lax.complex(b_re.astype(F32), b_im.astype(F32))
    eye = jnp.eye(SSM_GROUPS, dtype=F32)

    def in_op(m):
        return jnp.einsum('gpc,gh->gchp', m, eye).reshape(SSM_CH, SSM_STATES)

    def out_op(m):
        return jnp.einsum('gcp,gh->gphc', m, eye).reshape(SSM_STATES, SSM_CH)

    bd = jnp.concatenate([in_op(jnp.real(b_bar)), in_op(jnp.imag(b_bar))], axis=1).astype(BF16)
    cd = jnp.concatenate([out_op(c_re.astype(F32)), -out_op(c_im.astype(F32))], axis=0).astype(BF16)
    ar = jnp.broadcast_to(jnp.real(a_bar).reshape(1, SSM_STATES), (batch, SSM_STATES))
    ai = jnp.broadcast_to(jnp.imag(a_bar).reshape(1, SSM_STATES), (batch, SSM_STATES))
    return bd, ar, ai, cd


def _ffn_kernel(x_ref, oa_ref, ob_ref, oc_ref, wo_ref, g2_ref, wg_ref, wu_ref, wd_ref, gf_ref, o_ref, h_scr,
                *, final_norm):
    wo_a = wo_ref[:SWA_Q, :]
    wo_b = wo_ref[SWA_Q:SWA_Q + SSM_CH, :]
    wo_c = wo_ref[SWA_Q + SSM_CH:, :]
    x = x_ref[...] + _dot(oa_ref[...], wo_a) + _dot(ob_ref[...], wo_b) + _dot(oc_ref[...], wo_c)
    o_ref[...] = x
    h_scr[...] = _rms(x, g2_ref[...]).astype(BF16)

    h = h_scr[...]
    gate = _dot(h, wg_ref[...])
    up = _dot(h, wu_ref[...])
    act = (gate * jax.nn.sigmoid(gate) * up).astype(BF16)
    o_ref[...] += _dot(act, wd_ref[...])
    if final_norm:
        o_ref[...] = _rms(o_ref[...], gf_ref[...])


def _ffn(x, oa, ob, oc, l, wo, g2, wg, wu, wd, gf, final_norm):
    b_, s_, _ = x.shape
    tm = FFN_TM
    row3 = lambda b, i: (b, i, 0)
    whole = lambda wt: pl.BlockSpec(wt.shape, lambda b, i: (0, 0))
    return pl.pallas_call(
        functools.partial(_ffn_kernel, final_norm=final_norm),
        grid=(b_, s_ // tm),
        in_specs=[pl.BlockSpec((None, tm, D_MODEL), row3),
                  pl.BlockSpec((None, tm, SWA_Q), row3),
                  pl.BlockSpec((tm, SSM_CH), lambda b, i: (i, b)),
                  pl.BlockSpec((None, tm, MLA_HEADS * MLA_V), row3),
                  whole(wo), _layer_spec(g2, l), whole(wg), whole(wu), whole(wd),
                  pl.BlockSpec((1, D_MODEL), lambda b, i: (0, 0))],
        out_specs=pl.BlockSpec((None, tm, D_MODEL), row3),
        out_shape=jax.ShapeDtypeStruct(x.shape, F32),
        scratch_shapes=[pltpu.VMEM((tm, D_MODEL), BF16)],
        compiler_params=pltpu.CompilerParams(
            dimension_semantics=("arbitrary", "arbitrary"), vmem_limit_bytes=VMEM_LIMIT),
        name="out_proj_ffn",
    )(x, oa, ob, oc, wo, g2, wg, wu, wd, gf)


def _rot_half_cols(w):
    half = w.shape[-1] // 2
    return jnp.concatenate([-w[..., half:], w[..., :half]], axis=-1)


def _rope_tile(w):
    z = jnp.zeros(w.shape[:-1] + (MLA_NOPE,), w.dtype)
    z2 = jnp.zeros(w.shape[:-1] + (MLA_HEAD_PAD - MLA_NOPE - MLA_ROPE,), w.dtype)
    return jnp.concatenate([z, w, z2], axis=-1)


def _rope_key_placement():
    half = MLA_ROPE // 2
    j = jnp.arange(MLA_ROPE)
    plain = jnp.zeros((MLA_ROPE, LANES), F32).at[j, MLA_NOPE + j].set(1.0)
    sign = jnp.where(j < half, 1.0, -1.0)
    rot = jnp.zeros((MLA_ROPE, LANES), F32).at[j, MLA_NOPE + (j + half) % MLA_ROPE].set(sign)
    return jnp.concatenate([plain, rot], axis=1).astype(BF16)


def _stacked_weights(w_q_up, w_kv_up):
    lead = w_q_up.shape[:-2]
    qh =w_q_up.reshape(lead + (MLA_Q_RANK, MLA_HEADS, MLA_NOPE + MLA_ROPE))
    pad = jnp.zeros(lead + (MLA_Q_RANK, MLA_HEADS, MLA_HEAD_PAD - MLA_NOPE - MLA_ROPE), w_q_up.dtype)
    q_tiles = jnp.concatenate([qh, pad], axis=-1).reshape(lead + (MLA_Q_RANK, -1))
    q_rot = _rope_tile(_rot_half_cols(qh[..., MLA_NOPE:])).reshape(lead + (MLA_Q_RANK, -1))
    wq_ext = jnp.concatenate([q_tiles, q_rot], axis=-1)
    kvh = w_kv_up.reshape(lead + (MLA_KV_RANK, MLA_HEADS, MLA_NOPE + MLA_V))
    kpad = jnp.zeros(lead + (MLA_KV_RANK, MLA_HEADS, MLA_HEAD_PAD - MLA_NOPE), w_kv_up.dtype)
    k_tiles = jnp.concatenate([kvh[..., :MLA_NOPE], kpad], axis=-1).reshape(lead + (MLA_KV_RANK, -1))
    v_cols = kvh[..., MLA_NOPE:].reshape(lead + (MLA_KV_RANK, -1))
    wkv_ext = jnp.concatenate([k_tiles, v_cols], axis=-1)
    return wq_ext.astype(BF16), wkv_ext.astype(BF16)


def kernel(x, positions, rel_bias, ln1_g, w_in, sinks, ssm_a_re, ssm_a_im, ssm_log_dt, ssm_b_re, ssm_b_im,
           ssm_c_re, ssm_c_im, ssm_d, ssm_w_glu, mla_q_norm_g, mla_w_q_up, mla_kv_norm_g, mla_w_kv_up, w_out,
           ln2_g, w_gate, w_up, w_down, final_g):
    b_ = x.shape[0]
    cos_t, sin_t = _trig_tables(positions)
    bias = _band_bias_tables(rel_bias, sinks)
    row = lambda g: g[:, None, :]
    wq_ext, wkv_ext = _stacked_weights(mla_w_q_up, mla_w_kv_up)
    bd, ar, ai, cd = jax.vmap(functools.partial(_ssm_params, batch=b_))(
        ssm_a_re, ssm_a_im, ssm_log_dt, ssm_b_re, ssm_b_im, ssm_c_re, ssm_c_im)
    wglu = ssm_w_glu.astype(BF16)
    gf = final_g.reshape(1, D_MODEL)
    for l in range(DEPTH):
        qa, ka, va, u, qm, km, vm = _in_proj(x, l, row(ln1_g), w_in, cos_t, sin_t,
                                             row(mla_q_norm_g), wq_ext, row(mla_kv_norm_g), wkv_ext)
        o_a = _swa(qa, ka, va, l, bias)
        o_b = _ssm(u, b_, l, bd, ar, ai, cd, row(ssm_d), wglu)
        o_c, wo, wg, wu, wd = _mla(qm, km, vm, l, (w_out, w_gate, w_up, w_down))
        x = _ffn(x, o_a, o_b, o_c, l, wo, row(ln2_g), wg, wu, wd, gf, final_norm=(l == DEPTH - 1))
    return x
```

```python
import functools
import math

import jax
import jax.numpy as jnp
import numpy as np
from jax import lax
from jax.experimental import pallas as pl
from jax.experimental.pallas import tpu as pltpu

D_MODEL = 1024
DEPTH = 2
HEAD_DIM = 64
SWA_Q_HEADS = 8
SWA_KV_HEADS = 2
SWA_BLOCK = 128
SWA_Q = SWA_Q_HEADS * HEAD_DIM
SWA_KV = SWA_KV_HEADS * HEAD_DIM
REL_BUCKETS = 32
REL_MAX_DIST = 128
SSM_CH = 256
SSM_GROUP = 16
SSM_GROUPS = SSM_CH // SSM_GROUP
SSM_STATE = 64
SSM_STATES = SSM_GROUPS * SSM_STATE
MLA_HEADS = 4
MLA_Q_RANK = 256
MLA_KV_RANK = 128
MLA_NOPE = 64
MLA_ROPE = 32
MLA_V = 64
ROPE_THETA = 10000.0
D_FF = 2816
EPS = 1e-6
NEG = -1e30

LANES = 128
V7X_VMEM_BYTES = 64 * 1024 * 1024
VMEM_LIMIT = V7X_VMEM_BYTES * 7 // 8
MLA_HEAD_PAD = LANES
MLA_EXP2_SCALE = (MLA_NOPE + MLA_ROPE) ** -0.5 * math.log2(math.e)

IN_PROJ_TM = 1024
SWA_TQ = 512
MLA_TQ = 512
MLA_TK = 512
SSM_TS = 128
SSM_SUB = 2
FFN_TM = 512

BF16 = jnp.bfloat16
F32 = jnp.float32


def _rms(x, g):
    return x * lax.rsqrt(jnp.mean(x * x, axis=-1, keepdims=True) + EPS) * g


def _dot(a, b):
    return jnp.dot(a, b, preferred_element_type=F32)


def _dot_nt(a, b):
    return lax.dot_general(a, b, (((1,), (1,)), ((), ())), preferred_element_type=F32)


def _bf16_pieces(t):
    hi = t.astype(BF16)
    rest = t - hi.astype(F32)
    mid = rest.astype(BF16)
    return hi, mid, (rest - mid.astype(F32)).astype(BF16)


def _layer_spec(arr, l):
    return pl.BlockSpec((None,) + arr.shape[1:], lambda *_: (l,) + (0,) * (arr.ndim - 1))


def _trig_kernel(pos_ref, freq_ref, place_ref, one_ref, cos_ref, sin_ref):
    pos = pos_ref[pl.ds(pl.program_id(0), 1), :].astype(F32)
    ang = freq_ref[...] * pos

    def place(t):
        return sum(lax.dot_general(piece, place_ref[...], (((0,), (0,)), ((), ())), preferred_element_type=F32)
                   for piece in _bf16_pieces(t))

    cos_ref[...] = place(jnp.cos(ang)) + one_ref[...]
    sin_ref[...] = place(jnp.sin(ang))


def _trig_tables(positions):
    b_, s_ = positions.shape
    half = MLA_ROPE // 2
    inv_freq = jnp.power(ROPE_THETA, -jnp.arange(half, dtype=F32) * 2.0 / MLA_ROPE).reshape(half, 1)
    lane = np.arange(MLA_HEAD_PAD)[None, :]
    rope_lane = (lane >= MLA_NOPE) & (lane < MLA_NOPE + MLA_ROPE)
    placement = jnp.asarray(rope_lane & ((lane - MLA_NOPE) % half == np.arange(half)[:, None]), BF16)
    off_rope = jnp.asarray(~rope_lane, F32)
    const = lambda b: (0, 0)
    cos_t, sin_t = pl.pallas_call(
        _trig_kernel,
        grid=(b_,),
        in_specs=[pl.BlockSpec((b_, s_), const),
                  pl.BlockSpec((half, 1), const),
                  pl.BlockSpec((half, MLA_HEAD_PAD), const),
                  pl.BlockSpec((1, MLA_HEAD_PAD), const)],
        out_specs=[pl.BlockSpec((None, s_, MLA_HEAD_PAD), lambda b: (b, 0, 0))] * 2,
        out_shape=[jax.ShapeDtypeStruct((b_, s_, MLA_HEAD_PAD), F32)] * 2,
        name="rope_tables",
    )(positions, inv_freq, placement, off_rope)
    return cos_t.reshape(b_ * s_, MLA_HEAD_PAD), sin_t.reshape(b_ * s_, MLA_HEAD_PAD)


def _in_proj_kernel(x_ref, g_ref, w_ref, kplace_ref, cos_ref, sin_ref, qg_ref, wq_ref, kvg_ref, wkv_ref,
                    qa_ref, ka_ref, va_ref, u_ref, qm_ref, km_ref, vm_ref):
    h = _rms(x_ref[...], g_ref[...]).astype(BF16)
    proj = _dot(h, w_ref[...])
    qa_ref[...] = (proj[:, :SWA_Q] * (HEAD_DIM ** -0.5)).astype(BF16)
    c = SWA_Q
    ka_ref[...] = proj[:, c:c + SWA_KV].astype(BF16)
    c += SWA_KV
    va_ref[...] = proj[:, c:c + SWA_KV].astype(BF16)
    c += SWA_KV
    u_ref[...] = proj[:, c:c + SSM_CH]
    c += SSM_CH
    c_q = proj[:, c:c + MLA_Q_RANK]
    c += MLA_Q_RANK
    c_kv = proj[:, c:c + MLA_KV_RANK]
    c += MLA_KV_RANK
    k_r2 = sum(_dot(piece, kplace_ref[...]) for piece in _bf16_pieces(proj[:, c:c + MLA_ROPE]))
    k_r, k_r_rot = k_r2[:, :LANES], k_r2[:, LANES:]
    cos = cos_ref[...]
    sin = sin_ref[...]
    width = MLA_HEADS * MLA_HEAD_PAD
    q2 = _dot(_rms(c_q, qg_ref[...]).astype(BF16), wq_ref[...])
    kv2 = _dot(_rms(c_kv, kvg_ref[...]).astype(BF16), wkv_ref[...])
    k_pe = k_r * cos + k_r_rot * sin
    low = lax.broadcasted_iota(jnp.int32, (cos.shape[0], LANES), 1) < MLA_V
    for hd in range(MLA_HEADS):
        sl = slice(hd * MLA_HEAD_PAD, (hd + 1) * MLA_HEAD_PAD)
        sl_rot = slice(width + hd * MLA_HEAD_PAD, width + (hd + 1) * MLA_HEAD_PAD)
        qm_ref[:, sl] = (q2[:, sl] * cos + q2[:, sl_rot] * sin).astype(BF16)
        km_ref[:, sl] = (kv2[:, sl] + k_pe).astype(BF16)
        pair = kv2[:, width + (hd // 2) * LANES:width + (hd // 2 + 1) * LANES]
        keep = low if hd % 2 == 0 else jnp.logical_not(low)
        vm_ref[:, sl] = jnp.where(keep, pair, 1.0).astype(BF16)


def _in_proj(x, l, g, w_in, cos_t, sin_t, qg, wq_ext, kvg, wkv_ext):
    b_, s_, _ = x.shape
    tm = IN_PROJ_TM
    nt = s_ // tm
    row3 = lambda b, i: (b, i, 0)
    trig = pl.BlockSpec((tm, MLA_HEAD_PAD), lambda b, i: (b * nt + i, 0))
    mla_w = MLA_HEADS * MLA_HEAD_PAD
    kplace = _rope_key_placement()
    return pl.pallas_call(
        _in_proj_kernel,
        grid=(b_, nt),
        in_specs=[pl.BlockSpec((None, tm, D_MODEL), row3),
                  _layer_spec(g, l), _layer_spec(w_in, l),
                  pl.BlockSpec(kplace.shape, lambda b, i: (0, 0)),
                  trig, trig,
                  _layer_spec(qg, l), _layer_spec(wq_ext, l), _layer_spec(kvg, l), _layer_spec(wkv_ext, l)],
        out_specs=[pl.BlockSpec((None, tm, SWA_Q), row3),
                   pl.BlockSpec((None, tm, SWA_KV), row3),
                   pl.BlockSpec((None, tm, SWA_KV), row3),
                   pl.BlockSpec((tm, SSM_CH), lambda b, i: (i, b)),
                   pl.BlockSpec((None, tm, mla_w), row3),
                   pl.BlockSpec((None, tm, mla_w), row3),
                   pl.BlockSpec((None, tm, mla_w), row3)],
        out_shape=[jax.ShapeDtypeStruct((b_, s_, SWA_Q), BF16),
                   jax.ShapeDtypeStruct((b_, s_, SWA_KV), BF16),
                   jax.ShapeDtypeStruct((b_, s_, SWA_KV), BF16),
                   jax.ShapeDtypeStruct((s_, b_ * SSM_CH), F32),
                   jax.ShapeDtypeStruct((b_, s_, mla_w), BF16),
                   jax.ShapeDtypeStruct((b_, s_, mla_w), BF16),
                   jax.ShapeDtypeStruct((b_, s_, mla_w), BF16)],
        compiler_params=pltpu.CompilerParams(
            dimension_semantics=("arbitrary", "arbitrary"), vmem_limit_bytes=VMEM_LIMIT),
        name="in_proj",
    )(x, g, w_in, kplace, cos_t, sin_t, qg, wq_ext, kvg, wkv_ext)


def _swa_kernel(q_ref, k_ref, v_ref, bias_ref, o_ref, s_scr, m_scr):
    qi = pl.program_id(1)
    blocks = SWA_TQ // SWA_BLOCK
    tiles = SWA_Q // LANES
    band = 2 * SWA_BLOCK
    lane = lax.broadcasted_iota(jnp.int32, (band, LANES), 1)
    key = lax.broadcasted_iota(jnp.int32, (band, LANES), 0)
    low = lane < HEAD_DIM
    out_low = lax.broadcasted_iota(jnp.int32, (SWA_BLOCK, LANES), 1) < HEAD_DIM

    def band_of(ref, j):
        blk = qi * blocks + j
        start = pl.multiple_of(jnp.maximum(blk - 1, 0) * SWA_BLOCK, SWA_BLOCK)
        keep = key != jnp.where(blk == 0, band - 1, 0)
        t = ref[pl.ds(start, band), :]
        t = jnp.where(keep, t, jnp.zeros_like(t))
        return t, pltpu.roll(t, HEAD_DIM, 1)

    for j in range(blocks):
        table = jnp.minimum(qi * blocks + j, 1)
        kb, kb_sw = band_of(k_ref, j)
        zero = jnp.zeros_like(kb)
        k_var = ((jnp.where(low, kb, zero), jnp.where(low, zero, kb_sw)),
                 (jnp.where(low, kb_sw, zero), jnp.where(low, zero, kb)))
        rows = slice(j * SWA_BLOCK, (j + 1) * SWA_BLOCK)
        for i in range(tiles):
            kvh = (2 * i) // (SWA_Q_HEADS // SWA_KV_HEADS)
            q = q_ref[rows, i * LANES:(i + 1) * LANES]
            for half in range(2):
                u = (j * tiles + i) * 2 + half
                s = _dot_nt(q, k_var[kvh][half]) + bias_ref[table, 2 * i + half]
                s_scr[u] = s
                m_scr[u] = jnp.broadcast_to(jnp.max(s, axis=-1, keepdims=True), (SWA_BLOCK, LANES))
    for j in range(blocks):
        vb, vb_sw = band_of(v_ref, j)
        one = jnp.ones_like(vb)
        v_var = ((jnp.where(low, vb, one), jnp.where(low, one, vb_sw)),
                 (jnp.where(low, vb_sw, one), jnp.where(low, one, vb)))
        rows = slice(j * SWA_BLOCK, (j + 1) * SWA_BLOCK)
        for i in range(tiles):
            kvh = (2 * i) // (SWA_Q_HEADS // SWA_KV_HEADS)
            r = []
            for half in range(2):
                u = (j * tiles + i) * 2 + half
                m = m_scr[u]
                p = jnp.concatenate([jnp.exp(s_scr[u, :, t * LANES:(t + 1) * LANES] - m).astype(BF16)
                                     for t in range(band // LANES)], axis=-1)
                r.append(_dot(p, v_var[kvh][half]))
            denom = pltpu.roll(jnp.where(out_low, r[1], r[0]), HEAD_DIM, 1)
            o_ref[rows, i * LANES:(i + 1) * LANES] = (jnp.where(out_low, r[0], r[1]) / denom).astype(BF16)


def _swa(q, k, v, l, bias):
    b_, s_, _ = q.shape
    tq = SWA_TQ
    units = (tq // SWA_BLOCK) * SWA_Q_HEADS
    return pl.pallas_call(
        _swa_kernel,
        grid=(b_, s_ // tq),
        in_specs=[pl.BlockSpec((None, tq, SWA_Q), lambda b, i: (b, i, 0)),
                  pl.BlockSpec((None, s_, SWA_KV), lambda b, i: (b, 0, 0)),
                  pl.BlockSpec((None, s_, SWA_KV), lambda b, i: (b, 0, 0)),
                  _layer_spec(bias, l)],
        out_specs=pl.BlockSpec((None, tq, SWA_Q), lambda b, i: (b, i, 0)),
        out_shape=jax.ShapeDtypeStruct((b_, s_, SWA_Q), BF16),
        scratch_shapes=[pltpu.VMEM((units, SWA_BLOCK, 2 * SWA_BLOCK), F32),
                        pltpu.VMEM((units, SWA_BLOCK, LANES), F32)],
        compiler_params=pltpu.CompilerParams(
            dimension_semantics=("arbitrary", "arbitrary"), vmem_limit_bytes=VMEM_LIMIT),
        name="swa_attention",
    )(q, k, v, bias)


def _t5_bucket(dist):
    n = jnp.maximum(dist, 0)
    max_exact = REL_BUCKETS // 2
    large = max_exact + (jnp.log(jnp.maximum(n, 1).astype(F32) / max_exact)
                         / math.log(REL_MAX_DIST / max_exact)
                         * (REL_BUCKETS - max_exact)).astype(jnp.int32)
    large = jnp.minimum(large, REL_BUCKETS - 1)
    return jnp.where(n < max_exact, n, large)


def _band_bias_tables(rel_bias, sinks):
    qi = jnp.arange(SWA_BLOCK)[:, None]
    kj = jnp.arange(2 * SWA_BLOCK)[None, :]
    dist = qi + SWA_BLOCK - kj
    onehot = (_t5_bucket(dist)[None] == jnp.arange(REL_BUCKETS)[:, None, None]).astype(F32)
    b = jnp.einsum('nh,nqk->hqk', rel_bias.astype(F32), onehot, precision=lax.Precision.HIGHEST)
    valid = (dist >= 0) & (dist < SWA_BLOCK)
    normal = jnp.where(valid[None], b, NEG)
    first = jnp.concatenate([normal[:, :, SWA_BLOCK:], jnp.full_like(normal[:, :, SWA_BLOCK:], NEG)], axis=-1)
    sink_col = jnp.array([2 * SWA_BLOCK - 1, 0])[:, None, None, None]
    sink = sinks.astype(F32)[:, None, :, None, None]
    return jnp.where(kj == sink_col, sink, jnp.stack([first, normal])[None])


def _mla_kernel(q_ref, k_ref, v_ref, *rest):
    n_w = (len(rest) - 4) // 2
    w_f32, o_ref, w_bf16 = rest[:n_w], rest[n_w], rest[n_w + 1:2 * n_w + 1]
    s_scr, m_scr, acc_scr = rest[2 * n_w + 1:]
    for src, dst in zip(w_f32, w_bf16):
        dst[...] = src[...].astype(BF16)
    p = pl.program_id(1)
    tq, tk = MLA_TQ, MLA_TK
    n = q_ref.shape[0] // tq
    row = lax.broadcasted_iota(jnp.int32, (tq, tk), 0)
    col = lax.broadcasted_iota(jnp.int32, (tq, tk), 1)
    causal = col <= row
    out_low = lax.broadcasted_iota(jnp.int32, (tq, LANES), 1) < MLA_V
    heads = [slice(hd * MLA_HEAD_PAD, (hd + 1) * MLA_HEAD_PAD) for hd in range(MLA_HEADS)]
    lane_tiles = [slice(t * LANES, (t + 1) * LANES) for t in range(tk // LANES)]
    q_tiles = (p, n - 1 - p)

    def slot(t):
        if t >= n // 2:
            return 1, t - p - 1, t == n
        first = t <= p
        return jnp.where(first, 0, 1), jnp.where(first, t, t - p - 1), jnp.logical_and(first, t == p)

    def rows(idx, size):
        return pl.ds(pl.multiple_of(idx * size, size), size)

    m_scr[...] = jnp.full(m_scr.shape, NEG, F32)
    acc_scr[...] = jnp.zeros(acc_scr.shape, F32)
    for t in range(n + 1):
        which, kt, diag = slot(t)
        qt = q_tiles[which] if isinstance(which, int) else jnp.where(which == 0, q_tiles[0], q_tiles[1])
        for hd in range(MLA_HEADS):
            s = _dot_nt(q_ref[rows(qt, tq), heads[hd]], k_ref[rows(kt, tk), heads[hd]])
            if diag is True:
                s = jnp.where(causal, s, NEG)
            elif diag is not False:
                s = jnp.where(jnp.logical_or(causal, jnp.logical_not(diag)), s, NEG)
            s_scr[hd, t] = s
            m_scr[which, hd] = functools.reduce(jnp.maximum, [s[:, lt] for lt in lane_tiles], m_scr[which, hd])
    for which in range(2):
        for hd in range(MLA_HEADS):
            m_scr[which, hd] = jnp.broadcast_to(jnp.max(m_scr[which, hd], axis=-1, keepdims=True), (tq, LANES))
    for t in range(n + 1):
        which, kt, _ = slot(t)
        for hd in range(MLA_HEADS):
            m = m_scr[which, hd]
            e = jnp.concatenate([jnp.exp2((s_scr[hd, t, :, lt] - m) * MLA_EXP2_SCALE).astype(BF16)
                                 for lt in lane_tiles], axis=-1)
            acc_scr[which, hd] += _dot(e, v_ref[rows(kt, tk), heads[hd]])
    for which in range(2):
        for t in range(MLA_HEADS // 2):
            lo, hi = acc_scr[which, 2 * t], acc_scr[which, 2 * t + 1]
            lo = lo / lo[:, MLA_V:MLA_V + 1]
            hi = hi / hi[:, 0:1]
            o_ref[rows(q_tiles[which], tq), t * LANES:(t + 1) * LANES] = jnp.where(out_low, lo, hi).astype(BF16)


def _mla(q, k, v, l, weights):
    b_, s_, w = q.shape
    n = s_ // MLA_TQ
    steps = b_ * (n // 2)
    whole = lambda width: pl.BlockSpec((None, s_, width), lambda b, p: (b, 0, 0))
    slab_in = [pl.BlockSpec((None, wt.shape[1] // steps, wt.shape[2]), lambda b, p: (l, b * (n // 2) + p, 0))
               for wt in weights]
    slab_out = [pl.BlockSpec((wt.shape[1] // steps, wt.shape[2]), lambda b, p: (b * (n // 2) + p, 0))
                for wt in weights]
    return pl.pallas_call(
        _mla_kernel,
        grid=(b_, n // 2),
        in_specs=[whole(w), whole(w), whole(w)] + slab_in,
        out_specs=[whole(MLA_HEADS * MLA_V)] + slab_out,
        out_shape=[jax.ShapeDtypeStruct((b_, s_, MLA_HEADS * MLA_V), BF16)]
        + [jax.ShapeDtypeStruct(wt.shape[1:], BF16) for wt in weights],
        scratch_shapes=[pltpu.VMEM((MLA_HEADS, n + 1, MLA_TQ, MLA_TK), F32),
                        pltpu.VMEM((2, MLA_HEADS, MLA_TQ, LANES), F32),
                        pltpu.VMEM((2, MLA_HEADS, MLA_TQ, LANES), F32)],
        compiler_params=pltpu.CompilerParams(
            dimension_semantics=("arbitrary", "arbitrary"), vmem_limit_bytes=VMEM_LIMIT),
        name="mla_attention",
    )(q, k, v, *weights)


def _gelu_tanh(x):
    return 0.5 * x * (1.0 + jnp.tanh(math.sqrt(2.0 / math.pi) * (x + 0.044715 * (x * x * x))))


def _ssm_kernel(u_ref, bd_ref, ar_ref, ai_ref, cd_ref, d_ref, wglu_ref, y_ref,
                u_scr, x_scr, y_scr, sr_scr, si_scr):
    n = SSM_STATES
    batch = sr_scr.shape[0]
    lane_tiles = SSM_CH // LANES

    @pl.when(pl.program_id(0) == 0)
    def _():
        sr_scr[...] = jnp.zeros_like(sr_scr)
        si_scr[...] = jnp.zeros_like(si_scr)

    for b in range(batch):
        for lt in range(lane_tiles):
            c0 = b * SSM_CH + lt * LANES
            u_scr[lt, pl.ds(b, SSM_TS, stride=batch), :] = u_ref[:, c0:c0 + LANES]
    sub_t = SSM_TS // SSM_SUB
    sub_r = sub_t * batch
    subs = [slice(q * sub_r, (q + 1) * sub_r) for q in range(SSM_SUB)]
    us = [jnp.concatenate([u_scr[lt, rows, :] for lt in range(lane_tiles)], axis=-1) for rows in subs]
    for rows, u in zip(subs, us):
        u_bf = u.astype(BF16)
        x_scr[rows, :n] = _dot(u_bf, bd_ref[:, :n])
        x_scr[rows, n:] = _dot(u_bf, bd_ref[:, n:])
    ar = ar_ref[...]
    ai = ai_ref[...]
    xr, xi = sr_scr[...], si_scr[...]
    for rows, u in zip(subs, us):
        for t in range(sub_t):
            r = slice(rows.start + t * batch, rows.start + (t + 1) * batch)
            xr, xi = (ar * xr - ai * xi + x_scr[r, :n], ar * xi + ai * xr + x_scr[r, n:])
            x_scr[r, :n] = xr
            x_scr[r, n:] = xi
        mid = rows.start + sub_r // 2
        cx = jnp.concatenate([_dot(x_scr[rows.start:mid, :].astype(BF16), cd_ref[...]),
                              _dot(x_scr[mid:rows.stop, :].astype(BF16), cd_ref[...])], axis=0)
        y = cx + d_ref[...] * u
        y = _gelu_tanh(y)
        y = y * jax.nn.sigmoid(_dot(y.astype(BF16), wglu_ref[...]))
        for lt in range(lane_tiles):
            y_scr[lt, rows, :] = y[:, lt * LANES:(lt + 1) * LANES]
    sr_scr[...] = xr
    si_scr[...] = xi
    for b in range(batch):
        for lt in range(lane_tiles):
            c0 = b * SSM_CH + lt * LANES
            y_ref[:, c0:c0 + LANES] = y_scr[lt, pl.ds(b, SSM_TS, stride=batch), :].astype(BF16)


def _ssm(u, batch, l, bd, ar, ai, cd, d, wglu):
    s_, width = u.shape
    tr = SSM_TS * batch
    return pl.pallas_call(
        _ssm_kernel,
        grid=(s_ // SSM_TS,),
        in_specs=[pl.BlockSpec((SSM_TS, width), lambda i: (i, 0))]
        + [_layer_spec(a, l) for a in (bd, ar, ai, cd, d, wglu)],
        out_specs=pl.BlockSpec((SSM_TS, width), lambda i: (i, 0)),
        out_shape=jax.ShapeDtypeStruct((s_, width), BF16),
        scratch_shapes=[pltpu.VMEM((SSM_CH // LANES, tr, LANES), F32),
                        pltpu.VMEM((tr, 2 * SSM_STATES), F32),
                        pltpu.VMEM((SSM_CH // LANES, tr, LANES), F32),
                        pltpu.VMEM((batch, SSM_STATES), F32),
                        pltpu.VMEM((batch, SSM_STATES), F32)],
        compiler_params=pltpu.CompilerParams(
            dimension_semantics=("arbitrary",), vmem_limit_bytes=VMEM_LIMIT),
        name="s5_mixer",
    )(u, bd, ar, ai, cd, d, wglu)


def _ssm_params(a_re, a_im, log_dt, b_re, b_im, c_re, c_im, batch):
    lam = lax.complex(a_re.astype(F32), a_im.astype(F32))
    dt = jnp.exp(log_dt.astype(F32))
    a_bar = jnp.exp(lam * dt[:, None])
    b_bar = ((a_bar - 1.0) / lam)[..., None] * lax.complex(b_re.astype(F32), b_im.astype(F32))
    eye = jnp.eye(SSM_GROUPS, dtype=F32)

    def in_op(m):
        return jnp.einsum('gpc,gh->gchp', m, eye).reshape(SSM_CH, SSM_STATES)

    def out_op(m):
        return jnp.einsum('gcp,gh->gphc', m, eye).reshape(SSM_STATES, SSM_CH)

    bd = jnp.concatenate([in_op(jnp.real(b_bar)), in_op(jnp.imag(b_bar))], axis=1).astype(BF16)
    cd = jnp.concatenate([out_op(c_re.astype(F32)), -out_op(c_im.astype(F32))], axis=0).astype(BF16)
    ar = jnp.broadcast_to(jnp.real(a_bar).reshape(1, SSM_STATES), (batch, SSM_STATES))
    ai = jnp.broadcast_to(jnp.imag(a_bar).reshape(1, SSM_STATES), (batch, SSM_STATES))
    return bd, ar, ai, cd


def _ffn_kernel(x_ref, oa_ref, ob_ref, oc_ref, wo_ref, g2_ref, wg_ref, wu_ref, wd_ref, gf_ref, o_ref, h_scr,
                *, final_norm):
    wo_a = wo_ref[:SWA_Q, :]
    wo_b = wo_ref[SWA_Q:SWA_Q + SSM_CH, :]
    wo_c = wo_ref[SWA_Q + SSM_CH:, :]
    x = x_ref[...] + _dot(oa_ref[...], wo_a) + _dot(ob_ref[...], wo_b) + _dot(oc_ref[...], wo_c)
    o_ref[...] = x
    h_scr[...] = _rms(x, g2_ref[...]).astype(BF16)

    h = h_scr[...]
    gate = _dot(h, wg_ref[...])
    up = _dot(h, wu_ref[...])
    act = (gate * jax.nn.sigmoid(gate) * up).astype(BF16)
    o_ref[...] += _dot(act, wd_ref[...])
    if final_norm:
        o_ref[...] = _rms(o_ref[...], gf_ref[...])


def _ffn(x, oa, ob, oc, l, wo, g2, wg, wu, wd, gf, final_norm):
    b_, s_, _ = x.shape
    tm = FFN_TM
    row3 = lambda b, i: (b, i, 0)
    whole = lambda wt: pl.BlockSpec(wt.shape, lambda b, i: (0, 0))
    return pl.pallas_call(
        functools.partial(_ffn_kernel, final_norm=final_norm),
        grid=(b_, s_ // tm),
        in_specs=[pl.BlockSpec((None, tm, D_MODEL), row3),
                  pl.BlockSpec((None, tm, SWA_Q), row3),
                  pl.BlockSpec((tm, SSM_CH), lambda b, i: (i, b)),
                  pl.BlockSpec((None, tm, MLA_HEADS * MLA_V), row3),
                  whole(wo), _layer_spec(g2, l), whole(wg), whole(wu), whole(wd),
                  pl.BlockSpec((1, D_MODEL), lambda b, i: (0, 0))],
        out_specs=pl.BlockSpec((None, tm, D_MODEL), row3),
        out_shape=jax.ShapeDtypeStruct(x.shape, F32),
        scratch_shapes=[pltpu.VMEM((tm, D_MODEL), BF16)],
        compiler_params=pltpu.CompilerParams(
            dimension_semantics=("arbitrary", "arbitrary"), vmem_limit_bytes=VMEM_LIMIT),
        name="out_proj_ffn",
    )(x, oa, ob, oc, wo, g2, wg, wu, wd, gf)


def _rot_half_cols(w):
    half = w.shape[-1] // 2
    return jnp.concatenate([-w[..., half:], w[..., :half]], axis=-1)


def _rope_tile(w):
    z = jnp.zeros(w.shape[:-1] + (MLA_NOPE,), w.dtype)
    z2 = jnp.zeros(w.shape[:-1] + (MLA_HEAD_PAD - MLA_NOPE - MLA_ROPE,), w.dtype)
    return jnp.concatenate([z, w, z2], axis=-1)


def _rope_key_placement():
    half = MLA_ROPE // 2
    j = np.arange(MLA_ROPE)
    place = np.zeros((MLA_ROPE, 2 * LANES), np.float32)
    place[j, MLA_NOPE + j] = 1.0
    place[j, LANES + MLA_NOPE + (j + half) % MLA_ROPE] = np.where(j < half, 1.0, -1.0)
    return jnp.asarray(place, BF16)


def _stacked_weights(w_q_up, w_kv_up):
    lead = w_q_up.shape[:-2]
    qh = w_q_up.reshape(lead + (MLA_Q_RANK, MLA_HEADS, MLA_NOPE + MLA_ROPE))
    pad = jnp.zeros(lead + (MLA_Q_RANK, MLA_HEADS, MLA_HEAD_PAD - MLA_NOPE - MLA_ROPE), w_q_up.dtype)
    q_tiles = jnp.concatenate([qh, pad], axis=-1).reshape(lead + (MLA_Q_RANK, -1))
    q_rot = _rope_tile(_rot_half_cols(qh[..., MLA_NOPE:])).reshape(lead + (MLA_Q_RANK, -1))
    wq_ext = jnp.concatenate([q_tiles, q_rot], axis=-1)
    kvh = w_kv_up.reshape(lead + (MLA_KV_RANK, MLA_HEADS, MLA_NOPE + MLA_V))
    kpad = jnp.zeros(lead + (MLA_KV_RANK, MLA_HEADS, MLA_HEAD_PAD - MLA_NOPE), w_kv_up.dtype)
    k_tiles = jnp.concatenate([kvh[..., :MLA_NOPE], kpad], axis=-1).reshape(lead + (MLA_KV_RANK, -1))
    v_cols = kvh[..., MLA_NOPE:].reshape(lead + (MLA_KV_RANK, -1))
    wkv_ext = jnp.concatenate([k_tiles, v_cols], axis=-1)
    return wq_ext.astype(BF16), wkv_ext.astype(BF16)


def kernel(x, positions, rel_bias, ln1_g, w_in, sinks, ssm_a_re, ssm_a_im, ssm_log_dt, ssm_b_re, ssm_b_im,
           ssm_c_re, ssm_c_im, ssm_d, ssm_w_glu, mla_q_norm_g, mla_w_q_up, mla_kv_norm_g, mla_w_kv_up, w_out,
           ln2_g, w_gate, w_up, w_down, final_g):
    b_ = x.shape[0]
    cos_t, sin_t = _trig_tables(positions)
    bias = _band_bias_tables(rel_bias, sinks)
    row = lambda g: g[:, None, :]
    wq_ext, wkv_ext = _stacked_weights(mla_w_q_up, mla_w_kv_up)
    w_in_bf = w_in.astype(BF16)
    bd, ar, ai, cd = jax.vmap(functools.partial(_ssm_params, batch=b_))(
        ssm_a_re, ssm_a_im, ssm_log_dt, ssm_b_re, ssm_b_im, ssm_c_re, ssm_c_im)
    wglu = ssm_w_glu.astype(BF16)
    gf = final_g.reshape(1, D_MODEL)
    for l in range(DEPTH):
        qa, ka, va, u, qm, km, vm = _in_proj(x, l, row(ln1_g), w_in_bf, cos_t, sin_t,
                                             row(mla_q_norm_g), wq_ext, row(mla_kv_norm_g), wkv_ext)
        o_a = _swa(qa, ka, va, l, bias)
        o_b = _ssm(u, b_, l, bd, ar, ai, cd, row(ssm_d), wglu)
        o_c, wo, wg, wu, wd = _mla(qm, km, vm, l, (w_out, w_gate, w_up, w_down))
        x = _ffn(x, o_a, o_b, o_c, l, wo, row(ln2_g), wg, wu, wd, gf, final_norm=(l == DEPTH - 1))
    return x
```

```python
import functools
import math

import jax
import jax.numpy as jnp
import numpy as np
from jax import lax
from jax.experimental import pallas as pl
from jax.experimental.pallas import tpu as pltpu

D_MODEL = 1024
DEPTH = 2
HEAD_DIM = 64
SWA_Q_HEADS = 8
SWA_KV_HEADS = 2
SWA_BLOCK = 128
SWA_Q = SWA_Q_HEADS * HEAD_DIM
SWA_KV = SWA_KV_HEADS * HEAD_DIM
REL_BUCKETS = 32
REL_MAX_DIST = 128
SSM_CH = 256
SSM_GROUP = 16
SSM_GROUPS = SSM_CH // SSM_GROUP
SSM_STATE = 64
SSM_STATES = SSM_GROUPS * SSM_STATE
MLA_HEADS = 4
MLA_Q_RANK = 256
MLA_KV_RANK = 128
MLA_NOPE = 64
MLA_ROPE = 32
MLA_V = 64
ROPE_THETA = 10000.0
D_FF = 2816
EPS = 1e-6
NEG = -1e30

LANES = 128
V7X_VMEM_BYTES = 64 * 1024 * 1024
VMEM_LIMIT = V7X_VMEM_BYTES * 7 // 8
MLA_HEAD_PAD = LANES
MLA_EXP2_SCALE = (MLA_NOPE + MLA_ROPE) ** -0.5 * math.log2(math.e)

IN_PROJ_TM = 1024
SWA_TQ = 512
MLA_TQ = 512
MLA_TK = 512
SSM_TS = 128
SSM_SUB = 2
FFN_TM = 512

BF16 = jnp.bfloat16
F32 = jnp.float32


def _rms(x, g):
    return x * lax.rsqrt(jnp.mean(x * x, axis=-1, keepdims=True) + EPS) * g


def _dot(a, b):
    return jnp.dot(a, b, preferred_element_type=F32)


def _dot_nt(a, b):
    return lax.dot_general(a, b, (((1,), (1,)), ((), ())), preferred_element_type=F32)


def _bf16_pieces(t):
    hi = t.astype(BF16)
    rest = t - hi.astype(F32)
    mid = rest.astype(BF16)
    return hi, mid, (rest - mid.astype(F32)).astype(BF16)


def _layer_spec(arr, l):
    return pl.BlockSpec((None,) + arr.shape[1:], lambda *_: (l,) + (0,) * (arr.ndim - 1))


def _trig_kernel(pos_ref, freq_ref, place_ref, one_ref, cos_ref, sin_ref):
    pos = pos_ref[pl.ds(pl.program_id(0), 1), :].astype(F32)
    ang = freq_ref[...] * pos

    def place(t):
        return sum(lax.dot_general(piece, place_ref[...], (((0,), (0,)), ((), ())), preferred_element_type=F32)
                   for piece in _bf16_pieces(t))

    cos_ref[...] = place(jnp.cos(ang)) + one_ref[...]
    sin_ref[...] = place(jnp.sin(ang))


def _trig_tables(positions):
    b_, s_ = positions.shape
    half = MLA_ROPE // 2
    inv_freq = jnp.power(ROPE_THETA, -jnp.arange(half, dtype=F32) * 2.0 / MLA_ROPE).reshape(half, 1)
    lane = np.arange(MLA_HEAD_PAD)[None, :]
    rope_lane = (lane >= MLA_NOPE) & (lane < MLA_NOPE + MLA_ROPE)
    placement = jnp.asarray(rope_lane & ((lane - MLA_NOPE) % half == np.arange(half)[:, None]), BF16)
    off_rope = jnp.asarray(~rope_lane, F32)
    const = lambda b: (0, 0)
    cos_t, sin_t = pl.pallas_call(
        _trig_kernel,
        grid=(b_,),
        in_specs=[pl.BlockSpec((b_, s_), const),
                  pl.BlockSpec((half, 1), const),
                  pl.BlockSpec((half, MLA_HEAD_PAD), const),
                  pl.BlockSpec((1, MLA_HEAD_PAD), const)],
        out_specs=[pl.BlockSpec((None, s_, MLA_HEAD_PAD), lambda b: (b, 0, 0))] * 2,
        out_shape=[jax.ShapeDtypeStruct((b_, s_, MLA_HEAD_PAD), F32)] * 2,
        name="rope_tables",
    )(positions, inv_freq, placement, off_rope)
    return cos_t.reshape(b_ * s_, MLA_HEAD_PAD), sin_t.reshape(b_ * s_, MLA_HEAD_PAD)


def _in_proj_kernel(x_ref, g_ref, w_ref, kplace_ref, cos_ref, sin_ref, qg_ref, wq_ref, kvg_ref, wkv_ref,
                    qa_ref, ka_ref, va_ref, u_ref, qm_ref, km_ref, vm_ref):
    h = _rms(x_ref[...], g_ref[...]).astype(BF16)
    proj = _dot(h, w_ref[...])
    qa_ref[...] = (proj[:, :SWA_Q] * (HEAD_DIM ** -0.5)).astype(BF16)
    c = SWA_Q
    ka_ref[...] = proj[:, c:c + SWA_KV].astype(BF16)
    c += SWA_KV
    va_ref[...] = proj[:, c:c + SWA_KV].astype(BF16)
    c += SWA_KV
    u_ref[...] = proj[:, c:c + SSM_CH]
    c += SSM_CH
    c_q = proj[:, c:c + MLA_Q_RANK]
    c += MLA_Q_RANK
    c_kv = proj[:, c:c + MLA_KV_RANK]
    c += MLA_KV_RANK
    k_r2 = sum(_dot(piece, kplace_ref[...]) for piece in _bf16_pieces(proj[:, c:c + MLA_ROPE]))
    k_r, k_r_rot = k_r2[:, :LANES], k_r2[:, LANES:]
    cos = cos_ref[...]
    sin = sin_ref[...]
    width = MLA_HEADS * MLA_HEAD_PAD
    q2 = _dot(_rms(c_q, qg_ref[...]).astype(BF16), wq_ref[...])
    kv2 = _dot(_rms(c_kv, kvg_ref[...]).astype(BF16), wkv_ref[...])
    k_pe = k_r * cos + k_r_rot * sin
    low = lax.broadcasted_iota(jnp.int32, (cos.shape[0], LANES), 1) < MLA_V
    for hd in range(MLA_HEADS):
        sl = slice(hd * MLA_HEAD_PAD, (hd + 1) * MLA_HEAD_PAD)
        sl_rot = slice(width + hd * MLA_HEAD_PAD, width + (hd + 1) * MLA_HEAD_PAD)
        qm_ref[:, sl] = (q2[:, sl] * cos + q2[:, sl_rot] * sin).astype(BF16)
        km_ref[:, sl] = (kv2[:, sl] + k_pe).astype(BF16)
        pair = kv2[:, width + (hd // 2) * LANES:width + (hd // 2 + 1) * LANES]
        keep = low if hd % 2 == 0 else jnp.logical_not(low)
        vm_ref[:, sl] = jnp.where(keep, pair, 1.0).astype(BF16)


def _in_proj(x, l, g, w_in, cos_t, sin_t, qg, wq_ext, kvg, wkv_ext):
    b_, s_, _ = x.shape
    tm = IN_PROJ_TM
    nt = s_ // tm
    row3 = lambda b, i: (b, i, 0)
    trig = pl.BlockSpec((tm, MLA_HEAD_PAD), lambda b, i: (b * nt + i, 0))
    mla_w = MLA_HEADS * MLA_HEAD_PAD
    kplace = _rope_key_placement()
    return pl.pallas_call(
        _in_proj_kernel,
        grid=(b_, nt),
        in_specs=[pl.BlockSpec((None, tm, D_MODEL), row3),
                  _layer_spec(g, l), _layer_spec(w_in, l),
                  pl.BlockSpec(kplace.shape, lambda b, i: (0, 0)),
                  trig, trig,
                  _layer_spec(qg, l), _layer_spec(wq_ext, l), _layer_spec(kvg, l), _layer_spec(wkv_ext, l)],
        out_specs=[pl.BlockSpec((None, tm, SWA_Q), row3),
                   pl.BlockSpec((None, tm, SWA_KV), row3),
                   pl.BlockSpec((None, tm, SWA_KV), row3),
                   pl.BlockSpec((tm, SSM_CH), lambda b, i: (i, b)),
                   pl.BlockSpec((None, tm, mla_w), row3),
                   pl.BlockSpec((None, tm, mla_w), row3),
                   pl.BlockSpec((None, tm, mla_w), row3)],
        out_shape=[jax.ShapeDtypeStruct((b_, s_, SWA_Q), BF16),
                   jax.ShapeDtypeStruct((b_, s_, SWA_KV), BF16),
                   jax.ShapeDtypeStruct((b_, s_, SWA_KV), BF16),
                   jax.ShapeDtypeStruct((s_, b_ * SSM_CH), F32),
                   jax.ShapeDtypeStruct((b_, s_, mla_w), BF16),
                   jax.ShapeDtypeStruct((b_, s_, mla_w), BF16),
                   jax.ShapeDtypeStruct((b_, s_, mla_w), BF16)],
        compiler_params=pltpu.CompilerParams(
            dimension_semantics=("arbitrary", "arbitrary"), vmem_limit_bytes=VMEM_LIMIT),
        name="in_proj",
    )(x, g, w_in, kplace, cos_t, sin_t, qg, wq_ext, kvg, wkv_ext)


def _swa_kernel(q_ref, k_ref, v_ref, bias_ref, o_ref, s_scr, m_scr):
    qi = pl.program_id(1)
    blocks = SWA_TQ // SWA_BLOCK
    tiles = SWA_Q // LANES
    band = 2 * SWA_BLOCK
    lane = lax.broadcasted_iota(jnp.int32, (band, LANES), 1)
    key = lax.broadcasted_iota(jnp.int32, (band, LANES), 0)
    low = lane < HEAD_DIM
    out_low = lax.broadcasted_iota(jnp.int32, (SWA_BLOCK, LANES), 1) < HEAD_DIM

    def band_of(ref, j):
        blk = qi * blocks + j
        start = pl.multiple_of(jnp.maximum(blk - 1, 0) * SWA_BLOCK, SWA_BLOCK)
        keep = key != jnp.where(blk == 0, band - 1, 0)
        t = ref[pl.ds(start, band), :]
        t = jnp.where(keep, t, jnp.zeros_like(t))
        return t, pltpu.roll(t, HEAD_DIM, 1)

    for j in range(blocks):
        table = jnp.minimum(qi * blocks + j, 1)
        kb, kb_sw = band_of(k_ref, j)
        zero = jnp.zeros_like(kb)
        k_var = ((jnp.where(low, kb, zero), jnp.where(low, zero, kb_sw)),
                 (jnp.where(low, kb_sw, zero), jnp.where(low, zero, kb)))
        rows = slice(j * SWA_BLOCK, (j + 1) * SWA_BLOCK)
        for i in range(tiles):
            kvh = (2 * i) // (SWA_Q_HEADS // SWA_KV_HEADS)
            q = q_ref[rows, i * LANES:(i + 1) * LANES]
            for half in range(2):
                u = (j * tiles + i) * 2 + half
                s = _dot_nt(q, k_var[kvh][half]) + bias_ref[table, 2 * i + half]
                s_scr[u] = s
                m_scr[u] = jnp.broadcast_to(jnp.max(s, axis=-1, keepdims=True), (SWA_BLOCK, LANES))
    for j in range(blocks):
        vb, vb_sw = band_of(v_ref, j)
        one = jnp.ones_like(vb)
        v_var = ((jnp.where(low, vb, one), jnp.where(low, one, vb_sw)),
                 (jnp.where(low, vb_sw, one), jnp.where(low, one, vb)))
        rows = slice(j * SWA_BLOCK, (j + 1) * SWA_BLOCK)
        for i in range(tiles):
            kvh = (2 * i) // (SWA_Q_HEADS // SWA_KV_HEADS)
            r = []
            for half in range(2):
                u = (j * tiles + i) * 2 + half
                m = m_scr[u]
                p = jnp.concatenate([jnp.exp(s_scr[u, :, t * LANES:(t + 1) * LANES] - m).astype(BF16)
                                     for t in range(band // LANES)], axis=-1)
                r.append(_dot(p, v_var[kvh][half]))
            denom = pltpu.roll(jnp.where(out_low, r[1], r[0]), HEAD_DIM, 1)
            o_ref[rows, i * LANES:(i + 1) * LANES] = (jnp.where(out_low, r[0], r[1]) / denom).astype(BF16)


def _swa(q, k, v, l, bias):
    b_, s_, _ = q.shape
    tq = SWA_TQ
    units = (tq // SWA_BLOCK) * SWA_Q_HEADS
    return pl.pallas_call(
        _swa_kernel,
        grid=(b_, s_ // tq),
        in_specs=[pl.BlockSpec((None, tq, SWA_Q), lambda b, i: (b, i, 0)),
                  pl.BlockSpec((None, s_, SWA_KV), lambda b, i: (b, 0, 0)),
                  pl.BlockSpec((None, s_, SWA_KV), lambda b, i: (b, 0, 0)),
                  _layer_spec(bias, l)],
        out_specs=pl.BlockSpec((None, tq, SWA_Q), lambda b, i: (b, i, 0)),
        out_shape=jax.ShapeDtypeStruct((b_, s_, SWA_Q), BF16),
        scratch_shapes=[pltpu.VMEM((units, SWA_BLOCK, 2 * SWA_BLOCK), F32),
                        pltpu.VMEM((units, SWA_BLOCK, LANES), F32)],
        compiler_params=pltpu.CompilerParams(
            dimension_semantics=("arbitrary", "arbitrary"), vmem_limit_bytes=VMEM_LIMIT),
        name="swa_attention",
    )(q, k, v, bias)


def _t5_bucket(dist):
    n = jnp.maximum(dist, 0)
    max_exact = REL_BUCKETS // 2
    large = max_exact + (jnp.log(jnp.maximum(n, 1).astype(F32) / max_exact)
                         / math.log(REL_MAX_DIST / max_exact)
                         * (REL_BUCKETS - max_exact)).astype(jnp.int32)
    large = jnp.minimum(large, REL_BUCKETS - 1)
    return jnp.where(n < max_exact, n, large)


def _band_bias_tables(rel_bias, sinks):
    qi = jnp.arange(SWA_BLOCK)[:, None]
    kj = jnp.arange(2 * SWA_BLOCK)[None, :]
    dist = qi + SWA_BLOCK - kj
    onehot = (_t5_bucket(dist)[None] == jnp.arange(REL_BUCKETS)[:, None, None]).astype(F32)
    b = jnp.einsum('nh,nqk->hqk', rel_bias.astype(F32), onehot, precision=lax.Precision.HIGHEST)
    valid = (dist >= 0) & (dist < SWA_BLOCK)
    normal = jnp.where(valid[None], b, NEG)
    first = jnp.concatenate([normal[:, :, SWA_BLOCK:], jnp.full_like(normal[:, :, SWA_BLOCK:], NEG)], axis=-1)
    sink_col = jnp.array([2 * SWA_BLOCK - 1, 0])[:, None, None, None]
    sink = sinks.astype(F32)[:, None, :, None, None]
    return jnp.where(kj == sink_col, sink, jnp.stack([first, normal])[None])


def _mla_kernel(q_ref, k_ref, v_ref, *rest):
    n_w = (len(rest) - 4) // 2
    w_f32, o_ref, w_bf16 = rest[:n_w], rest[n_w], rest[n_w + 1:2 * n_w + 1]
    s_scr, m_scr, acc_scr = rest[2 * n_w + 1:]
    for src, dst in zip(w_f32, w_bf16):
        dst[...] = src[...].astype(BF16)
    p = pl.program_id(1)
    tq, tk = MLA_TQ, MLA_TK
    n = q_ref.shape[0] // tq
    row = lax.broadcasted_iota(jnp.int32, (tq, tk), 0)
    col = lax.broadcasted_iota(jnp.int32, (tq, tk), 1)
    causal = col <= row
    out_low = lax.broadcasted_iota(jnp.int32, (tq, LANES), 1) < MLA_V
    heads = [slice(hd * MLA_HEAD_PAD, (hd + 1) * MLA_HEAD_PAD) for hd in range(MLA_HEADS)]
    q_tiles = (p, n - 1 - p)
    half = tq // 2
    causal_half = causal[:half, :half]
    causal_bottom = jnp.concatenate([jnp.ones((half, half), jnp.bool_), causal_half], axis=1)

    def slot(j):
        if j >= n // 2 - 1:
            return 1, j - p
        first = j < p
        return jnp.where(first, 0, 1), jnp.where(first, j, j - p)

    def rows(start, size, align):
        return pl.ds(pl.multiple_of(start, align), size)

    def lane_max(m, s):
        return functools.reduce(jnp.maximum, [s[:, t * LANES:(t + 1) * LANES] for t in range(s.shape[1] // LANES)], m)

    def probs(s, m):
        return jnp.concatenate([jnp.exp2((s[:, t * LANES:(t + 1) * LANES] - m) * MLA_EXP2_SCALE).astype(BF16)
                                for t in range(s.shape[1] // LANES)], axis=-1)

    m_scr[...] = jnp.full(m_scr.shape, NEG, F32)
    acc_scr[...] = jnp.zeros(acc_scr.shape, F32)
    for which in range(2):
        r0 = q_tiles[which] * tq
        top, bottom, whole = rows(r0, half, tq), rows(r0 + half, half, half), rows(r0, tq, tq)
        for hd in range(MLA_HEADS):
            s_top = jnp.where(causal_half, _dot_nt(q_ref[top, heads[hd]], k_ref[top, heads[hd]]), NEG)
            s_bot = jnp.where(causal_bottom, _dot_nt(q_ref[bottom, heads[hd]], k_ref[whole, heads[hd]]), NEG)
            s_scr[hd, which, :half, :half] = s_top
            s_scr[hd, which, half:, :] = s_bot
            m_scr[which, hd, :half] = lane_max(m_scr[which, hd, :half], s_top)
            m_scr[which, hd, half:] = lane_max(m_scr[which, hd, half:], s_bot)
    for j in range(n - 1):
        which, kt = slot(j)
        qt = q_tiles[which] if isinstance(which, int) else jnp.where(which == 0, q_tiles[0], q_tiles[1])
        for hd in range(MLA_HEADS):
            s = _dot_nt(q_ref[rows(qt * tq, tq, tq), heads[hd]], k_ref[rows(kt * tk, tk, tk), heads[hd]])
            s_scr[hd, 2 + j] = s
            m_scr[which, hd] = lane_max(m_scr[which, hd], s)
    for which in range(2):
        for hd in range(MLA_HEADS):
            m_scr[which, hd] = jnp.broadcast_to(jnp.max(m_scr[which, hd], axis=-1, keepdims=True), (tq, LANES))
    for which in range(2):
        r0 = q_tiles[which] * tq
        for hd in range(MLA_HEADS):
            e_top = probs(s_scr[hd, which, :half, :half], m_scr[which, hd, :half])
            e_bot = probs(s_scr[hd, which, half:, :], m_scr[which, hd, half:])
            acc_scr[which, hd, :half] += _dot(e_top, v_ref[rows(r0, half, tq), heads[hd]])
            acc_scr[which, hd, half:] += _dot(e_bot, v_ref[rows(r0, tq, tq), heads[hd]])
    for j in range(n - 1):
        which, kt = slot(j)
        for hd in range(MLA_HEADS):
            e = probs(s_scr[hd, 2 + j], m_scr[which, hd])
            acc_scr[which, hd] += _dot(e, v_ref[rows(kt * tk, tk, tk), heads[hd]])
    for which in range(2):
        for t in range(MLA_HEADS // 2):
            lo, hi = acc_scr[which, 2 * t], acc_scr[which, 2 * t + 1]
            lo = lo / lo[:, MLA_V:MLA_V + 1]
            hi = hi / hi[:, 0:1]
            o_ref[rows(q_tiles[which] * tq, tq, tq), t * LANES:(t + 1) * LANES] = (
                jnp.where(out_low, lo, hi).astype(BF16))


def _mla(q, k, v, l, weights):
    b_, s_, w = q.shape
    n = s_ // MLA_TQ
    steps = b_ * (n // 2)
    whole = lambda width: pl.BlockSpec((None, s_, width), lambda b, p: (b, 0, 0))
    slab_in = [pl.BlockSpec((None, wt.shape[1] // steps, wt.shape[2]), lambda b, p: (l, b * (n // 2) + p, 0))
               for wt in weights]
    slab_out = [pl.BlockSpec((wt.shape[1] // steps, wt.shape[2]), lambda b, p: (b * (n // 2) + p, 0))
                for wt in weights]
    return pl.pallas_call(
        _mla_kernel,
        grid=(b_, n // 2),
        in_specs=[whole(w), whole(w), whole(w)] + slab_in,
        out_specs=[whole(MLA_HEADS * MLA_V)] + slab_out,
        out_shape=[jax.ShapeDtypeStruct((b_, s_, MLA_HEADS * MLA_V), BF16)]
        + [jax.ShapeDtypeStruct(wt.shape[1:], BF16) for wt in weights],
        scratch_shapes=[pltpu.VMEM((MLA_HEADS, n + 1, MLA_TQ, MLA_TK), F32),
                        pltpu.VMEM((2, MLA_HEADS, MLA_TQ, LANES), F32),
                        pltpu.VMEM((2, MLA_HEADS, MLA_TQ, LANES), F32)],
        compiler_params=pltpu.CompilerParams(
            dimension_semantics=("arbitrary", "arbitrary"), vmem_limit_bytes=VMEM_LIMIT),
        name="mla_attention",
    )(q, k, v, *weights)


def _gelu_tanh(x):
    return 0.5 * x * (1.0 + jnp.tanh(math.sqrt(2.0 / math.pi) * (x + 0.044715 * (x * x * x))))


def _ssm_kernel(u_ref, bd_ref, ar_ref, ai_ref, cd_ref, d_ref, wglu_ref, y_ref,
                u_scr, x_scr, y_scr, sr_scr, si_scr):
    n = SSM_STATES
    batch = sr_scr.shape[0]
    lane_tiles = SSM_CH // LANES

    @pl.when(pl.program_id(0) == 0)
    def _():
        sr_scr[...] = jnp.zeros_like(sr_scr)
        si_scr[...] = jnp.zeros_like(si_scr)

    for b in range(batch):
        for lt in range(lane_tiles):
            c0 = b * SSM_CH + lt * LANES
            u_scr[lt, pl.ds(b, SSM_TS, stride=batch), :] = u_ref[:, c0:c0 + LANES]
    sub_t = SSM_TS // SSM_SUB
    sub_r = sub_t * batch
    subs = [slice(q * sub_r, (q + 1) * sub_r) for q in range(SSM_SUB)]
    us = [jnp.concatenate([u_scr[lt, rows, :] for lt in range(lane_tiles)], axis=-1) for rows in subs]
    for rows, u in zip(subs, us):
        u_bf = u.astype(BF16)
        x_scr[rows, :n] = _dot(u_bf, bd_ref[:, :n])
        x_scr[rows, n:] = _dot(u_bf, bd_ref[:, n:])
    ar = ar_ref[...]
    ai = ai_ref[...]
    xr, xi = sr_scr[...], si_scr[...]
    for rows, u in zip(subs, us):
        for t in range(sub_t):
            r = slice(rows.start + t * batch, rows.start + (t + 1) * batch)
            xr, xi = (ar * xr - ai * xi + x_scr[r, :n], ar * xi + ai * xr + x_scr[r, n:])
            x_scr[r, :n] = xr
            x_scr[r, n:] = xi
        mid = rows.start + sub_r // 2
        cx = jnp.concatenate([_dot(x_scr[rows.start:mid, :].astype(BF16), cd_ref[...]),
                              _dot(x_scr[mid:rows.stop, :].astype(BF16), cd_ref[...])], axis=0)
        y = cx + d_ref[...] * u
        y = _gelu_tanh(y)
        y = y * jax.nn.sigmoid(_dot(y.astype(BF16), wglu_ref[...]))
        for lt in range(lane_tiles):
            y_scr[lt, rows, :] = y[:, lt * LANES:(lt + 1) * LANES]
    sr_scr[...] = xr
    si_scr[...] = xi
    for b in range(batch):
        for lt in range(lane_tiles):
            c0 = b * SSM_CH + lt * LANES
            y_ref[:, c0:c0 + LANES] = y_scr[lt, pl.ds(b, SSM_TS, stride=batch), :].astype(BF16)


def _ssm(u, batch, l, bd, ar, ai, cd, d, wglu):
    s_, width = u.shape
    tr = SSM_TS * batch
    return pl.pallas_call(
        _ssm_kernel,
        grid=(s_ // SSM_TS,),
        in_specs=[pl.BlockSpec((SSM_TS, width), lambda i: (i, 0))]
        + [_layer_spec(a, l) for a in (bd, ar, ai, cd, d, wglu)],
        out_specs=pl.BlockSpec((SSM_TS, width), lambda i: (i, 0)),
        out_shape=jax.ShapeDtypeStruct((s_, width), BF16),
        scratch_shapes=[pltpu.VMEM((SSM_CH // LANES, tr, LANES), F32),
                        pltpu.VMEM((tr, 2 * SSM_STATES), F32),
                        pltpu.VMEM((SSM_CH // LANES, tr, LANES), F32),
                        pltpu.VMEM((batch, SSM_STATES), F32),
                        pltpu.VMEM((batch, SSM_STATES), F32)],
        compiler_params=pltpu.CompilerParams(
            dimension_semantics=("arbitrary",), vmem_limit_bytes=VMEM_LIMIT),
        name="s5_mixer",
    )(u, bd, ar, ai, cd, d, wglu)


def _ssm_params(a_re, a_im, log_dt, b_re, b_im, c_re, c_im, batch):
    lam = lax.complex(a_re.astype(F32), a_im.astype(F32))
    dt = jnp.exp(log_dt.astype(F32))
    a_bar = jnp.exp(lam * dt[:, None])
    b_bar = ((a_bar - 1.0) / lam)[..., None] * lax.complex(b_re.astype(F32), b_im.astype(F32))
    eye = jnp.eye(SSM_GROUPS, dtype=F32)

    def in_op(m):
        return jnp.einsum('gpc,gh->gchp', m, eye).reshape(SSM_CH, SSM_STATES)

    def out_op(m):
        return jnp.einsum('gcp,gh->gphc', m, eye).reshape(SSM_STATES, SSM_CH)

    bd = jnp.concatenate([in_op(jnp.real(b_bar)), in_op(jnp.imag(b_bar))], axis=1).astype(BF16)
    cd = jnp.concatenate([out_op(c_re.astype(F32)), -out_op(c_im.astype(F32))], axis=0).astype(BF16)
    ar = jnp.broadcast_to(jnp.real(a_bar).reshape(1, SSM_STATES), (batch, SSM_STATES))
    ai = jnp.broadcast_to(jnp.imag(a_bar).reshape(1, SSM_STATES), (batch, SSM_STATES))
    return bd, ar, ai, cd


def _ffn_kernel(x_ref, oa_ref, ob_ref, oc_ref, wo_ref, g2_ref, wg_ref, wu_ref, wd_ref, gf_ref, o_ref, h_scr,
                *, final_norm):
    wo_a = wo_ref[:SWA_Q, :]
    wo_b = wo_ref[SWA_Q:SWA_Q + SSM_CH, :]
    wo_c = wo_ref[SWA_Q + SSM_CH:, :]
    x = x_ref[...] + _dot(oa_ref[...], wo_a) + _dot(ob_ref[...], wo_b) + _dot(oc_ref[...], wo_c)
    o_ref[...] = x
    h_scr[...] = _rms(x, g2_ref[...]).astype(BF16)

    h = h_scr[...]
    gate = _dot(h, wg_ref[...])
    up = _dot(h, wu_ref[...])
    act = (gate * jax.nn.sigmoid(gate) * up).astype(BF16)
    o_ref[...] += _dot(act, wd_ref[...])
    if final_norm:
        o_ref[...] = _rms(o_ref[...], gf_ref[...])


def _ffn(x, oa, ob, oc, l, wo, g2, wg, wu, wd, gf, final_norm):
    b_, s_, _ = x.shape
    tm = FFN_TM
    row3 = lambda b, i: (b, i, 0)
    whole = lambda wt: pl.BlockSpec(wt.shape, lambda b, i: (0, 0))
    return pl.pallas_call(
        functools.partial(_ffn_kernel, final_norm=final_norm),
        grid=(b_, s_ // tm),
        in_specs=[pl.BlockSpec((None, tm, D_MODEL), row3),
                  pl.BlockSpec((None, tm, SWA_Q), row3),
                  pl.BlockSpec((tm, SSM_CH), lambda b, i: (i, b)),
                  pl.BlockSpec((None, tm, MLA_HEADS * MLA_V), row3),
                  whole(wo), _layer_spec(g2, l), whole(wg), whole(wu), whole(wd),
                  pl.BlockSpec((1, D_MODEL), lambda b, i: (0, 0))],
        out_specs=pl.BlockSpec((None, tm, D_MODEL), row3),
        out_shape=jax.ShapeDtypeStruct(x.shape, F32),
        scratch_shapes=[pltpu.VMEM((tm, D_MODEL), BF16)],
        compiler_params=pltpu.CompilerParams(
            dimension_semantics=("arbitrary", "arbitrary"), vmem_limit_bytes=VMEM_LIMIT),
        name="out_proj_ffn",
    )(x, oa, ob, oc, wo, g2, wg, wu, wd, gf)


def _rot_half_cols(w):
    half = w.shape[-1] // 2
    return jnp.concatenate([-w[..., half:], w[..., :half]], axis=-1)


def _rope_tile(w):
    z = jnp.zeros(w.shape[:-1] + (MLA_NOPE,), w.dtype)
    z2 = jnp.zeros(w.shape[:-1] + (MLA_HEAD_PAD - MLA_NOPE - MLA_ROPE,), w.dtype)
    return jnp.concatenate([z, w, z2], axis=-1)


def _rope_key_placement():
    half = MLA_ROPE // 2
    j = np.arange(MLA_ROPE)
    place = np.zeros((MLA_ROPE, 2 * LANES), np.float32)
    place[j, MLA_NOPE + j] = 1.0
    place[j, LANES + MLA_NOPE + (j + half) % MLA_ROPE] = np.where(j < half, 1.0, -1.0)
    return jnp.asarray(place, BF16)


def _stacked_weights(w_q_up, w_kv_up):
    lead = w_q_up.shape[:-2]
    qh = w_q_up.reshape(lead + (MLA_Q_RANK, MLA_HEADS, MLA_NOPE + MLA_ROPE))
    pad = jnp.zeros(lead + (MLA_Q_RANK, MLA_HEADS, MLA_HEAD_PAD - MLA_NOPE - MLA_ROPE), w_q_up.dtype)
    q_tiles = jnp.concatenate([qh, pad], axis=-1).reshape(lead + (MLA_Q_RANK, -1))
    q_rot = _rope_tile(_rot_half_cols(qh[..., MLA_NOPE:])).reshape(lead + (MLA_Q_RANK, -1))
    wq_ext = jnp.concatenate([q_tiles, q_rot], axis=-1)
    kvh = w_kv_up.reshape(lead + (MLA_KV_RANK, MLA_HEADS, MLA_NOPE + MLA_V))
    kpad = jnp.zeros(lead + (MLA_KV_RANK, MLA_HEADS, MLA_HEAD_PAD - MLA_NOPE), w_kv_up.dtype)
    k_tiles = jnp.concatenate([kvh[..., :MLA_NOPE], kpad], axis=-1).reshape(lead + (MLA_KV_RANK, -1))
    v_cols = kvh[..., MLA_NOPE:].reshape(lead + (MLA_KV_RANK, -1))
    wkv_ext = jnp.concatenate([k_tiles, v_cols], axis=-1)
    return wq_ext.astype(BF16), wkv_ext.astype(BF16)


def kernel(x, positions, rel_bias, ln1_g, w_in, sinks, ssm_a_re, ssm_a_im, ssm_log_dt, ssm_b_re, ssm_b_im,
           ssm_c_re, ssm_c_im, ssm_d, ssm_w_glu, mla_q_norm_g, mla_w_q_up, mla_kv_norm_g, mla_w_kv_up, w_out,
           ln2_g, w_gate, w_up, w_down, final_g):
    b_ = x.shape[0]
    cos_t, sin_t = _trig_tables(positions)
    bias = _band_bias_tables(rel_bias, sinks)
    row = lambda g: g[:, None, :]
    wq_ext, wkv_ext = _stacked_weights(mla_w_q_up, mla_w_kv_up)
    w_in_bf = w_in.astype(BF16)
    bd, ar, ai, cd = jax.vmap(functools.partial(_ssm_params, batch=b_))(
        ssm_a_re, ssm_a_im, ssm_log_dt, ssm_b_re, ssm_b_im, ssm_c_re, ssm_c_im)
    wglu = ssm_w_glu.astype(BF16)
    gf = final_g.reshape(1, D_MODEL)
    for l in range(DEPTH):
        qa, ka, va, u, qm, km, vm = _in_proj(x, l, row(ln1_g), w_in_bf, cos_t, sin_t,
                                             row(mla_q_norm_g), wq_ext, row(mla_kv_norm_g), wkv_ext)
        o_a = _swa(qa, ka, va, l, bias)
        o_b = _ssm(u, b_, l, bd, ar, ai, cd, row(ssm_d), wglu)
        o_c, wo, wg, wu, wd = _mla(qm, km, vm, l, (w_out, w_gate, w_up, w_down))
        x = _ffn(x, o_a, o_b, o_c, l, wo, row(ln2_g), wg, wu, wd, gf, final_norm=(l == DEPTH - 1))
    return x
```

```python
import functools
import math

import jax
import jax.numpy as jnp
import numpy as np
from jax import lax
from jax.experimental import pallas as pl
from jax.experimental.pallas import tpu as pltpu

D_MODEL = 1024
DEPTH = 2
HEAD_DIM = 64
SWA_Q_HEADS = 8
SWA_KV_HEADS = 2
SWA_BLOCK = 128
SWA_Q = SWA_Q_HEADS * HEAD_DIM
SWA_KV = SWA_KV_HEADS * HEAD_DIM
REL_BUCKETS = 32
REL_MAX_DIST = 128
SSM_CH = 256
SSM_GROUP = 16
SSM_GROUPS = SSM_CH // SSM_GROUP
SSM_STATE = 64
SSM_STATES = SSM_GROUPS * SSM_STATE
MLA_HEADS = 4
MLA_Q_RANK = 256
MLA_KV_RANK = 128
MLA_NOPE = 64
MLA_ROPE = 32
MLA_V = 64
ROPE_THETA = 10000.0
D_FF = 2816
EPS = 1e-6
NEG = -1e30

LANES = 128
V7X_VMEM_BYTES = 64 * 1024 * 1024
VMEM_LIMIT = V7X_VMEM_BYTES * 7 // 8
MLA_HEAD_PAD = LANES
MLA_EXP2_SCALE = (MLA_NOPE + MLA_ROPE) ** -0.5 * math.log2(math.e)

IN_PROJ_TM = 1024
SWA_TQ = 512
MLA_TQ = 512
MLA_TK = 512
SSM_TS = 128
SSM_SUB = 2
FFN_TM = 1024

BF16 = jnp.bfloat16
F32 = jnp.float32


def _rms(x, g):
    return x * lax.rsqrt(jnp.mean(x * x, axis=-1, keepdims=True) + EPS) * g


def _dot(a, b):
    return jnp.dot(a, b, preferred_element_type=F32)


def _dot_nt(a, b):
    return lax.dot_general(a, b, (((1,), (1,)), ((), ())), preferred_element_type=F32)


def _bf16_pieces(t):
    hi = t.astype(BF16)
    rest = t - hi.astype(F32)
    mid = rest.astype(BF16)
    return hi, mid, (rest - mid.astype(F32)).astype(BF16)


def _layer_spec(arr, l):
    return pl.BlockSpec((None,) + arr.shape[1:], lambda *_: (l,) + (0,) * (arr.ndim - 1))


def _trig_kernel(pos_ref, freq_ref, place_ref, one_ref, cos_ref, sin_ref):
    pos = pos_ref[pl.ds(pl.program_id(0), 1), :].astype(F32)
    ang = freq_ref[...] * pos

    def place(t):
        return sum(lax.dot_general(piece, place_ref[...], (((0,), (0,)), ((), ())), preferred_element_type=F32)
                   for piece in _bf16_pieces(t))

    cos_ref[...] = place(jnp.cos(ang)) + one_ref[...]
    sin_ref[...] = place(jnp.sin(ang))


def _trig_tables(positions):
    b_, s_ = positions.shape
    half = MLA_ROPE // 2
    inv_freq = jnp.power(ROPE_THETA, -jnp.arange(half, dtype=F32) * 2.0 / MLA_ROPE).reshape(half, 1)
    lane = np.arange(MLA_HEAD_PAD)[None, :]
    rope_lane = (lane >= MLA_NOPE) & (lane < MLA_NOPE + MLA_ROPE)
    placement = jnp.asarray(rope_lane & ((lane - MLA_NOPE) % half == np.arange(half)[:, None]), BF16)
    off_rope = jnp.asarray(~rope_lane, F32)
    const = lambda b: (0, 0)
    cos_t, sin_t = pl.pallas_call(
        _trig_kernel,
        grid=(b_,),
        in_specs=[pl.BlockSpec((b_, s_), const),
                  pl.BlockSpec((half, 1), const),
                  pl.BlockSpec((half, MLA_HEAD_PAD), const),
                  pl.BlockSpec((1, MLA_HEAD_PAD), const)],
        out_specs=[pl.BlockSpec((None, s_, MLA_HEAD_PAD), lambda b: (b, 0, 0))] * 2,
        out_shape=[jax.ShapeDtypeStruct((b_, s_, MLA_HEAD_PAD), F32)] * 2,
        name="rope_tables",
    )(positions, inv_freq, placement, off_rope)
    return cos_t.reshape(b_ * s_, MLA_HEAD_PAD), sin_t.reshape(b_ * s_, MLA_HEAD_PAD)


def _in_proj_kernel(x_ref, g_ref, w_ref, kplace_ref, cos_ref, sin_ref, qg_ref, wq_ref, kvg_ref, wkv_ref,
                    qa_ref, ka_ref, va_ref, u_ref, qm_ref, km_ref, vm_ref):
    h = _rms(x_ref[...], g_ref[...]).astype(BF16)
    proj = _dot(h, w_ref[...])
    qa_ref[...] = (proj[:, :SWA_Q] * (HEAD_DIM ** -0.5)).astype(BF16)
    c = SWA_Q
    ka_ref[...] = proj[:, c:c + SWA_KV].astype(BF16)
    c += SWA_KV
    va_ref[...] = proj[:, c:c + SWA_KV].astype(BF16)
    c += SWA_KV
    u_ref[...] = proj[:, c:c + SSM_CH]
    c += SSM_CH
    c_q = proj[:, c:c + MLA_Q_RANK]
    c += MLA_Q_RANK
    c_kv = proj[:, c:c + MLA_KV_RANK]
    c += MLA_KV_RANK
    k_r2 = sum(_dot(piece, kplace_ref[...]) for piece in _bf16_pieces(proj[:, c:c + MLA_ROPE]))
    k_r, k_r_rot = k_r2[:, :LANES], k_r2[:, LANES:]
    cos = cos_ref[...]
    sin = sin_ref[...]
    width = MLA_HEADS * MLA_HEAD_PAD
    q2 = _dot(_rms(c_q, qg_ref[...]).astype(BF16), wq_ref[...])
    kv2 = _dot(_rms(c_kv, kvg_ref[...]).astype(BF16), wkv_ref[...])
    k_pe = k_r * cos + k_r_rot * sin
    low = lax.broadcasted_iota(jnp.int32, (cos.shape[0], LANES), 1) < MLA_V
    for hd in range(MLA_HEADS):
        sl = slice(hd * MLA_HEAD_PAD, (hd + 1) * MLA_HEAD_PAD)
        sl_rot = slice(width + hd * MLA_HEAD_PAD, width + (hd + 1) * MLA_HEAD_PAD)
        qm_ref[:, sl] = (q2[:, sl] * cos + q2[:, sl_rot] * sin).astype(BF16)
        km_ref[:, sl] = (kv2[:, sl] + k_pe).astype(BF16)
        pair = kv2[:, width + (hd // 2) * LANES:width + (hd // 2 + 1) * LANES]
        keep = low if hd % 2 == 0 else jnp.logical_not(low)
        vm_ref[:, sl] = jnp.where(keep, pair, 1.0).astype(BF16)


def _in_proj(x, l, g, w_in, cos_t, sin_t, qg, wq_ext, kvg, wkv_ext):
    b_, s_, _ = x.shape
    tm = IN_PROJ_TM
    nt = s_ // tm
    row3 = lambda b, i: (b, i, 0)
    trig = pl.BlockSpec((tm, MLA_HEAD_PAD), lambda b, i: (b * nt + i, 0))
    mla_w = MLA_HEADS * MLA_HEAD_PAD
    kplace = _rope_key_placement()
    return pl.pallas_call(
        _in_proj_kernel,
        grid=(b_, nt),
        in_specs=[pl.BlockSpec((None, tm, D_MODEL), row3),
                  _layer_spec(g, l), _layer_spec(w_in, l),
                  pl.BlockSpec(kplace.shape, lambda b, i: (0, 0)),
                  trig, trig,
                  _layer_spec(qg, l), _layer_spec(wq_ext, l), _layer_spec(kvg, l), _layer_spec(wkv_ext, l)],
        out_specs=[pl.BlockSpec((None, tm, SWA_Q), row3),
                   pl.BlockSpec((None, tm, SWA_KV), row3),
                   pl.BlockSpec((None, tm, SWA_KV), row3),
                   pl.BlockSpec((tm, SSM_CH), lambda b, i: (i, b)),
                   pl.BlockSpec((None, tm, mla_w), row3),
                   pl.BlockSpec((None, tm, mla_w), row3),
                   pl.BlockSpec((None, tm, mla_w), row3)],
        out_shape=[jax.ShapeDtypeStruct((b_, s_, SWA_Q), BF16),
                   jax.ShapeDtypeStruct((b_, s_, SWA_KV), BF16),
                   jax.ShapeDtypeStruct((b_, s_, SWA_KV), BF16),
                   jax.ShapeDtypeStruct((s_, b_ * SSM_CH), F32),
                   jax.ShapeDtypeStruct((b_, s_, mla_w), BF16),
                   jax.ShapeDtypeStruct((b_, s_, mla_w), BF16),
                   jax.ShapeDtypeStruct((b_, s_, mla_w), BF16)],
        compiler_params=pltpu.CompilerParams(
            dimension_semantics=("arbitrary", "arbitrary"), vmem_limit_bytes=VMEM_LIMIT),
        name="in_proj",
    )(x, g, w_in, kplace, cos_t, sin_t, qg, wq_ext, kvg, wkv_ext)


def _swa_kernel(q_ref, k_ref, v_ref, bias_ref, o_ref, s_scr, m_scr):
    qi = pl.program_id(1)
    blocks = SWA_TQ // SWA_BLOCK
    tiles = SWA_Q // LANES
    band = 2 * SWA_BLOCK
    lane = lax.broadcasted_iota(jnp.int32, (band, LANES), 1)
    key = lax.broadcasted_iota(jnp.int32, (band, LANES), 0)
    low = lane < HEAD_DIM
    out_low = lax.broadcasted_iota(jnp.int32, (SWA_BLOCK, LANES), 1) < HEAD_DIM

    def band_of(ref, j):
        blk = qi * blocks + j
        start = pl.multiple_of(jnp.maximum(blk - 1, 0) * SWA_BLOCK, SWA_BLOCK)
        keep = key != jnp.where(blk == 0, band - 1, 0)
        t = ref[pl.ds(start, band), :]
        t = jnp.where(keep, t, jnp.zeros_like(t))
        return t, pltpu.roll(t, HEAD_DIM, 1)

    for j in range(blocks):
        table = jnp.minimum(qi * blocks + j, 1)
        kb, kb_sw = band_of(k_ref, j)
        zero = jnp.zeros_like(kb)
        k_var = ((jnp.where(low, kb, zero), jnp.where(low, zero, kb_sw)),
                 (jnp.where(low, kb_sw, zero), jnp.where(low, zero, kb)))
        rows = slice(j * SWA_BLOCK, (j + 1) * SWA_BLOCK)
        for i in range(tiles):
            kvh = (2 * i) // (SWA_Q_HEADS // SWA_KV_HEADS)
            q = q_ref[rows, i * LANES:(i + 1) * LANES]
            for half in range(2):
                u = (j * tiles + i) * 2 + half
                s = _dot_nt(q, k_var[kvh][half]) + bias_ref[table, 2 * i + half]
                s_scr[u] = s
                m_scr[u] = jnp.broadcast_to(jnp.max(s, axis=-1, keepdims=True), (SWA_BLOCK, LANES))
    for j in range(blocks):
        vb, vb_sw = band_of(v_ref, j)
        one = jnp.ones_like(vb)
        v_var = ((jnp.where(low, vb, one), jnp.where(low, one, vb_sw)),
                 (jnp.where(low, vb_sw, one), jnp.where(low, one, vb)))
        rows = slice(j * SWA_BLOCK, (j + 1) * SWA_BLOCK)
        for i in range(tiles):
            kvh = (2 * i) // (SWA_Q_HEADS // SWA_KV_HEADS)
            r = []
            for half in range(2):
                u = (j * tiles + i) * 2 + half
                m = m_scr[u]
                p = jnp.concatenate([jnp.exp(s_scr[u, :, t * LANES:(t + 1) * LANES] - m).astype(BF16)
                                     for t in range(band // LANES)], axis=-1)
                r.append(_dot(p, v_var[kvh][half]))
            denom = pltpu.roll(jnp.where(out_low, r[1], r[0]), HEAD_DIM, 1)
            o_ref[rows, i * LANES:(i + 1) * LANES] = (jnp.where(out_low, r[0], r[1]) / denom).astype(BF16)


def _swa(q, k, v, l, bias):
    b_, s_, _ = q.shape
    tq = SWA_TQ
    units = (tq // SWA_BLOCK) * SWA_Q_HEADS
    return pl.pallas_call(
        _swa_kernel,
        grid=(b_, s_ // tq),
        in_specs=[pl.BlockSpec((None, tq, SWA_Q), lambda b, i: (b, i, 0)),
                  pl.BlockSpec((None, s_, SWA_KV), lambda b, i: (b, 0, 0)),
                  pl.BlockSpec((None, s_, SWA_KV), lambda b, i: (b, 0, 0)),
                  _layer_spec(bias, l)],
        out_specs=pl.BlockSpec((None, tq, SWA_Q), lambda b, i: (b, i, 0)),
        out_shape=jax.ShapeDtypeStruct((b_, s_, SWA_Q), BF16),
        scratch_shapes=[pltpu.VMEM((units, SWA_BLOCK, 2 * SWA_BLOCK), F32),
                        pltpu.VMEM((units, SWA_BLOCK, LANES), F32)],
        compiler_params=pltpu.CompilerParams(
            dimension_semantics=("arbitrary", "arbitrary"), vmem_limit_bytes=VMEM_LIMIT),
        name="swa_attention",
    )(q, k, v, bias)


def _t5_bucket(dist):
    n = jnp.maximum(dist, 0)
    max_exact = REL_BUCKETS // 2
    large = max_exact + (jnp.log(jnp.maximum(n, 1).astype(F32) / max_exact)
                         / math.log(REL_MAX_DIST / max_exact)
                         * (REL_BUCKETS - max_exact)).astype(jnp.int32)
    large = jnp.minimum(large, REL_BUCKETS - 1)
    return jnp.where(n < max_exact, n, large)


def _band_bias_tables(rel_bias, sinks):
    qi = jnp.arange(SWA_BLOCK)[:, None]
    kj = jnp.arange(2 * SWA_BLOCK)[None, :]
    dist = qi + SWA_BLOCK - kj
    onehot = (_t5_bucket(dist)[None] == jnp.arange(REL_BUCKETS)[:, None, None]).astype(F32)
    b = jnp.einsum('nh,nqk->hqk', rel_bias.astype(F32), onehot, precision=lax.Precision.HIGHEST)
    valid = (dist >= 0) & (dist < SWA_BLOCK)
    normal = jnp.where(valid[None], b, NEG)
    first = jnp.concatenate([normal[:, :, SWA_BLOCK:], jnp.full_like(normal[:, :, SWA_BLOCK:], NEG)], axis=-1)
    sink_col = jnp.array([2 * SWA_BLOCK - 1, 0])[:, None, None, None]
    sink = sinks.astype(F32)[:, None, :, None, None]
    return jnp.where(kj == sink_col, sink, jnp.stack([first, normal])[None])


def _mla_kernel(q_ref, k_ref, v_ref, *rest):
    n_w = (len(rest) - 4) // 2
    w_f32, o_ref, w_bf16 = rest[:n_w], rest[n_w], rest[n_w + 1:2 * n_w + 1]
    s_scr, m_scr, acc_scr = rest[2 * n_w + 1:]
    for src, dst in zip(w_f32, w_bf16):
        dst[...] = src[...].astype(BF16)
    p = pl.program_id(1)
    tq, tk = MLA_TQ, MLA_TK
    n = q_ref.shape[0] // tq
    row = lax.broadcasted_iota(jnp.int32, (tq, tk), 0)
    col = lax.broadcasted_iota(jnp.int32, (tq, tk), 1)
    causal = col <= row
    out_low = lax.broadcasted_iota(jnp.int32, (tq, LANES), 1) < MLA_V
    heads = [slice(hd * MLA_HEAD_PAD, (hd + 1) * MLA_HEAD_PAD) for hd in range(MLA_HEADS)]
    q_tiles = (p, n - 1 - p)
    half = tq // 2
    causal_half = causal[:half, :half]
    causal_bottom = jnp.concatenate([jnp.ones((half, half), jnp.bool_), causal_half], axis=1)

    def slot(j):
        if j >= n // 2 - 1:
            return 1, j - p
        first = j < p
        return jnp.where(first, 0, 1), jnp.where(first, j, j - p)

    def rows(start, size, align):
        return pl.ds(pl.multiple_of(start, align), size)

    def lane_max(m, s):
        return functools.reduce(jnp.maximum, [s[:, t * LANES:(t + 1) * LANES] for t in range(s.shape[1] // LANES)], m)

    def probs(s, m):
        return jnp.concatenate([jnp.exp2((s[:, t * LANES:(t + 1) * LANES] - m) * MLA_EXP2_SCALE).astype(BF16)
                                for t in range(s.shape[1] // LANES)], axis=-1)

    m_scr[...] = jnp.full(m_scr.shape, NEG, F32)
    acc_scr[...] = jnp.zeros(acc_scr.shape, F32)
    for which in range(2):
        r0 = q_tiles[which] * tq
        top, bottom, whole = rows(r0, half, tq), rows(r0 + half, half, half), rows(r0, tq, tq)
        for hd in range(MLA_HEADS):
            s_top = jnp.where(causal_half, _dot_nt(q_ref[top, heads[hd]], k_ref[top, heads[hd]]), NEG)
            s_bot = jnp.where(causal_bottom, _dot_nt(q_ref[bottom, heads[hd]], k_ref[whole, heads[hd]]), NEG)
            s_scr[hd, which, :half, :half] = s_top
            s_scr[hd, which, half:, :] = s_bot
            m_scr[which, hd, :half] = lane_max(m_scr[which, hd, :half], s_top)
            m_scr[which, hd, half:] = lane_max(m_scr[which, hd, half:], s_bot)
    for j in range(n - 1):
        which, kt = slot(j)
        qt = q_tiles[which] if isinstance(which, int) else jnp.where(which == 0, q_tiles[0], q_tiles[1])
        for hd in range(MLA_HEADS):
            s = _dot_nt(q_ref[rows(qt * tq, tq, tq), heads[hd]], k_ref[rows(kt * tk, tk, tk), heads[hd]])
            s_scr[hd, 2 + j] = s
            m_scr[which, hd] = lane_max(m_scr[which, hd], s)
    for which in range(2):
        for hd in range(MLA_HEADS):
            m_scr[which, hd] = jnp.broadcast_to(jnp.max(m_scr[which, hd], axis=-1, keepdims=True), (tq, LANES))
    for which in range(2):
        r0 = q_tiles[which] * tq
        for hd in range(MLA_HEADS):
            e_top = probs(s_scr[hd, which, :half, :half], m_scr[which, hd, :half])
            e_bot = probs(s_scr[hd, which, half:, :], m_scr[which, hd, half:])
            acc_scr[which, hd, :half] += _dot(e_top, v_ref[rows(r0, half, tq), heads[hd]])
            acc_scr[which, hd, half:] += _dot(e_bot, v_ref[rows(r0, tq, tq), heads[hd]])
    for j in range(n - 1):
        which, kt = slot(j)
        for hd in range(MLA_HEADS):
            e = probs(s_scr[hd, 2 + j], m_scr[which, hd])
            acc_scr[which, hd] += _dot(e, v_ref[rows(kt * tk, tk, tk), heads[hd]])
    for which in range(2):
        for t in range(MLA_HEADS // 2):
            lo, hi = acc_scr[which, 2 * t], acc_scr[which, 2 * t + 1]
            lo = lo / lo[:, MLA_V:MLA_V + 1]
            hi = hi / hi[:, 0:1]
            o_ref[rows(q_tiles[which] * tq, tq, tq), t * LANES:(t + 1) * LANES] = (
                jnp.where(out_low, lo, hi).astype(BF16))


def _mla(q, k, v, l, weights):
    b_, s_, w = q.shape
    n = s_ // MLA_TQ
    steps = b_ * (n // 2)
    whole = lambda width: pl.BlockSpec((None, s_, width), lambda b, p: (b, 0, 0))
    slab_in = [pl.BlockSpec((None, wt.shape[1] // steps, wt.shape[2]), lambda b, p: (l, b * (n // 2) + p, 0))
               for wt in weights]
    slab_out = [pl.BlockSpec((wt.shape[1] // steps, wt.shape[2]), lambda b, p: (b * (n // 2) + p, 0))
                for wt in weights]
    return pl.pallas_call(
        _mla_kernel,
        grid=(b_, n // 2),
        in_specs=[whole(w), whole(w), whole(w)] + slab_in,
        out_specs=[whole(MLA_HEADS * MLA_V)] + slab_out,
        out_shape=[jax.ShapeDtypeStruct((b_, s_, MLA_HEADS * MLA_V), BF16)]
        + [jax.ShapeDtypeStruct(wt.shape[1:], BF16) for wt in weights],
        scratch_shapes=[pltpu.VMEM((MLA_HEADS, n + 1, MLA_TQ, MLA_TK), F32),
                        pltpu.VMEM((2, MLA_HEADS, MLA_TQ, LANES), F32),
                        pltpu.VMEM((2, MLA_HEADS, MLA_TQ, LANES), F32)],
        compiler_params=pltpu.CompilerParams(
            dimension_semantics=("arbitrary", "arbitrary"), vmem_limit_bytes=VMEM_LIMIT),
        name="mla_attention",
    )(q, k, v, *weights)


def _gelu_tanh(x):
    return 0.5 * x * (1.0 + jnp.tanh(math.sqrt(2.0 / math.pi) * (x + 0.044715 * (x * x * x))))


def _ssm_kernel(u_ref, bd_ref, ar_ref, ai_ref, cd_ref, d_ref, wglu_ref, y_ref,
                u_scr, x_scr, y_scr, sr_scr, si_scr):
    n = SSM_STATES
    batch = sr_scr.shape[0]
    lane_tiles = SSM_CH // LANES

    @pl.when(pl.program_id(0) == 0)
    def _():
        sr_scr[...] = jnp.zeros_like(sr_scr)
        si_scr[...] = jnp.zeros_like(si_scr)

    for b in range(batch):
        for lt in range(lane_tiles):
            c0 = b * SSM_CH + lt * LANES
            u_scr[lt, pl.ds(b, SSM_TS, stride=batch), :] = u_ref[:, c0:c0 + LANES]
    sub_t = SSM_TS // SSM_SUB
    sub_r = sub_t * batch
    subs = [slice(q * sub_r, (q + 1) * sub_r) for q in range(SSM_SUB)]
    us = [jnp.concatenate([u_scr[lt, rows, :] for lt in range(lane_tiles)], axis=-1) for rows in subs]
    for rows, u in zip(subs, us):
        u_bf = u.astype(BF16)
        x_scr[rows, :n] = _dot(u_bf, bd_ref[:, :n])
        x_scr[rows, n:] = _dot(u_bf, bd_ref[:, n:])
    ar = ar_ref[...]
    ai = ai_ref[...]
    xr, xi = sr_scr[...], si_scr[...]
    for rows, u in zip(subs, us):
        for t in range(sub_t):
            r = slice(rows.start + t * batch, rows.start + (t + 1) * batch)
            xr, xi = (ar * xr - ai * xi + x_scr[r, :n], ar * xi + ai * xr + x_scr[r, n:])
            x_scr[r, :n] = xr
            x_scr[r, n:] = xi
        mid = rows.start + sub_r // 2
        cx = jnp.concatenate([_dot(x_scr[rows.start:mid, :].astype(BF16), cd_ref[...]),
                              _dot(x_scr[mid:rows.stop, :].astype(BF16), cd_ref[...])], axis=0)
        y = cx + d_ref[...] * u
        y = _gelu_tanh(y)
        y = y * jax.nn.sigmoid(_dot(y.astype(BF16), wglu_ref[...]))
        for lt in range(lane_tiles):
            y_scr[lt, rows, :] = y[:, lt * LANES:(lt + 1) * LANES]
    sr_scr[...] = xr
    si_scr[...] = xi
    for b in range(batch):
        for lt in range(lane_tiles):
            c0 = b * SSM_CH + lt * LANES
            y_ref[:, c0:c0 + LANES] = y_scr[lt, pl.ds(b, SSM_TS, stride=batch), :].astype(BF16)


def _ssm(u, batch, l, bd, ar, ai, cd, d, wglu):
    s_, width = u.shape
    tr = SSM_TS * batch
    return pl.pallas_call(
        _ssm_kernel,
        grid=(s_ // SSM_TS,),
        in_specs=[pl.BlockSpec((SSM_TS, width), lambda i: (i, 0))]
        + [_layer_spec(a, l) for a in (bd, ar, ai, cd, d, wglu)],
        out_specs=pl.BlockSpec((SSM_TS, width), lambda i: (i, 0)),
        out_shape=jax.ShapeDtypeStruct((s_, width), BF16),
        scratch_shapes=[pltpu.VMEM((SSM_CH // LANES, tr, LANES), F32),
                        pltpu.VMEM((tr, 2 * SSM_STATES), F32),
                        pltpu.VMEM((SSM_CH // LANES, tr, LANES), F32),
                        pltpu.VMEM((batch, SSM_STATES), F32),
                        pltpu.VMEM((batch, SSM_STATES), F32)],
        compiler_params=pltpu.CompilerParams(
            dimension_semantics=("arbitrary",), vmem_limit_bytes=VMEM_LIMIT),
        name="s5_mixer",
    )(u, bd, ar, ai, cd, d, wglu)


def _ssm_params(a_re, a_im, log_dt, b_re, b_im, c_re, c_im, batch):
    lam = lax.complex(a_re.astype(F32), a_im.astype(F32))
    dt = jnp.exp(log_dt.astype(F32))
    a_bar = jnp.exp(lam * dt[:, None])
    b_bar = ((a_bar - 1.0) / lam)[..., None] * lax.complex(b_re.astype(F32), b_im.astype(F32))
    eye = jnp.eye(SSM_GROUPS, dtype=F32)

    def in_op(m):
        return jnp.einsum('gpc,gh->gchp', m, eye).reshape(SSM_CH, SSM_STATES)

    def out_op(m):
        return jnp.einsum('gcp,gh->gphc', m, eye).reshape(SSM_STATES, SSM_CH)

    bd = jnp.concatenate([in_op(jnp.real(b_bar)), in_op(jnp.imag(b_bar))], axis=1).astype(BF16)
    cd = jnp.concatenate([out_op(c_re.astype(F32)), -out_op(c_im.astype(F32))], axis=0).astype(BF16)
    ar = jnp.broadcast_to(jnp.real(a_bar).reshape(1, SSM_STATES), (batch, SSM_STATES))
    ai = jnp.broadcast_to(jnp.imag(a_bar).reshape(1, SSM_STATES), (batch, SSM_STATES))
    return bd, ar, ai, cd


def _ffn_kernel(x_ref, oa_ref, ob_ref, oc_ref, wo_ref, g2_ref, wg_ref, wu_ref, wd_ref, gf_ref, o_ref, h_scr,
                *, final_norm):
    wo_a = wo_ref[:SWA_Q, :]
    wo_b = wo_ref[SWA_Q:SWA_Q + SSM_CH, :]
    wo_c = wo_ref[SWA_Q + SSM_CH:, :]
    x = x_ref[...] + _dot(oa_ref[...], wo_a) + _dot(ob_ref[...], wo_b) + _dot(oc_ref[...], wo_c)
    o_ref[...] = x
    h_scr[...] = _rms(x, g2_ref[...]).astype(BF16)

    h = h_scr[...]
    gate = _dot(h, wg_ref[...])
    up = _dot(h, wu_ref[...])
    act = (gate * jax.nn.sigmoid(gate) * up).astype(BF16)
    o_ref[...] += _dot(act, wd_ref[...])
    if final_norm:
        o_ref[...] = _rms(o_ref[...], gf_ref[...])


def _ffn(x, oa, ob, oc, l, wo, g2, wg, wu, wd, gf, final_norm):
    b_, s_, _ = x.shape
    tm = FFN_TM
    row3 = lambda b, i: (b, i, 0)
    whole = lambda wt: pl.BlockSpec(wt.shape, lambda b, i: (0, 0))
    return pl.pallas_call(
        functools.partial(_ffn_kernel, final_norm=final_norm),
        grid=(b_, s_ // tm),
        in_specs=[pl.BlockSpec((None, tm, D_MODEL), row3),
                  pl.BlockSpec((None, tm, SWA_Q), row3),
                  pl.BlockSpec((tm, SSM_CH), lambda b, i: (i, b)),
                  pl.BlockSpec((None, tm, MLA_HEADS * MLA_V), row3),
                  whole(wo), _layer_spec(g2, l), whole(wg), whole(wu), whole(wd),
                  pl.BlockSpec((1, D_MODEL), lambda b, i: (0, 0))],
        out_specs=pl.BlockSpec((None, tm, D_MODEL), row3),
        out_shape=jax.ShapeDtypeStruct(x.shape, F32),
        scratch_shapes=[pltpu.VMEM((tm, D_MODEL), BF16)],
        compiler_params=pltpu.CompilerParams(
            dimension_semantics=("arbitrary", "arbitrary"), vmem_limit_bytes=VMEM_LIMIT),
        name="out_proj_ffn",
    )(x, oa, ob, oc, wo, g2, wg, wu, wd, gf)


def _rot_half_cols(w):
    half = w.shape[-1] // 2
    return jnp.concatenate([-w[..., half:], w[..., :half]], axis=-1)


def _rope_tile(w):
    z = jnp.zeros(w.shape[:-1] + (MLA_NOPE,), w.dtype)
    z2 = jnp.zeros(w.shape[:-1] + (MLA_HEAD_PAD - MLA_NOPE - MLA_ROPE,), w.dtype)
    return jnp.concatenate([z, w, z2], axis=-1)


def _rope_key_placement():
    half = MLA_ROPE // 2
    j = np.arange(MLA_ROPE)
    place = np.zeros((MLA_ROPE, 2 * LANES), np.float32)
    place[j, MLA_NOPE + j] = 1.0
    place[j, LANES + MLA_NOPE + (j + half) % MLA_ROPE] = np.where(j < half, 1.0, -1.0)
    return jnp.asarray(place, BF16)


def _stacked_weights(w_q_up, w_kv_up):
    lead = w_q_up.shape[:-2]
    qh = w_q_up.reshape(lead + (MLA_Q_RANK, MLA_HEADS, MLA_NOPE + MLA_ROPE))
    pad = jnp.zeros(lead + (MLA_Q_RANK, MLA_HEADS, MLA_HEAD_PAD - MLA_NOPE - MLA_ROPE), w_q_up.dtype)
    q_tiles = jnp.concatenate([qh, pad], axis=-1).reshape(lead + (MLA_Q_RANK, -1))
    q_rot = _rope_tile(_rot_half_cols(qh[..., MLA_NOPE:])).reshape(lead + (MLA_Q_RANK, -1))
    wq_ext = jnp.concatenate([q_tiles, q_rot], axis=-1)
    kvh = w_kv_up.reshape(lead + (MLA_KV_RANK, MLA_HEADS, MLA_NOPE + MLA_V))
    kpad = jnp.zeros(lead + (MLA_KV_RANK, MLA_HEADS, MLA_HEAD_PAD - MLA_NOPE), w_kv_up.dtype)
    k_tiles = jnp.concatenate([kvh[..., :MLA_NOPE], kpad], axis=-1).reshape(lead + (MLA_KV_RANK, -1))
    v_cols = kvh[..., MLA_NOPE:].reshape(lead + (MLA_KV_RANK, -1))
    wkv_ext = jnp.concatenate([k_tiles, v_cols], axis=-1)
    return wq_ext.astype(BF16), wkv_ext.astype(BF16)


def kernel(x, positions, rel_bias, ln1_g, w_in, sinks, ssm_a_re, ssm_a_im, ssm_log_dt, ssm_b_re, ssm_b_im,
           ssm_c_re, ssm_c_im, ssm_d, ssm_w_glu, mla_q_norm_g, mla_w_q_up, mla_kv_norm_g, mla_w_kv_up, w_out,
           ln2_g, w_gate, w_up, w_down, final_g):
    b_ = x.shape[0]
    cos_t, sin_t = _trig_tables(positions)
    bias = _band_bias_tables(rel_bias, sinks)
    row = lambda g: g[:, None, :]
    wq_ext, wkv_ext = _stacked_weights(mla_w_q_up, mla_w_kv_up)
    w_in_bf = w_in.astype(BF16)
    bd, ar, ai, cd = jax.vmap(functools.partial(_ssm_params, batch=b_))(
        ssm_a_re, ssm_a_im, ssm_log_dt, ssm_b_re, ssm_b_im, ssm_c_re, ssm_c_im)
    wglu = ssm_w_glu.astype(BF16)
    gf = final_g.reshape(1, D_MODEL)
    for l in range(DEPTH):
        qa, ka, va, u, qm, km, vm = _in_proj(x, l, row(ln1_g), w_in_bf, cos_t, sin_t,
                                             row(mla_q_norm_g), wq_ext, row(mla_kv_norm_g), wkv_ext)
        o_a = _swa(qa, ka, va, l, bias)
        o_b = _ssm(u, b_, l, bd, ar, ai, cd, row(ssm_d), wglu)
        o_c, wo, wg, wu, wd = _mla(qm, km, vm, l, (w_out, w_gate, w_up, w_down))
        x = _ffn(x, o_a, o_b, o_c, l, wo, row(ln2_g), wg, wu, wd, gf, final_norm=(l == DEPTH - 1))
    return x
```

```python
import functools
import math

import jax
import jax.numpy as jnp
import numpy as np
from jax import lax
from jax.experimental import pallas as pl
from jax.experimental.pallas import tpu as pltpu

D_MODEL = 1024
DEPTH = 2
HEAD_DIM = 64
SWA_Q_HEADS = 8
SWA_KV_HEADS = 2
SWA_BLOCK = 128
SWA_Q = SWA_Q_HEADS * HEAD_DIM
SWA_KV = SWA_KV_HEADS * HEAD_DIM
REL_BUCKETS = 32
REL_MAX_DIST = 128
SSM_CH = 256
SSM_GROUP = 16
SSM_GROUPS = SSM_CH // SSM_GROUP
SSM_STATE = 64
SSM_STATES = SSM_GROUPS * SSM_STATE
MLA_HEADS = 4
MLA_Q_RANK = 256
MLA_KV_RANK = 128
MLA_NOPE = 64
MLA_ROPE = 32
MLA_V = 64
ROPE_THETA = 10000.0
D_FF = 2816
EPS = 1e-6
NEG = -1e30

LANES = 128
V7X_VMEM_BYTES = 64 * 1024 * 1024
VMEM_LIMIT = V7X_VMEM_BYTES * 7 // 8
MLA_HEAD_PAD = LANES
MLA_EXP2_SCALE = (MLA_NOPE + MLA_ROPE) ** -0.5 * math.log2(math.e)

IN_PROJ_TM = 1024
SWA_TQ = 512
MLA_TQ = 512
MLA_TK = 512
SSM_TS = 128
SSM_SUB = 2
FFN_TM = 1024

BF16 = jnp.bfloat16
F32 = jnp.float32


def _rms(x, g):
    return x * lax.rsqrt(jnp.mean(x * x, axis=-1, keepdims=True) + EPS) * g


def _dot(a, b):
    return jnp.dot(a, b, preferred_element_type=F32)


def _dot_nt(a, b):
    return lax.dot_general(a, b, (((1,), (1,)), ((), ())), preferred_element_type=F32)


def _bf16_pieces(t):
    hi = t.astype(BF16)
    rest = t - hi.astype(F32)
    mid = rest.astype(BF16)
    return hi, mid, (rest - mid.astype(F32)).astype(BF16)


def _table_spec(arr):
    return pl.BlockSpec(arr.shape, lambda *_: (0,) * arr.ndim)


def _layer_spec(arr, l):
    return pl.BlockSpec((None,) + arr.shape[1:], lambda *_: (l,) + (0,) * (arr.ndim - 1))


def _trig_kernel(pos_ref, freq_ref, place_ref, one_ref, cos_ref, sin_ref):
    pos = pos_ref[pl.ds(pl.program_id(0), 1), :].astype(F32)
    ang = freq_ref[...] * pos

    def place(t):
        return sum(lax.dot_general(piece, place_ref[...], (((0,), (0,)), ((), ())), preferred_element_type=F32)
                   for piece in _bf16_pieces(t))

    cos_ref[...] = place(jnp.cos(ang)) + one_ref[...]
    sin_ref[...] = place(jnp.sin(ang))


def _trig_tables(positions):
    b_, s_ = positions.shape
    half = MLA_ROPE // 2
    inv_freq = jnp.power(ROPE_THETA, -jnp.arange(half, dtype=F32) * 2.0 / MLA_ROPE).reshape(half, 1)
    lane = np.arange(MLA_HEAD_PAD)[None, :]
    rope_lane = (lane >= MLA_NOPE) & (lane < MLA_NOPE + MLA_ROPE)
    placement = jnp.asarray(rope_lane & ((lane - MLA_NOPE) % half == np.arange(half)[:, None]), BF16)
    off_rope = jnp.asarray(~rope_lane, F32)
    const = lambda b: (0, 0)
    cos_t, sin_t = pl.pallas_call(
        _trig_kernel,
        grid=(b_,),
        in_specs=[pl.BlockSpec((b_, s_), const),
                  pl.BlockSpec((half, 1), const),
                  pl.BlockSpec((half, MLA_HEAD_PAD), const),
                  pl.BlockSpec((1, MLA_HEAD_PAD), const)],
        out_specs=[pl.BlockSpec((None, s_, MLA_HEAD_PAD), lambda b: (b, 0, 0))] * 2,
        out_shape=[jax.ShapeDtypeStruct((b_, s_, MLA_HEAD_PAD), F32)] * 2,
        name="rope_tables",
    )(positions, inv_freq, placement, off_rope)
    return cos_t.reshape(b_ * s_, MLA_HEAD_PAD), sin_t.reshape(b_ * s_, MLA_HEAD_PAD)


def _in_proj_kernel(x_ref, g_ref, w_ref, kplace_ref, cos_ref, sin_ref, qg_ref, wq_ref, kvg_ref, wkv_ref,
                    qa_ref, ka_ref, va_ref, u_ref, qm_ref, km_ref, vm_ref, *, layer):
    row = slice(layer, layer + 1)
    h = _rms(x_ref[...], g_ref[row, :]).astype(BF16)
    proj = _dot(h, w_ref[...])
    qa_ref[...] = (proj[:, :SWA_Q] * (HEAD_DIM ** -0.5)).astype(BF16)
    c = SWA_Q
    ka_ref[...] = proj[:, c:c + SWA_KV].astype(BF16)
    c += SWA_KV
    va_ref[...] = proj[:, c:c + SWA_KV].astype(BF16)
    c += SWA_KV
    u_ref[...] = proj[:, c:c + SSM_CH]
    c += SSM_CH
    c_q = proj[:, c:c + MLA_Q_RANK]
    c += MLA_Q_RANK
    c_kv = proj[:, c:c + MLA_KV_RANK]
    c += MLA_KV_RANK
    k_r2 = sum(_dot(piece, kplace_ref[...]) for piece in _bf16_pieces(proj[:, c:c + MLA_ROPE]))
    k_r, k_r_rot = k_r2[:, :LANES], k_r2[:, LANES:]
    cos = cos_ref[...]
    sin = sin_ref[...]
    width = MLA_HEADS * MLA_HEAD_PAD
    q2 = _dot(_rms(c_q, qg_ref[row, :]).astype(BF16), wq_ref[...])
    kv2 = _dot(_rms(c_kv, kvg_ref[row, :]).astype(BF16), wkv_ref[...])
    k_pe = k_r * cos + k_r_rot * sin
    low = lax.broadcasted_iota(jnp.int32, (cos.shape[0], LANES), 1) < MLA_V
    for hd in range(MLA_HEADS):
        sl = slice(hd * MLA_HEAD_PAD, (hd + 1) * MLA_HEAD_PAD)
        sl_rot = slice(width + hd * MLA_HEAD_PAD, width + (hd + 1) * MLA_HEAD_PAD)
        qm_ref[:, sl] = (q2[:, sl] * cos + q2[:, sl_rot] * sin).astype(BF16)
        km_ref[:, sl] = (kv2[:, sl] + k_pe).astype(BF16)
        pair = kv2[:, width + (hd // 2) * LANES:width + (hd // 2 + 1) * LANES]
        keep = low if hd % 2 == 0 else jnp.logical_not(low)
        vm_ref[:, sl] = jnp.where(keep, pair, 1.0).astype(BF16)


def _in_proj(x, l, g, w_in, cos_t, sin_t, qg, wq_ext, kvg, wkv_ext):
    b_, s_, _ = x.shape
    tm = IN_PROJ_TM
    nt = s_ // tm
    row3 = lambda b, i: (b, i, 0)
    trig = pl.BlockSpec((tm, MLA_HEAD_PAD), lambda b, i: (b * nt + i, 0))
    mla_w = MLA_HEADS * MLA_HEAD_PAD
    kplace = _rope_key_placement()
    return pl.pallas_call(
        functools.partial(_in_proj_kernel, layer=l),
        grid=(b_, nt),
        in_specs=[pl.BlockSpec((None, tm, D_MODEL), row3),
                  _table_spec(g), _layer_spec(w_in, l),
                  pl.BlockSpec(kplace.shape, lambda b, i: (0, 0)),
                  trig, trig,
                  _table_spec(qg), _layer_spec(wq_ext, l), _table_spec(kvg), _layer_spec(wkv_ext, l)],
        out_specs=[pl.BlockSpec((None, tm, SWA_Q), row3),
                   pl.BlockSpec((None, tm, SWA_KV), row3),
                   pl.BlockSpec((None, tm, SWA_KV), row3),
                   pl.BlockSpec((tm, SSM_CH), lambda b, i: (i, b)),
                   pl.BlockSpec((None, tm, mla_w), row3),
                   pl.BlockSpec((None, tm, mla_w), row3),
                   pl.BlockSpec((None, tm, mla_w), row3)],
        out_shape=[jax.ShapeDtypeStruct((b_, s_, SWA_Q), BF16),
                   jax.ShapeDtypeStruct((b_, s_, SWA_KV), BF16),
                   jax.ShapeDtypeStruct((b_, s_, SWA_KV), BF16),
                   jax.ShapeDtypeStruct((s_, b_ * SSM_CH), F32),
                   jax.ShapeDtypeStruct((b_, s_, mla_w), BF16),
                   jax.ShapeDtypeStruct((b_, s_, mla_w), BF16),
                   jax.ShapeDtypeStruct((b_, s_, mla_w), BF16)],
        compiler_params=pltpu.CompilerParams(
            dimension_semantics=("arbitrary", "arbitrary"), vmem_limit_bytes=VMEM_LIMIT),
        name="in_proj",
    )(x, g, w_in, kplace, cos_t, sin_t, qg, wq_ext, kvg, wkv_ext)


def _swa_kernel(q_ref, k_ref, v_ref, bias_ref, o_ref, s_scr, m_scr):
    qi = pl.program_id(1)
    blocks = SWA_TQ // SWA_BLOCK
    tiles = SWA_Q // LANES
    band = 2 * SWA_BLOCK
    lane = lax.broadcasted_iota(jnp.int32, (band, LANES), 1)
    key = lax.broadcasted_iota(jnp.int32, (band, LANES), 0)
    low = lane < HEAD_DIM
    out_low = lax.broadcasted_iota(jnp.int32, (SWA_BLOCK, LANES), 1) < HEAD_DIM

    def band_of(ref, j):
        blk = qi * blocks + j
        start = pl.multiple_of(jnp.maximum(blk - 1, 0) * SWA_BLOCK, SWA_BLOCK)
        keep = key != jnp.where(blk == 0, band - 1, 0)
        t = ref[pl.ds(start, band), :]
        t = jnp.where(keep, t, jnp.zeros_like(t))
        return t, pltpu.roll(t, HEAD_DIM, 1)

    for j in range(blocks):
        table = jnp.minimum(qi * blocks + j, 1)
        kb, kb_sw = band_of(k_ref, j)
        zero = jnp.zeros_like(kb)
        k_var = ((jnp.where(low, kb, zero), jnp.where(low, zero, kb_sw)),
                 (jnp.where(low, kb_sw, zero), jnp.where(low, zero, kb)))
        rows = slice(j * SWA_BLOCK, (j + 1) * SWA_BLOCK)
        for i in range(tiles):
            kvh = (2 * i) // (SWA_Q_HEADS // SWA_KV_HEADS)
            q = q_ref[rows, i * LANES:(i + 1) * LANES]
            for half in range(2):
                u = (j * tiles + i) * 2 + half
                s = _dot_nt(q, k_var[kvh][half]) + bias_ref[table, 2 * i + half]
                s_scr[u] = s
                m_scr[u] = jnp.broadcast_to(jnp.max(s, axis=-1, keepdims=True), (SWA_BLOCK, LANES))
    for j in range(blocks):
        vb, vb_sw = band_of(v_ref, j)
        one = jnp.ones_like(vb)
        v_var = ((jnp.where(low, vb, one), jnp.where(low, one, vb_sw)),
                 (jnp.where(low, vb_sw, one), jnp.where(low, one, vb)))
        rows = slice(j * SWA_BLOCK, (j + 1) * SWA_BLOCK)
        for i in range(tiles):
            kvh = (2 * i) // (SWA_Q_HEADS // SWA_KV_HEADS)
            r = []
            for half in range(2):
                u = (j * tiles + i) * 2 + half
                m = m_scr[u]
                p = jnp.concatenate([jnp.exp(s_scr[u, :, t * LANES:(t + 1) * LANES] - m).astype(BF16)
                                     for t in range(band // LANES)], axis=-1)
                r.append(_dot(p, v_var[kvh][half]))
            denom = pltpu.roll(jnp.where(out_low, r[1], r[0]), HEAD_DIM, 1)
            o_ref[rows, i * LANES:(i + 1) * LANES] = (jnp.where(out_low, r[0], r[1]) / denom).astype(BF16)


def _swa(q, k, v, l, bias):
    b_, s_, _ = q.shape
    tq = SWA_TQ
    units = (tq // SWA_BLOCK) * SWA_Q_HEADS
    return pl.pallas_call(
        _swa_kernel,
        grid=(b_, s_ // tq),
        in_specs=[pl.BlockSpec((None, tq, SWA_Q), lambda b, i: (b, i, 0)),
                  pl.BlockSpec((None, s_, SWA_KV), lambda b, i: (b, 0, 0)),
                  pl.BlockSpec((None, s_, SWA_KV), lambda b, i: (b, 0, 0)),
                  _layer_spec(bias, l)],
        out_specs=pl.BlockSpec((None, tq, SWA_Q), lambda b, i: (b, i, 0)),
        out_shape=jax.ShapeDtypeStruct((b_, s_, SWA_Q), BF16),
        scratch_shapes=[pltpu.VMEM((units, SWA_BLOCK, 2 * SWA_BLOCK), F32),
                        pltpu.VMEM((units, SWA_BLOCK, LANES), F32)],
        compiler_params=pltpu.CompilerParams(
            dimension_semantics=("arbitrary", "arbitrary"), vmem_limit_bytes=VMEM_LIMIT),
        name="swa_attention",
    )(q, k, v, bias)


def _t5_bucket(dist):
    n = jnp.maximum(dist, 0)
    max_exact = REL_BUCKETS // 2
    large = max_exact + (jnp.log(jnp.maximum(n, 1).astype(F32) / max_exact)
                         / math.log(REL_MAX_DIST / max_exact)
                         * (REL_BUCKETS - max_exact)).astype(jnp.int32)
    large = jnp.minimum(large, REL_BUCKETS - 1)
    return jnp.where(n < max_exact, n, large)


def _band_bias_tables(rel_bias, sinks):
    qi = jnp.arange(SWA_BLOCK)[:, None]
    kj = jnp.arange(2 * SWA_BLOCK)[None, :]
    dist = qi + SWA_BLOCK - kj
    onehot = (_t5_bucket(dist)[None] == jnp.arange(REL_BUCKETS)[:, None, None]).astype(F32)
    b = jnp.einsum('nh,nqk->hqk', rel_bias.astype(F32), onehot, precision=lax.Precision.HIGHEST)
    valid = (dist >= 0) & (dist < SWA_BLOCK)
    normal = jnp.where(valid[None], b, NEG)
    first = jnp.concatenate([normal[:, :, SWA_BLOCK:], jnp.full_like(normal[:, :, SWA_BLOCK:], NEG)], axis=-1)
    sink_col = jnp.array([2 * SWA_BLOCK - 1, 0])[:, None, None, None]
    sink = sinks.astype(F32)[:, None, :, None, None]
    return jnp.where(kj == sink_col, sink, jnp.stack([first, normal])[None])


def _mla_kernel(q_ref, k_ref, v_ref, *rest):
    n_w = (len(rest) - 4) // 2
    w_f32, o_ref, w_bf16 = rest[:n_w], rest[n_w], rest[n_w + 1:2 * n_w + 1]
    s_scr, m_scr, acc_scr = rest[2 * n_w + 1:]
    for src, dst in zip(w_f32, w_bf16):
        dst[...] = src[...].astype(BF16)
    p = pl.program_id(1)
    tq, tk = MLA_TQ, MLA_TK
    n = q_ref.shape[0] // tq
    row = lax.broadcasted_iota(jnp.int32, (tq, tk), 0)
    col = lax.broadcasted_iota(jnp.int32, (tq, tk), 1)
    causal = col <= row
    out_low = lax.broadcasted_iota(jnp.int32, (tq, LANES), 1) < MLA_V
    heads = [slice(hd * MLA_HEAD_PAD, (hd + 1) * MLA_HEAD_PAD) for hd in range(MLA_HEADS)]
    q_tiles = (p, n - 1 - p)
    half = tq // 2
    causal_half = causal[:half, :half]
    causal_bottom = jnp.concatenate([jnp.ones((half, half), jnp.bool_), causal_half], axis=1)

    def slot(j):
        if j >= n // 2 - 1:
            return 1, j - p
        first = j < p
        return jnp.where(first, 0, 1), jnp.where(first, j, j - p)

    def rows(start, size, align):
        return pl.ds(pl.multiple_of(start, align), size)

    def lane_max(m, s):
        return functools.reduce(jnp.maximum, [s[:, t * LANES:(t + 1) * LANES] for t in range(s.shape[1] // LANES)], m)

    def probs(s, m):
        return jnp.concatenate([jnp.exp2((s[:, t * LANES:(t + 1) * LANES] - m) * MLA_EXP2_SCALE).astype(BF16)
                                for t in range(s.shape[1] // LANES)], axis=-1)

    m_scr[...] = jnp.full(m_scr.shape, NEG, F32)
    acc_scr[...] = jnp.zeros(acc_scr.shape, F32)
    for which in range(2):
        r0 = q_tiles[which] * tq
        top, bottom, whole = rows(r0, half, tq), rows(r0 + half, half, half), rows(r0, tq, tq)
        for hd in range(MLA_HEADS):
            s_top = jnp.where(causal_half, _dot_nt(q_ref[top, heads[hd]], k_ref[top, heads[hd]]), NEG)
            s_bot = jnp.where(causal_bottom, _dot_nt(q_ref[bottom, heads[hd]], k_ref[whole, heads[hd]]), NEG)
            s_scr[hd, which, :half, :half] = s_top
            s_scr[hd, which, half:, :] = s_bot
            m_scr[which, hd, :half] = lane_max(m_scr[which, hd, :half], s_top)
            m_scr[which, hd, half:] = lane_max(m_scr[which, hd, half:], s_bot)
    for j in range(n - 1):
        which, kt = slot(j)
        qt = q_tiles[which] if isinstance(which, int) else jnp.where(which == 0, q_tiles[0], q_tiles[1])
        for hd in range(MLA_HEADS):
            s = _dot_nt(q_ref[rows(qt * tq, tq, tq), heads[hd]], k_ref[rows(kt * tk, tk, tk), heads[hd]])
            s_scr[hd, 2 + j] = s
            m_scr[which, hd] = lane_max(m_scr[which, hd], s)
    for which in range(2):
        for hd in range(MLA_HEADS):
            m_scr[which, hd] = jnp.broadcast_to(jnp.max(m_scr[which, hd], axis=-1, keepdims=True), (tq, LANES))
    for which in range(2):
        r0 = q_tiles[which] * tq
        for hd in range(MLA_HEADS):
            e_top = probs(s_scr[hd, which, :half, :half], m_scr[which, hd, :half])
            e_bot = probs(s_scr[hd, which, half:, :], m_scr[which, hd, half:])
            acc_scr[which, hd, :half] += _dot(e_top, v_ref[rows(r0, half, tq), heads[hd]])
            acc_scr[which, hd, half:] += _dot(e_bot, v_ref[rows(r0, tq, tq), heads[hd]])
    for j in range(n - 1):
        which, kt = slot(j)
        for hd in range(MLA_HEADS):
            e = probs(s_scr[hd, 2 + j], m_scr[which, hd])
            acc_scr[which, hd] += _dot(e, v_ref[rows(kt * tk, tk, tk), heads[hd]])
    for which in range(2):
        for t in range(MLA_HEADS // 2):
            lo, hi = acc_scr[which, 2 * t], acc_scr[which, 2 * t + 1]
            lo = lo / lo[:, MLA_V:MLA_V + 1]
            hi = hi / hi[:, 0:1]
            o_ref[rows(q_tiles[which] * tq, tq, tq), t * LANES:(t + 1) * LANES] = (
                jnp.where(out_low, lo, hi).astype(BF16))


def _mla(q, k, v, l, weights):
    b_, s_, w = q.shape
    n = s_ // MLA_TQ
    steps = b_ * (n // 2)
    whole = lambda width: pl.BlockSpec((None, s_, width), lambda b, p: (b, 0, 0))
    slab_in = [pl.BlockSpec((None, wt.shape[1] // steps, wt.shape[2]), lambda b, p: (l, b * (n // 2) + p, 0))
               for wt in weights]
    slab_out = [pl.BlockSpec((wt.shape[1] // steps, wt.shape[2]), lambda b, p: (b * (n // 2) + p, 0))
                for wt in weights]
    return pl.pallas_call(
        _mla_kernel,
        grid=(b_, n // 2),
        in_specs=[whole(w), whole(w), whole(w)] + slab_in,
        out_specs=[whole(MLA_HEADS * MLA_V)] + slab_out,
        out_shape=[jax.ShapeDtypeStruct((b_, s_, MLA_HEADS * MLA_V), BF16)]
        + [jax.ShapeDtypeStruct(wt.shape[1:], BF16) for wt in weights],
        scratch_shapes=[pltpu.VMEM((MLA_HEADS, n + 1, MLA_TQ, MLA_TK), F32),
                        pltpu.VMEM((2, MLA_HEADS, MLA_TQ, LANES), F32),
                        pltpu.VMEM((2, MLA_HEADS, MLA_TQ, LANES), F32)],
        compiler_params=pltpu.CompilerParams(
            dimension_semantics=("arbitrary", "arbitrary"), vmem_limit_bytes=VMEM_LIMIT),
        name="mla_attention",
    )(q, k, v, *weights)


def _gelu_tanh(x):
    return 0.5 * x * (1.0 + jnp.tanh(math.sqrt(2.0 / math.pi) * (x + 0.044715 * (x * x * x))))


def _ssm_kernel(u_ref, bd_ref, ar_ref, ai_ref, cd_ref, d_ref, wglu_ref, y_ref,
                u_scr, x_scr, y_scr, sr_scr, si_scr, *, layer):
    n = SSM_STATES
    batch = sr_scr.shape[0]
    lane_tiles = SSM_CH // LANES

    @pl.when(pl.program_id(0) == 0)
    def _():
        sr_scr[...] = jnp.zeros_like(sr_scr)
        si_scr[...] = jnp.zeros_like(si_scr)

    for b in range(batch):
        for lt in range(lane_tiles):
            c0 = b * SSM_CH + lt * LANES
            u_scr[lt, pl.ds(b, SSM_TS, stride=batch), :] = u_ref[:, c0:c0 + LANES]
    sub_t = SSM_TS // SSM_SUB
    sub_r = sub_t * batch
    subs = [slice(q * sub_r, (q + 1) * sub_r) for q in range(SSM_SUB)]
    us = [jnp.concatenate([u_scr[lt, rows, :] for lt in range(lane_tiles)], axis=-1) for rows in subs]
    for rows, u in zip(subs, us):
        u_bf = u.astype(BF16)
        x_scr[rows, :n] = _dot(u_bf, bd_ref[:, :n])
        x_scr[rows, n:] = _dot(u_bf, bd_ref[:, n:])
    ar = ar_ref[...]
    ai = ai_ref[...]
    xr, xi = sr_scr[...], si_scr[...]
    for rows, u in zip(subs, us):
        for t in range(sub_t):
            r = slice(rows.start + t * batch, rows.start + (t + 1) * batch)
            xr, xi = (ar * xr - ai * xi + x_scr[r, :n], ar * xi + ai * xr + x_scr[r, n:])
            x_scr[r, :n] = xr
            x_scr[r, n:] = xi
        mid = rows.start + sub_r // 2
        cx = jnp.concatenate([_dot(x_scr[rows.start:mid, :].astype(BF16), cd_ref[...]),
                              _dot(x_scr[mid:rows.stop, :].astype(BF16), cd_ref[...])], axis=0)
        y = cx + d_ref[layer:layer + 1, :] * u
        y = _gelu_tanh(y)
        y = y * jax.nn.sigmoid(_dot(y.astype(BF16), wglu_ref[...]))
        for lt in range(lane_tiles):
            y_scr[lt, rows, :] = y[:, lt * LANES:(lt + 1) * LANES]
    sr_scr[...] = xr
    si_scr[...] = xi
    for b in range(batch):
        for lt in range(lane_tiles):
            c0 = b * SSM_CH + lt * LANES
            y_ref[:, c0:c0 + LANES] = y_scr[lt, pl.ds(b, SSM_TS, stride=batch), :].astype(BF16)


def _ssm(u, batch, l, bd, ar, ai, cd, d, wglu):
    s_, width = u.shape
    tr = SSM_TS * batch
    return pl.pallas_call(
        functools.partial(_ssm_kernel, layer=l),
        grid=(s_ // SSM_TS,),
        in_specs=[pl.BlockSpec((SSM_TS, width), lambda i: (i, 0))]
        + [_layer_spec(a, l) for a in (bd, ar, ai, cd)] + [_table_spec(d), _layer_spec(wglu, l)],
        out_specs=pl.BlockSpec((SSM_TS, width), lambda i: (i, 0)),
        out_shape=jax.ShapeDtypeStruct((s_, width), BF16),
        scratch_shapes=[pltpu.VMEM((SSM_CH // LANES, tr, LANES), F32),
                        pltpu.VMEM((tr, 2 * SSM_STATES), F32),
                        pltpu.VMEM((SSM_CH // LANES, tr, LANES), F32),
                        pltpu.VMEM((batch, SSM_STATES), F32),
                        pltpu.VMEM((batch, SSM_STATES), F32)],
        compiler_params=pltpu.CompilerParams(
            dimension_semantics=("arbitrary",), vmem_limit_bytes=VMEM_LIMIT),
        name="s5_mixer",
    )(u, bd, ar, ai, cd, d, wglu)


def _ssm_params(a_re, a_im, log_dt, b_re, b_im, c_re, c_im, batch):
    lam = lax.complex(a_re.astype(F32), a_im.astype(F32))
    dt = jnp.exp(log_dt.astype(F32))
    a_bar = jnp.exp(lam * dt[:, None])
    b_bar = ((a_bar - 1.0) / lam)[..., None] * lax.complex(b_re.astype(F32), b_im.astype(F32))
    eye = jnp.eye(SSM_GROUPS, dtype=F32)

    def in_op(m):
        return jnp.einsum('gpc,gh->gchp', m, eye).reshape(SSM_CH, SSM_STATES)

    def out_op(m):
        return jnp.einsum('gcp,gh->gphc', m, eye).reshape(SSM_STATES, SSM_CH)

    bd = jnp.concatenate([in_op(jnp.real(b_bar)), in_op(jnp.imag(b_bar))], axis=1).astype(BF16)
    cd = jnp.concatenate([out_op(c_re.astype(F32)), -out_op(c_im.astype(F32))], axis=0).astype(BF16)
    ar = jnp.broadcast_to(jnp.real(a_bar).reshape(1, SSM_STATES), (batch, SSM_STATES))
    ai = jnp.broadcast_to(jnp.imag(a_bar).reshape(1, SSM_STATES), (batch, SSM_STATES))
    return bd, ar, ai, cd


def _ffn_kernel(x_ref, oa_ref, ob_ref, oc_ref, wo_ref, g2_ref, wg_ref, wu_ref, wd_ref, gf_ref, o_ref, h_scr,
                *, layer, final_norm):
    wo_a = wo_ref[:SWA_Q, :]
    wo_b = wo_ref[SWA_Q:SWA_Q + SSM_CH, :]
    wo_c = wo_ref[SWA_Q + SSM_CH:, :]
    x = x_ref[...] + _dot(oa_ref[...], wo_a) + _dot(ob_ref[...], wo_b) + _dot(oc_ref[...], wo_c)
    o_ref[...] = x
    h_scr[...] = _rms(x, g2_ref[layer:layer + 1, :]).astype(BF16)

    h = h_scr[...]
    gate = _dot(h, wg_ref[...])
    up = _dot(h, wu_ref[...])
    act = (gate * jax.nn.sigmoid(gate) * up).astype(BF16)
    o_ref[...] += _dot(act, wd_ref[...])
    if final_norm:
        o_ref[...] = _rms(o_ref[...], gf_ref[...])


def _ffn(x, oa, ob, oc, l, wo, g2, wg, wu, wd, gf, final_norm):
    b_, s_, _ = x.shape
    tm = FFN_TM
    row3 = lambda b, i: (b, i, 0)
    whole = lambda wt: pl.BlockSpec(wt.shape, lambda b, i: (0, 0))
    return pl.pallas_call(
        functools.partial(_ffn_kernel, layer=l, final_norm=final_norm),
        grid=(b_, s_ // tm),
        in_specs=[pl.BlockSpec((None, tm, D_MODEL), row3),
                  pl.BlockSpec((None, tm, SWA_Q), row3),
                  pl.BlockSpec((tm, SSM_CH), lambda b, i: (i, b)),
                  pl.BlockSpec((None, tm, MLA_HEADS * MLA_V), row3),
                  whole(wo), _table_spec(g2), whole(wg), whole(wu), whole(wd),
                  pl.BlockSpec((1, D_MODEL), lambda b, i: (0, 0))],
        out_specs=pl.BlockSpec((None, tm, D_MODEL), row3),
        out_shape=jax.ShapeDtypeStruct(x.shape, F32),
        scratch_shapes=[pltpu.VMEM((tm, D_MODEL), BF16)],
        compiler_params=pltpu.CompilerParams(
            dimension_semantics=("arbitrary", "arbitrary"), vmem_limit_bytes=VMEM_LIMIT),
        name="out_proj_ffn",
    )(x, oa, ob, oc, wo, g2, wg, wu, wd, gf)


def _rot_half_cols(w):
    half = w.shape[-1] // 2
    return jnp.concatenate([-w[..., half:], w[..., :half]], axis=-1)


def _rope_tile(w):
    z = jnp.zeros(w.shape[:-1] + (MLA_NOPE,), w.dtype)
    z2 = jnp.zeros(w.shape[:-1] + (MLA_HEAD_PAD - MLA_NOPE - MLA_ROPE,), w.dtype)
    return jnp.concatenate([z, w, z2], axis=-1)


def _rope_key_placement():
    half = MLA_ROPE // 2
    j = np.arange(MLA_ROPE)
    place = np.zeros((MLA_ROPE, 2 * LANES), np.float32)
    place[j, MLA_NOPE + j] = 1.0
    place[j, LANES + MLA_NOPE + (j + half) % MLA_ROPE] = np.where(j < half, 1.0, -1.0)
    return jnp.asarray(place, BF16)


def _stacked_weights(w_q_up, w_kv_up):
    lead = w_q_up.shape[:-2]
    qh = w_q_up.reshape(lead + (MLA_Q_RANK, MLA_HEADS, MLA_NOPE + MLA_ROPE))
    pad = jnp.zeros(lead + (MLA_Q_RANK, MLA_HEADS, MLA_HEAD_PAD - MLA_NOPE - MLA_ROPE), w_q_up.dtype)
    q_tiles = jnp.concatenate([qh, pad], axis=-1).reshape(lead + (MLA_Q_RANK, -1))
    q_rot = _rope_tile(_rot_half_cols(qh[..., MLA_NOPE:])).reshape(lead + (MLA_Q_RANK, -1))
    wq_ext = jnp.concatenate([q_tiles, q_rot], axis=-1)
    kvh = w_kv_up.reshape(lead + (MLA_KV_RANK, MLA_HEADS, MLA_NOPE + MLA_V))
    kpad = jnp.zeros(lead + (MLA_KV_RANK, MLA_HEADS, MLA_HEAD_PAD - MLA_NOPE), w_kv_up.dtype)
    k_tiles = jnp.concatenate([kvh[..., :MLA_NOPE], kpad], axis=-1).reshape(lead + (MLA_KV_RANK, -1))
    v_cols = kvh[..., MLA_NOPE:].reshape(lead + (MLA_KV_RANK, -1))
    wkv_ext = jnp.concatenate([k_tiles, v_cols], axis=-1)
    return wq_ext.astype(BF16), wkv_ext.astype(BF16)


def kernel(x, positions, rel_bias, ln1_g, w_in, sinks, ssm_a_re, ssm_a_im, ssm_log_dt, ssm_b_re, ssm_b_im,
           ssm_c_re, ssm_c_im, ssm_d, ssm_w_glu, mla_q_norm_g, mla_w_q_up, mla_kv_norm_g, mla_w_kv_up, w_out,
           ln2_g, w_gate, w_up, w_down, final_g):
    b_ = x.shape[0]
    cos_t, sin_t = _trig_tables(positions)
    bias = _band_bias_tables(rel_bias, sinks)
    wq_ext, wkv_ext = _stacked_weights(mla_w_q_up, mla_w_kv_up)
    w_in_bf = w_in.astype(BF16)
    bd, ar, ai, cd = jax.vmap(functools.partial(_ssm_params, batch=b_))(
        ssm_a_re, ssm_a_im, ssm_log_dt, ssm_b_re, ssm_b_im, ssm_c_re, ssm_c_im)
    wglu = ssm_w_glu.astype(BF16)
    gf = final_g.reshape(1, D_MODEL)
    for l in range(DEPTH):
        qa, ka, va, u, qm, km, vm = _in_proj(x, l, ln1_g, w_in_bf, cos_t, sin_t,
                                             mla_q_norm_g, wq_ext, mla_kv_norm_g, wkv_ext)
        o_a = _swa(qa, ka, va, l, bias)
        o_b = _ssm(u, b_, l, bd, ar, ai, cd, ssm_d, wglu)
        o_c, wo, wg, wu, wd = _mla(qm, km, vm, l, (w_out, w_gate, w_up, w_down))
        x = _ffn(x, o_a, o_b, o_c, l, wo, ln2_g, wg, wu, wd, gf, final_norm=(l == DEPTH - 1))
    return x
```

```python
import functools
import math

import jax
import jax.numpy as jnp
import numpy as np
from jax import lax
from jax.experimental import pallas as pl
from jax.experimental.pallas import tpu as pltpu

D_MODEL = 1024
DEPTH = 2
HEAD_DIM = 64
SWA_Q_HEADS = 8
SWA_KV_HEADS = 2
SWA_BLOCK = 128
SWA_Q = SWA_Q_HEADS * HEAD_DIM
SWA_KV = SWA_KV_HEADS * HEAD_DIM
REL_BUCKETS = 32
REL_MAX_DIST = 128
SSM_CH = 256
SSM_GROUP = 16
SSM_GROUPS = SSM_CH // SSM_GROUP
SSM_STATE = 64
SSM_STATES = SSM_GROUPS * SSM_STATE
MLA_HEADS = 4
MLA_Q_RANK = 256
MLA_KV_RANK = 128
MLA_NOPE = 64
MLA_ROPE = 32
MLA_V = 64
ROPE_THETA = 10000.0
D_FF = 2816
EPS = 1e-6
NEG = -1e30

LANES = 128
V7X_VMEM_BYTES = 64 * 1024 * 1024
VMEM_LIMIT = V7X_VMEM_BYTES * 7 // 8
MLA_HEAD_PAD = LANES
MLA_EXP2_SCALE = (MLA_NOPE + MLA_ROPE) ** -0.5 * math.log2(math.e)

IN_PROJ_TM = 1024
SWA_TQ = 512
MLA_TQ = 512
MLA_TK = 512
SSM_TS = 128
SSM_SUB = 2
FFN_TM = 1024

BF16 = jnp.bfloat16
F32 = jnp.float32


def _rms(x, g):
    return x * lax.rsqrt(jnp.mean(x * x, axis=-1, keepdims=True) + EPS) * g


def _dot(a, b):
    return jnp.dot(a, b, preferred_element_type=F32)


def _dot_nt(a, b):
    return lax.dot_general(a, b, (((1,), (1,)), ((), ())), preferred_element_type=F32)


def _bf16_pieces(t):
    hi = t.astype(BF16)
    rest = t - hi.astype(F32)
    mid = rest.astype(BF16)
    return hi, mid, (rest - mid.astype(F32)).astype(BF16)


def _table_spec(arr):
    return pl.BlockSpec(arr.shape, lambda *_: (0,) * arr.ndim)


def _layer_spec(arr, l):
    return pl.BlockSpec((None,) + arr.shape[1:], lambda *_: (l,) + (0,) * (arr.ndim - 1))


def _trig_kernel(pos_ref, freq_ref, place_ref, one_ref, cos_ref, sin_ref):
    pos = pos_ref[pl.ds(pl.program_id(0), 1), :].astype(F32)
    ang = freq_ref[...] * pos

    def place(t):
        return sum(lax.dot_general(piece, place_ref[...], (((0,), (0,)), ((), ())), preferred_element_type=F32)
                   for piece in _bf16_pieces(t))

    cos_ref[...] = place(jnp.cos(ang)) + one_ref[...]
    sin_ref[...] = place(jnp.sin(ang))


def _trig_tables(positions):
    b_, s_ = positions.shape
    half = MLA_ROPE // 2
    inv_freq = jnp.power(ROPE_THETA, -jnp.arange(half, dtype=F32) * 2.0 / MLA_ROPE).reshape(half, 1)
    lane = np.arange(MLA_HEAD_PAD)[None, :]
    rope_lane = (lane >= MLA_NOPE) & (lane < MLA_NOPE + MLA_ROPE)
    placement = jnp.asarray(rope_lane & ((lane - MLA_NOPE) % half == np.arange(half)[:, None]), BF16)
    off_rope = jnp.asarray(~rope_lane, F32)
    const = lambda b: (0, 0)
    cos_t, sin_t = pl.pallas_call(
        _trig_kernel,
        grid=(b_,),
        in_specs=[pl.BlockSpec((b_, s_), const),
                  pl.BlockSpec((half, 1), const),
                  pl.BlockSpec((half, MLA_HEAD_PAD), const),
                  pl.BlockSpec((1, MLA_HEAD_PAD), const)],
        out_specs=[pl.BlockSpec((None, s_, MLA_HEAD_PAD), lambda b: (b, 0, 0))] * 2,
        out_shape=[jax.ShapeDtypeStruct((b_, s_, MLA_HEAD_PAD), F32)] * 2,
        name="rope_tables",
    )(positions, inv_freq, placement, off_rope)
    return cos_t.reshape(b_ * s_, MLA_HEAD_PAD), sin_t.reshape(b_ * s_, MLA_HEAD_PAD)


def _in_proj_kernel(x_ref, g_ref, w_ref, kplace_ref, cos_ref, sin_ref, qg_ref, wq_ref, kvg_ref, wkv_ref,
                    qa_ref, ka_ref, va_ref, u_ref, qm_ref, km_ref, vm_ref, *, layer):
    row = slice(layer, layer + 1)
    h = _rms(x_ref[...], g_ref[row, :]).astype(BF16)
    proj = _dot(h, w_ref[...])
    qa_ref[...] = (proj[:, :SWA_Q] * (HEAD_DIM ** -0.5)).astype(BF16)
    c = SWA_Q
    ka_ref[...] = proj[:, c:c + SWA_KV].astype(BF16)
    c += SWA_KV
    va_ref[...] = proj[:, c:c + SWA_KV].astype(BF16)
    c += SWA_KV
    u_ref[...] = proj[:, c:c + SSM_CH]
    c += SSM_CH
    c_q = proj[:, c:c + MLA_Q_RANK]
    c += MLA_Q_RANK
    c_kv = proj[:, c:c + MLA_KV_RANK]
    c += MLA_KV_RANK
    k_r2 = sum(_dot(piece, kplace_ref[...]) for piece in _bf16_pieces(proj[:, c:c + MLA_ROPE]))
    k_r, k_r_rot = k_r2[:, :LANES], k_r2[:, LANES:]
    cos = cos_ref[...]
    sin = sin_ref[...]
    width = MLA_HEADS * MLA_HEAD_PAD
    q2 = _dot(_rms(c_q, qg_ref[row, :]).astype(BF16), wq_ref[...])
    kv2 = _dot(_rms(c_kv, kvg_ref[row, :]).astype(BF16), wkv_ref[...])
    k_pe = k_r * cos + k_r_rot * sin
    low = lax.broadcasted_iota(jnp.int32, (cos.shape[0], LANES), 1) < MLA_V
    for hd in range(MLA_HEADS):
        sl = slice(hd * MLA_HEAD_PAD, (hd + 1) * MLA_HEAD_PAD)
        sl_rot = slice(width + hd * MLA_HEAD_PAD, width + (hd + 1) * MLA_HEAD_PAD)
        qm_ref[:, sl] = (q2[:, sl] * cos + q2[:, sl_rot] * sin).astype(BF16)
        km_ref[:, sl] = (kv2[:, sl] + k_pe).astype(BF16)
        pair = kv2[:, width + (hd // 2) * LANES:width + (hd // 2 + 1) * LANES]
        keep = low if hd % 2 == 0 else jnp.logical_not(low)
        vm_ref[:, sl] = jnp.where(keep, pair, 1.0).astype(BF16)


def _in_proj(x, l, g, w_in, cos_t, sin_t, qg, wq_ext, kvg, wkv_ext):
    b_, s_, _ = x.shape
    tm = IN_PROJ_TM
    nt = s_ // tm
    row3 = lambda b, i: (b, i, 0)
    trig = pl.BlockSpec((tm, MLA_HEAD_PAD), lambda b, i: (b * nt + i, 0))
    mla_w = MLA_HEADS * MLA_HEAD_PAD
    kplace = _rope_key_placement()
    return pl.pallas_call(
        functools.partial(_in_proj_kernel, layer=l),
        grid=(b_, nt),
        in_specs=[pl.BlockSpec((None, tm, D_MODEL), row3),
                  _table_spec(g), _layer_spec(w_in, l),
                  pl.BlockSpec(kplace.shape, lambda b, i: (0, 0)),
                  trig, trig,
                  _table_spec(qg), _layer_spec(wq_ext, l), _table_spec(kvg), _layer_spec(wkv_ext, l)],
        out_specs=[pl.BlockSpec((None, tm, SWA_Q), row3),
                   pl.BlockSpec((None, tm, SWA_KV), row3),
                   pl.BlockSpec((None, tm, SWA_KV), row3),
                   pl.BlockSpec((tm, SSM_CH), lambda b, i: (i, b)),
                   pl.BlockSpec((None, tm, mla_w), row3),
                   pl.BlockSpec((None, tm, mla_w), row3),
                   pl.BlockSpec((None, tm, mla_w), row3)],
        out_shape=[jax.ShapeDtypeStruct((b_, s_, SWA_Q), BF16),
                   jax.ShapeDtypeStruct((b_, s_, SWA_KV), BF16),
                   jax.ShapeDtypeStruct((b_, s_, SWA_KV), BF16),
                   jax.ShapeDtypeStruct((s_, b_ * SSM_CH), F32),
                   jax.ShapeDtypeStruct((b_, s_, mla_w), BF16),
                   jax.ShapeDtypeStruct((b_, s_, mla_w), BF16),
                   jax.ShapeDtypeStruct((b_, s_, mla_w), BF16)],
        compiler_params=pltpu.CompilerParams(
            dimension_semantics=("arbitrary", "arbitrary"), vmem_limit_bytes=VMEM_LIMIT),
        name="in_proj",
    )(x, g, w_in, kplace, cos_t, sin_t, qg, wq_ext, kvg, wkv_ext)


def _swa_kernel(q_ref, k_ref, v_ref, bias_ref, o_ref, s_scr, m_scr):
    qi = pl.program_id(1)
    blocks = SWA_TQ // SWA_BLOCK
    tiles = SWA_Q // LANES
    band = 2 * SWA_BLOCK
    lane = lax.broadcasted_iota(jnp.int32, (band, LANES), 1)
    key = lax.broadcasted_iota(jnp.int32, (band, LANES), 0)
    low = lane < HEAD_DIM
    out_low = lax.broadcasted_iota(jnp.int32, (SWA_BLOCK, LANES), 1) < HEAD_DIM

    def band_of(ref, j):
        blk = qi * blocks + j
        start = pl.multiple_of(jnp.maximum(blk - 1, 0) * SWA_BLOCK, SWA_BLOCK)
        keep = key != jnp.where(blk == 0, band - 1, 0)
        t = ref[pl.ds(start, band), :]
        t = jnp.where(keep, t, jnp.zeros_like(t))
        return t, pltpu.roll(t, HEAD_DIM, 1)

    for j in range(blocks):
        table = jnp.minimum(qi * blocks + j, 1)
        kb, kb_sw = band_of(k_ref, j)
        zero = jnp.zeros_like(kb)
        k_var = ((jnp.where(low, kb, zero), jnp.where(low, zero, kb_sw)),
                 (jnp.where(low, kb_sw, zero), jnp.where(low, zero, kb)))
        rows = slice(j * SWA_BLOCK, (j + 1) * SWA_BLOCK)
        for i in range(tiles):
            kvh = (2 * i) // (SWA_Q_HEADS // SWA_KV_HEADS)
            q = q_ref[rows, i * LANES:(i + 1) * LANES]
            for half in range(2):
                u = (j * tiles + i) * 2 + half
                s = _dot_nt(q, k_var[kvh][half]) + bias_ref[table, 2 * i + half]
                s_scr[u] = s
                m_scr[u] = jnp.broadcast_to(jnp.max(s, axis=-1, keepdims=True), (SWA_BLOCK, LANES))
    for j in range(blocks):
        vb, vb_sw = band_of(v_ref, j)
        one = jnp.ones_like(vb)
        v_var = ((jnp.where(low, vb, one), jnp.where(low, one, vb_sw)),
                 (jnp.where(low, vb_sw, one), jnp.where(low, one, vb)))
        rows = slice(j * SWA_BLOCK, (j + 1) * SWA_BLOCK)
        for i in range(tiles):
            kvh = (2 * i) // (SWA_Q_HEADS // SWA_KV_HEADS)
            r = []
            for half in range(2):
                u = (j * tiles + i) * 2 + half
                m = m_scr[u]
                p = jnp.concatenate([jnp.exp(s_scr[u, :, t * LANES:(t + 1) * LANES] - m).astype(BF16)
                                     for t in range(band // LANES)], axis=-1)
                r.append(_dot(p, v_var[kvh][half]))
            denom = pltpu.roll(jnp.where(out_low, r[1], r[0]), HEAD_DIM, 1)
            o_ref[rows, i * LANES:(i + 1) * LANES] = (jnp.where(out_low, r[0], r[1]) / denom).astype(BF16)


def _swa(q, k, v, l, bias):
    b_, s_, _ = q.shape
    tq = SWA_TQ
    units = (tq // SWA_BLOCK) * SWA_Q_HEADS
    return pl.pallas_call(
        _swa_kernel,
        grid=(b_, s_ // tq),
        in_specs=[pl.BlockSpec((None, tq, SWA_Q), lambda b, i: (b, i, 0)),
                  pl.BlockSpec((None, s_, SWA_KV), lambda b, i: (b, 0, 0)),
                  pl.BlockSpec((None, s_, SWA_KV), lambda b, i: (b, 0, 0)),
                  _layer_spec(bias, l)],
        out_specs=pl.BlockSpec((None, tq, SWA_Q), lambda b, i: (b, i, 0)),
        out_shape=jax.ShapeDtypeStruct((b_, s_, SWA_Q), BF16),
        scratch_shapes=[pltpu.VMEM((units, SWA_BLOCK, 2 * SWA_BLOCK), F32),
                        pltpu.VMEM((units, SWA_BLOCK, LANES), F32)],
        compiler_params=pltpu.CompilerParams(
            dimension_semantics=("arbitrary", "arbitrary"), vmem_limit_bytes=VMEM_LIMIT),
        name="swa_attention",
    )(q, k, v, bias)


def _t5_bucket(dist):
    n = jnp.maximum(dist, 0)
    max_exact = REL_BUCKETS // 2
    large = max_exact + (jnp.log(jnp.maximum(n, 1).astype(F32) / max_exact)
                         / math.log(REL_MAX_DIST / max_exact)
                         * (REL_BUCKETS - max_exact)).astype(jnp.int32)
    large = jnp.minimum(large, REL_BUCKETS - 1)
    return jnp.where(n < max_exact, n, large)


def _band_bias_tables(rel_bias, sinks):
    qi = jnp.arange(SWA_BLOCK)[:, None]
    kj = jnp.arange(2 * SWA_BLOCK)[None, :]
    dist = qi + SWA_BLOCK - kj
    onehot = (_t5_bucket(dist)[None] == jnp.arange(REL_BUCKETS)[:, None, None]).astype(F32)
    b = jnp.einsum('nh,nqk->hqk', rel_bias.astype(F32), onehot, precision=lax.Precision.HIGHEST)
    valid = (dist >= 0) & (dist < SWA_BLOCK)
    normal = jnp.where(valid[None], b, NEG)
    first = jnp.concatenate([normal[:, :, SWA_BLOCK:], jnp.full_like(normal[:, :, SWA_BLOCK:], NEG)], axis=-1)
    sink_col = jnp.array([2 * SWA_BLOCK - 1, 0])[:, None, None, None]
    sink = sinks.astype(F32)[:, None, :, None, None]
    return jnp.where(kj == sink_col, sink, jnp.stack([first, normal])[None])


def _mla_kernel(q_ref, k_ref, v_ref, *rest):
    n_w = (len(rest) - 4) // 2
    w_f32, o_ref, w_bf16 = rest[:n_w], rest[n_w], rest[n_w + 1:2 * n_w + 1]
    s_scr, m_scr, acc_scr = rest[2 * n_w + 1:]
    for src, dst in zip(w_f32, w_bf16):
        dst[...] = src[...].astype(BF16)
    p = pl.program_id(1)
    tq, tk = MLA_TQ, MLA_TK
    n = q_ref.shape[0] // tq
    row = lax.broadcasted_iota(jnp.int32, (tq, tk), 0)
    col = lax.broadcasted_iota(jnp.int32, (tq, tk), 1)
    causal = col <= row
    out_low = lax.broadcasted_iota(jnp.int32, (tq, LANES), 1) < MLA_V
    heads = [slice(hd * MLA_HEAD_PAD, (hd + 1) * MLA_HEAD_PAD) for hd in range(MLA_HEADS)]
    q_tiles = (p, n - 1 - p)
    half = tq // 2
    causal_half = causal[:half, :half]
    causal_bottom = jnp.concatenate([jnp.ones((half, half), jnp.bool_), causal_half], axis=1)

    def slot(j):
        if j >= n // 2 - 1:
            return 1, j - p
        first = j < p
        return jnp.where(first, 0, 1), jnp.where(first, j, j - p)

    def rows(start, size, align):
        return pl.ds(pl.multiple_of(start, align), size)

    def lane_max(m, s):
        return functools.reduce(jnp.maximum, [s[:, t * LANES:(t + 1) * LANES] for t in range(s.shape[1] // LANES)], m)

    def probs(s, m):
        return jnp.concatenate([jnp.exp2((s[:, t * LANES:(t + 1) * LANES] - m) * MLA_EXP2_SCALE).astype(BF16)
                                for t in range(s.shape[1] // LANES)], axis=-1)

    m_scr[...] = jnp.full(m_scr.shape, NEG, F32)
    acc_scr[...] = jnp.zeros(acc_scr.shape, F32)
    for which in range(2):
        r0 = q_tiles[which] * tq
        top, bottom, whole = rows(r0, half, tq), rows(r0 + half, half, half), rows(r0, tq, tq)
        for hd in range(MLA_HEADS):
            s_top = jnp.where(causal_half, _dot_nt(q_ref[top, heads[hd]], k_ref[top, heads[hd]]), NEG)
            s_bot = jnp.where(causal_bottom, _dot_nt(q_ref[bottom, heads[hd]], k_ref[whole, heads[hd]]), NEG)
            s_scr[hd, which, :half, :half] = s_top
            s_scr[hd, which, half:, :] = s_bot
            m_scr[which, hd, :half] = lane_max(m_scr[which, hd, :half], s_top)
            m_scr[which, hd, half:] = lane_max(m_scr[which, hd, half:], s_bot)
    for j in range(n - 1):
        which, kt = slot(j)
        qt = q_tiles[which] if isinstance(which, int) else jnp.where(which == 0, q_tiles[0], q_tiles[1])
        for hd in range(MLA_HEADS):
            s = _dot_nt(q_ref[rows(qt * tq, tq, tq), heads[hd]], k_ref[rows(kt * tk, tk, tk), heads[hd]])
            s_scr[hd, 2 + j] = s
            m_scr[which, hd] = lane_max(m_scr[which, hd], s)
    for which in range(2):
        for hd in range(MLA_HEADS):
            m_scr[which, hd] = jnp.broadcast_to(jnp.max(m_scr[which, hd], axis=-1, keepdims=True), (tq, LANES))
    for which in range(2):
        r0 = q_tiles[which] * tq
        for hd in range(MLA_HEADS):
            e_top = probs(s_scr[hd, which, :half, :half], m_scr[which, hd, :half])
            e_bot = probs(s_scr[hd, which, half:, :], m_scr[which, hd, half:])
            acc_scr[which, hd, :half] += _dot(e_top, v_ref[rows(r0, half, tq), heads[hd]])
            acc_scr[which, hd, half:] += _dot(e_bot, v_ref[rows(r0, tq, tq), heads[hd]])
    for j in range(n - 1):
        which, kt = slot(j)
        for hd in range(MLA_HEADS):
            e = probs(s_scr[hd, 2 + j], m_scr[which, hd])
            acc_scr[which, hd] += _dot(e, v_ref[rows(kt * tk, tk, tk), heads[hd]])
    for which in range(2):
        for t in range(MLA_HEADS // 2):
            lo, hi = acc_scr[which, 2 * t], acc_scr[which, 2 * t + 1]
            lo = lo / lo[:, MLA_V:MLA_V + 1]
            hi = hi / hi[:, 0:1]
            o_ref[rows(q_tiles[which] * tq, tq, tq), t * LANES:(t + 1) * LANES] = (
                jnp.where(out_low, lo, hi).astype(BF16))


def _mla(q, k, v, l, weights):
    b_, s_, w = q.shape
    n = s_ // MLA_TQ
    steps = b_ * (n // 2)
    whole = lambda width: pl.BlockSpec((None, s_, width), lambda b, p: (b, 0, 0))
    slab_in = [pl.BlockSpec((None, wt.shape[1] // steps, wt.shape[2]), lambda b, p: (l, b * (n // 2) + p, 0))
               for wt in weights]
    slab_out = [pl.BlockSpec((wt.shape[1] // steps, wt.shape[2]), lambda b, p: (b * (n // 2) + p, 0))
                for wt in weights]
    return pl.pallas_call(
        _mla_kernel,
        grid=(b_, n // 2),
        in_specs=[whole(w), whole(w), whole(w)] + slab_in,
        out_specs=[whole(MLA_HEADS * MLA_V)] + slab_out,
        out_shape=[jax.ShapeDtypeStruct((b_, s_, MLA_HEADS * MLA_V), BF16)]
        + [jax.ShapeDtypeStruct(wt.shape[1:], BF16) for wt in weights],
        scratch_shapes=[pltpu.VMEM((MLA_HEADS, n + 1, MLA_TQ, MLA_TK), F32),
                        pltpu.VMEM((2, MLA_HEADS, MLA_TQ, LANES), F32),
                        pltpu.VMEM((2, MLA_HEADS, MLA_TQ, LANES), F32)],
        compiler_params=pltpu.CompilerParams(
            dimension_semantics=("arbitrary", "arbitrary"), vmem_limit_bytes=VMEM_LIMIT),
        name="mla_attention",
    )(q, k, v, *weights)


def _gelu_tanh(x):
    return 0.5 * x * (1.0 + jnp.tanh(math.sqrt(2.0 / math.pi) * (x + 0.044715 * (x * x * x))))


def _ssm_kernel(u_ref, bd_ref, ar_ref, ai_ref, cd_ref, d_ref, wglu_ref, y_ref,
                u_scr, x_scr, y_scr, sr_scr, si_scr, *, layer):
    n = SSM_STATES
    batch = sr_scr.shape[0]
    lane_tiles = SSM_CH // LANES

    @pl.when(pl.program_id(0) == 0)
    def _():
        sr_scr[...] = jnp.zeros_like(sr_scr)
        si_scr[...] = jnp.zeros_like(si_scr)

    for b in range(batch):
        for lt in range(lane_tiles):
            c0 = b * SSM_CH + lt * LANES
            u_scr[lt, pl.ds(b, SSM_TS, stride=batch), :] = u_ref[:, c0:c0 + LANES]
    sub_t = SSM_TS // SSM_SUB
    sub_r = sub_t * batch
    subs = [slice(q * sub_r, (q + 1) * sub_r) for q in range(SSM_SUB)]
    us = [jnp.concatenate([u_scr[lt, rows, :] for lt in range(lane_tiles)], axis=-1) for rows in subs]
    for rows, u in zip(subs, us):
        u_bf = u.astype(BF16)
        x_scr[rows, :n] = _dot(u_bf, bd_ref[:, :n])
        x_scr[rows, n:] = _dot(u_bf, bd_ref[:, n:])
    ar = ar_ref[...]
    ai = ai_ref[...]
    xr, xi = sr_scr[...], si_scr[...]
    for rows, u in zip(subs, us):
        for t in range(sub_t):
            r = slice(rows.start + t * batch, rows.start + (t + 1) * batch)
            xr, xi = (ar * xr - ai * xi + x_scr[r, :n], ar * xi + ai * xr + x_scr[r, n:])
            x_scr[r, :n] = xr
            x_scr[r, n:] = xi
        mid = rows.start + sub_r // 2
        cx = jnp.concatenate([_dot(x_scr[rows.start:mid, :].astype(BF16), cd_ref[...]),
                              _dot(x_scr[mid:rows.stop, :].astype(BF16), cd_ref[...])], axis=0)
        y = cx + d_ref[layer:layer + 1, :] * u
        y = _gelu_tanh(y)
        y = y * jax.nn.sigmoid(_dot(y.astype(BF16), wglu_ref[...]))
        for lt in range(lane_tiles):
            y_scr[lt, rows, :] = y[:, lt * LANES:(lt + 1) * LANES]
    sr_scr[...] = xr
    si_scr[...] = xi
    for b in range(batch):
        for lt in range(lane_tiles):
            c0 = b * SSM_CH + lt * LANES
            y_ref[:, c0:c0 + LANES] = y_scr[lt, pl.ds(b, SSM_TS, stride=batch), :].astype(BF16)


def _ssm(u, batch, l, bd, ar, ai, cd, d, wglu):
    s_, width = u.shape
    tr = SSM_TS * batch
    return pl.pallas_call(
        functools.partial(_ssm_kernel, layer=l),
        grid=(s_ // SSM_TS,),
        in_specs=[pl.BlockSpec((SSM_TS, width), lambda i: (i, 0))]
        + [_layer_spec(a, l) for a in (bd, ar, ai, cd)] + [_table_spec(d), _layer_spec(wglu, l)],
        out_specs=pl.BlockSpec((SSM_TS, width), lambda i: (i, 0)),
        out_shape=jax.ShapeDtypeStruct((s_, width), BF16),
        scratch_shapes=[pltpu.VMEM((SSM_CH // LANES, tr, LANES), F32),
                        pltpu.VMEM((tr, 2 * SSM_STATES), F32),
                        pltpu.VMEM((SSM_CH // LANES, tr, LANES), F32),
                        pltpu.VMEM((batch, SSM_STATES), F32),
                        pltpu.VMEM((batch, SSM_STATES), F32)],
        compiler_params=pltpu.CompilerParams(
            dimension_semantics=("arbitrary",), vmem_limit_bytes=VMEM_LIMIT),
        name="s5_mixer",
    )(u, bd, ar, ai, cd, d, wglu)


def _ssm_params(a_re, a_im, log_dt, b_re, b_im, c_re, c_im, batch):
    depth = a_re.shape[0]
    lam = lax.complex(a_re.astype(F32), a_im.astype(F32))
    dt = jnp.exp(log_dt.astype(F32))
    a_bar = jnp.exp(lam * dt[..., None])
    b_bar = ((a_bar - 1.0) / lam)[..., None] * lax.complex(b_re.astype(F32), b_im.astype(F32))
    same_group = (np.arange(SSM_CH)[:, None] // SSM_GROUP) == (np.arange(SSM_STATES)[None, :] // SSM_STATE)

    def in_op(m):
        rows = jnp.swapaxes(m, -1, -2).reshape(depth, SSM_CH, SSM_STATE)
        return jnp.where(same_group, jnp.tile(rows, (1, 1, SSM_GROUPS)), 0.0)

    def out_op(m):
        cols = jnp.swapaxes(m.reshape(depth, SSM_CH, SSM_STATE), -1, -2)
        return jnp.where(same_group.T, jnp.tile(cols, (1, SSM_GROUPS, 1)), 0.0)

    bd = jnp.concatenate([in_op(jnp.real(b_bar)), in_op(jnp.imag(b_bar))], axis=-1).astype(BF16)
    cd = jnp.concatenate([out_op(c_re.astype(F32)), -out_op(c_im.astype(F32))], axis=-2).astype(BF16)
    state_rows = (depth, batch, SSM_STATES)
    ar = jnp.broadcast_to(jnp.real(a_bar).reshape(depth, 1, SSM_STATES), state_rows)
    ai = jnp.broadcast_to(jnp.imag(a_bar).reshape(depth, 1, SSM_STATES), state_rows)
    return bd, ar, ai, cd


def _ffn_kernel(x_ref, oa_ref, ob_ref, oc_ref, wo_ref, g2_ref, wg_ref, wu_ref, wd_ref, gf_ref, o_ref, h_scr,
                *, layer, final_norm):
    wo_a = wo_ref[:SWA_Q, :]
    wo_b = wo_ref[SWA_Q:SWA_Q + SSM_CH, :]
    wo_c = wo_ref[SWA_Q + SSM_CH:, :]
    x = x_ref[...] + _dot(oa_ref[...], wo_a) + _dot(ob_ref[...], wo_b) + _dot(oc_ref[...], wo_c)
    o_ref[...] = x
    h_scr[...] = _rms(x, g2_ref[layer:layer + 1, :]).astype(BF16)

    h = h_scr[...]
    gate = _dot(h, wg_ref[...])
    up = _dot(h, wu_ref[...])
    act = (gate * jax.nn.sigmoid(gate) * up).astype(BF16)
    o_ref[...] += _dot(act, wd_ref[...])
    if final_norm:
        o_ref[...] = _rms(o_ref[...], gf_ref[...])


def _ffn(x, oa, ob, oc, l, wo, g2, wg, wu, wd, gf, final_norm):
    b_, s_, _ = x.shape
    tm = FFN_TM
    row3 = lambda b, i: (b, i, 0)
    whole = lambda wt: pl.BlockSpec(wt.shape, lambda b, i: (0, 0))
    return pl.pallas_call(
        functools.partial(_ffn_kernel, layer=l, final_norm=final_norm),
        grid=(b_, s_ // tm),
        in_specs=[pl.BlockSpec((None, tm, D_MODEL), row3),
                  pl.BlockSpec((None, tm, SWA_Q), row3),
                  pl.BlockSpec((tm, SSM_CH), lambda b, i: (i, b)),
                  pl.BlockSpec((None, tm, MLA_HEADS * MLA_V), row3),
                  whole(wo), _table_spec(g2), whole(wg), whole(wu), whole(wd),
                  pl.BlockSpec((1, D_MODEL), lambda b, i: (0, 0))],
        out_specs=pl.BlockSpec((None, tm, D_MODEL), row3),
        out_shape=jax.ShapeDtypeStruct(x.shape, F32),
        scratch_shapes=[pltpu.VMEM((tm, D_MODEL), BF16)],
        compiler_params=pltpu.CompilerParams(
            dimension_semantics=("arbitrary", "arbitrary"), vmem_limit_bytes=VMEM_LIMIT),
        name="out_proj_ffn",
    )(x, oa, ob, oc, wo, g2, wg, wu, wd, gf)


def _rot_half_cols(w):
    half = w.shape[-1] // 2
    return jnp.concatenate([-w[..., half:], w[..., :half]], axis=-1)


def _rope_tile(w):
    z = jnp.zeros(w.shape[:-1] + (MLA_NOPE,), w.dtype)
    z2 = jnp.zeros(w.shape[:-1] + (MLA_HEAD_PAD - MLA_NOPE - MLA_ROPE,), w.dtype)
    return jnp.concatenate([z, w, z2], axis=-1)


def _rope_key_placement():
    half = MLA_ROPE // 2
    j = np.arange(MLA_ROPE)
    place = np.zeros((MLA_ROPE, 2 * LANES), np.float32)
    place[j, MLA_NOPE + j] = 1.0
    place[j, LANES + MLA_NOPE + (j + half) % MLA_ROPE] = np.where(j < half, 1.0, -1.0)
    return jnp.asarray(place, BF16)


def _stacked_weights(w_q_up, w_kv_up):
    lead = w_q_up.shape[:-2]
    qh = w_q_up.reshape(lead + (MLA_Q_RANK, MLA_HEADS, MLA_NOPE + MLA_ROPE))
    pad = jnp.zeros(lead + (MLA_Q_RANK, MLA_HEADS, MLA_HEAD_PAD - MLA_NOPE - MLA_ROPE), w_q_up.dtype)
    q_tiles = jnp.concatenate([qh, pad], axis=-1).reshape(lead + (MLA_Q_RANK, -1))
    q_rot = _rope_tile(_rot_half_cols(qh[..., MLA_NOPE:])).reshape(lead + (MLA_Q_RANK, -1))
    wq_ext = jnp.concatenate([q_tiles, q_rot], axis=-1)
    kvh = w_kv_up.reshape(lead + (MLA_KV_RANK, MLA_HEADS, MLA_NOPE + MLA_V))
    kpad = jnp.zeros(lead + (MLA_KV_RANK, MLA_HEADS, MLA_HEAD_PAD - MLA_NOPE), w_kv_up.dtype)
    k_tiles = jnp.concatenate([kvh[..., :MLA_NOPE], kpad], axis=-1).reshape(lead + (MLA_KV_RANK, -1))
    v_cols = kvh[..., MLA_NOPE:].reshape(lead + (MLA_KV_RANK, -1))
    wkv_ext = jnp.concatenate([k_tiles, v_cols], axis=-1)
    return wq_ext.astype(BF16), wkv_ext.astype(BF16)


def kernel(x, positions, rel_bias, ln1_g, w_in, sinks, ssm_a_re, ssm_a_im, ssm_log_dt, ssm_b_re, ssm_b_im,
           ssm_c_re, ssm_c_im, ssm_d, ssm_w_glu, mla_q_norm_g, mla_w_q_up, mla_kv_norm_g, mla_w_kv_up, w_out,
           ln2_g, w_gate, w_up, w_down, final_g):
    b_ = x.shape[0]
    cos_t, sin_t = _trig_tables(positions)
    bias = _band_bias_tables(rel_bias, sinks)
    wq_ext, wkv_ext = _stacked_weights(mla_w_q_up, mla_w_kv_up)
    w_in_bf = w_in.astype(BF16)
    bd, ar, ai, cd = _ssm_params(ssm_a_re, ssm_a_im, ssm_log_dt, ssm_b_re, ssm_b_im, ssm_c_re, ssm_c_im, b_)
    wglu = ssm_w_glu.astype(BF16)
    gf = final_g.reshape(1, D_MODEL)
    for l in range(DEPTH):
        qa, ka, va, u, qm, km, vm = _in_proj(x, l, ln1_g, w_in_bf, cos_t, sin_t,
                                             mla_q_norm_g, wq_ext, mla_kv_norm_g, wkv_ext)
        o_a = _swa(qa, ka, va, l, bias)
        o_b = _ssm(u, b_, l, bd, ar, ai, cd, ssm_d, wglu)
        o_c, wo, wg, wu, wd = _mla(qm, km, vm, l, (w_out, w_gate, w_up, w_down))
        x = _ffn(x, o_a, o_b, o_c, l, wo, ln2_g, wg, wu, wd, gf, final_norm=(l == DEPTH - 1))
    return x
```

```python
import functools
import math

import jax
import jax.numpy as jnp
import numpy as np
from jax import lax
from jax.experimental import pallas as pl
from jax.experimental.pallas import tpu as pltpu

D_MODEL = 1024
DEPTH = 2
HEAD_DIM = 64
SWA_Q_HEADS = 8
SWA_KV_HEADS = 2
SWA_BLOCK = 128
SWA_Q = SWA_Q_HEADS * HEAD_DIM
SWA_KV = SWA_KV_HEADS * HEAD_DIM
REL_BUCKETS = 32
REL_MAX_DIST = 128
SSM_CH = 256
SSM_GROUP = 16
SSM_GROUPS = SSM_CH // SSM_GROUP
SSM_STATE = 64
SSM_STATES = SSM_GROUPS * SSM_STATE
MLA_HEADS = 4
MLA_Q_RANK = 256
MLA_KV_RANK = 128
MLA_NOPE = 64
MLA_ROPE = 32
MLA_V = 64
ROPE_THETA = 10000.0
EPS = 1e-6
NEG = -1e30

LANES = 128
V7X_VMEM_BYTES = 64 * 1024 * 1024
VMEM_LIMIT = V7X_VMEM_BYTES * 7 // 8
MLA_HEAD_PAD = LANES
MLA_EXP2_SCALE = (MLA_NOPE + MLA_ROPE) ** -0.5 * math.log2(math.e)

IN_PROJ_TM = 1024
SWA_TQ = 512
MLA_TQ = 512
MLA_TK = MLA_TQ
SSM_TS = 128
SSM_SUB = 2
FFN_TM = 1024

BF16 = jnp.bfloat16
F32 = jnp.float32


def _rms(x, g):
    return x * lax.rsqrt(jnp.mean(x * x, axis=-1, keepdims=True) + EPS) * g


def _dot(a, b):
    return jnp.dot(a, b, preferred_element_type=F32)


def _dot_nt(a, b):
    return lax.dot_general(a, b, (((1,), (1,)), ((), ())), preferred_element_type=F32)


def _bf16_pieces(t):
    hi = t.astype(BF16)
    rest = t - hi.astype(F32)
    mid = rest.astype(BF16)
    return hi, mid, (rest - mid.astype(F32)).astype(BF16)


def _table_spec(arr):
    return pl.BlockSpec(arr.shape, lambda *_: (0,) * arr.ndim)


def _layer_spec(arr, l):
    return pl.BlockSpec((None,) + arr.shape[1:], lambda *_: (l,) + (0,) * (arr.ndim - 1))


def _trig_kernel(pos_ref, freq_ref, place_ref, one_ref, cos_ref, sin_ref):
    pos = pos_ref[pl.ds(pl.program_id(0), 1), :].astype(F32)
    ang = freq_ref[...] * pos

    def place(t):
        return sum(lax.dot_general(piece, place_ref[...], (((0,), (0,)), ((), ())), preferred_element_type=F32)
                   for piece in _bf16_pieces(t))

    cos_ref[...] = place(jnp.cos(ang)) + one_ref[...]
    sin_ref[...] = place(jnp.sin(ang))


def _trig_tables(positions):
    b_, s_ = positions.shape
    half = MLA_ROPE // 2
    inv_freq = jnp.power(ROPE_THETA, -jnp.arange(half, dtype=F32) * 2.0 / MLA_ROPE).reshape(half, 1)
    lane = np.arange(MLA_HEAD_PAD)[None, :]
    rope_lane = (lane >= MLA_NOPE) & (lane < MLA_NOPE + MLA_ROPE)
    placement = jnp.asarray(rope_lane & ((lane - MLA_NOPE) % half == np.arange(half)[:, None]), BF16)
    off_rope = jnp.asarray(~rope_lane, F32)
    const = lambda b: (0, 0)
    cos_t, sin_t = pl.pallas_call(
        _trig_kernel,
        grid=(b_,),
        in_specs=[pl.BlockSpec((b_, s_), const),
                  pl.BlockSpec((half, 1), const),
                  pl.BlockSpec((half, MLA_HEAD_PAD), const),
                  pl.BlockSpec((1, MLA_HEAD_PAD), const)],
        out_specs=[pl.BlockSpec((None, s_, MLA_HEAD_PAD), lambda b: (b, 0, 0))] * 2,
        out_shape=[jax.ShapeDtypeStruct((b_, s_, MLA_HEAD_PAD), F32)] * 2,
        name="rope_tables",
    )(positions, inv_freq, placement, off_rope)
    return cos_t.reshape(b_ * s_, MLA_HEAD_PAD), sin_t.reshape(b_ * s_, MLA_HEAD_PAD)


def _in_proj_kernel(x_ref, g_ref, w_ref, kplace_ref, cos_ref, sin_ref, qg_ref, wq_ref, kvg_ref, wkv_ref,
                    qa_ref, ka_ref, va_ref, u_ref, qm_ref, km_ref, vm_ref, *, layer):
    row = slice(layer, layer + 1)
    h = _rms(x_ref[...], g_ref[row, :]).astype(BF16)
    proj = _dot(h, w_ref[...])
    qa_ref[...] = (proj[:, :SWA_Q] * (HEAD_DIM ** -0.5)).astype(BF16)
    c = SWA_Q
    ka_ref[...] = proj[:, c:c + SWA_KV].astype(BF16)
    c += SWA_KV
    va_ref[...] = proj[:, c:c + SWA_KV].astype(BF16)
    c += SWA_KV
    u_ref[...] = proj[:, c:c + SSM_CH]
    c += SSM_CH
    c_q = proj[:, c:c + MLA_Q_RANK]
    c += MLA_Q_RANK
    c_kv = proj[:, c:c + MLA_KV_RANK]
    c += MLA_KV_RANK
    k_r2 = sum(_dot(piece, kplace_ref[...]) for piece in _bf16_pieces(proj[:, c:c + MLA_ROPE]))
    k_r, k_r_rot = k_r2[:, :LANES], k_r2[:, LANES:]
    cos = cos_ref[...]
    sin = sin_ref[...]
    width = MLA_HEADS * MLA_HEAD_PAD
    q2 = _dot(_rms(c_q, qg_ref[row, :]).astype(BF16), wq_ref[...])
    kv2 = _dot(_rms(c_kv, kvg_ref[row, :]).astype(BF16), wkv_ref[...])
    k_pe = k_r * cos + k_r_rot * sin
    low = lax.broadcasted_iota(jnp.int32, (cos.shape[0], LANES), 1) < MLA_V
    for hd in range(MLA_HEADS):
        sl = slice(hd * MLA_HEAD_PAD, (hd + 1) * MLA_HEAD_PAD)
        sl_rot = slice(width + hd * MLA_HEAD_PAD, width + (hd + 1) * MLA_HEAD_PAD)
        qm_ref[:, sl] = (q2[:, sl] * cos + q2[:, sl_rot] * sin).astype(BF16)
        km_ref[:, sl] = (kv2[:, sl] + k_pe).astype(BF16)
        pair = kv2[:, width + (hd // 2) * LANES:width + (hd // 2 + 1) * LANES]
        keep = low if hd % 2 == 0 else jnp.logical_not(low)
        vm_ref[:, sl] = jnp.where(keep, pair, 1.0).astype(BF16)


def _in_proj(x, l, g, w_in, cos_t, sin_t, qg, wq_ext, kvg, wkv_ext):
    b_, s_, _ = x.shape
    tm = IN_PROJ_TM
    nt = s_ // tm
    row3 = lambda b, i: (b, i, 0)
    trig = pl.BlockSpec((tm, MLA_HEAD_PAD), lambda b, i: (b * nt + i, 0))
    mla_w = MLA_HEADS * MLA_HEAD_PAD
    kplace = _rope_key_placement()
    return pl.pallas_call(
        functools.partial(_in_proj_kernel, layer=l),
        grid=(b_, nt),
        in_specs=[pl.BlockSpec((None, tm, D_MODEL), row3),
                  _table_spec(g), _layer_spec(w_in, l),
                  pl.BlockSpec(kplace.shape, lambda b, i: (0, 0)),
                  trig, trig,
                  _table_spec(qg), _layer_spec(wq_ext, l), _table_spec(kvg), _layer_spec(wkv_ext, l)],
        out_specs=[pl.BlockSpec((None, tm, SWA_Q), row3),
                   pl.BlockSpec((None, tm, SWA_KV), row3),
                   pl.BlockSpec((None, tm, SWA_KV), row3),
                   pl.BlockSpec((tm, SSM_CH), lambda b, i: (i, b)),
                   pl.BlockSpec((None, tm, mla_w), row3),
                   pl.BlockSpec((None, tm, mla_w), row3),
                   pl.BlockSpec((None, tm, mla_w), row3)],
        out_shape=[jax.ShapeDtypeStruct((b_, s_, SWA_Q), BF16),
                   jax.ShapeDtypeStruct((b_, s_, SWA_KV), BF16),
                   jax.ShapeDtypeStruct((b_, s_, SWA_KV), BF16),
                   jax.ShapeDtypeStruct((s_, b_ * SSM_CH), F32),
                   jax.ShapeDtypeStruct((b_, s_, mla_w), BF16),
                   jax.ShapeDtypeStruct((b_, s_, mla_w), BF16),
                   jax.ShapeDtypeStruct((b_, s_, mla_w), BF16)],
        compiler_params=pltpu.CompilerParams(
            dimension_semantics=("arbitrary", "arbitrary"), vmem_limit_bytes=VMEM_LIMIT),
        name="in_proj",
    )(x, g, w_in, kplace, cos_t, sin_t, qg, wq_ext, kvg, wkv_ext)


def _swa_kernel(q_ref, k_ref, v_ref, bias_ref, o_ref, s_scr, m_scr):
    qi = pl.program_id(1)
    blocks = SWA_TQ // SWA_BLOCK
    tiles = SWA_Q // LANES
    band = 2 * SWA_BLOCK
    lane = lax.broadcasted_iota(jnp.int32, (band, LANES), 1)
    key = lax.broadcasted_iota(jnp.int32, (band, LANES), 0)
    low = lane < HEAD_DIM
    out_low = lax.broadcasted_iota(jnp.int32, (SWA_BLOCK, LANES), 1) < HEAD_DIM

    def band_of(ref, j):
        blk = qi * blocks + j
        start = pl.multiple_of(jnp.maximum(blk - 1, 0) * SWA_BLOCK, SWA_BLOCK)
        keep = key != jnp.where(blk == 0, band - 1, 0)
        t = ref[pl.ds(start, band), :]
        t = jnp.where(keep, t, jnp.zeros_like(t))
        return t, pltpu.roll(t, HEAD_DIM, 1)

    for j in range(blocks):
        table = jnp.minimum(qi * blocks + j, 1)
        kb, kb_sw = band_of(k_ref, j)
        zero = jnp.zeros_like(kb)
        k_var = ((jnp.where(low, kb, zero), jnp.where(low, zero, kb_sw)),
                 (jnp.where(low, kb_sw, zero), jnp.where(low, zero, kb)))
        rows = slice(j * SWA_BLOCK, (j + 1) * SWA_BLOCK)
        for i in range(tiles):
            kvh = (2 * i) // (SWA_Q_HEADS // SWA_KV_HEADS)
            q = q_ref[rows, i * LANES:(i + 1) * LANES]
            for half in range(2):
                u = (j * tiles + i) * 2 + half
                s = _dot_nt(q, k_var[kvh][half]) + bias_ref[table, 2 * i + half]
                s_scr[u] = s
                m_scr[u] = jnp.broadcast_to(jnp.max(s, axis=-1, keepdims=True), (SWA_BLOCK, LANES))
    for j in range(blocks):
        vb, vb_sw = band_of(v_ref, j)
        one = jnp.ones_like(vb)
        v_var = ((jnp.where(low, vb, one), jnp.where(low, one, vb_sw)),
                 (jnp.where(low, vb_sw, one), jnp.where(low, one, vb)))
        rows = slice(j * SWA_BLOCK, (j + 1) * SWA_BLOCK)
        for i in range(tiles):
            kvh = (2 * i) // (SWA_Q_HEADS // SWA_KV_HEADS)
            r = []
            for half in range(2):
                u = (j * tiles + i) * 2 + half
                m = m_scr[u]
                p = jnp.concatenate([jnp.exp(s_scr[u, :, t * LANES:(t + 1) * LANES] - m).astype(BF16)
                                     for t in range(band // LANES)], axis=-1)
                r.append(_dot(p, v_var[kvh][half]))
            denom = pltpu.roll(jnp.where(out_low, r[1], r[0]), HEAD_DIM, 1)
            o_ref[rows, i * LANES:(i + 1) * LANES] = (jnp.where(out_low, r[0], r[1]) / denom).astype(BF16)


def _swa(q, k, v, l, bias):
    b_, s_, _ = q.shape
    tq = SWA_TQ
    units = (tq // SWA_BLOCK) * SWA_Q_HEADS
    return pl.pallas_call(
        _swa_kernel,
        grid=(b_, s_ // tq),
        in_specs=[pl.BlockSpec((None, tq, SWA_Q), lambda b, i: (b, i, 0)),
                  pl.BlockSpec((None, s_, SWA_KV), lambda b, i: (b, 0, 0)),
                  pl.BlockSpec((None, s_, SWA_KV), lambda b, i: (b, 0, 0)),
                  _layer_spec(bias, l)],
        out_specs=pl.BlockSpec((None, tq, SWA_Q), lambda b, i: (b, i, 0)),
        out_shape=jax.ShapeDtypeStruct((b_, s_, SWA_Q), BF16),
        scratch_shapes=[pltpu.VMEM((units, SWA_BLOCK, 2 * SWA_BLOCK), F32),
                        pltpu.VMEM((units, SWA_BLOCK, LANES), F32)],
        compiler_params=pltpu.CompilerParams(
            dimension_semantics=("arbitrary", "arbitrary"), vmem_limit_bytes=VMEM_LIMIT),
        name="swa_attention",
    )(q, k, v, bias)


def _t5_bucket(dist):
    n = jnp.maximum(dist, 0)
    max_exact = REL_BUCKETS // 2
    large = max_exact + (jnp.log(jnp.maximum(n, 1).astype(F32) / max_exact)
                         / math.log(REL_MAX_DIST / max_exact)
                         * (REL_BUCKETS - max_exact)).astype(jnp.int32)
    large = jnp.minimum(large, REL_BUCKETS - 1)
    return jnp.where(n < max_exact, n, large)


def _band_bias_tables(rel_bias, sinks):
    qi = jnp.arange(SWA_BLOCK)[:, None]
    kj = jnp.arange(2 * SWA_BLOCK)[None, :]
    dist = qi + SWA_BLOCK - kj
    onehot = (_t5_bucket(dist)[None] == jnp.arange(REL_BUCKETS)[:, None, None]).astype(F32)
    b = jnp.einsum('nh,nqk->hqk', rel_bias.astype(F32), onehot, precision=lax.Precision.HIGHEST)
    valid = (dist >= 0) & (dist < SWA_BLOCK)
    normal = jnp.where(valid[None], b, NEG)
    first = jnp.concatenate([normal[:, :, SWA_BLOCK:], jnp.full_like(normal[:, :, SWA_BLOCK:], NEG)], axis=-1)
    sink_col = jnp.array([2 * SWA_BLOCK - 1, 0])[:, None, None, None]
    sink = sinks.astype(F32)[:, None, :, None, None]
    return jnp.where(kj == sink_col, sink, jnp.stack([first, normal])[None])


def _mla_kernel(q_ref, k_ref, v_ref, *rest):
    n_w = (len(rest) - 4) // 2
    w_f32, o_ref, w_bf16 = rest[:n_w], rest[n_w], rest[n_w + 1:2 * n_w + 1]
    s_scr, m_scr, acc_scr = rest[2 * n_w + 1:]
    for src, dst in zip(w_f32, w_bf16):
        dst[...] = src[...].astype(BF16)
    p = pl.program_id(1)
    tq, tk = MLA_TQ, MLA_TK
    n = q_ref.shape[0] // tq
    row = lax.broadcasted_iota(jnp.int32, (tq, tk), 0)
    col = lax.broadcasted_iota(jnp.int32, (tq, tk), 1)
    causal = col <= row
    out_low = lax.broadcasted_iota(jnp.int32, (tq, LANES), 1) < MLA_V
    heads = [slice(hd * MLA_HEAD_PAD, (hd + 1) * MLA_HEAD_PAD) for hd in range(MLA_HEADS)]
    q_tiles = (p, n - 1 - p)
    half = tq // 2
    causal_half = causal[:half, :half]
    causal_bottom = jnp.concatenate([jnp.ones((half, half), jnp.bool_), causal_half], axis=1)

    def slot(j):
        if j >= n // 2 - 1:
            return 1, j - p
        first = j < p
        return jnp.where(first, 0, 1), jnp.where(first, j, j - p)

    def rows(start, size, align):
        return pl.ds(pl.multiple_of(start, align), size)

    def lane_max(m, s):
        return functools.reduce(jnp.maximum, [s[:, t * LANES:(t + 1) * LANES] for t in range(s.shape[1] // LANES)], m)

    def probs(s, m):
        return jnp.concatenate([jnp.exp2((s[:, t * LANES:(t + 1) * LANES] - m) * MLA_EXP2_SCALE).astype(BF16)
                                for t in range(s.shape[1] // LANES)], axis=-1)

    m_scr[...] = jnp.full(m_scr.shape, NEG, F32)
    acc_scr[...] = jnp.zeros(acc_scr.shape, F32)
    for which in range(2):
        r0 = q_tiles[which] * tq
        top, bottom, whole = rows(r0, half, tq), rows(r0 + half, half, half), rows(r0, tq, tq)
        for hd in range(MLA_HEADS):
            s_top = jnp.where(causal_half, _dot_nt(q_ref[top, heads[hd]], k_ref[top, heads[hd]]), NEG)
            s_bot = jnp.where(causal_bottom, _dot_nt(q_ref[bottom, heads[hd]], k_ref[whole, heads[hd]]), NEG)
            s_scr[hd, which, :half, :half] = s_top
            s_scr[hd, which, half:, :] = s_bot
            m_scr[which, hd, :half] = lane_max(m_scr[which, hd, :half], s_top)
            m_scr[which, hd, half:] = lane_max(m_scr[which, hd, half:], s_bot)
    for j in range(n - 1):
        which, kt = slot(j)
        qt = q_tiles[which] if isinstance(which, int) else jnp.where(which == 0, q_tiles[0], q_tiles[1])
        for hd in range(MLA_HEADS):
            s = _dot_nt(q_ref[rows(qt * tq, tq, tq), heads[hd]], k_ref[rows(kt * tk, tk, tk), heads[hd]])
            s_scr[hd, 2 + j] = s
            m_scr[which, hd] = lane_max(m_scr[which, hd], s)
    for which in range(2):
        for hd in range(MLA_HEADS):
            m_scr[which, hd] = jnp.broadcast_to(jnp.max(m_scr[which, hd], axis=-1, keepdims=True), (tq, LANES))
    for which in range(2):
        r0 = q_tiles[which] * tq
        for hd in range(MLA_HEADS):
            e_top = probs(s_scr[hd, which, :half, :half], m_scr[which, hd, :half])
            e_bot = probs(s_scr[hd, which, half:, :], m_scr[which, hd, half:])
            acc_scr[which, hd, :half] += _dot(e_top, v_ref[rows(r0, half, tq), heads[hd]])
            acc_scr[which, hd, half:] += _dot(e_bot, v_ref[rows(r0, tq, tq), heads[hd]])
    for j in range(n - 1):
        which, kt = slot(j)
        for hd in range(MLA_HEADS):
            e = probs(s_scr[hd, 2 + j], m_scr[which, hd])
            acc_scr[which, hd] += _dot(e, v_ref[rows(kt * tk, tk, tk), heads[hd]])
    for which in range(2):
        for t in range(MLA_HEADS // 2):
            lo, hi = acc_scr[which, 2 * t], acc_scr[which, 2 * t + 1]
            lo = lo / lo[:, MLA_V:MLA_V + 1]
            hi = hi / hi[:, 0:1]
            o_ref[rows(q_tiles[which] * tq, tq, tq), t * LANES:(t + 1) * LANES] = (
                jnp.where(out_low, lo, hi).astype(BF16))


def _mla(q, k, v, l, weights):
    b_, s_, w = q.shape
    n = s_ // MLA_TQ
    steps = b_ * (n // 2)
    whole = lambda width: pl.BlockSpec((None, s_, width), lambda b, p: (b, 0, 0))
    slab_in = [pl.BlockSpec((None, wt.shape[1] // steps, wt.shape[2]), lambda b, p: (l, b * (n // 2) + p, 0))
               for wt in weights]
    slab_out = [pl.BlockSpec((wt.shape[1] // steps, wt.shape[2]), lambda b, p: (b * (n // 2) + p, 0))
                for wt in weights]
    return pl.pallas_call(
        _mla_kernel,
        grid=(b_, n // 2),
        in_specs=[whole(w), whole(w), whole(w)] + slab_in,
        out_specs=[whole(MLA_HEADS * MLA_V)] + slab_out,
        out_shape=[jax.ShapeDtypeStruct((b_, s_, MLA_HEADS * MLA_V), BF16)]
        + [jax.ShapeDtypeStruct(wt.shape[1:], BF16) for wt in weights],
        scratch_shapes=[pltpu.VMEM((MLA_HEADS, n + 1, MLA_TQ, MLA_TK), F32),
                        pltpu.VMEM((2, MLA_HEADS, MLA_TQ, LANES), F32),
                        pltpu.VMEM((2, MLA_HEADS, MLA_TQ, LANES), F32)],
        compiler_params=pltpu.CompilerParams(
            dimension_semantics=("arbitrary", "arbitrary"), vmem_limit_bytes=VMEM_LIMIT),
        name="mla_attention",
    )(q, k, v, *weights)


def _gelu_tanh(x):
    return 0.5 * x * (1.0 + jnp.tanh(math.sqrt(2.0 / math.pi) * (x + 0.044715 * (x * x * x))))


def _ssm_kernel(u_ref, bd_ref, ar_ref, ai_ref, cd_ref, d_ref, wglu_ref, y_ref,
                u_scr, x_scr, y_scr, sr_scr, si_scr, *, layer):
    n = SSM_STATES
    batch = sr_scr.shape[0]
    lane_tiles = SSM_CH // LANES

    @pl.when(pl.program_id(0) == 0)
    def _():
        sr_scr[...] = jnp.zeros_like(sr_scr)
        si_scr[...] = jnp.zeros_like(si_scr)

    for b in range(batch):
        for lt in range(lane_tiles):
            c0 = b * SSM_CH + lt * LANES
            u_scr[lt, pl.ds(b, SSM_TS, stride=batch), :] = u_ref[:, c0:c0 + LANES]
    sub_t = SSM_TS // SSM_SUB
    sub_r = sub_t * batch
    subs = [slice(q * sub_r, (q + 1) * sub_r) for q in range(SSM_SUB)]
    us = [jnp.concatenate([u_scr[lt, rows, :] for lt in range(lane_tiles)], axis=-1) for rows in subs]
    for rows, u in zip(subs, us):
        u_bf = u.astype(BF16)
        x_scr[rows, :n] = _dot(u_bf, bd_ref[:, :n])
        x_scr[rows, n:] = _dot(u_bf, bd_ref[:, n:])
    ar = ar_ref[...]
    ai = ai_ref[...]
    xr, xi = sr_scr[...], si_scr[...]
    for rows, u in zip(subs, us):
        for t in range(sub_t):
            r = slice(rows.start + t * batch, rows.start + (t + 1) * batch)
            xr, xi = (ar * xr - ai * xi + x_scr[r, :n], ar * xi + ai * xr + x_scr[r, n:])
            x_scr[r, :n] = xr
            x_scr[r, n:] = xi
        mid = rows.start + sub_r // 2
        cx = jnp.concatenate([_dot(x_scr[rows.start:mid, :].astype(BF16), cd_ref[...]),
                              _dot(x_scr[mid:rows.stop, :].astype(BF16), cd_ref[...])], axis=0)
        y = cx + d_ref[layer:layer + 1, :] * u
        y = _gelu_tanh(y)
        y = y * jax.nn.sigmoid(_dot(y.astype(BF16), wglu_ref[...]))
        for lt in range(lane_tiles):
            y_scr[lt, rows, :] = y[:, lt * LANES:(lt + 1) * LANES]
    sr_scr[...] = xr
    si_scr[...] = xi
    for b in range(batch):
        for lt in range(lane_tiles):
            c0 = b * SSM_CH + lt * LANES
            y_ref[:, c0:c0 + LANES] = y_scr[lt, pl.ds(b, SSM_TS, stride=batch), :].astype(BF16)


def _ssm(u, batch, l, bd, ar, ai, cd, d, wglu):
    s_, width = u.shape
    tr = SSM_TS * batch
    return pl.pallas_call(
        functools.partial(_ssm_kernel, layer=l),
        grid=(s_ // SSM_TS,),
        in_specs=[pl.BlockSpec((SSM_TS, width), lambda i: (i, 0))]
        + [_layer_spec(a, l) for a in (bd, ar, ai, cd)] + [_table_spec(d), _layer_spec(wglu, l)],
        out_specs=pl.BlockSpec((SSM_TS, width), lambda i: (i, 0)),
        out_shape=jax.ShapeDtypeStruct((s_, width), BF16),
        scratch_shapes=[pltpu.VMEM((SSM_CH // LANES, tr, LANES), F32),
                        pltpu.VMEM((tr, 2 * SSM_STATES), F32),
                        pltpu.VMEM((SSM_CH // LANES, tr, LANES), F32),
                        pltpu.VMEM((batch, SSM_STATES), F32),
                        pltpu.VMEM((batch, SSM_STATES), F32)],
        compiler_params=pltpu.CompilerParams(
            dimension_semantics=("arbitrary",), vmem_limit_bytes=VMEM_LIMIT),
        name="s5_mixer",
    )(u, bd, ar, ai, cd, d, wglu)


def _ssm_params(a_re, a_im, log_dt, b_re, b_im, c_re, c_im, batch):
    depth = a_re.shape[0]
    lam = lax.complex(a_re.astype(F32), a_im.astype(F32))
    dt = jnp.exp(log_dt.astype(F32))
    a_bar = jnp.exp(lam * dt[..., None])
    b_bar = ((a_bar - 1.0) / lam)[..., None] * lax.complex(b_re.astype(F32), b_im.astype(F32))
    same_group = (np.arange(SSM_CH)[:, None] // SSM_GROUP) == (np.arange(SSM_STATES)[None, :] // SSM_STATE)

    def in_op(m):
        rows = jnp.swapaxes(m, -1, -2).reshape(depth, SSM_CH, SSM_STATE)
        return jnp.where(same_group, jnp.tile(rows, (1, 1, SSM_GROUPS)), 0.0)

    def out_op(m):
        cols = jnp.swapaxes(m.reshape(depth, SSM_CH, SSM_STATE), -1, -2)
        return jnp.where(same_group.T, jnp.tile(cols, (1, SSM_GROUPS, 1)), 0.0)

    bd = jnp.concatenate([in_op(jnp.real(b_bar)), in_op(jnp.imag(b_bar))], axis=-1).astype(BF16)
    cd = jnp.concatenate([out_op(c_re.astype(F32)), -out_op(c_im.astype(F32))], axis=-2).astype(BF16)
    state_rows = (depth, batch, SSM_STATES)
    ar = jnp.broadcast_to(jnp.real(a_bar).reshape(depth, 1, SSM_STATES), state_rows)
    ai = jnp.broadcast_to(jnp.imag(a_bar).reshape(depth, 1, SSM_STATES), state_rows)
    return bd, ar, ai, cd


def _ffn_kernel(x_ref, oa_ref, ob_ref, oc_ref, wo_ref, g2_ref, wg_ref, wu_ref, wd_ref, gf_ref, o_ref, h_scr,
                *, layer, final_norm):
    wo_a = wo_ref[:SWA_Q, :]
    wo_b = wo_ref[SWA_Q:SWA_Q + SSM_CH, :]
    wo_c = wo_ref[SWA_Q + SSM_CH:, :]
    x = x_ref[...] + _dot(oa_ref[...], wo_a) + _dot(ob_ref[...], wo_b) + _dot(oc_ref[...], wo_c)
    o_ref[...] = x
    h_scr[...] = _rms(x, g2_ref[layer:layer + 1, :]).astype(BF16)

    h = h_scr[...]
    gate = _dot(h, wg_ref[...])
    up = _dot(h, wu_ref[...])
    act = (gate * jax.nn.sigmoid(gate) * up).astype(BF16)
    o_ref[...] += _dot(act, wd_ref[...])
    if final_norm:
        o_ref[...] = _rms(o_ref[...], gf_ref[...])


def _ffn(x, oa, ob, oc, l, wo, g2, wg, wu, wd, gf, final_norm):
    b_, s_, _ = x.shape
    tm = FFN_TM
    row3 = lambda b, i: (b, i, 0)
    whole = lambda wt: pl.BlockSpec(wt.shape, lambda b, i: (0, 0))
    return pl.pallas_call(
        functools.partial(_ffn_kernel, layer=l, final_norm=final_norm),
        grid=(b_, s_ // tm),
        in_specs=[pl.BlockSpec((None, tm, D_MODEL), row3),
                  pl.BlockSpec((None, tm, SWA_Q), row3),
                  pl.BlockSpec((tm, SSM_CH), lambda b, i: (i, b)),
                  pl.BlockSpec((None, tm, MLA_HEADS * MLA_V), row3),
                  whole(wo), _table_spec(g2), whole(wg), whole(wu), whole(wd),
                  pl.BlockSpec((1, D_MODEL), lambda b, i: (0, 0))],
        out_specs=pl.BlockSpec((None, tm, D_MODEL), row3),
        out_shape=jax.ShapeDtypeStruct(x.shape, F32),
        scratch_shapes=[pltpu.VMEM((tm, D_MODEL), BF16)],
        compiler_params=pltpu.CompilerParams(
            dimension_semantics=("arbitrary", "arbitrary"), vmem_limit_bytes=VMEM_LIMIT),
        name="out_proj_ffn",
    )(x, oa, ob, oc, wo, g2, wg, wu, wd, gf)


def _rot_half_cols(w):
    half = w.shape[-1] // 2
    return jnp.concatenate([-w[..., half:], w[..., :half]], axis=-1)


def _rope_tile(w):
    z = jnp.zeros(w.shape[:-1] + (MLA_NOPE,), w.dtype)
    z2 = jnp.zeros(w.shape[:-1] + (MLA_HEAD_PAD - MLA_NOPE - MLA_ROPE,), w.dtype)
    return jnp.concatenate([z, w, z2], axis=-1)


def _rope_key_placement():
    half = MLA_ROPE // 2
    j = np.arange(MLA_ROPE)
    place = np.zeros((MLA_ROPE, 2 * LANES), np.float32)
    place[j, MLA_NOPE + j] = 1.0
    place[j, LANES + MLA_NOPE + (j + half) % MLA_ROPE] = np.where(j < half, 1.0, -1.0)
    return jnp.asarray(place, BF16)


def _stacked_weights(w_q_up, w_kv_up):
    lead = w_q_up.shape[:-2]
    qh = w_q_up.reshape(lead + (MLA_Q_RANK, MLA_HEADS, MLA_NOPE + MLA_ROPE))
    pad = jnp.zeros(lead + (MLA_Q_RANK, MLA_HEADS, MLA_HEAD_PAD - MLA_NOPE - MLA_ROPE), w_q_up.dtype)
    q_tiles = jnp.concatenate([qh, pad], axis=-1).reshape(lead + (MLA_Q_RANK, -1))
    q_rot = _rope_tile(_rot_half_cols(qh[..., MLA_NOPE:])).reshape(lead + (MLA_Q_RANK, -1))
    wq_ext = jnp.concatenate([q_tiles, q_rot], axis=-1)
    kvh = w_kv_up.reshape(lead + (MLA_KV_RANK, MLA_HEADS, MLA_NOPE + MLA_V))
    kpad = jnp.zeros(lead + (MLA_KV_RANK, MLA_HEADS, MLA_HEAD_PAD - MLA_NOPE), w_kv_up.dtype)
    k_tiles = jnp.concatenate([kvh[..., :MLA_NOPE], kpad], axis=-1).reshape(lead + (MLA_KV_RANK, -1))
    v_cols = kvh[..., MLA_NOPE:].reshape(lead + (MLA_KV_RANK, -1))
    wkv_ext = jnp.concatenate([k_tiles, v_cols], axis=-1)
    return wq_ext.astype(BF16), wkv_ext.astype(BF16)


def kernel(x, positions, rel_bias, ln1_g, w_in, sinks, ssm_a_re, ssm_a_im, ssm_log_dt, ssm_b_re, ssm_b_im,
           ssm_c_re, ssm_c_im, ssm_d, ssm_w_glu, mla_q_norm_g, mla_w_q_up, mla_kv_norm_g, mla_w_kv_up, w_out,
           ln2_g, w_gate, w_up, w_down, final_g):
    b_ = x.shape[0]
    cos_t, sin_t = _trig_tables(positions)
    bias = _band_bias_tables(rel_bias, sinks)
    wq_ext, wkv_ext = _stacked_weights(mla_w_q_up, mla_w_kv_up)
    w_in_bf = w_in.astype(BF16)
    bd, ar, ai, cd = _ssm_params(ssm_a_re, ssm_a_im, ssm_log_dt, ssm_b_re, ssm_b_im, ssm_c_re, ssm_c_im, b_)
    wglu = ssm_w_glu.astype(BF16)
    gf = final_g.reshape(1, D_MODEL)
    for l in range(DEPTH):
        qa, ka, va, u, qm, km, vm = _in_proj(x, l, ln1_g, w_in_bf, cos_t, sin_t,
                                             mla_q_norm_g, wq_ext, mla_kv_norm_g, wkv_ext)
        o_a = _swa(qa, ka, va, l, bias)
        o_b = _ssm(u, b_, l, bd, ar, ai, cd, ssm_d, wglu)
        o_c, wo, wg, wu, wd = _mla(qm, km, vm, l, (w_out, w_gate, w_up, w_down))
        x = _ffn(x, o_a, o_b, o_c, l, wo, ln2_g, wg, wu, wd, gf, final_norm=(l == DEPTH - 1))
    return x
```

```python
import functools
import math

import jax
import jax.numpy as jnp
import numpy as np
from jax import lax
from jax.experimental import pallas as pl
from jax.experimental.pallas import tpu as pltpu

D_MODEL = 1024
DEPTH = 2
HEAD_DIM = 64
SWA_Q_HEADS = 8
SWA_KV_HEADS = 2
SWA_BLOCK = 128
SWA_Q = SWA_Q_HEADS * HEAD_DIM
SWA_KV = SWA_KV_HEADS * HEAD_DIM
REL_BUCKETS = 32
REL_MAX_DIST = 128
SSM_CH = 256
SSM_GROUP = 16
SSM_GROUPS = SSM_CH // SSM_GROUP
SSM_STATE = 64
SSM_STATES = SSM_GROUPS * SSM_STATE
MLA_HEADS = 4
MLA_Q_RANK = 256
MLA_KV_RANK = 128
MLA_NOPE = 64
MLA_ROPE = 32
MLA_V = 64
ROPE_THETA = 10000.0
EPS = 1e-6
NEG = -1e30

LANES = 128
V7X_VMEM_BYTES = 64 * 1024 * 1024
VMEM_LIMIT = V7X_VMEM_BYTES * 7 // 8
MLA_HEAD_PAD = LANES
MLA_EXP2_SCALE = (MLA_NOPE + MLA_ROPE) ** -0.5 * math.log2(math.e)

IN_PROJ_TM = 1024
SWA_TQ = 512
MLA_TQ = 512
MLA_TK = MLA_TQ
SSM_TS = 128
SSM_SUB = 2
FFN_TM = 1024

BF16 = jnp.bfloat16
F32 = jnp.float32


def _rms(x, g):
    return x * lax.rsqrt(jnp.mean(x * x, axis=-1, keepdims=True) + EPS) * g


def _dot(a, b):
    return jnp.dot(a, b, preferred_element_type=F32)


def _dot_nt(a, b):
    return lax.dot_general(a, b, (((1,), (1,)), ((), ())), preferred_element_type=F32)


def _bf16_pieces(t):
    hi = t.astype(BF16)
    rest = t - hi.astype(F32)
    mid = rest.astype(BF16)
    return hi, mid, (rest - mid.astype(F32)).astype(BF16)


def _table_spec(arr):
    return pl.BlockSpec(arr.shape, lambda *_: (0,) * arr.ndim)


def _layer_spec(arr, l):
    return pl.BlockSpec((None,) + arr.shape[1:], lambda *_: (l,) + (0,) * (arr.ndim - 1))


def _trig_kernel(pos_ref, freq_ref, place_ref, one_ref, cos_ref, sin_ref):
    pos = pos_ref[pl.ds(pl.program_id(0), 1), :].astype(F32)
    ang = freq_ref[...] * pos

    def place(t):
        return sum(lax.dot_general(piece, place_ref[...], (((0,), (0,)), ((), ())), preferred_element_type=F32)
                   for piece in _bf16_pieces(t))

    cos_ref[...] = place(jnp.cos(ang)) + one_ref[...]
    sin_ref[...] = place(jnp.sin(ang))


def _trig_tables(positions):
    b_, s_ = positions.shape
    half = MLA_ROPE // 2
    inv_freq = jnp.power(ROPE_THETA, -jnp.arange(half, dtype=F32) * 2.0 / MLA_ROPE).reshape(half, 1)
    lane = np.arange(MLA_HEAD_PAD)[None, :]
    rope_lane = (lane >= MLA_NOPE) & (lane < MLA_NOPE + MLA_ROPE)
    placement = jnp.asarray(rope_lane & ((lane - MLA_NOPE) % half == np.arange(half)[:, None]), BF16)
    off_rope = jnp.asarray(~rope_lane, F32)
    const = lambda b: (0, 0)
    cos_t, sin_t = pl.pallas_call(
        _trig_kernel,
        grid=(b_,),
        in_specs=[pl.BlockSpec((b_, s_), const),
                  pl.BlockSpec((half, 1), const),
                  pl.BlockSpec((half, MLA_HEAD_PAD), const),
                  pl.BlockSpec((1, MLA_HEAD_PAD), const)],
        out_specs=[pl.BlockSpec((None, s_, MLA_HEAD_PAD), lambda b: (b, 0, 0))] * 2,
        out_shape=[jax.ShapeDtypeStruct((b_, s_, MLA_HEAD_PAD), F32)] * 2,
        name="rope_tables",
    )(positions, inv_freq, placement, off_rope)
    return cos_t.reshape(b_ * s_, MLA_HEAD_PAD), sin_t.reshape(b_ * s_, MLA_HEAD_PAD)


def _in_proj_kernel(x_ref, g_ref, w_ref, kplace_ref, cos_ref, sin_ref, qg_ref, wq_ref, kvg_ref, wkv_ref,
                    qa_ref, ka_ref, va_ref, u_ref, qm_ref, km_ref, vm_ref, *, layer):
    row = slice(layer, layer + 1)
    h = _rms(x_ref[...], g_ref[row, :]).astype(BF16)
    proj = _dot(h, w_ref[...])
    qa_ref[...] = (proj[:, :SWA_Q] * (HEAD_DIM ** -0.5)).astype(BF16)
    c = SWA_Q
    ka_ref[...] = proj[:, c:c + SWA_KV].astype(BF16)
    c += SWA_KV
    va_ref[...] = proj[:, c:c + SWA_KV].astype(BF16)
    c += SWA_KV
    u_ref[...] = proj[:, c:c + SSM_CH]
    c += SSM_CH
    c_q = proj[:, c:c + MLA_Q_RANK]
    c += MLA_Q_RANK
    c_kv = proj[:, c:c + MLA_KV_RANK]
    c += MLA_KV_RANK
    k_r2 = sum(_dot(piece, kplace_ref[...]) for piece in _bf16_pieces(proj[:, c:c + MLA_ROPE]))
    k_r, k_r_rot = k_r2[:, :LANES], k_r2[:, LANES:]
    cos = cos_ref[...]
    sin = sin_ref[...]
    width = MLA_HEADS * MLA_HEAD_PAD
    q2 = _dot(_rms(c_q, qg_ref[row, :]).astype(BF16), wq_ref[...])
    kv2 = _dot(_rms(c_kv, kvg_ref[row, :]).astype(BF16), wkv_ref[...])
    k_pe = k_r * cos + k_r_rot * sin
    low = lax.broadcasted_iota(jnp.int32, (cos.shape[0], LANES), 1) < MLA_V
    for hd in range(MLA_HEADS):
        sl = slice(hd * MLA_HEAD_PAD, (hd + 1) * MLA_HEAD_PAD)
        sl_rot = slice(width + hd * MLA_HEAD_PAD, width + (hd + 1) * MLA_HEAD_PAD)
        qm_ref[:, sl] = (q2[:, sl] * cos + q2[:, sl_rot] * sin).astype(BF16)
        km_ref[:, sl] = (kv2[:, sl] + k_pe).astype(BF16)
        pair = kv2[:, width + (hd // 2) * LANES:width + (hd // 2 + 1) * LANES]
        keep = low if hd % 2 == 0 else jnp.logical_not(low)
        vm_ref[:, sl] = jnp.where(keep, pair, 1.0).astype(BF16)


def _in_proj(x, l, g, w_in, cos_t, sin_t, qg, wq_ext, kvg, wkv_ext):
    b_, s_, _ = x.shape
    tm = IN_PROJ_TM
    nt = s_ // tm
    row3 = lambda b, i: (b, i, 0)
    trig = pl.BlockSpec((tm, MLA_HEAD_PAD), lambda b, i: (b * nt + i, 0))
    mla_w = MLA_HEADS * MLA_HEAD_PAD
    kplace = _rope_key_placement()
    return pl.pallas_call(
        functools.partial(_in_proj_kernel, layer=l),
        grid=(b_, nt),
        in_specs=[pl.BlockSpec((None, tm, D_MODEL), row3),
                  _table_spec(g), _layer_spec(w_in, l),
                  pl.BlockSpec(kplace.shape, lambda b, i: (0, 0)),
                  trig, trig,
                  _table_spec(qg), _layer_spec(wq_ext, l), _table_spec(kvg), _layer_spec(wkv_ext, l)],
        out_specs=[pl.BlockSpec((None, tm, SWA_Q), row3),
                   pl.BlockSpec((None, tm, SWA_KV), row3),
                   pl.BlockSpec((None, tm, SWA_KV), row3),
                   pl.BlockSpec((tm, SSM_CH), lambda b, i: (i, b)),
                   pl.BlockSpec((None, tm, mla_w), row3),
                   pl.BlockSpec((None, tm, mla_w), row3),
                   pl.BlockSpec((None, tm, mla_w), row3)],
        out_shape=[jax.ShapeDtypeStruct((b_, s_, SWA_Q), BF16),
                   jax.ShapeDtypeStruct((b_, s_, SWA_KV), BF16),
                   jax.ShapeDtypeStruct((b_, s_, SWA_KV), BF16),
                   jax.ShapeDtypeStruct((s_, b_ * SSM_CH), F32),
                   jax.ShapeDtypeStruct((b_, s_, mla_w), BF16),
                   jax.ShapeDtypeStruct((b_, s_, mla_w), BF16),
                   jax.ShapeDtypeStruct((b_, s_, mla_w), BF16)],
        compiler_params=pltpu.CompilerParams(
            dimension_semantics=("arbitrary", "arbitrary"), vmem_limit_bytes=VMEM_LIMIT),
        name="in_proj",
    )(x, g, w_in, kplace, cos_t, sin_t, qg, wq_ext, kvg, wkv_ext)


def _swa_tile(qi, q_ref, k_ref, v_ref, bias_ref, o_ref, s_scr, m_scr):
    blocks = SWA_TQ // SWA_BLOCK
    tiles = SWA_Q // LANES
    band = 2 * SWA_BLOCK
    lane = lax.broadcasted_iota(jnp.int32, (band, LANES), 1)
    key = lax.broadcasted_iota(jnp.int32, (band, LANES), 0)
    low = lane < HEAD_DIM
    out_low = lax.broadcasted_iota(jnp.int32, (SWA_BLOCK, LANES), 1) < HEAD_DIM

    def band_of(ref, j):
        blk = qi * blocks + j
        start = max(blk - 1, 0) * SWA_BLOCK
        keep = key != (band - 1 if blk == 0 else 0)
        t = ref[start:start + band, :]
        t = jnp.where(keep, t, jnp.zeros_like(t))
        return t, pltpu.roll(t, HEAD_DIM, 1)

    for j in range(blocks):
        table = min(qi * blocks + j, 1)
        kb, kb_sw = band_of(k_ref, j)
        zero = jnp.zeros_like(kb)
        k_var = ((jnp.where(low, kb, zero), jnp.where(low, zero, kb_sw)),
                 (jnp.where(low, kb_sw, zero), jnp.where(low, zero, kb)))
        rows = slice(j * SWA_BLOCK, (j + 1) * SWA_BLOCK)
        for i in range(tiles):
            kvh = (2 * i) // (SWA_Q_HEADS // SWA_KV_HEADS)
            q = q_ref[rows, i * LANES:(i + 1) * LANES]
            for half in range(2):
                u = (j * tiles + i) * 2 + half
                s = _dot_nt(q, k_var[kvh][half]) + bias_ref[table, 2 * i + half]
                s_scr[u] = s
                m_scr[u] = jnp.broadcast_to(jnp.max(s, axis=-1, keepdims=True), (SWA_BLOCK, LANES))
    for j in range(blocks):
        vb, vb_sw = band_of(v_ref, j)
        one = jnp.ones_like(vb)
        v_var = ((jnp.where(low, vb, one), jnp.where(low, one, vb_sw)),
                 (jnp.where(low, vb_sw, one), jnp.where(low, one, vb)))
        rows = slice(j * SWA_BLOCK, (j + 1) * SWA_BLOCK)
        for i in range(tiles):
            kvh = (2 * i) // (SWA_Q_HEADS // SWA_KV_HEADS)
            r = []
            for half in range(2):
                u = (j * tiles + i) * 2 + half
                m = m_scr[u]
                p = jnp.concatenate([jnp.exp(s_scr[u, :, t * LANES:(t + 1) * LANES] - m).astype(BF16)
                                     for t in range(band // LANES)], axis=-1)
                r.append(_dot(p, v_var[kvh][half]))
            denom = pltpu.roll(jnp.where(out_low, r[1], r[0]), HEAD_DIM, 1)
            o_ref[rows, i * LANES:(i + 1) * LANES] = (jnp.where(out_low, r[0], r[1]) / denom).astype(BF16)


def _swa_kernel(q_ref, *rest):
    for qi in range(rest[0].shape[0] // SWA_TQ):
        pl.when(pl.program_id(1) == qi)(functools.partial(_swa_tile, qi, q_ref, *rest))


def _swa(q, k, v, l, bias):
    b_, s_, _ = q.shape
    tq = SWA_TQ
    units = (tq // SWA_BLOCK) * SWA_Q_HEADS
    return pl.pallas_call(
        _swa_kernel,
        grid=(b_, s_ // tq),
        in_specs=[pl.BlockSpec((None, tq, SWA_Q), lambda b, i: (b, i, 0)),
                  pl.BlockSpec((None, s_, SWA_KV), lambda b, i: (b, 0, 0)),
                  pl.BlockSpec((None, s_, SWA_KV), lambda b, i: (b, 0, 0)),
                  _layer_spec(bias, l)],
        out_specs=pl.BlockSpec((None, tq, SWA_Q), lambda b, i: (b, i, 0)),
        out_shape=jax.ShapeDtypeStruct((b_, s_, SWA_Q), BF16),
        scratch_shapes=[pltpu.VMEM((units, SWA_BLOCK, 2 * SWA_BLOCK), F32),
                        pltpu.VMEM((units, SWA_BLOCK, LANES), F32)],
        compiler_params=pltpu.CompilerParams(
            dimension_semantics=("arbitrary", "arbitrary"), vmem_limit_bytes=VMEM_LIMIT),
        name="swa_attention",
    )(q, k, v, bias)


def _t5_bucket(dist):
    n = jnp.maximum(dist, 0)
    max_exact = REL_BUCKETS // 2
    large = max_exact + (jnp.log(jnp.maximum(n, 1).astype(F32) / max_exact)
                         / math.log(REL_MAX_DIST / max_exact)
                         * (REL_BUCKETS - max_exact)).astype(jnp.int32)
    large = jnp.minimum(large, REL_BUCKETS - 1)
    return jnp.where(n < max_exact, n, large)


def _band_bias_tables(rel_bias, sinks):
    qi = jnp.arange(SWA_BLOCK)[:, None]
    kj = jnp.arange(2 * SWA_BLOCK)[None, :]
    dist = qi + SWA_BLOCK - kj
    onehot = (_t5_bucket(dist)[None] == jnp.arange(REL_BUCKETS)[:, None, None]).astype(F32)
    b = jnp.einsum('nh,nqk->hqk', rel_bias.astype(F32), onehot, precision=lax.Precision.HIGHEST)
    valid = (dist >= 0) & (dist < SWA_BLOCK)
    normal = jnp.where(valid[None], b, NEG)
    first = jnp.concatenate([normal[:, :, SWA_BLOCK:], jnp.full_like(normal[:, :, SWA_BLOCK:], NEG)], axis=-1)
    sink_col = jnp.array([2 * SWA_BLOCK - 1, 0])[:, None, None, None]
    sink = sinks.astype(F32)[:, None, :, None, None]
    return jnp.where(kj == sink_col, sink, jnp.stack([first, normal])[None])


def _mla_pair(p, q_ref, k_ref, v_ref, o_ref, s_scr, m_scr, acc_scr):
    tq, tk = MLA_TQ, MLA_TK
    n = q_ref.shape[0] // tq
    half = tq // 2
    row = lax.broadcasted_iota(jnp.int32, (half, half), 0)
    col = lax.broadcasted_iota(jnp.int32, (half, half), 1)
    causal_half = col <= row
    causal_bottom = jnp.concatenate([jnp.ones((half, half), jnp.bool_), causal_half], axis=1)
    out_low = lax.broadcasted_iota(jnp.int32, (tq, LANES), 1) < MLA_V
    heads = [slice(hd * MLA_HEAD_PAD, (hd + 1) * MLA_HEAD_PAD) for hd in range(MLA_HEADS)]
    q_tiles = (p, n - 1 - p)
    slots = [(0, j) if j < p else (1, j - p) for j in range(n - 1)]

    def rows(tile, size=tq):
        return slice(tile * tq, tile * tq + size)

    def lane_max(m, s):
        return functools.reduce(jnp.maximum, [s[:, t * LANES:(t + 1) * LANES] for t in range(s.shape[1] // LANES)], m)

    def probs(s, m):
        return jnp.concatenate([jnp.exp2((s[:, t * LANES:(t + 1) * LANES] - m) * MLA_EXP2_SCALE).astype(BF16)
                                for t in range(s.shape[1] // LANES)], axis=-1)

    m_scr[...] = jnp.full(m_scr.shape, NEG, F32)
    acc_scr[...] = jnp.zeros(acc_scr.shape, F32)
    for which, tile in enumerate(q_tiles):
        top, bottom = rows(tile, half), slice(tile * tq + half, (tile + 1) * tq)
        for hd in range(MLA_HEADS):
            s_top = jnp.where(causal_half, _dot_nt(q_ref[top, heads[hd]], k_ref[top, heads[hd]]), NEG)
            s_bot = jnp.where(causal_bottom, _dot_nt(q_ref[bottom, heads[hd]], k_ref[rows(tile), heads[hd]]), NEG)
            s_scr[hd, which, :half, :half] = s_top
            s_scr[hd, which, half:, :] = s_bot
            m_scr[which, hd, :half] = lane_max(m_scr[which, hd, :half], s_top)
            m_scr[which, hd, half:] = lane_max(m_scr[which, hd, half:], s_bot)
    for j, (which, kt) in enumerate(slots):
        for hd in range(MLA_HEADS):
            s = _dot_nt(q_ref[rows(q_tiles[which]), heads[hd]], k_ref[rows(kt), heads[hd]])
            s_scr[hd, 2 + j] = s
            m_scr[which, hd] = lane_max(m_scr[which, hd], s)
    for which in range(2):
        for hd in range(MLA_HEADS):
            m_scr[which, hd] = jnp.broadcast_to(jnp.max(m_scr[which, hd], axis=-1, keepdims=True), (tq, LANES))
    for which, tile in enumerate(q_tiles):
        for hd in range(MLA_HEADS):
            e_top = probs(s_scr[hd, which, :half, :half], m_scr[which, hd, :half])
            e_bot = probs(s_scr[hd, which, half:, :], m_scr[which, hd, half:])
            acc_scr[which, hd, :half] += _dot(e_top, v_ref[rows(tile, half), heads[hd]])
            acc_scr[which, hd, half:] += _dot(e_bot, v_ref[rows(tile), heads[hd]])
    for j, (which, kt) in enumerate(slots):
        for hd in range(MLA_HEADS):
            acc_scr[which, hd] += _dot(probs(s_scr[hd, 2 + j], m_scr[which, hd]), v_ref[rows(kt), heads[hd]])
    for which, tile in enumerate(q_tiles):
        for t in range(MLA_HEADS // 2):
            lo, hi = acc_scr[which, 2 * t], acc_scr[which, 2 * t + 1]
            lo = lo / lo[:, MLA_V:MLA_V + 1]
            hi = hi / hi[:, 0:1]
            o_ref[rows(tile), t * LANES:(t + 1) * LANES] = jnp.where(out_low, lo, hi).astype(BF16)


def _mla_kernel(q_ref, k_ref, v_ref, *rest):
    n_w = (len(rest) - 4) // 2
    w_f32, o_ref, w_bf16 = rest[:n_w], rest[n_w], rest[n_w + 1:2 * n_w + 1]
    for src, dst in zip(w_f32, w_bf16):
        dst[...] = src[...].astype(BF16)
    for p in range(q_ref.shape[0] // MLA_TQ // 2):
        pl.when(pl.program_id(1) == p)(
            functools.partial(_mla_pair, p, q_ref, k_ref, v_ref, o_ref, *rest[2 * n_w + 1:]))


def _mla(q, k, v, l, weights):
    b_, s_, w = q.shape
    n = s_ // MLA_TQ
    steps = b_ * (n // 2)
    whole = lambda width: pl.BlockSpec((None, s_, width), lambda b, p: (b, 0, 0))
    slab_in = [pl.BlockSpec((None, wt.shape[1] // steps, wt.shape[2]), lambda b, p: (l, b * (n // 2) + p, 0))
               for wt in weights]
    slab_out = [pl.BlockSpec((wt.shape[1] // steps, wt.shape[2]), lambda b, p: (b * (n // 2) + p, 0))
                for wt in weights]
    return pl.pallas_call(
        _mla_kernel,
        grid=(b_, n // 2),
        in_specs=[whole(w), whole(w), whole(w)] + slab_in,
        out_specs=[whole(MLA_HEADS * MLA_V)] + slab_out,
        out_shape=[jax.ShapeDtypeStruct((b_, s_, MLA_HEADS * MLA_V), BF16)]
        + [jax.ShapeDtypeStruct(wt.shape[1:], BF16) for wt in weights],
        scratch_shapes=[pltpu.VMEM((MLA_HEADS, n + 1, MLA_TQ, MLA_TK), F32),
                        pltpu.VMEM((2, MLA_HEADS, MLA_TQ, LANES), F32),
                        pltpu.VMEM((2, MLA_HEADS, MLA_TQ, LANES), F32)],
        compiler_params=pltpu.CompilerParams(
            dimension_semantics=("arbitrary", "arbitrary"), vmem_limit_bytes=VMEM_LIMIT),
        name="mla_attention",
    )(q, k, v, *weights)


def _gelu_tanh(x):
    return 0.5 * x * (1.0 + jnp.tanh(math.sqrt(2.0 / math.pi) * (x + 0.044715 * (x * x * x))))


def _ssm_kernel(u_ref, bd_ref, ar_ref, ai_ref, cd_ref, d_ref, wglu_ref, y_ref,
                u_scr, x_scr, y_scr, sr_scr, si_scr, *, layer):
    n = SSM_STATES
    batch = sr_scr.shape[0]
    lane_tiles = SSM_CH // LANES

    @pl.when(pl.program_id(0) == 0)
    def _():
        sr_scr[...] = jnp.zeros_like(sr_scr)
        si_scr[...] = jnp.zeros_like(si_scr)

    for b in range(batch):
        for lt in range(lane_tiles):
            c0 = b * SSM_CH + lt * LANES
            u_scr[lt, pl.ds(b, SSM_TS, stride=batch), :] = u_ref[:, c0:c0 + LANES]
    sub_t = SSM_TS // SSM_SUB
    sub_r = sub_t * batch
    subs = [slice(q * sub_r, (q + 1) * sub_r) for q in range(SSM_SUB)]
    us = [jnp.concatenate([u_scr[lt, rows, :] for lt in range(lane_tiles)], axis=-1) for rows in subs]
    for rows, u in zip(subs, us):
        u_bf = u.astype(BF16)
        x_scr[rows, :n] = _dot(u_bf, bd_ref[:, :n])
        x_scr[rows, n:] = _dot(u_bf, bd_ref[:, n:])
    ar = ar_ref[...]
    ai = ai_ref[...]
    xr, xi = sr_scr[...], si_scr[...]
    for rows, u in zip(subs, us):
        for t in range(sub_t):
            r = slice(rows.start + t * batch, rows.start + (t + 1) * batch)
            xr, xi = (ar * xr - ai * xi + x_scr[r, :n], ar * xi + ai * xr + x_scr[r, n:])
            x_scr[r, :n] = xr
            x_scr[r, n:] = xi
        mid = rows.start + sub_r // 2
        cx = jnp.concatenate([_dot(x_scr[rows.start:mid, :].astype(BF16), cd_ref[...]),
                              _dot(x_scr[mid:rows.stop, :].astype(BF16), cd_ref[...])], axis=0)
        y = cx + d_ref[layer:layer + 1, :] * u
        y = _gelu_tanh(y)
        y = y * jax.nn.sigmoid(_dot(y.astype(BF16), wglu_ref[...]))
        for lt in range(lane_tiles):
            y_scr[lt, rows, :] = y[:, lt * LANES:(lt + 1) * LANES]
    sr_scr[...] = xr
    si_scr[...] = xi
    for b in range(batch):
        for lt in range(lane_tiles):
            c0 = b * SSM_CH + lt * LANES
            y_ref[:, c0:c0 + LANES] = y_scr[lt, pl.ds(b, SSM_TS, stride=batch), :].astype(BF16)


def _ssm(u, batch, l, bd, ar, ai, cd, d, wglu):
    s_, width = u.shape
    tr = SSM_TS * batch
    return pl.pallas_call(
        functools.partial(_ssm_kernel, layer=l),
        grid=(s_ // SSM_TS,),
        in_specs=[pl.BlockSpec((SSM_TS, width), lambda i: (i, 0))]
        + [_layer_spec(a, l) for a in (bd, ar, ai, cd)] + [_table_spec(d), _layer_spec(wglu, l)],
        out_specs=pl.BlockSpec((SSM_TS, width), lambda i: (i, 0)),
        out_shape=jax.ShapeDtypeStruct((s_, width), BF16),
        scratch_shapes=[pltpu.VMEM((SSM_CH // LANES, tr, LANES), F32),
                        pltpu.VMEM((tr, 2 * SSM_STATES), F32),
                        pltpu.VMEM((SSM_CH // LANES, tr, LANES), F32),
                        pltpu.VMEM((batch, SSM_STATES), F32),
                        pltpu.VMEM((batch, SSM_STATES), F32)],
        compiler_params=pltpu.CompilerParams(
            dimension_semantics=("arbitrary",), vmem_limit_bytes=VMEM_LIMIT),
        name="s5_mixer",
    )(u, bd, ar, ai, cd, d, wglu)


def _ssm_params(a_re, a_im, log_dt, b_re, b_im, c_re, c_im, batch):
    depth = a_re.shape[0]
    lam = lax.complex(a_re.astype(F32), a_im.astype(F32))
    dt = jnp.exp(log_dt.astype(F32))
    a_bar = jnp.exp(lam * dt[..., None])
    b_bar = ((a_bar - 1.0) / lam)[..., None] * lax.complex(b_re.astype(F32), b_im.astype(F32))
    same_group = (np.arange(SSM_CH)[:, None] // SSM_GROUP) == (np.arange(SSM_STATES)[None, :] // SSM_STATE)

    def in_op(m):
        rows = jnp.swapaxes(m, -1, -2).reshape(depth, SSM_CH, SSM_STATE)
        return jnp.where(same_group, jnp.tile(rows, (1, 1, SSM_GROUPS)), 0.0)

    def out_op(m):
        cols = jnp.swapaxes(m.reshape(depth, SSM_CH, SSM_STATE), -1, -2)
        return jnp.where(same_group.T, jnp.tile(cols, (1, SSM_GROUPS, 1)), 0.0)

    bd = jnp.concatenate([in_op(jnp.real(b_bar)), in_op(jnp.imag(b_bar))], axis=-1).astype(BF16)
    cd = jnp.concatenate([out_op(c_re.astype(F32)), -out_op(c_im.astype(F32))], axis=-2).astype(BF16)
    state_rows = (depth, batch, SSM_STATES)
    ar = jnp.broadcast_to(jnp.real(a_bar).reshape(depth, 1, SSM_STATES), state_rows)
    ai = jnp.broadcast_to(jnp.imag(a_bar).reshape(depth, 1, SSM_STATES), state_rows)
    return bd, ar, ai, cd


def _ffn_kernel(x_ref, oa_ref, ob_ref, oc_ref, wo_ref, g2_ref, wg_ref, wu_ref, wd_ref, gf_ref, o_ref, h_scr,
                *, layer, final_norm):
    wo_a = wo_ref[:SWA_Q, :]
    wo_b = wo_ref[SWA_Q:SWA_Q + SSM_CH, :]
    wo_c = wo_ref[SWA_Q + SSM_CH:, :]
    x = x_ref[...] + _dot(oa_ref[...], wo_a) + _dot(ob_ref[...], wo_b) + _dot(oc_ref[...], wo_c)
    o_ref[...] = x
    h_scr[...] = _rms(x, g2_ref[layer:layer + 1, :]).astype(BF16)

    h = h_scr[...]
    gate = _dot(h, wg_ref[...])
    up = _dot(h, wu_ref[...])
    act = (gate * jax.nn.sigmoid(gate) * up).astype(BF16)
    o_ref[...] += _dot(act, wd_ref[...])
    if final_norm:
        o_ref[...] = _rms(o_ref[...], gf_ref[...])


def _ffn(x, oa, ob, oc, l, wo, g2, wg, wu, wd, gf, final_norm):
    b_, s_, _ = x.shape
    tm = FFN_TM
    row3 = lambda b, i: (b, i, 0)
    whole = lambda wt: pl.BlockSpec(wt.shape, lambda b, i: (0, 0))
    return pl.pallas_call(
        functools.partial(_ffn_kernel, layer=l, final_norm=final_norm),
        grid=(b_, s_ // tm),
        in_specs=[pl.BlockSpec((None, tm, D_MODEL), row3),
                  pl.BlockSpec((None, tm, SWA_Q), row3),
                  pl.BlockSpec((tm, SSM_CH), lambda b, i: (i, b)),
                  pl.BlockSpec((None, tm, MLA_HEADS * MLA_V), row3),
                  whole(wo), _table_spec(g2), whole(wg), whole(wu), whole(wd),
                  pl.BlockSpec((1, D_MODEL), lambda b, i: (0, 0))],
        out_specs=pl.BlockSpec((None, tm, D_MODEL), row3),
        out_shape=jax.ShapeDtypeStruct(x.shape, F32),
        scratch_shapes=[pltpu.VMEM((tm, D_MODEL), BF16)],
        compiler_params=pltpu.CompilerParams(
            dimension_semantics=("arbitrary", "arbitrary"), vmem_limit_bytes=VMEM_LIMIT),
        name="out_proj_ffn",
    )(x, oa, ob, oc, wo, g2, wg, wu, wd, gf)


def _rot_half_cols(w):
    half = w.shape[-1] // 2
    return jnp.concatenate([-w[..., half:], w[..., :half]], axis=-1)


def _rope_tile(w):
    z = jnp.zeros(w.shape[:-1] + (MLA_NOPE,), w.dtype)
    z2 = jnp.zeros(w.shape[:-1] + (MLA_HEAD_PAD - MLA_NOPE - MLA_ROPE,), w.dtype)
    return jnp.concatenate([z, w, z2], axis=-1)


def _rope_key_placement():
    half = MLA_ROPE // 2
    j = np.arange(MLA_ROPE)
    place = np.zeros((MLA_ROPE, 2 * LANES), np.float32)
    place[j, MLA_NOPE + j] = 1.0
    place[j, LANES + MLA_NOPE + (j + half) % MLA_ROPE] = np.where(j < half, 1.0, -1.0)
    return jnp.asarray(place, BF16)


def _stacked_weights(w_q_up, w_kv_up):
    lead = w_q_up.shape[:-2]
    qh = w_q_up.reshape(lead + (MLA_Q_RANK, MLA_HEADS, MLA_NOPE + MLA_ROPE))
    pad = jnp.zeros(lead + (MLA_Q_RANK, MLA_HEADS, MLA_HEAD_PAD - MLA_NOPE - MLA_ROPE), w_q_up.dtype)
    q_tiles = jnp.concatenate([qh, pad], axis=-1).reshape(lead + (MLA_Q_RANK, -1))
    q_rot = _rope_tile(_rot_half_cols(qh[..., MLA_NOPE:])).reshape(lead + (MLA_Q_RANK, -1))
    wq_ext = jnp.concatenate([q_tiles, q_rot], axis=-1)
    kvh = w_kv_up.reshape(lead + (MLA_KV_RANK, MLA_HEADS, MLA_NOPE + MLA_V))
    kpad = jnp.zeros(lead + (MLA_KV_RANK, MLA_HEADS, MLA_HEAD_PAD - MLA_NOPE), w_kv_up.dtype)
    k_tiles = jnp.concatenate([kvh[..., :MLA_NOPE], kpad], axis=-1).reshape(lead + (MLA_KV_RANK, -1))
    v_cols = kvh[..., MLA_NOPE:].reshape(lead + (MLA_KV_RANK, -1))
    wkv_ext = jnp.concatenate([k_tiles, v_cols], axis=-1)
    return wq_ext.astype(BF16), wkv_ext.astype(BF16)


def kernel(x, positions, rel_bias, ln1_g, w_in, sinks, ssm_a_re, ssm_a_im, ssm_log_dt, ssm_b_re, ssm_b_im,
           ssm_c_re, ssm_c_im, ssm_d, ssm_w_glu, mla_q_norm_g, mla_w_q_up, mla_kv_norm_g, mla_w_kv_up, w_out,
           ln2_g, w_gate, w_up, w_down, final_g):
    b_ = x.shape[0]
    cos_t, sin_t = _trig_tables(positions)
    bias = _band_bias_tables(rel_bias, sinks)
    wq_ext, wkv_ext = _stacked_weights(mla_w_q_up, mla_w_kv_up)
    w_in_bf = w_in.astype(BF16)
    bd, ar, ai, cd = _ssm_params(ssm_a_re, ssm_a_im, ssm_log_dt, ssm_b_re, ssm_b_im, ssm_c_re, ssm_c_im, b_)
    wglu = ssm_w_glu.astype(BF16)
    gf = final_g.reshape(1, D_MODEL)
    for l in range(DEPTH):
        qa, ka, va, u, qm, km, vm = _in_proj(x, l, ln1_g, w_in_bf, cos_t, sin_t,
                                             mla_q_norm_g, wq_ext, mla_kv_norm_g, wkv_ext)
        o_a = _swa(qa, ka, va, l, bias)
        o_b = _ssm(u, b_, l, bd, ar, ai, cd, ssm_d, wglu)
        o_c, wo, wg, wu, wd = _mla(qm, km, vm, l, (w_out, w_gate, w_up, w_down))
        x = _ffn(x, o_a, o_b, o_c, l, wo, ln2_g, wg, wu, wd, gf, final_norm=(l == DEPTH - 1))
    return x
```

```python
import functools
import math

import jax
import jax.numpy as jnp
import numpy as np
from jax import lax
from jax.experimental import pallas as pl
from jax.experimental.pallas import tpu as pltpu

D_MODEL = 1024
DEPTH = 2
HEAD_DIM = 64
SWA_Q_HEADS = 8
SWA_KV_HEADS = 2
SWA_BLOCK = 128
SWA_Q = SWA_Q_HEADS * HEAD_DIM
SWA_KV = SWA_KV_HEADS * HEAD_DIM
REL_BUCKETS = 32
REL_MAX_DIST = 128
SSM_CH = 256
SSM_GROUP = 16
SSM_GROUPS = SSM_CH // SSM_GROUP
SSM_STATE = 64
SSM_STATES = SSM_GROUPS * SSM_STATE
MLA_HEADS = 4
MLA_Q_RANK = 256
MLA_KV_RANK = 128
MLA_NOPE = 64
MLA_ROPE = 32
MLA_V = 64
ROPE_THETA = 10000.0
EPS = 1e-6
NEG = -1e30

LANES = 128
V7X_VMEM_BYTES = 64 * 1024 * 1024
VMEM_LIMIT = V7X_VMEM_BYTES * 7 // 8
MLA_HEAD_PAD = LANES
MLA_EXP2_SCALE = (MLA_NOPE + MLA_ROPE) ** -0.5 * math.log2(math.e)

IN_PROJ_TM = 1024
IN_PROJ_SUB = 2
SWA_TQ = 512
MLA_TQ = 512
MLA_TK = MLA_TQ
SSM_TS = 128
SSM_SPLIT = (32, 96)
FFN_TM = 1024

BF16 = jnp.bfloat16
F32 = jnp.float32


def _rms(x, g):
    return x * lax.rsqrt(jnp.mean(x * x, axis=-1, keepdims=True) + EPS) * g


def _dot(a, b):
    return jnp.dot(a, b, preferred_element_type=F32)


def _dot_nt(a, b):
    return lax.dot_general(a, b, (((1,), (1,)), ((), ())), preferred_element_type=F32)


def _bf16_pieces(t):
    hi = t.astype(BF16)
    rest = t - hi.astype(F32)
    mid = rest.astype(BF16)
    return hi, mid, (rest - mid.astype(F32)).astype(BF16)


def _table_spec(arr):
    return pl.BlockSpec(arr.shape, lambda *_: (0,) * arr.ndim)


def _layer_spec(arr, l):
    return pl.BlockSpec((None,) + arr.shape[1:], lambda *_: (l,) + (0,) * (arr.ndim - 1))


def _trig_kernel(pos_ref, freq_ref, place_ref, one_ref, cos_ref, sin_ref):
    pos = pos_ref[pl.ds(pl.program_id(0), 1), :].astype(F32)
    ang = freq_ref[...] * pos

    def place(t):
        return sum(lax.dot_general(piece, place_ref[...], (((0,), (0,)), ((), ())), preferred_element_type=F32)
                   for piece in _bf16_pieces(t))

    cos_ref[...] = place(jnp.cos(ang)) + one_ref[...]
    sin_ref[...] = place(jnp.sin(ang))


def _trig_tables(positions):
    b_, s_ = positions.shape
    half = MLA_ROPE // 2
    inv_freq = jnp.power(ROPE_THETA, -jnp.arange(half, dtype=F32) * 2.0 / MLA_ROPE).reshape(half, 1)
    lane = np.arange(MLA_HEAD_PAD)[None, :]
    rope_lane = (lane >= MLA_NOPE) & (lane < MLA_NOPE + MLA_ROPE)
    placement = jnp.asarray(rope_lane & ((lane - MLA_NOPE) % half == np.arange(half)[:, None]), BF16)
    off_rope = jnp.asarray(~rope_lane, F32)
    const = lambda b: (0, 0)
    cos_t, sin_t = pl.pallas_call(
        _trig_kernel,
        grid=(b_,),
        in_specs=[pl.BlockSpec((b_, s_), const),
                  pl.BlockSpec((half, 1), const),
                  pl.BlockSpec((half, MLA_HEAD_PAD), const),
                  pl.BlockSpec((1, MLA_HEAD_PAD), const)],
        out_specs=[pl.BlockSpec((None, s_, MLA_HEAD_PAD), lambda b: (b, 0, 0))] * 2,
        out_shape=[jax.ShapeDtypeStruct((b_, s_, MLA_HEAD_PAD), F32)] * 2,
        name="rope_tables",
    )(positions, inv_freq, placement, off_rope)
    return cos_t.reshape(b_ * s_, MLA_HEAD_PAD), sin_t.reshape(b_ * s_, MLA_HEAD_PAD)


def _in_proj_kernel(x_ref, g_ref, w_ref, kplace_ref, cos_ref, sin_ref, qg_ref, wq_ref, kvg_ref, wkv_ref,
                    qa_ref, ka_ref, va_ref, u_ref, qm_ref, km_ref, vm_ref, *, layer):
    row = slice(layer, layer + 1)
    sub = x_ref.shape[0] // IN_PROJ_SUB
    width = MLA_HEADS * MLA_HEAD_PAD
    low = lax.broadcasted_iota(jnp.int32, (sub, LANES), 1) < MLA_V
    for r0 in range(0, x_ref.shape[0], sub):
        rs = slice(r0, r0 + sub)
        h = _rms(x_ref[rs, :], g_ref[row, :]).astype(BF16)
        proj = _dot(h, w_ref[...])
        qa_ref[rs, :] = (proj[:, :SWA_Q] * (HEAD_DIM ** -0.5)).astype(BF16)
        c = SWA_Q
        ka_ref[rs, :] = proj[:, c:c + SWA_KV].astype(BF16)
        c += SWA_KV
        va_ref[rs, :] = proj[:, c:c + SWA_KV].astype(BF16)
        c += SWA_KV
        u_ref[rs, :] = proj[:, c:c + SSM_CH]
        c += SSM_CH
        c_q = proj[:, c:c + MLA_Q_RANK]
        c += MLA_Q_RANK
        c_kv = proj[:, c:c + MLA_KV_RANK]
        c += MLA_KV_RANK
        k_r2 = sum(_dot(piece, kplace_ref[...]) for piece in _bf16_pieces(proj[:, c:c + MLA_ROPE]))
        k_r, k_r_rot = k_r2[:, :LANES], k_r2[:, LANES:]
        cos = cos_ref[rs, :]
        sin = sin_ref[rs, :]
        q2 = _dot(_rms(c_q, qg_ref[row, :]).astype(BF16), wq_ref[...])
        kv2 = _dot(_rms(c_kv, kvg_ref[row, :]).astype(BF16), wkv_ref[...])
        k_pe = k_r * cos + k_r_rot * sin
        for hd in range(MLA_HEADS):
            sl = slice(hd * MLA_HEAD_PAD, (hd + 1) * MLA_HEAD_PAD)
            sl_rot = slice(width + hd * MLA_HEAD_PAD, width + (hd + 1) * MLA_HEAD_PAD)
            qm_ref[rs, sl] = (q2[:, sl] * cos + q2[:, sl_rot] * sin).astype(BF16)
            km_ref[rs, sl] = (kv2[:, sl] + k_pe).astype(BF16)
            pair = kv2[:, width + (hd // 2) * LANES:width + (hd // 2 + 1) * LANES]
            keep = low if hd % 2 == 0 else jnp.logical_not(low)
            vm_ref[rs, sl] = jnp.where(keep, pair, 1.0).astype(BF16)


def _in_proj(x, l, g, w_in, cos_t, sin_t, qg, wq_ext, kvg, wkv_ext):
    b_, s_, _ = x.shape
    tm = IN_PROJ_TM
    nt = s_ // tm
    row3 = lambda b, i: (b, i, 0)
    trig = pl.BlockSpec((tm, MLA_HEAD_PAD), lambda b, i: (b * nt + i, 0))
    mla_w = MLA_HEADS * MLA_HEAD_PAD
    kplace = _rope_key_placement()
    return pl.pallas_call(
        functools.partial(_in_proj_kernel, layer=l),
        grid=(b_, nt),
        in_specs=[pl.BlockSpec((None, tm, D_MODEL), row3),
                  _table_spec(g), _layer_spec(w_in, l),
                  pl.BlockSpec(kplace.shape, lambda b, i: (0, 0)),
                  trig, trig,
                  _table_spec(qg), _layer_spec(wq_ext, l), _table_spec(kvg), _layer_spec(wkv_ext, l)],
        out_specs=[pl.BlockSpec((None, tm, SWA_Q), row3),
                   pl.BlockSpec((None, tm, SWA_KV), row3),
                   pl.BlockSpec((None, tm, SWA_KV), row3),
                   pl.BlockSpec((tm, SSM_CH), lambda b, i: (i, b)),
                   pl.BlockSpec((None, tm, mla_w), row3),
                   pl.BlockSpec((None, tm, mla_w), row3),
                   pl.BlockSpec((None, tm, mla_w), row3)],
        out_shape=[jax.ShapeDtypeStruct((b_, s_, SWA_Q), BF16),
                   jax.ShapeDtypeStruct((b_, s_, SWA_KV), BF16),
                   jax.ShapeDtypeStruct((b_, s_, SWA_KV), BF16),
                   jax.ShapeDtypeStruct((s_, b_ * SSM_CH), F32),
                   jax.ShapeDtypeStruct((b_, s_, mla_w), BF16),
                   jax.ShapeDtypeStruct((b_, s_, mla_w), BF16),
                   jax.ShapeDtypeStruct((b_, s_, mla_w), BF16)],
        compiler_params=pltpu.CompilerParams(
            dimension_semantics=("arbitrary", "arbitrary"), vmem_limit_bytes=VMEM_LIMIT),
        name="in_proj",
    )(x, g, w_in, kplace, cos_t, sin_t, qg, wq_ext, kvg, wkv_ext)


def _swa_tile(qi, q_ref, k_ref, v_ref, bias_ref, o_ref, s_scr, m_scr):
    blocks = SWA_TQ // SWA_BLOCK
    tiles = SWA_Q // LANES
    band = 2 * SWA_BLOCK
    lane = lax.broadcasted_iota(jnp.int32, (band, LANES), 1)
    key = lax.broadcasted_iota(jnp.int32, (band, LANES), 0)
    low = lane < HEAD_DIM
    out_low = lax.broadcasted_iota(jnp.int32, (SWA_BLOCK, LANES), 1) < HEAD_DIM

    def band_of(ref, j):
        blk = qi * blocks + j
        start = max(blk - 1, 0) * SWA_BLOCK
        keep = key != (band - 1 if blk == 0 else 0)
        t = ref[start:start + band, :]
        t = jnp.where(keep, t, jnp.zeros_like(t))
        return t, pltpu.roll(t, HEAD_DIM, 1)

    for j in range(blocks):
        table = min(qi * blocks + j, 1)
        kb, kb_sw = band_of(k_ref, j)
        zero = jnp.zeros_like(kb)
        k_var = ((jnp.where(low, kb, zero), jnp.where(low, zero, kb_sw)),
                 (jnp.where(low, kb_sw, zero), jnp.where(low, zero, kb)))
        rows = slice(j * SWA_BLOCK, (j + 1) * SWA_BLOCK)
        for i in range(tiles):
            kvh = (2 * i) // (SWA_Q_HEADS // SWA_KV_HEADS)
            q = q_ref[rows, i * LANES:(i + 1) * LANES]
            for half in range(2):
                u = (j * tiles + i) * 2 + half
                s = _dot_nt(q, k_var[kvh][half]) + bias_ref[table, 2 * i + half]
                s_scr[u] = s
                m_scr[u] = jnp.broadcast_to(jnp.max(s, axis=-1, keepdims=True), (SWA_BLOCK, LANES))
    for j in range(blocks):
        vb, vb_sw = band_of(v_ref, j)
        one = jnp.ones_like(vb)
        v_var = ((jnp.where(low, vb, one), jnp.where(low, one, vb_sw)),
                 (jnp.where(low, vb_sw, one), jnp.where(low, one, vb)))
        rows = slice(j * SWA_BLOCK, (j + 1) * SWA_BLOCK)
        for i in range(tiles):
            kvh = (2 * i) // (SWA_Q_HEADS // SWA_KV_HEADS)
            r = []
            for half in range(2):
                u = (j * tiles + i) * 2 + half
                m = m_scr[u]
                p = jnp.concatenate([jnp.exp(s_scr[u, :, t * LANES:(t + 1) * LANES] - m).astype(BF16)
                                     for t in range(band // LANES)], axis=-1)
                r.append(_dot(p, v_var[kvh][half]))
            denom = pltpu.roll(jnp.where(out_low, r[1], r[0]), HEAD_DIM, 1)
            o_ref[rows, i * LANES:(i + 1) * LANES] = (jnp.where(out_low, r[0], r[1]) / denom).astype(BF16)


def _swa_kernel(q_ref, *rest):
    for qi in range(rest[0].shape[0] // SWA_TQ):
        pl.when(pl.program_id(1) == qi)(functools.partial(_swa_tile, qi, q_ref, *rest))


def _swa(q, k, v, l, bias):
    b_, s_, _ = q.shape
    tq = SWA_TQ
    units = (tq // SWA_BLOCK) * SWA_Q_HEADS
    return pl.pallas_call(
        _swa_kernel,
        grid=(b_, s_ // tq),
        in_specs=[pl.BlockSpec((None, tq, SWA_Q), lambda b, i: (b, i, 0)),
                  pl.BlockSpec((None, s_, SWA_KV), lambda b, i: (b, 0, 0)),
                  pl.BlockSpec((None, s_, SWA_KV), lambda b, i: (b, 0, 0)),
                  _layer_spec(bias, l)],
        out_specs=pl.BlockSpec((None, tq, SWA_Q), lambda b, i: (b, i, 0)),
        out_shape=jax.ShapeDtypeStruct((b_, s_, SWA_Q), BF16),
        scratch_shapes=[pltpu.VMEM((units, SWA_BLOCK, 2 * SWA_BLOCK), F32),
                        pltpu.VMEM((units, SWA_BLOCK, LANES), F32)],
        compiler_params=pltpu.CompilerParams(
            dimension_semantics=("arbitrary", "arbitrary"), vmem_limit_bytes=VMEM_LIMIT),
        name="swa_attention",
    )(q, k, v, bias)


def _t5_bucket(dist):
    n = jnp.maximum(dist, 0)
    max_exact = REL_BUCKETS // 2
    large = max_exact + (jnp.log(jnp.maximum(n, 1).astype(F32) / max_exact)
                         / math.log(REL_MAX_DIST / max_exact)
                         * (REL_BUCKETS - max_exact)).astype(jnp.int32)
    large = jnp.minimum(large, REL_BUCKETS - 1)
    return jnp.where(n < max_exact, n, large)


def _band_bias_tables(rel_bias, sinks):
    qi = jnp.arange(SWA_BLOCK)[:, None]
    kj = jnp.arange(2 * SWA_BLOCK)[None, :]
    dist = qi + SWA_BLOCK - kj
    onehot = (_t5_bucket(dist)[None] == jnp.arange(REL_BUCKETS)[:, None, None]).astype(F32)
    b = jnp.einsum('nh,nqk->hqk', rel_bias.astype(F32), onehot, precision=lax.Precision.HIGHEST)
    valid = (dist >= 0) & (dist < SWA_BLOCK)
    normal = jnp.where(valid[None], b, NEG)
    first = jnp.concatenate([normal[:, :, SWA_BLOCK:], jnp.full_like(normal[:, :, SWA_BLOCK:], NEG)], axis=-1)
    sink_col = jnp.array([2 * SWA_BLOCK - 1, 0])[:, None, None, None]
    sink = sinks.astype(F32)[:, None, :, None, None]
    return jnp.where(kj == sink_col, sink, jnp.stack([first, normal])[None])


def _mla_pair(p, q_ref, k_ref, v_ref, o_ref, s_scr, m_scr, acc_scr):
    tq, tk = MLA_TQ, MLA_TK
    n = q_ref.shape[0] // tq
    half = tq // 2
    row = lax.broadcasted_iota(jnp.int32, (half, half), 0)
    col = lax.broadcasted_iota(jnp.int32, (half, half), 1)
    causal_half = col <= row
    causal_bottom = jnp.concatenate([jnp.ones((half, half), jnp.bool_), causal_half], axis=1)
    out_low = lax.broadcasted_iota(jnp.int32, (tq, LANES), 1) < MLA_V
    heads = [slice(hd * MLA_HEAD_PAD, (hd + 1) * MLA_HEAD_PAD) for hd in range(MLA_HEADS)]
    q_tiles = (p, n - 1 - p)
    slots = [(0, j) if j < p else (1, j - p) for j in range(n - 1)]

    def rows(tile, size=tq):
        return slice(tile * tq, tile * tq + size)

    def lane_max(m, s):
        return functools.reduce(jnp.maximum, [s[:, t * LANES:(t + 1) * LANES] for t in range(s.shape[1] // LANES)], m)

    def probs(s, m):
        return jnp.concatenate([jnp.exp2((s[:, t * LANES:(t + 1) * LANES] - m) * MLA_EXP2_SCALE).astype(BF16)
                                for t in range(s.shape[1] // LANES)], axis=-1)

    m_scr[...] = jnp.full(m_scr.shape, NEG, F32)
    acc_scr[...] = jnp.zeros(acc_scr.shape, F32)
    for which, tile in enumerate(q_tiles):
        top, bottom = rows(tile, half), slice(tile * tq + half, (tile + 1) * tq)
        for hd in range(MLA_HEADS):
            s_top = jnp.where(causal_half, _dot_nt(q_ref[top, heads[hd]], k_ref[top, heads[hd]]), NEG)
            s_bot = jnp.where(causal_bottom, _dot_nt(q_ref[bottom, heads[hd]], k_ref[rows(tile), heads[hd]]), NEG)
            s_scr[hd, which, :half, :half] = s_top
            s_scr[hd, which, half:, :] = s_bot
            m_scr[which, hd, :half] = lane_max(m_scr[which, hd, :half], s_top)
            m_scr[which, hd, half:] = lane_max(m_scr[which, hd, half:], s_bot)
    for j, (which, kt) in enumerate(slots):
        for hd in range(MLA_HEADS):
            s = _dot_nt(q_ref[rows(q_tiles[which]), heads[hd]], k_ref[rows(kt), heads[hd]])
            s_scr[hd, 2 + j] = s
            m_scr[which, hd] = lane_max(m_scr[which, hd], s)
    for which in range(2):
        for hd in range(MLA_HEADS):
            m_scr[which, hd] = jnp.broadcast_to(jnp.max(m_scr[which, hd], axis=-1, keepdims=True), (tq, LANES))
    for which, tile in enumerate(q_tiles):
        for hd in range(MLA_HEADS):
            e_top = probs(s_scr[hd, which, :half, :half], m_scr[which, hd, :half])
            e_bot = probs(s_scr[hd, which, half:, :], m_scr[which, hd, half:])
            acc_scr[which, hd, :half] += _dot(e_top, v_ref[rows(tile, half), heads[hd]])
            acc_scr[which, hd, half:] += _dot(e_bot, v_ref[rows(tile), heads[hd]])
    for j, (which, kt) in enumerate(slots):
        for hd in range(MLA_HEADS):
            acc_scr[which, hd] += _dot(probs(s_scr[hd, 2 + j], m_scr[which, hd]), v_ref[rows(kt), heads[hd]])
    for which, tile in enumerate(q_tiles):
        for t in range(MLA_HEADS // 2):
            lo, hi = acc_scr[which, 2 * t], acc_scr[which, 2 * t + 1]
            lo = lo / lo[:, MLA_V:MLA_V + 1]
            hi = hi / hi[:, 0:1]
            o_ref[rows(tile), t * LANES:(t + 1) * LANES] = jnp.where(out_low, lo, hi).astype(BF16)


def _mla_kernel(q_ref, k_ref, v_ref, *rest):
    n_w = (len(rest) - 4) // 2
    w_f32, o_ref, w_bf16 = rest[:n_w], rest[n_w], rest[n_w + 1:2 * n_w + 1]
    for src, dst in zip(w_f32, w_bf16):
        dst[...] = src[...].astype(BF16)
    for p in range(q_ref.shape[0] // MLA_TQ // 2):
        pl.when(pl.program_id(1) == p)(
            functools.partial(_mla_pair, p, q_ref, k_ref, v_ref, o_ref, *rest[2 * n_w + 1:]))


def _mla(q, k, v, l, weights):
    b_, s_, w = q.shape
    n = s_ // MLA_TQ
    steps = b_ * (n // 2)
    whole = lambda width: pl.BlockSpec((None, s_, width), lambda b, p: (b, 0, 0))
    slab_in = [pl.BlockSpec((None, wt.shape[1] // steps, wt.shape[2]), lambda b, p: (l, b * (n // 2) + p, 0))
               for wt in weights]
    slab_out = [pl.BlockSpec((wt.shape[1] // steps, wt.shape[2]), lambda b, p: (b * (n // 2) + p, 0))
                for wt in weights]
    return pl.pallas_call(
        _mla_kernel,
        grid=(b_, n // 2),
        in_specs=[whole(w), whole(w), whole(w)] + slab_in,
        out_specs=[whole(MLA_HEADS * MLA_V)] + slab_out,
        out_shape=[jax.ShapeDtypeStruct((b_, s_, MLA_HEADS * MLA_V), BF16)]
        + [jax.ShapeDtypeStruct(wt.shape[1:], BF16) for wt in weights],
        scratch_shapes=[pltpu.VMEM((MLA_HEADS, n + 1, MLA_TQ, MLA_TK), F32),
                        pltpu.VMEM((2, MLA_HEADS, MLA_TQ, LANES), F32),
                        pltpu.VMEM((2, MLA_HEADS, MLA_TQ, LANES), F32)],
        compiler_params=pltpu.CompilerParams(
            dimension_semantics=("arbitrary", "arbitrary"), vmem_limit_bytes=VMEM_LIMIT),
        name="mla_attention",
    )(q, k, v, *weights)


def _gelu_tanh(x):
    return 0.5 * x * (1.0 + jnp.tanh(math.sqrt(2.0 / math.pi) * (x + 0.044715 * (x * x * x))))


def _ssm_kernel(u_ref, bd_ref, ar_ref, ai_ref, cd_ref, d_ref, wglu_ref, y_ref,
                u_scr, x_scr, y_scr, sr_scr, si_scr, *, layer):
    n = SSM_STATES
    batch = sr_scr.shape[0]
    lane_tiles = SSM_CH // LANES

    @pl.when(pl.program_id(0) == 0)
    def _():
        sr_scr[...] = jnp.zeros_like(sr_scr)
        si_scr[...] = jnp.zeros_like(si_scr)

    for b in range(batch):
        for lt in range(lane_tiles):
            c0 = b * SSM_CH + lt * LANES
            u_scr[lt, pl.ds(b, SSM_TS, stride=batch), :] = u_ref[:, c0:c0 + LANES]
    bounds = [0] + [int(t) * batch for t in np.cumsum(SSM_SPLIT)]
    subs = [slice(a, b) for a, b in zip(bounds[:-1], bounds[1:])]
    us = [jnp.concatenate([u_scr[lt, rows, :] for lt in range(lane_tiles)], axis=-1) for rows in subs]
    for rows, u in zip(subs, us):
        u_bf = u.astype(BF16)
        x_scr[rows, :n] = _dot(u_bf, bd_ref[:, :n])
        x_scr[rows, n:] = _dot(u_bf, bd_ref[:, n:])
    ar = ar_ref[...]
    ai = ai_ref[...]
    xr, xi = sr_scr[...], si_scr[...]
    for rows, u in zip(subs, us):
        for r0 in range(rows.start, rows.stop, batch):
            r = slice(r0, r0 + batch)
            xr, xi = (ar * xr - ai * xi + x_scr[r, :n], ar * xi + ai * xr + x_scr[r, n:])
            x_scr[r, :n] = xr
            x_scr[r, n:] = xi
        mid = (rows.start + rows.stop) // 2
        cx = jnp.concatenate([_dot(x_scr[rows.start:mid, :].astype(BF16), cd_ref[...]),
                              _dot(x_scr[mid:rows.stop, :].astype(BF16), cd_ref[...])], axis=0)
        y = cx + d_ref[layer:layer + 1, :] * u
        y = _gelu_tanh(y)
        y = y * jax.nn.sigmoid(_dot(y.astype(BF16), wglu_ref[...]))
        for lt in range(lane_tiles):
            y_scr[lt, rows, :] = y[:, lt * LANES:(lt + 1) * LANES]
    sr_scr[...] = xr
    si_scr[...] = xi
    for b in range(batch):
        for lt in range(lane_tiles):
            c0 = b * SSM_CH + lt * LANES
            y_ref[:, c0:c0 + LANES] = y_scr[lt, pl.ds(b, SSM_TS, stride=batch), :].astype(BF16)


def _ssm(u, batch, l, bd, ar, ai, cd, d, wglu):
    s_, width = u.shape
    tr = SSM_TS * batch
    return pl.pallas_call(
        functools.partial(_ssm_kernel, layer=l),
        grid=(s_ // SSM_TS,),
        in_specs=[pl.BlockSpec((SSM_TS, width), lambda i: (i, 0))]
        + [_layer_spec(a, l) for a in (bd, ar, ai, cd)] + [_table_spec(d), _layer_spec(wglu, l)],
        out_specs=pl.BlockSpec((SSM_TS, width), lambda i: (i, 0)),
        out_shape=jax.ShapeDtypeStruct((s_, width), BF16),
        scratch_shapes=[pltpu.VMEM((SSM_CH // LANES, tr, LANES), F32),
                        pltpu.VMEM((tr, 2 * SSM_STATES), F32),
                        pltpu.VMEM((SSM_CH // LANES, tr, LANES), F32),
                        pltpu.VMEM((batch, SSM_STATES), F32),
                        pltpu.VMEM((batch, SSM_STATES), F32)],
        compiler_params=pltpu.CompilerParams(
            dimension_semantics=("arbitrary",), vmem_limit_bytes=VMEM_LIMIT),
        name="s5_mixer",
    )(u, bd, ar, ai, cd, d, wglu)


def _ssm_params(a_re, a_im, log_dt, b_re, b_im, c_re, c_im, batch):
    depth = a_re.shape[0]
    lam = lax.complex(a_re.astype(F32), a_im.astype(F32))
    dt = jnp.exp(log_dt.astype(F32))
    a_bar = jnp.exp(lam * dt[..., None])
    b_bar = ((a_bar - 1.0) / lam)[..., None] * lax.complex(b_re.astype(F32), b_im.astype(F32))
    same_group = (np.arange(SSM_CH)[:, None] // SSM_GROUP) == (np.arange(SSM_STATES)[None, :] // SSM_STATE)

    def in_op(m):
        rows = jnp.swapaxes(m, -1, -2).reshape(depth, SSM_CH, SSM_STATE)
        return jnp.where(same_group, jnp.tile(rows, (1, 1, SSM_GROUPS)), 0.0)

    def out_op(m):
        cols = jnp.swapaxes(m.reshape(depth, SSM_CH, SSM_STATE), -1, -2)
        return jnp.where(same_group.T, jnp.tile(cols, (1, SSM_GROUPS, 1)), 0.0)

    bd = jnp.concatenate([in_op(jnp.real(b_bar)), in_op(jnp.imag(b_bar))], axis=-1).astype(BF16)
    cd = jnp.concatenate([out_op(c_re.astype(F32)), -out_op(c_im.astype(F32))], axis=-2).astype(BF16)
    state_rows = (depth, batch, SSM_STATES)
    ar = jnp.broadcast_to(jnp.real(a_bar).reshape(depth, 1, SSM_STATES), state_rows)
    ai = jnp.broadcast_to(jnp.imag(a_bar).reshape(depth, 1, SSM_STATES), state_rows)
    return bd, ar, ai, cd


def _ffn_kernel(x_ref, oa_ref, ob_ref, oc_ref, wo_ref, g2_ref, wg_ref, wu_ref, wd_ref, gf_ref, o_ref, h_scr,
                *, layer, final_norm):
    wo_a = wo_ref[:SWA_Q, :]
    wo_b = wo_ref[SWA_Q:SWA_Q + SSM_CH, :]
    wo_c = wo_ref[SWA_Q + SSM_CH:, :]
    x = x_ref[...] + _dot(oa_ref[...], wo_a) + _dot(ob_ref[...], wo_b) + _dot(oc_ref[...], wo_c)
    o_ref[...] = x
    h_scr[...] = _rms(x, g2_ref[layer:layer + 1, :]).astype(BF16)

    h = h_scr[...]
    gate = _dot(h, wg_ref[...])
    up = _dot(h, wu_ref[...])
    act = (gate * jax.nn.sigmoid(gate) * up).astype(BF16)
    o_ref[...] += _dot(act, wd_ref[...])
    if final_norm:
        o_ref[...] = _rms(o_ref[...], gf_ref[...])


def _ffn(x, oa, ob, oc, l, wo, g2, wg, wu, wd, gf, final_norm):
    b_, s_, _ = x.shape
    tm = FFN_TM
    row3 = lambda b, i: (b, i, 0)
    whole = lambda wt: pl.BlockSpec(wt.shape, lambda b, i: (0, 0))
    return pl.pallas_call(
        functools.partial(_ffn_kernel, layer=l, final_norm=final_norm),
        grid=(b_, s_ // tm),
        in_specs=[pl.BlockSpec((None, tm, D_MODEL), row3),
                  pl.BlockSpec((None, tm, SWA_Q), row3),
                  pl.BlockSpec((tm, SSM_CH), lambda b, i: (i, b)),
                  pl.BlockSpec((None, tm, MLA_HEADS * MLA_V), row3),
                  whole(wo), _table_spec(g2), whole(wg), whole(wu), whole(wd),
                  pl.BlockSpec((1, D_MODEL), lambda b, i: (0, 0))],
        out_specs=pl.BlockSpec((None, tm, D_MODEL), row3),
        out_shape=jax.ShapeDtypeStruct(x.shape, F32),
        scratch_shapes=[pltpu.VMEM((tm, D_MODEL), BF16)],
        compiler_params=pltpu.CompilerParams(
            dimension_semantics=("arbitrary", "arbitrary"), vmem_limit_bytes=VMEM_LIMIT),
        name="out_proj_ffn",
    )(x, oa, ob, oc, wo, g2, wg, wu, wd, gf)


def _rot_half_cols(w):
    half = w.shape[-1] // 2
    return jnp.concatenate([-w[..., half:], w[..., :half]], axis=-1)


def _rope_tile(w):
    z = jnp.zeros(w.shape[:-1] + (MLA_NOPE,), w.dtype)
    z2 = jnp.zeros(w.shape[:-1] + (MLA_HEAD_PAD - MLA_NOPE - MLA_ROPE,), w.dtype)
    return jnp.concatenate([z, w, z2], axis=-1)


def _rope_key_placement():
    half = MLA_ROPE // 2
    j = np.arange(MLA_ROPE)
    place = np.zeros((MLA_ROPE, 2 * LANES), np.float32)
    place[j, MLA_NOPE + j] = 1.0
    place[j, LANES + MLA_NOPE + (j + half) % MLA_ROPE] = np.where(j < half, 1.0, -1.0)
    return jnp.asarray(place, BF16)


def _stacked_weights(w_q_up, w_kv_up):
    lead = w_q_up.shape[:-2]
    qh = w_q_up.reshape(lead + (MLA_Q_RANK, MLA_HEADS, MLA_NOPE + MLA_ROPE))
    pad = jnp.zeros(lead + (MLA_Q_RANK, MLA_HEADS, MLA_HEAD_PAD - MLA_NOPE - MLA_ROPE), w_q_up.dtype)
    q_tiles = jnp.concatenate([qh, pad], axis=-1).reshape(lead + (MLA_Q_RANK, -1))
    q_rot = _rope_tile(_rot_half_cols(qh[..., MLA_NOPE:])).reshape(lead + (MLA_Q_RANK, -1))
    wq_ext = jnp.concatenate([q_tiles, q_rot], axis=-1)
    kvh = w_kv_up.reshape(lead + (MLA_KV_RANK, MLA_HEADS, MLA_NOPE + MLA_V))
    kpad = jnp.zeros(lead + (MLA_KV_RANK, MLA_HEADS, MLA_HEAD_PAD - MLA_NOPE), w_kv_up.dtype)
    k_tiles = jnp.concatenate([kvh[..., :MLA_NOPE], kpad], axis=-1).reshape(lead + (MLA_KV_RANK, -1))
    v_cols = kvh[..., MLA_NOPE:].reshape(lead + (MLA_KV_RANK, -1))
    wkv_ext = jnp.concatenate([k_tiles, v_cols], axis=-1)
    return wq_ext.astype(BF16), wkv_ext.astype(BF16)


def kernel(x, positions, rel_bias, ln1_g, w_in, sinks, ssm_a_re, ssm_a_im, ssm_log_dt, ssm_b_re, ssm_b_im,
           ssm_c_re, ssm_c_im, ssm_d, ssm_w_glu, mla_q_norm_g, mla_w_q_up, mla_kv_norm_g, mla_w_kv_up, w_out,
           ln2_g, w_gate, w_up, w_down, final_g):
    b_ = x.shape[0]
    cos_t, sin_t = _trig_tables(positions)
    bias = _band_bias_tables(rel_bias, sinks)
    wq_ext, wkv_ext = _stacked_weights(mla_w_q_up, mla_w_kv_up)
    w_in_bf = w_in.astype(BF16)
    bd, ar, ai, cd = _ssm_params(ssm_a_re, ssm_a_im, ssm_log_dt, ssm_b_re, ssm_b_im, ssm_c_re, ssm_c_im, b_)
    wglu = ssm_w_glu.astype(BF16)
    gf = final_g.reshape(1, D_MODEL)
    for l in range(DEPTH):
        qa, ka, va, u, qm, km, vm = _in_proj(x, l, ln1_g, w_in_bf, cos_t, sin_t,
                                             mla_q_norm_g, wq_ext, mla_kv_norm_g, wkv_ext)
        o_a = _swa(qa, ka, va, l, bias)
        o_b = _ssm(u, b_, l, bd, ar, ai, cd, ssm_d, wglu)
        o_c, wo, wg, wu, wd = _mla(qm, km, vm, l, (w_out, w_gate, w_up, w_down))
        x = _ffn(x, o_a, o_b, o_c, l, wo, ln2_g, wg, wu, wd, gf, final_norm=(l == DEPTH - 1))
    return x
```

```python
import functools
import math

import jax
import jax.numpy as jnp
import numpy as np
from jax import lax
from jax.experimental import pallas as pl
from jax.experimental.pallas import tpu as pltpu

D_MODEL = 1024
DEPTH = 2
HEAD_DIM = 64
SWA_Q_HEADS = 8
SWA_KV_HEADS = 2
SWA_BLOCK = 128
SWA_Q = SWA_Q_HEADS * HEAD_DIM
SWA_KV = SWA_KV_HEADS * HEAD_DIM
REL_BUCKETS = 32
REL_MAX_DIST = 128
SSM_CH = 256
SSM_GROUP = 16
SSM_GROUPS = SSM_CH // SSM_GROUP
SSM_STATE = 64
SSM_STATES = SSM_GROUPS * SSM_STATE
MLA_HEADS = 4
MLA_Q_RANK = 256
MLA_KV_RANK = 128
MLA_NOPE = 64
MLA_ROPE = 32
MLA_V = 64
ROPE_THETA = 10000.0
EPS = 1e-6
NEG = -1e30

LANES = 128
V7X_VMEM_BYTES = 64 * 1024 * 1024
VMEM_LIMIT = V7X_VMEM_BYTES * 7 // 8
MLA_HEAD_PAD = LANES
MLA_EXP2_SCALE = (MLA_NOPE + MLA_ROPE) ** -0.5 * math.log2(math.e)

IN_PROJ_TM = 2048
SWA_TQ = 512
MLA_TQ = 512
MLA_TK = MLA_TQ
SSM_TS = 128
SSM_SPLIT = (32, 96)
FFN_TM = 1024

BF16 = jnp.bfloat16
F32 = jnp.float32


def _rms(x, g):
    return x * lax.rsqrt(jnp.mean(x * x, axis=-1, keepdims=True) + EPS) * g


def _dot(a, b):
    return jnp.dot(a, b, preferred_element_type=F32)


def _dot_nt(a, b):
    return lax.dot_general(a, b, (((1,), (1,)), ((), ())), preferred_element_type=F32)


def _bf16_pieces(t):
    hi = t.astype(BF16)
    rest = t - hi.astype(F32)
    mid = rest.astype(BF16)
    return hi, mid, (rest - mid.astype(F32)).astype(BF16)


def _table_spec(arr):
    return pl.BlockSpec(arr.shape, lambda *_: (0,) * arr.ndim)


def _layer_spec(arr, l):
    return pl.BlockSpec((None,) + arr.shape[1:], lambda *_: (l,) + (0,) * (arr.ndim - 1))


def _trig_kernel(pos_ref, freq_ref, place_ref, one_ref, cos_ref, sin_ref):
    pos = pos_ref[pl.ds(pl.program_id(0), 1), :].astype(F32)
    ang = freq_ref[...] * pos

    def place(t):
        return sum(lax.dot_general(piece, place_ref[...], (((0,), (0,)), ((), ())), preferred_element_type=F32)
                   for piece in _bf16_pieces(t))

    cos_ref[...] = place(jnp.cos(ang)) + one_ref[...]
    sin_ref[...] = place(jnp.sin(ang))


def _trig_tables(positions):
    b_, s_ = positions.shape
    half = MLA_ROPE // 2
    inv_freq = jnp.power(ROPE_THETA, -jnp.arange(half, dtype=F32) * 2.0 / MLA_ROPE).reshape(half, 1)
    lane = np.arange(MLA_HEAD_PAD)[None, :]
    rope_lane = (lane >= MLA_NOPE) & (lane < MLA_NOPE + MLA_ROPE)
    placement = jnp.asarray(rope_lane & ((lane - MLA_NOPE) % half == np.arange(half)[:, None]), BF16)
    off_rope = jnp.asarray(~rope_lane, F32)
    const = lambda b: (0, 0)
    cos_t, sin_t = pl.pallas_call(
        _trig_kernel,
        grid=(b_,),
        in_specs=[pl.BlockSpec((b_, s_), const),
                  pl.BlockSpec((half, 1), const),
                  pl.BlockSpec((half, MLA_HEAD_PAD), const),
                  pl.BlockSpec((1, MLA_HEAD_PAD), const)],
        out_specs=[pl.BlockSpec((None, s_, MLA_HEAD_PAD), lambda b: (b, 0, 0))] * 2,
        out_shape=[jax.ShapeDtypeStruct((b_, s_, MLA_HEAD_PAD), F32)] * 2,
        name="rope_tables",
    )(positions, inv_freq, placement, off_rope)
    return cos_t.reshape(b_ * s_, MLA_HEAD_PAD), sin_t.reshape(b_ * s_, MLA_HEAD_PAD)


def _in_proj_kernel(x_ref, g_ref, w_ref, kplace_ref, cos_ref, sin_ref, qg_ref, wq_ref, kvg_ref, wkv_ref,
                    qa_ref, ka_ref, va_ref, u_ref, qm_ref, km_ref, vm_ref, *, layer):
    row = slice(layer, layer + 1)
    h = _rms(x_ref[...], g_ref[row, :]).astype(BF16)
    proj = _dot(h, w_ref[...])
    qa_ref[...] = (proj[:, :SWA_Q] * (HEAD_DIM ** -0.5)).astype(BF16)
    c = SWA_Q
    ka_ref[...] = proj[:, c:c + SWA_KV].astype(BF16)
    c += SWA_KV
    va_ref[...] = proj[:, c:c + SWA_KV].astype(BF16)
    c += SWA_KV
    u_ref[...] = proj[:, c:c + SSM_CH]
    c += SSM_CH
    c_q = proj[:, c:c + MLA_Q_RANK]
    c += MLA_Q_RANK
    c_kv = proj[:, c:c + MLA_KV_RANK]
    c += MLA_KV_RANK
    k_r2 = sum(_dot(piece, kplace_ref[...]) for piece in _bf16_pieces(proj[:, c:c + MLA_ROPE]))
    k_r, k_r_rot = k_r2[:, :LANES], k_r2[:, LANES:]
    cos = cos_ref[...]
    sin = sin_ref[...]
    width = MLA_HEADS * MLA_HEAD_PAD
    q2 = _dot(_rms(c_q, qg_ref[row, :]).astype(BF16), wq_ref[...])
    kv2 = _dot(_rms(c_kv, kvg_ref[row, :]).astype(BF16), wkv_ref[...])
    k_pe = k_r * cos + k_r_rot * sin
    low = lax.broadcasted_iota(jnp.int32, (cos.shape[0], LANES), 1) < MLA_V
    for hd in range(MLA_HEADS):
        sl = slice(hd * MLA_HEAD_PAD, (hd + 1) * MLA_HEAD_PAD)
        sl_rot = slice(width + hd * MLA_HEAD_PAD, width + (hd + 1) * MLA_HEAD_PAD)
        qm_ref[:, sl] = (q2[:, sl] * cos + q2[:, sl_rot] * sin).astype(BF16)
        km_ref[:, sl] = (kv2[:, sl] + k_pe).astype(BF16)
        pair = kv2[:, width + (hd // 2) * LANES:width + (hd // 2 + 1) * LANES]
        keep = low if hd % 2 == 0 else jnp.logical_not(low)
        vm_ref[:, sl] = jnp.where(keep, pair, 1.0).astype(BF16)


def _in_proj(x, l, g, w_in, cos_t, sin_t, qg, wq_ext, kvg, wkv_ext):
    b_, s_, _ = x.shape
    tm = IN_PROJ_TM
    nt = s_ // tm
    row3 = lambda b, i: (b, i, 0)
    trig = pl.BlockSpec((tm, MLA_HEAD_PAD), lambda b, i: (b * nt + i, 0))
    mla_w = MLA_HEADS * MLA_HEAD_PAD
    kplace = _rope_key_placement()
    return pl.pallas_call(
        functools.partial(_in_proj_kernel, layer=l),
        grid=(b_, nt),
        in_specs=[pl.BlockSpec((None, tm, D_MODEL), row3),
                  _table_spec(g), _layer_spec(w_in, l),
                  pl.BlockSpec(kplace.shape, lambda b, i: (0, 0)),
                  trig, trig,
                  _table_spec(qg), _layer_spec(wq_ext, l), _table_spec(kvg), _layer_spec(wkv_ext, l)],
        out_specs=[pl.BlockSpec((None, tm, SWA_Q), row3),
                   pl.BlockSpec((None, tm, SWA_KV), row3),
                   pl.BlockSpec((None, tm, SWA_KV), row3),
                   pl.BlockSpec((tm, SSM_CH), lambda b, i: (i, b)),
                   pl.BlockSpec((None, tm, mla_w), row3),
                   pl.BlockSpec((None, tm, mla_w), row3),
                   pl.BlockSpec((None, tm, mla_w), row3)],
        out_shape=[jax.ShapeDtypeStruct((b_, s_, SWA_Q), BF16),
                   jax.ShapeDtypeStruct((b_, s_, SWA_KV), BF16),
                   jax.ShapeDtypeStruct((b_, s_, SWA_KV), BF16),
                   jax.ShapeDtypeStruct((s_, b_ * SSM_CH), F32),
                   jax.ShapeDtypeStruct((b_, s_, mla_w), BF16),
                   jax.ShapeDtypeStruct((b_, s_, mla_w), BF16),
                   jax.ShapeDtypeStruct((b_, s_, mla_w), BF16)],
        compiler_params=pltpu.CompilerParams(
            dimension_semantics=("arbitrary", "arbitrary"), vmem_limit_bytes=VMEM_LIMIT),
        name="in_proj",
    )(x, g, w_in, kplace, cos_t, sin_t, qg, wq_ext, kvg, wkv_ext)


def _swa_tile(qi, q_ref, k_ref, v_ref, bias_ref, o_ref, s_scr, m_scr):
    blocks = SWA_TQ // SWA_BLOCK
    tiles = SWA_Q // LANES
    band = 2 * SWA_BLOCK
    lane = lax.broadcasted_iota(jnp.int32, (band, LANES), 1)
    key = lax.broadcasted_iota(jnp.int32, (band, LANES), 0)
    low = lane < HEAD_DIM
    out_low = lax.broadcasted_iota(jnp.int32, (SWA_BLOCK, LANES), 1) < HEAD_DIM

    def band_of(ref, j):
        blk = qi * blocks + j
        start = max(blk - 1, 0) * SWA_BLOCK
        keep = key != (band - 1 if blk == 0 else 0)
        t = ref[start:start + band, :]
        t = jnp.where(keep, t, jnp.zeros_like(t))
        return t, pltpu.roll(t, HEAD_DIM, 1)

    for j in range(blocks):
        table = min(qi * blocks + j, 1)
        kb, kb_sw = band_of(k_ref, j)
        zero = jnp.zeros_like(kb)
        k_var = ((jnp.where(low, kb, zero), jnp.where(low, zero, kb_sw)),
                 (jnp.where(low, kb_sw, zero), jnp.where(low, zero, kb)))
        rows = slice(j * SWA_BLOCK, (j + 1) * SWA_BLOCK)
        for i in range(tiles):
            kvh = (2 * i) // (SWA_Q_HEADS // SWA_KV_HEADS)
            q = q_ref[rows, i * LANES:(i + 1) * LANES]
            for half in range(2):
                u = (j * tiles + i) * 2 + half
                s = _dot_nt(q, k_var[kvh][half]) + bias_ref[table, 2 * i + half]
                s_scr[u] = s
                m_scr[u] = jnp.broadcast_to(jnp.max(s, axis=-1, keepdims=True), (SWA_BLOCK, LANES))
    for j in range(blocks):
        vb, vb_sw = band_of(v_ref, j)
        one = jnp.ones_like(vb)
        v_var = ((jnp.where(low, vb, one), jnp.where(low, one, vb_sw)),
                 (jnp.where(low, vb_sw, one), jnp.where(low, one, vb)))
        rows = slice(j * SWA_BLOCK, (j + 1) * SWA_BLOCK)
        for i in range(tiles):
            kvh = (2 * i) // (SWA_Q_HEADS // SWA_KV_HEADS)
            r = []
            for half in range(2):
                u = (j * tiles + i) * 2 + half
                m = m_scr[u]
                p = jnp.concatenate([jnp.exp(s_scr[u, :, t * LANES:(t + 1) * LANES] - m).astype(BF16)
                                     for t in range(band // LANES)], axis=-1)
                r.append(_dot(p, v_var[kvh][half]))
            denom = pltpu.roll(jnp.where(out_low, r[1], r[0]), HEAD_DIM, 1)
            o_ref[rows, i * LANES:(i + 1) * LANES] = (jnp.where(out_low, r[0], r[1]) / denom).astype(BF16)


def _swa_kernel(q_ref, *rest):
    for qi in range(rest[0].shape[0] // SWA_TQ):
        pl.when(pl.program_id(1) == qi)(functools.partial(_swa_tile, qi, q_ref, *rest))


def _swa(q, k, v, l, bias):
    b_, s_, _ = q.shape
    tq = SWA_TQ
    units = (tq // SWA_BLOCK) * SWA_Q_HEADS
    return pl.pallas_call(
        _swa_kernel,
        grid=(b_, s_ // tq),
        in_specs=[pl.BlockSpec((None, tq, SWA_Q), lambda b, i: (b, i, 0)),
                  pl.BlockSpec((None, s_, SWA_KV), lambda b, i: (b, 0, 0)),
                  pl.BlockSpec((None, s_, SWA_KV), lambda b, i: (b, 0, 0)),
                  _layer_spec(bias, l)],
        out_specs=pl.BlockSpec((None, tq, SWA_Q), lambda b, i: (b, i, 0)),
        out_shape=jax.ShapeDtypeStruct((b_, s_, SWA_Q), BF16),
        scratch_shapes=[pltpu.VMEM((units, SWA_BLOCK, 2 * SWA_BLOCK), F32),
                        pltpu.VMEM((units, SWA_BLOCK, LANES), F32)],
        compiler_params=pltpu.CompilerParams(
            dimension_semantics=("arbitrary", "arbitrary"), vmem_limit_bytes=VMEM_LIMIT),
        name="swa_attention",
    )(q, k, v, bias)


def _t5_bucket(dist):
    n = jnp.maximum(dist, 0)
    max_exact = REL_BUCKETS // 2
    large = max_exact + (jnp.log(jnp.maximum(n, 1).astype(F32) / max_exact)
                         / math.log(REL_MAX_DIST / max_exact)
                         * (REL_BUCKETS - max_exact)).astype(jnp.int32)
    large = jnp.minimum(large, REL_BUCKETS - 1)
    return jnp.where(n < max_exact, n, large)


def _band_bias_tables(rel_bias, sinks):
    qi = jnp.arange(SWA_BLOCK)[:, None]
    kj = jnp.arange(2 * SWA_BLOCK)[None, :]
    dist = qi + SWA_BLOCK - kj
    onehot = (_t5_bucket(dist)[None] == jnp.arange(REL_BUCKETS)[:, None, None]).astype(F32)
    b = jnp.einsum('nh,nqk->hqk', rel_bias.astype(F32), onehot, precision=lax.Precision.HIGHEST)
    valid = (dist >= 0) & (dist < SWA_BLOCK)
    normal = jnp.where(valid[None], b, NEG)
    first = jnp.concatenate([normal[:, :, SWA_BLOCK:], jnp.full_like(normal[:, :, SWA_BLOCK:], NEG)], axis=-1)
    sink_col = jnp.array([2 * SWA_BLOCK - 1, 0])[:, None, None, None]
    sink = sinks.astype(F32)[:, None, :, None, None]
    return jnp.where(kj == sink_col, sink, jnp.stack([first, normal])[None])


def _mla_pair(p, q_ref, k_ref, v_ref, o_ref, s_scr, m_scr, acc_scr):
    tq, tk = MLA_TQ, MLA_TK
    n = q_ref.shape[0] // tq
    half = tq // 2
    row = lax.broadcasted_iota(jnp.int32, (half, half), 0)
    col = lax.broadcasted_iota(jnp.int32, (half, half), 1)
    causal_half = col <= row
    causal_bottom = jnp.concatenate([jnp.ones((half, half), jnp.bool_), causal_half], axis=1)
    out_low = lax.broadcasted_iota(jnp.int32, (tq, LANES), 1) < MLA_V
    heads = [slice(hd * MLA_HEAD_PAD, (hd + 1) * MLA_HEAD_PAD) for hd in range(MLA_HEADS)]
    q_tiles = (p, n - 1 - p)
    slots = [(0, j) if j < p else (1, j - p) for j in range(n - 1)]

    def rows(tile, size=tq):
        return slice(tile * tq, tile * tq + size)

    def lane_max(m, s):
        return functools.reduce(jnp.maximum, [s[:, t * LANES:(t + 1) * LANES] for t in range(s.shape[1] // LANES)], m)

    def probs(s, m):
        return jnp.concatenate([jnp.exp2((s[:, t * LANES:(t + 1) * LANES] - m) * MLA_EXP2_SCALE).astype(BF16)
                                for t in range(s.shape[1] // LANES)], axis=-1)

    m_scr[...] = jnp.full(m_scr.shape, NEG, F32)
    acc_scr[...] = jnp.zeros(acc_scr.shape, F32)
    for which, tile in enumerate(q_tiles):
        top, bottom = rows(tile, half), slice(tile * tq + half, (tile + 1) * tq)
        for hd in range(MLA_HEADS):
            s_top = jnp.where(causal_half, _dot_nt(q_ref[top, heads[hd]], k_ref[top, heads[hd]]), NEG)
            s_bot = jnp.where(causal_bottom, _dot_nt(q_ref[bottom, heads[hd]], k_ref[rows(tile), heads[hd]]), NEG)
            s_scr[hd, which, :half, :half] = s_top
            s_scr[hd, which, half:, :] = s_bot
            m_scr[which, hd, :half] = lane_max(m_scr[which, hd, :half], s_top)
            m_scr[which, hd, half:] = lane_max(m_scr[which, hd, half:], s_bot)
    for j, (which, kt) in enumerate(slots):
        for hd in range(MLA_HEADS):
            s = _dot_nt(q_ref[rows(q_tiles[which]), heads[hd]], k_ref[rows(kt), heads[hd]])
            s_scr[hd, 2 + j] = s
            m_scr[which, hd] = lane_max(m_scr[which, hd], s)
    for which in range(2):
        for hd in range(MLA_HEADS):
            m_scr[which, hd] = jnp.broadcast_to(jnp.max(m_scr[which, hd], axis=-1, keepdims=True), (tq, LANES))
    for which, tile in enumerate(q_tiles):
        for hd in range(MLA_HEADS):
            e_top = probs(s_scr[hd, which, :half, :half], m_scr[which, hd, :half])
            e_bot = probs(s_scr[hd, which, half:, :], m_scr[which, hd, half:])
            acc_scr[which, hd, :half] += _dot(e_top, v_ref[rows(tile, half), heads[hd]])
            acc_scr[which, hd, half:] += _dot(e_bot, v_ref[rows(tile), heads[hd]])
    for j, (which, kt) in enumerate(slots):
        for hd in range(MLA_HEADS):
            acc_scr[which, hd] += _dot(probs(s_scr[hd, 2 + j], m_scr[which, hd]), v_ref[rows(kt), heads[hd]])
    for which, tile in enumerate(q_tiles):
        for t in range(MLA_HEADS // 2):
            lo, hi = acc_scr[which, 2 * t], acc_scr[which, 2 * t + 1]
            lo = lo / lo[:, MLA_V:MLA_V + 1]
            hi = hi / hi[:, 0:1]
            o_ref[rows(tile), t * LANES:(t + 1) * LANES] = jnp.where(out_low, lo, hi).astype(BF16)


def _mla_kernel(q_ref, k_ref, v_ref, *rest):
    n_w = (len(rest) - 4) // 2
    w_f32, o_ref, w_bf16 = rest[:n_w], rest[n_w], rest[n_w + 1:2 * n_w + 1]
    for src, dst in zip(w_f32, w_bf16):
        dst[...] = src[...].astype(BF16)
    for p in range(q_ref.shape[0] // MLA_TQ // 2):
        pl.when(pl.program_id(1) == p)(
            functools.partial(_mla_pair, p, q_ref, k_ref, v_ref, o_ref, *rest[2 * n_w + 1:]))


def _mla(q, k, v, l, weights):
    b_, s_, w = q.shape
    n = s_ // MLA_TQ
    steps = b_ * (n // 2)
    whole = lambda width: pl.BlockSpec((None, s_, width), lambda b, p: (b, 0, 0))
    slab_in = [pl.BlockSpec((None, wt.shape[1] // steps, wt.shape[2]), lambda b, p: (l, b * (n // 2) + p, 0))
               for wt in weights]
    slab_out = [pl.BlockSpec((wt.shape[1] // steps, wt.shape[2]), lambda b, p: (b * (n // 2) + p, 0))
                for wt in weights]
    return pl.pallas_call(
        _mla_kernel,
        grid=(b_, n // 2),
        in_specs=[whole(w), whole(w), whole(w)] + slab_in,
        out_specs=[whole(MLA_HEADS * MLA_V)] + slab_out,
        out_shape=[jax.ShapeDtypeStruct((b_, s_, MLA_HEADS * MLA_V), BF16)]
        + [jax.ShapeDtypeStruct(wt.shape[1:], BF16) for wt in weights],
        scratch_shapes=[pltpu.VMEM((MLA_HEADS, n + 1, MLA_TQ, MLA_TK), F32),
                        pltpu.VMEM((2, MLA_HEADS, MLA_TQ, LANES), F32),
                        pltpu.VMEM((2, MLA_HEADS, MLA_TQ, LANES), F32)],
        compiler_params=pltpu.CompilerParams(
            dimension_semantics=("arbitrary", "arbitrary"), vmem_limit_bytes=VMEM_LIMIT),
        name="mla_attention",
    )(q, k, v, *weights)


def _gelu_tanh(x):
    return 0.5 * x * (1.0 + jnp.tanh(math.sqrt(2.0 / math.pi) * (x + 0.044715 * (x * x * x))))


def _ssm_kernel(u_ref, bd_ref, ar_ref, ai_ref, cd_ref, d_ref, wglu_ref, y_ref,
                u_scr, x_scr, y_scr, sr_scr, si_scr, *, layer):
    n = SSM_STATES
    batch = sr_scr.shape[0]
    lane_tiles = SSM_CH // LANES

    @pl.when(pl.program_id(0) == 0)
    def _():
        sr_scr[...] = jnp.zeros_like(sr_scr)
        si_scr[...] = jnp.zeros_like(si_scr)

    for b in range(batch):
        for lt in range(lane_tiles):
            c0 = b * SSM_CH + lt * LANES
            u_scr[lt, pl.ds(b, SSM_TS, stride=batch), :] = u_ref[:, c0:c0 + LANES]
    bounds = [0] + [int(t) * batch for t in np.cumsum(SSM_SPLIT)]
    subs = [slice(a, b) for a, b in zip(bounds[:-1], bounds[1:])]
    us = [jnp.concatenate([u_scr[lt, rows, :] for lt in range(lane_tiles)], axis=-1) for rows in subs]
    for rows, u in zip(subs, us):
        u_bf = u.astype(BF16)
        x_scr[rows, :n] = _dot(u_bf, bd_ref[:, :n])
        x_scr[rows, n:] = _dot(u_bf, bd_ref[:, n:])
    ar = ar_ref[...]
    ai = ai_ref[...]
    xr, xi = sr_scr[...], si_scr[...]
    for rows, u in zip(subs, us):
        for r0 in range(rows.start, rows.stop, batch):
            r = slice(r0, r0 + batch)
            xr, xi = (ar * xr - ai * xi + x_scr[r, :n], ar * xi + ai * xr + x_scr[r, n:])
            x_scr[r, :n] = xr
            x_scr[r, n:] = xi
        mid = (rows.start + rows.stop) // 2
        cx = jnp.concatenate([_dot(x_scr[rows.start:mid, :].astype(BF16), cd_ref[...]),
                              _dot(x_scr[mid:rows.stop, :].astype(BF16), cd_ref[...])], axis=0)
        y = cx + d_ref[layer:layer + 1, :] * u
        y = _gelu_tanh(y)
        y = y * jax.nn.sigmoid(_dot(y.astype(BF16), wglu_ref[...]))
        for lt in range(lane_tiles):
            y_scr[lt, rows, :] = y[:, lt * LANES:(lt + 1) * LANES]
    sr_scr[...] = xr
    si_scr[...] = xi
    for b in range(batch):
        for lt in range(lane_tiles):
            c0 = b * SSM_CH + lt * LANES
            y_ref[:, c0:c0 + LANES] = y_scr[lt, pl.ds(b, SSM_TS, stride=batch), :].astype(BF16)


def _ssm(u, batch, l, bd, ar, ai, cd, d, wglu):
    s_, width = u.shape
    tr = SSM_TS * batch
    return pl.pallas_call(
        functools.partial(_ssm_kernel, layer=l),
        grid=(s_ // SSM_TS,),
        in_specs=[pl.BlockSpec((SSM_TS, width), lambda i: (i, 0))]
        + [_layer_spec(a, l) for a in (bd, ar, ai, cd)] + [_table_spec(d), _layer_spec(wglu, l)],
        out_specs=pl.BlockSpec((SSM_TS, width), lambda i: (i, 0)),
        out_shape=jax.ShapeDtypeStruct((s_, width), BF16),
        scratch_shapes=[pltpu.VMEM((SSM_CH // LANES, tr, LANES), F32),
                        pltpu.VMEM((tr, 2 * SSM_STATES), F32),
                        pltpu.VMEM((SSM_CH // LANES, tr, LANES), F32),
                        pltpu.VMEM((batch, SSM_STATES), F32),
                        pltpu.VMEM((batch, SSM_STATES), F32)],
        compiler_params=pltpu.CompilerParams(
            dimension_semantics=("arbitrary",), vmem_limit_bytes=VMEM_LIMIT),
        name="s5_mixer",
    )(u, bd, ar, ai, cd, d, wglu)


def _ssm_params(a_re, a_im, log_dt, b_re, b_im, c_re, c_im, batch):
    depth = a_re.shape[0]
    lam = lax.complex(a_re.astype(F32), a_im.astype(F32))
    dt = jnp.exp(log_dt.astype(F32))
    a_bar = jnp.exp(lam * dt[..., None])
    b_bar = ((a_bar - 1.0) / lam)[..., None] * lax.complex(b_re.astype(F32), b_im.astype(F32))
    same_group = (np.arange(SSM_CH)[:, None] // SSM_GROUP) == (np.arange(SSM_STATES)[None, :] // SSM_STATE)

    def in_op(m):
        rows = jnp.swapaxes(m, -1, -2).reshape(depth, SSM_CH, SSM_STATE)
        return jnp.where(same_group, jnp.tile(rows, (1, 1, SSM_GROUPS)), 0.0)

    def out_op(m):
        cols = jnp.swapaxes(m.reshape(depth, SSM_CH, SSM_STATE), -1, -2)
        return jnp.where(same_group.T, jnp.tile(cols, (1, SSM_GROUPS, 1)), 0.0)

    bd = jnp.concatenate([in_op(jnp.real(b_bar)), in_op(jnp.imag(b_bar))], axis=-1).astype(BF16)
    cd = jnp.concatenate([out_op(c_re.astype(F32)), -out_op(c_im.astype(F32))], axis=-2).astype(BF16)
    state_rows = (depth, batch, SSM_STATES)
    ar = jnp.broadcast_to(jnp.real(a_bar).reshape(depth, 1, SSM_STATES), state_rows)
    ai = jnp.broadcast_to(jnp.imag(a_bar).reshape(depth, 1, SSM_STATES), state_rows)
    return bd, ar, ai, cd


def _ffn_kernel(x_ref, oa_ref, ob_ref, oc_ref, wo_ref, g2_ref, wg_ref, wu_ref, wd_ref, gf_ref, o_ref, h_scr,
                *, layer, final_norm):
    wo_a = wo_ref[:SWA_Q, :]
    wo_b = wo_ref[SWA_Q:SWA_Q + SSM_CH, :]
    wo_c = wo_ref[SWA_Q + SSM_CH:, :]
    x = x_ref[...] + _dot(oa_ref[...], wo_a) + _dot(ob_ref[...], wo_b) + _dot(oc_ref[...], wo_c)
    o_ref[...] = x
    h_scr[...] = _rms(x, g2_ref[layer:layer + 1, :]).astype(BF16)

    h = h_scr[...]
    gate = _dot(h, wg_ref[...])
    up = _dot(h, wu_ref[...])
    act = (gate * jax.nn.sigmoid(gate) * up).astype(BF16)
    o_ref[...] += _dot(act, wd_ref[...])
    if final_norm:
        o_ref[...] = _rms(o_ref[...], gf_ref[...])


def _ffn(x, oa, ob, oc, l, wo, g2, wg, wu, wd, gf, final_norm):
    b_, s_, _ = x.shape
    tm = FFN_TM
    row3 = lambda b, i: (b, i, 0)
    whole = lambda wt: pl.BlockSpec(wt.shape, lambda b, i: (0, 0))
    return pl.pallas_call(
        functools.partial(_ffn_kernel, layer=l, final_norm=final_norm),
        grid=(b_, s_ // tm),
        in_specs=[pl.BlockSpec((None, tm, D_MODEL), row3),
                  pl.BlockSpec((None, tm, SWA_Q), row3),
                  pl.BlockSpec((tm, SSM_CH), lambda b, i: (i, b)),
                  pl.BlockSpec((None, tm, MLA_HEADS * MLA_V), row3),
                  whole(wo), _table_spec(g2), whole(wg), whole(wu), whole(wd),
                  pl.BlockSpec((1, D_MODEL), lambda b, i: (0, 0))],
        out_specs=pl.BlockSpec((None, tm, D_MODEL), row3),
        out_shape=jax.ShapeDtypeStruct(x.shape, F32),
        scratch_shapes=[pltpu.VMEM((tm, D_MODEL), BF16)],
        compiler_params=pltpu.CompilerParams(
            dimension_semantics=("arbitrary", "arbitrary"), vmem_limit_bytes=VMEM_LIMIT),
        name="out_proj_ffn",
    )(x, oa, ob, oc, wo, g2, wg, wu, wd, gf)


def _rot_half_cols(w):
    half = w.shape[-1] // 2
    return jnp.concatenate([-w[..., half:], w[..., :half]], axis=-1)


def _rope_tile(w):
    z = jnp.zeros(w.shape[:-1] + (MLA_NOPE,), w.dtype)
    z2 = jnp.zeros(w.shape[:-1] + (MLA_HEAD_PAD - MLA_NOPE - MLA_ROPE,), w.dtype)
    return jnp.concatenate([z, w, z2], axis=-1)


def _rope_key_placement():
    half = MLA_ROPE // 2
    j = np.arange(MLA_ROPE)
    place = np.zeros((MLA_ROPE, 2 * LANES), np.float32)
    place[j, MLA_NOPE + j] = 1.0
    place[j, LANES + MLA_NOPE + (j + half) % MLA_ROPE] = np.where(j < half, 1.0, -1.0)
    return jnp.asarray(place, BF16)


def _stacked_weights(w_q_up, w_kv_up):
    lead = w_q_up.shape[:-2]
    qh = w_q_up.reshape(lead + (MLA_Q_RANK, MLA_HEADS, MLA_NOPE + MLA_ROPE))
    pad = jnp.zeros(lead + (MLA_Q_RANK, MLA_HEADS, MLA_HEAD_PAD - MLA_NOPE - MLA_ROPE), w_q_up.dtype)
    q_tiles = jnp.concatenate([qh, pad], axis=-1).reshape(lead + (MLA_Q_RANK, -1))
    q_rot = _rope_tile(_rot_half_cols(qh[..., MLA_NOPE:])).reshape(lead + (MLA_Q_RANK, -1))
    wq_ext = jnp.concatenate([q_tiles, q_rot], axis=-1)
    kvh = w_kv_up.reshape(lead + (MLA_KV_RANK, MLA_HEADS, MLA_NOPE + MLA_V))
    kpad = jnp.zeros(lead + (MLA_KV_RANK, MLA_HEADS, MLA_HEAD_PAD - MLA_NOPE), w_kv_up.dtype)
    k_tiles = jnp.concatenate([kvh[..., :MLA_NOPE], kpad], axis=-1).reshape(lead + (MLA_KV_RANK, -1))
    v_cols = kvh[..., MLA_NOPE:].reshape(lead + (MLA_KV_RANK, -1))
    wkv_ext = jnp.concatenate([k_tiles, v_cols], axis=-1)
    return wq_ext.astype(BF16), wkv_ext.astype(BF16)


def kernel(x, positions, rel_bias, ln1_g, w_in, sinks, ssm_a_re, ssm_a_im, ssm_log_dt, ssm_b_re, ssm_b_im,
           ssm_c_re, ssm_c_im, ssm_d, ssm_w_glu, mla_q_norm_g, mla_w_q_up, mla_kv_norm_g, mla_w_kv_up, w_out,
           ln2_g, w_gate, w_up, w_down, final_g):
    b_ = x.shape[0]
    cos_t, sin_t = _trig_tables(positions)
    bias = _band_bias_tables(rel_bias, sinks)
    wq_ext, wkv_ext = _stacked_weights(mla_w_q_up, mla_w_kv_up)
    w_in_bf = w_in.astype(BF16)
    bd, ar, ai, cd = _ssm_params(ssm_a_re, ssm_a_im, ssm_log_dt, ssm_b_re, ssm_b_im, ssm_c_re, ssm_c_im, b_)
    wglu = ssm_w_glu.astype(BF16)
    gf = final_g.reshape(1, D_MODEL)
    for l in range(DEPTH):
        qa, ka, va, u, qm, km, vm = _in_proj(x, l, ln1_g, w_in_bf, cos_t, sin_t,
                                             mla_q_norm_g, wq_ext, mla_kv_norm_g, wkv_ext)
        o_a = _swa(qa, ka, va, l, bias)
        o_b = _ssm(u, b_, l, bd, ar, ai, cd, ssm_d, wglu)
        o_c, wo, wg, wu, wd = _mla(qm, km, vm, l, (w_out, w_gate, w_up, w_down))
        x = _ffn(x, o_a, o_b, o_c, l, wo, ln2_g, wg, wu, wd, gf, final_norm=(l == DEPTH - 1))
    return x
```
